```python
import jax, jax.numpy as jnp
from jax import lax
import numpy as np

D_MODEL = 2048
BATCH = 8
SEQ = 8192
DEPTH = 2

MEM_LEN = 256
N_BRANCH = 4
BRANCH_W = 1024
POOL_GROUPS = 4
POOL_WINDOWS = (2, 4, 8, 16)
POOL_GW = BRANCH_W // POOL_GROUPS
MLA_HEADS = 8
Q_LORA = 512
KV_LORA = 512
QK_NOPE = 128
QK_ROPE = 64
QK_HEAD = QK_NOPE + QK_ROPE
V_HEAD = 128
ROPE_THETA = 10000.0
CONV_W = 3
XATTN_HEADS = 4
XATTN_HEAD_DIM = BRANCH_W // XATTN_HEADS
Q_BLOCK = 128
EPS = 1e-6
IN_SPLITS = (BRANCH_W, BRANCH_W, Q_LORA, KV_LORA, QK_ROPE, BRANCH_W, BRANCH_W, BRANCH_W, BRANCH_W, BRANCH_W, BRANCH_W, BRANCH_W, N_BRANCH * D_MODEL)
N_IN = 9 * BRANCH_W + Q_LORA + KV_LORA + QK_ROPE + N_BRANCH * D_MODEL

kernel_name = 'hybrid_gated_pool_mla_conv_memxattn'


def rms_norm(x, g):
    xf = x.astype(jnp.float32)
    y = xf * lax.rsqrt(jnp.mean(xf * xf, axis=-1, keepdims=True) + EPS)
    return (y * g.astype(jnp.float32)).astype(x.dtype)


def rope_tables(positions):
    inv = ROPE_THETA ** (-jnp.arange(0, QK_ROPE, 2, dtype=jnp.float32) / QK_ROPE)
    ang = positions.astype(jnp.float32)[..., None] * inv
    return jnp.cos(ang)[:, :, None, :], jnp.sin(ang)[:, :, None, :]


def rotate_tail(xh, cos, sin):
    nope = xh[..., :QK_NOPE]
    r = xh[..., QK_NOPE:].astype(jnp.float32)
    r1, r2 = r[..., :QK_ROPE // 2], r[..., QK_ROPE // 2:]
    rot = jnp.concatenate([r1 * cos - r2 * sin, r2 * cos + r1 * sin], axis=-1).astype(xh.dtype)
    return jnp.concatenate([nope, rot], axis=-1)


def pool_mixer(v, pool_w, pool_scale):
    B, S, _ = v.shape
    vg = v.reshape(B, S, POOL_GROUPS, POOL_GW).astype(jnp.float32)
    cs = jnp.cumsum(vg, axis=1)
    win = jnp.array(POOL_WINDOWS, dtype=jnp.int32)
    t = jnp.arange(S, dtype=jnp.int32)
    prev = t[:, None] - win[None, :]
    cs_prev = cs[:, jnp.maximum(prev, 0), jnp.arange(POOL_GROUPS)[None, :], :]
    cs_prev = jnp.where((prev >= 0)[None, :, :, None], cs_prev, 0.0)
    cnt = jnp.minimum(t[:, None] + 1, win[None, :]).astype(jnp.float32)
    mixed = ((cs - cs_prev) / cnt[None, :, :, None] - vg).astype(v.dtype)
    out = jnp.einsum('bsgc,gcd->bsgd', mixed, pool_w)
    return out.reshape(B, S, BRANCH_W) * pool_scale


def causal_block_attention(q, k, v):
    B, S, H, Dh = q.shape
    nb = S // Q_BLOCK
    scale = Dh ** -0.5
    qb = q.reshape(B, nb, Q_BLOCK, H, Dh).transpose(1, 0, 2, 3, 4)
    starts = jnp.arange(nb, dtype=jnp.int32) * Q_BLOCK
    kpos = jnp.arange(S, dtype=jnp.int32)

    def one_block(args):
        qi, s0 = args
        s = jnp.einsum('bqhd,bkhd->bhqk', qi, k).astype(jnp.float32) * scale
        mask = kpos[None, :] <= (s0 + jnp.arange(Q_BLOCK, dtype=jnp.int32))[:, None]
        s = jnp.where(mask[None, None], s, -jnp.inf)
        p = jax.nn.softmax(s, axis=-1).astype(v.dtype)
        return jnp.einsum('bhqk,bkhd->bqhd', p, v)

    o = lax.map(one_block, (qb, starts))
    return o.transpose(1, 0, 2, 3, 4).reshape(B, S, H, v.shape[-1])


def mla_mixer(cq, ckv, krope, cos, sin, q_a_g, kv_a_g, w_uq, w_ukv, q_g, k_g):
    B, S, _ = cq.shape
    q = (rms_norm(cq, q_a_g) @ w_uq).reshape(B, S, MLA_HEADS, QK_HEAD)
    kv = (rms_norm(ckv, kv_a_g) @ w_ukv).reshape(B, S, MLA_HEADS, QK_NOPE + V_HEAD)
    k_nope, v = kv[..., :QK_NOPE], kv[..., QK_NOPE:]
    k = jnp.concatenate([k_nope, jnp.broadcast_to(krope[:, :, None, :], (B, S, MLA_HEADS, QK_ROPE))], axis=-1)
    q = rotate_tail(rms_norm(q, q_g), cos, sin)
    k = rotate_tail(rms_norm(k, k_g), cos, sin)
    o = causal_block_attention(q, k, v)
    return o.reshape(B, S, MLA_HEADS * V_HEAD)


def conv_mixer(b, c, xc, conv_w):
    u = c * xc
    y = lax.conv_general_dilated(u, conv_w[:, None, :].astype(u.dtype), window_strides=(1,), padding=[(CONV_W - 1, 0)], dimension_numbers=('NWC', 'WIO', 'NWC'), feature_group_count=BRANCH_W)
    return b * y


def memory_xattn(q, mem_kv, q_g, k_g):
    B, S, _ = q.shape
    M = mem_kv.shape[1]
    qh = rms_norm(q.reshape(B, S, XATTN_HEADS, XATTN_HEAD_DIM), q_g)
    k = rms_norm(mem_kv[..., :BRANCH_W].reshape(B, M, XATTN_HEADS, XATTN_HEAD_DIM), k_g)
    v = mem_kv[..., BRANCH_W:].reshape(B, M, XATTN_HEADS, XATTN_HEAD_DIM)
    s = jnp.einsum('bshd,bmhd->bhsm', qh, k).astype(jnp.float32) * (XATTN_HEAD_DIM ** -0.5)
    p = jax.nn.softmax(s, axis=-1).astype(v.dtype)
    return jnp.einsum('bhsm,bmhd->bshd', p, v).reshape(B, S, BRANCH_W)


def _fwd_setup_inputs(seed: int = 0) -> dict:
    key = jax.random.key(seed)
    ks = jax.random.split(key, 24)
    f32 = jnp.float32

    def nrm(k, shape, scale):
        return jax.random.normal(k, shape, f32) * scale

    def gain(k, shape):
        return 1.0 + 0.02 * jax.random.normal(k, shape, f32)

    offs = jax.random.randint(ks[2], (BATCH, 1), 0, 4096, dtype=jnp.int32)
    positions = offs + jnp.arange(SEQ, dtype=jnp.int32)[None, :]
    return {
        'x': nrm(ks[0], (BATCH, SEQ, D_MODEL), 1.0),
        'mem': nrm(ks[1], (BATCH, MEM_LEN, D_MODEL), 1.0),
        'positions': positions,
        'norm_g': gain(ks[3], (DEPTH, D_MODEL)),
        'w_in': nrm(ks[4], (DEPTH, D_MODEL, N_IN), D_MODEL ** -0.5),
        'gate_b': nrm(ks[5], (DEPTH, N_BRANCH * D_MODEL), 0.02),
        'pool_w': nrm(ks[6], (DEPTH, POOL_GROUPS, POOL_GW, POOL_GW), POOL_GW ** -0.5),
        'pool_scale': gain(ks[7], (DEPTH, BRANCH_W)),
        'q_a_norm_g': gain(ks[8], (DEPTH, Q_LORA)),
        'kv_a_norm_g': gain(ks[9], (DEPTH, KV_LORA)),
        'w_uq': nrm(ks[10], (DEPTH, Q_LORA, MLA_HEADS * QK_HEAD), Q_LORA ** -0.5),
        'w_ukv': nrm(ks[11], (DEPTH, KV_LORA, MLA_HEADS * (QK_NOPE + V_HEAD)), KV_LORA ** -0.5),
        'mla_q_norm_g': gain(ks[12], (DEPTH, QK_HEAD)),
        'mla_k_norm_g': gain(ks[13], (DEPTH, QK_HEAD)),
        'conv_w': nrm(ks[14], (DEPTH, CONV_W, BRANCH_W), CONV_W ** -0.5),
        'mem_norm_g': gain(ks[15], (DEPTH, D_MODEL)),
        'w_mem_kv': nrm(ks[16], (DEPTH, D_MODEL, 2 * BRANCH_W), D_MODEL ** -0.5),
        'xattn_q_norm_g': gain(ks[17], (DEPTH, XATTN_HEAD_DIM)),
        'xattn_k_norm_g': gain(ks[18], (DEPTH, XATTN_HEAD_DIM)),
        'w_branch': nrm(ks[19], (DEPTH, N_BRANCH, BRANCH_W, D_MODEL), BRANCH_W ** -0.5),
        'w_out': nrm(ks[20], (DEPTH, D_MODEL, D_MODEL), D_MODEL ** -0.5),
    }


def _fwd_reference(x, mem, positions, norm_g, w_in, gate_b, pool_w, pool_scale, q_a_norm_g, kv_a_norm_g, w_uq, w_ukv, mla_q_norm_g, mla_k_norm_g, conv_w, mem_norm_g, w_mem_kv, xattn_q_norm_g, xattn_k_norm_g, w_branch, w_out):
    B, S, _ = x.shape
    cos, sin = rope_tables(positions)
    split_points = np.cumsum(IN_SPLITS)[:-1].tolist()
    for l in range(DEPTH):
        h = rms_norm(x, norm_g[l])
        proj = h @ w_in[l]
        (pv, pz, cq, ckv, kr, mz, cb, cc, cx, cz, xq, xz, gpre) = jnp.split(proj, split_points, axis=-1)
        y_pool = pool_mixer(pv, pool_w[l], pool_scale[l]) * jax.nn.silu(pz)
        y_mla = mla_mixer(cq, ckv, kr, cos, sin, q_a_norm_g[l], kv_a_norm_g[l], w_uq[l], w_ukv[l], mla_q_norm_g[l], mla_k_norm_g[l]) * jax.nn.silu(mz)
        y_conv = conv_mixer(cb, cc, cx, conv_w[l]) * jax.nn.silu(cz)
        mem_kv = rms_norm(mem, mem_norm_g[l]) @ w_mem_kv[l]
        y_mem = memory_xattn(xq, mem_kv, xattn_q_norm_g[l], xattn_k_norm_g[l]) * jax.nn.silu(xz)
        gates = jax.nn.sigmoid((gpre + gate_b[l]).astype(jnp.float32)).astype(x.dtype).reshape(B, S, N_BRANCH, D_MODEL)
        merged = gates[:, :, 0] * (y_pool @ w_branch[l, 0])
        merged = merged + gates[:, :, 1] * (y_mla @ w_branch[l, 1])
        merged = merged + gates[:, :, 2] * (y_conv @ w_branch[l, 2])
        merged = merged + gates[:, :, 3] * (y_mem @ w_branch[l, 3])
        x = x + merged @ w_out[l]
    return x


import jax as _jax
import jax.numpy as _jnp

TWIN_FORMAT = 'train_step'
FWD_PARAMS = ['x', 'mem', 'positions', 'norm_g', 'w_in', 'gate_b', 'pool_w', 'pool_scale', 'q_a_norm_g', 'kv_a_norm_g', 'w_uq', 'w_ukv', 'mla_q_norm_g', 'mla_k_norm_g', 'conv_w', 'mem_norm_g', 'w_mem_kv', 'xattn_q_norm_g', 'xattn_k_norm_g', 'w_branch', 'w_out']
TWIN_WEIGHTS = ['norm_g', 'w_in', 'gate_b', 'pool_w', 'pool_scale', 'q_a_norm_g', 'kv_a_norm_g', 'w_uq', 'w_ukv', 'mla_q_norm_g', 'mla_k_norm_g', 'conv_w', 'mem_norm_g', 'w_mem_kv', 'xattn_q_norm_g', 'xattn_k_norm_g', 'w_branch', 'w_out']
TWIN_DIFF_INPUT = 'x'
TWIN_INPUTS = ['x', 'mem', 'positions', 'norm_g', 'w_in', 'gate_b', 'pool_w', 'pool_scale', 'q_a_norm_g', 'kv_a_norm_g', 'w_uq', 'w_ukv', 'mla_q_norm_g', 'mla_k_norm_g', 'conv_w', 'mem_norm_g', 'w_mem_kv', 'xattn_q_norm_g', 'xattn_k_norm_g', 'w_branch', 'w_out', 'loss_target', 'm_norm_g', 'm_w_in', 'm_gate_b', 'm_pool_w', 'm_pool_scale', 'm_q_a_norm_g', 'm_kv_a_norm_g', 'm_w_uq', 'm_w_ukv', 'm_mla_q_norm_g', 'm_mla_k_norm_g', 'm_conv_w', 'm_mem_norm_g', 'm_w_mem_kv', 'm_xattn_q_norm_g', 'm_xattn_k_norm_g', 'm_w_branch', 'm_w_out', 'v_norm_g', 'v_w_in', 'v_gate_b', 'v_pool_w', 'v_pool_scale', 'v_q_a_norm_g', 'v_kv_a_norm_g', 'v_w_uq', 'v_w_ukv', 'v_mla_q_norm_g', 'v_mla_k_norm_g', 'v_conv_w', 'v_mem_norm_g', 'v_w_mem_kv', 'v_xattn_q_norm_g', 'v_xattn_k_norm_g', 'v_w_branch', 'v_w_out']
TWIN_OUTPUTS = ['loss', 'grad_x', 'grad_norm_g', 'grad_w_in', 'grad_gate_b', 'grad_pool_w', 'grad_pool_scale', 'grad_q_a_norm_g', 'grad_kv_a_norm_g', 'grad_w_uq', 'grad_w_ukv', 'grad_mla_q_norm_g', 'grad_mla_k_norm_g', 'grad_conv_w', 'grad_mem_norm_g', 'grad_w_mem_kv', 'grad_xattn_q_norm_g', 'grad_xattn_k_norm_g', 'grad_w_branch', 'grad_w_out', 'delta_norm_g', 'delta_w_in', 'delta_gate_b', 'delta_pool_w', 'delta_pool_scale', 'delta_q_a_norm_g', 'delta_kv_a_norm_g', 'delta_w_uq', 'delta_w_ukv', 'delta_mla_q_norm_g', 'delta_mla_k_norm_g', 'delta_conv_w', 'delta_mem_norm_g', 'delta_w_mem_kv', 'delta_xattn_q_norm_g', 'delta_xattn_k_norm_g', 'delta_w_branch', 'delta_w_out', 'new_m_norm_g', 'new_m_w_in', 'new_m_gate_b', 'new_m_pool_w', 'new_m_pool_scale', 'new_m_q_a_norm_g', 'new_m_kv_a_norm_g', 'new_m_w_uq', 'new_m_w_ukv', 'new_m_mla_q_norm_g', 'new_m_mla_k_norm_g', 'new_m_conv_w', 'new_m_mem_norm_g', 'new_m_w_mem_kv', 'new_m_xattn_q_norm_g', 'new_m_xattn_k_norm_g', 'new_m_w_branch', 'new_m_w_out', 'new_v_norm_g', 'new_v_w_in', 'new_v_gate_b', 'new_v_pool_w', 'new_v_pool_scale', 'new_v_q_a_norm_g', 'new_v_kv_a_norm_g', 'new_v_w_uq', 'new_v_w_ukv', 'new_v_mla_q_norm_g', 'new_v_mla_k_norm_g', 'new_v_conv_w', 'new_v_mem_norm_g', 'new_v_w_mem_kv', 'new_v_xattn_q_norm_g', 'new_v_xattn_k_norm_g', 'new_v_w_branch', 'new_v_w_out']
TWIN_LEAF_KINDS = {'loss': 'loss', 'grad_x': 'grad_x', 'grad_norm_g': 'grad_w', 'grad_w_in': 'grad_w', 'grad_gate_b': 'grad_w', 'grad_pool_w': 'grad_w', 'grad_pool_scale': 'grad_w', 'grad_q_a_norm_g': 'grad_w', 'grad_kv_a_norm_g': 'grad_w', 'grad_w_uq': 'grad_w', 'grad_w_ukv': 'grad_w', 'grad_mla_q_norm_g': 'grad_w', 'grad_mla_k_norm_g': 'grad_w', 'grad_conv_w': 'grad_w', 'grad_mem_norm_g': 'grad_w', 'grad_w_mem_kv': 'grad_w', 'grad_xattn_q_norm_g': 'grad_w', 'grad_xattn_k_norm_g': 'grad_w', 'grad_w_branch': 'grad_w', 'grad_w_out': 'grad_w', 'delta_norm_g': 'delta_w', 'delta_w_in': 'delta_w', 'delta_gate_b': 'delta_w', 'delta_pool_w': 'delta_w', 'delta_pool_scale': 'delta_w', 'delta_q_a_norm_g': 'delta_w', 'delta_kv_a_norm_g': 'delta_w', 'delta_w_uq': 'delta_w', 'delta_w_ukv': 'delta_w', 'delta_mla_q_norm_g': 'delta_w', 'delta_mla_k_norm_g': 'delta_w', 'delta_conv_w': 'delta_w', 'delta_mem_norm_g': 'delta_w', 'delta_w_mem_kv': 'delta_w', 'delta_xattn_q_norm_g': 'delta_w', 'delta_xattn_k_norm_g': 'delta_w', 'delta_w_branch': 'delta_w', 'delta_w_out': 'delta_w', 'new_m_norm_g': 'new_m', 'new_m_w_in': 'new_m', 'new_m_gate_b': 'new_m', 'new_m_pool_w': 'new_m', 'new_m_pool_scale': 'new_m', 'new_m_q_a_norm_g': 'new_m', 'new_m_kv_a_norm_g': 'new_m', 'new_m_w_uq': 'new_m', 'new_m_w_ukv': 'new_m', 'new_m_mla_q_norm_g': 'new_m', 'new_m_mla_k_norm_g': 'new_m', 'new_m_conv_w': 'new_m', 'new_m_mem_norm_g': 'new_m', 'new_m_w_mem_kv': 'new_m', 'new_m_xattn_q_norm_g': 'new_m', 'new_m_xattn_k_norm_g': 'new_m', 'new_m_w_branch': 'new_m', 'new_m_w_out': 'new_m', 'new_v_norm_g': 'new_v', 'new_v_w_in': 'new_v', 'new_v_gate_b': 'new_v', 'new_v_pool_w': 'new_v', 'new_v_pool_scale': 'new_v', 'new_v_q_a_norm_g': 'new_v', 'new_v_kv_a_norm_g': 'new_v', 'new_v_w_uq': 'new_v', 'new_v_w_ukv': 'new_v', 'new_v_mla_q_norm_g': 'new_v', 'new_v_mla_k_norm_g': 'new_v', 'new_v_conv_w': 'new_v', 'new_v_mem_norm_g': 'new_v', 'new_v_w_mem_kv': 'new_v', 'new_v_xattn_q_norm_g': 'new_v', 'new_v_xattn_k_norm_g': 'new_v', 'new_v_w_branch': 'new_v', 'new_v_w_out': 'new_v'}


def _forward(args):
    return _fwd_reference(*[args[k] for k in FWD_PARAMS])


def _output_shape():
    def fwd():
        inp = _fwd_setup_inputs(0)
        return _fwd_reference(*[inp[k] for k in FWD_PARAMS])
    out = _jax.eval_shape(fwd)
    return out.shape, out.dtype

N_MICROBATCH = 1
ADAM_LR = 0.001
ADAM_B1 = 0.9
ADAM_B2 = 0.999
ADAM_EPS = 1e-08
ADAM_WD = 0.01
ADAM_STEP = 10
PER_EXAMPLE_BATCH_AXIS = {'x': 0, 'mem': 0, 'positions': 0, 'loss_target': 0}
SHARED_INPUTS = []
_WEIGHT_DTYPES = {'norm_g': _jnp.float32, 'w_in': _jnp.float32, 'gate_b': _jnp.float32, 'pool_w': _jnp.float32, 'pool_scale': _jnp.float32, 'q_a_norm_g': _jnp.float32, 'kv_a_norm_g': _jnp.float32, 'w_uq': _jnp.float32, 'w_ukv': _jnp.float32, 'mla_q_norm_g': _jnp.float32, 'mla_k_norm_g': _jnp.float32, 'conv_w': _jnp.float32, 'mem_norm_g': _jnp.float32, 'w_mem_kv': _jnp.float32, 'xattn_q_norm_g': _jnp.float32, 'xattn_k_norm_g': _jnp.float32, 'w_branch': _jnp.float32, 'w_out': _jnp.float32}
MOMENT_SCALE = {'norm_g': 1.969443e+01, 'w_in': 1.488601e-01, 'gate_b': 7.201974e-01, 'pool_w': 2.483821e-01, 'pool_scale': 5.189003e+00, 'q_a_norm_g': 3.833701e-02, 'kv_a_norm_g': 1.240543e-01, 'w_uq': 2.210001e-02, 'w_ukv': 2.781009e-02, 'mla_q_norm_g': 1.793540e-01, 'mla_k_norm_g': 1.800738e-01, 'conv_w': 3.563691e+00, 'mem_norm_g': 2.682343e-02, 'w_mem_kv': 1.546074e-02, 'xattn_q_norm_g': 2.417151e-01, 'xattn_k_norm_g': 2.401612e-01, 'w_branch': 8.649415e-02, 'w_out': 1.644157e-01}


def _to_microbatches(a, axis):
    t = _jnp.moveaxis(a, axis, 0)
    t = t.reshape((N_MICROBATCH, t.shape[0] // N_MICROBATCH) + t.shape[1:])
    return _jnp.moveaxis(t, 1, axis + 1)


def setup_inputs(seed: int = 0) -> dict:
    inp = _fwd_setup_inputs(seed)
    key = _jax.random.fold_in(_jax.random.key(seed), 7919)
    shape, _ = _output_shape()
    out = dict(inp)
    out["loss_target"] = _jax.random.normal(_jax.random.fold_in(key, 0), shape, _jnp.float32)
    for i, name in enumerate(TWIN_WEIGHTS):
        w = inp[name].astype(_jnp.float32)
        if MOMENT_SCALE is None:
            s = _jnp.sqrt(_jnp.mean(_jnp.square(w)) + 1e-30)
        else:
            s = MOMENT_SCALE[name]
        km, kv = _jax.random.split(_jax.random.fold_in(key, i + 1))
        out[name] = w
        out["m_" + name] = s * _jax.random.normal(km, w.shape, _jnp.float32)
        out["v_" + name] = (s * s) * _jax.random.uniform(kv, w.shape, _jnp.float32, 0.5, 1.5)
    if N_MICROBATCH > 1:
        for name, axis in PER_EXAMPLE_BATCH_AXIS.items():
            out[name] = _to_microbatches(out[name], axis)
    return {'x': out['x'], 'mem': out['mem'], 'positions': out['positions'], 'norm_g': out['norm_g'], 'w_in': out['w_in'], 'gate_b': out['gate_b'], 'pool_w': out['pool_w'], 'pool_scale': out['pool_scale'], 'q_a_norm_g': out['q_a_norm_g'], 'kv_a_norm_g': out['kv_a_norm_g'], 'w_uq': out['w_uq'], 'w_ukv': out['w_ukv'], 'mla_q_norm_g': out['mla_q_norm_g'], 'mla_k_norm_g': out['mla_k_norm_g'], 'conv_w': out['conv_w'], 'mem_norm_g': out['mem_norm_g'], 'w_mem_kv': out['w_mem_kv'], 'xattn_q_norm_g': out['xattn_q_norm_g'], 'xattn_k_norm_g': out['xattn_k_norm_g'], 'w_branch': out['w_branch'], 'w_out': out['w_out'], 'loss_target': out['loss_target'], 'm_norm_g': out['m_norm_g'], 'm_w_in': out['m_w_in'], 'm_gate_b': out['m_gate_b'], 'm_pool_w': out['m_pool_w'], 'm_pool_scale': out['m_pool_scale'], 'm_q_a_norm_g': out['m_q_a_norm_g'], 'm_kv_a_norm_g': out['m_kv_a_norm_g'], 'm_w_uq': out['m_w_uq'], 'm_w_ukv': out['m_w_ukv'], 'm_mla_q_norm_g': out['m_mla_q_norm_g'], 'm_mla_k_norm_g': out['m_mla_k_norm_g'], 'm_conv_w': out['m_conv_w'], 'm_mem_norm_g': out['m_mem_norm_g'], 'm_w_mem_kv': out['m_w_mem_kv'], 'm_xattn_q_norm_g': out['m_xattn_q_norm_g'], 'm_xattn_k_norm_g': out['m_xattn_k_norm_g'], 'm_w_branch': out['m_w_branch'], 'm_w_out': out['m_w_out'], 'v_norm_g': out['v_norm_g'], 'v_w_in': out['v_w_in'], 'v_gate_b': out['v_gate_b'], 'v_pool_w': out['v_pool_w'], 'v_pool_scale': out['v_pool_scale'], 'v_q_a_norm_g': out['v_q_a_norm_g'], 'v_kv_a_norm_g': out['v_kv_a_norm_g'], 'v_w_uq': out['v_w_uq'], 'v_w_ukv': out['v_w_ukv'], 'v_mla_q_norm_g': out['v_mla_q_norm_g'], 'v_mla_k_norm_g': out['v_mla_k_norm_g'], 'v_conv_w': out['v_conv_w'], 'v_mem_norm_g': out['v_mem_norm_g'], 'v_w_mem_kv': out['v_w_mem_kv'], 'v_xattn_q_norm_g': out['v_xattn_q_norm_g'], 'v_xattn_k_norm_g': out['v_xattn_k_norm_g'], 'v_w_branch': out['v_w_branch'], 'v_w_out': out['v_w_out']}


def _loss(weights, diff, rest, loss_target):
    with _jax.named_scope("forward"):
        args = {**rest, TWIN_DIFF_INPUT: diff, **{k: w.astype(_WEIGHT_DTYPES[k]) for k, w in weights.items()}}
        y = _forward(args)
    with _jax.named_scope("loss_head"):
        err = _jnp.square(y.astype(_jnp.float32) - loss_target)
        return 0.5 * _jnp.sum(_jnp.mean(err, axis=-1)) if err.ndim else 0.5 * err


def _adamw(w, g, m, v):
    m = ADAM_B1 * m + (1.0 - ADAM_B1) * g
    v = ADAM_B2 * v + (1.0 - ADAM_B2) * _jnp.square(g)
    m_hat = m / (1.0 - ADAM_B1 ** ADAM_STEP)
    v_hat = v / (1.0 - ADAM_B2 ** ADAM_STEP)
    delta = -ADAM_LR * (m_hat / (_jnp.sqrt(v_hat) + ADAM_EPS) + ADAM_WD * w)
    return delta, m, v


def reference(x, mem, positions, norm_g, w_in, gate_b, pool_w, pool_scale, q_a_norm_g, kv_a_norm_g, w_uq, w_ukv, mla_q_norm_g, mla_k_norm_g, conv_w, mem_norm_g, w_mem_kv, xattn_q_norm_g, xattn_k_norm_g, w_branch, w_out, loss_target, m_norm_g, m_w_in, m_gate_b, m_pool_w, m_pool_scale, m_q_a_norm_g, m_kv_a_norm_g, m_w_uq, m_w_ukv, m_mla_q_norm_g, m_mla_k_norm_g, m_conv_w, m_mem_norm_g, m_w_mem_kv, m_xattn_q_norm_g, m_xattn_k_norm_g, m_w_branch, m_w_out, v_norm_g, v_w_in, v_gate_b, v_pool_w, v_pool_scale, v_q_a_norm_g, v_kv_a_norm_g, v_w_uq, v_w_ukv, v_mla_q_norm_g, v_mla_k_norm_g, v_conv_w, v_mem_norm_g, v_w_mem_kv, v_xattn_q_norm_g, v_xattn_k_norm_g, v_w_branch, v_w_out):
    given = dict(x=x, mem=mem, positions=positions, norm_g=norm_g, w_in=w_in, gate_b=gate_b, pool_w=pool_w, pool_scale=pool_scale, q_a_norm_g=q_a_norm_g, kv_a_norm_g=kv_a_norm_g, w_uq=w_uq, w_ukv=w_ukv, mla_q_norm_g=mla_q_norm_g, mla_k_norm_g=mla_k_norm_g, conv_w=conv_w, mem_norm_g=mem_norm_g, w_mem_kv=w_mem_kv, xattn_q_norm_g=xattn_q_norm_g, xattn_k_norm_g=xattn_k_norm_g, w_branch=w_branch, w_out=w_out, loss_target=loss_target, m_norm_g=m_norm_g, m_w_in=m_w_in, m_gate_b=m_gate_b, m_pool_w=m_pool_w, m_pool_scale=m_pool_scale, m_q_a_norm_g=m_q_a_norm_g, m_kv_a_norm_g=m_kv_a_norm_g, m_w_uq=m_w_uq, m_w_ukv=m_w_ukv, m_mla_q_norm_g=m_mla_q_norm_g, m_mla_k_norm_g=m_mla_k_norm_g, m_conv_w=m_conv_w, m_mem_norm_g=m_mem_norm_g, m_w_mem_kv=m_w_mem_kv, m_xattn_q_norm_g=m_xattn_q_norm_g, m_xattn_k_norm_g=m_xattn_k_norm_g, m_w_branch=m_w_branch, m_w_out=m_w_out, v_norm_g=v_norm_g, v_w_in=v_w_in, v_gate_b=v_gate_b, v_pool_w=v_pool_w, v_pool_scale=v_pool_scale, v_q_a_norm_g=v_q_a_norm_g, v_kv_a_norm_g=v_kv_a_norm_g, v_w_uq=v_w_uq, v_w_ukv=v_w_ukv, v_mla_q_norm_g=v_mla_q_norm_g, v_mla_k_norm_g=v_mla_k_norm_g, v_conv_w=v_conv_w, v_mem_norm_g=v_mem_norm_g, v_w_mem_kv=v_w_mem_kv, v_xattn_q_norm_g=v_xattn_q_norm_g, v_xattn_k_norm_g=v_xattn_k_norm_g, v_w_branch=v_w_branch, v_w_out=v_w_out)
    weights = {n: given[n] for n in TWIN_WEIGHTS}
    shared = {n: given[n] for n in SHARED_INPUTS}
    per_example = {n: given[n] for n in ['x', 'mem', 'positions']}
    grad_fn = _jax.value_and_grad(_loss, argnums=(0, 1))

    def one_microbatch(ex, loss_target):
        ex = dict(ex)
        diff = ex.pop(TWIN_DIFF_INPUT)
        return grad_fn(weights, diff, {**shared, **ex}, loss_target)

    if N_MICROBATCH == 1:
        loss, (grad_w, grad_x) = one_microbatch(per_example, given["loss_target"])
    else:
        def body(carry, xs):
            loss_sum, grad_sum = carry
            l_k, (gw_k, gx_k) = one_microbatch(xs[0], xs[1])
            with _jax.named_scope("update"):
                return (loss_sum + l_k, _jax.tree.map(_jnp.add, grad_sum, gw_k)), gx_k

        init = (_jnp.zeros((), _jnp.float32), _jax.tree.map(_jnp.zeros_like, weights))
        (loss, grad_w), grad_x = _jax.lax.scan(body, init, (per_example, given["loss_target"]))
    with _jax.named_scope("update"):
        delta_w, new_m, new_v = {}, {}, {}
        for n in TWIN_WEIGHTS:
            delta_w[n], new_m[n], new_v[n] = _adamw(weights[n], grad_w[n], given["m_" + n], given["v_" + n])
    return (loss, grad_x, *[grad_w[n] for n in TWIN_WEIGHTS], *[delta_w[n] for n in TWIN_WEIGHTS],
            *[new_m[n] for n in TWIN_WEIGHTS], *[new_v[n] for n in TWIN_WEIGHTS])
```

```python
import functools

import jax
import jax.numpy as jnp
from jax import lax
from jax.experimental import pallas as pl
from jax.experimental.pallas import tpu as pltpu

F32 = jnp.float32
BF16 = jnp.bfloat16
MESH_AXES = ("x", "y", "c")
N_DEV = 8

BRANCH_W = 1024
POOL_WINDOWS = (2, 4, 8, 16)
POOL_GW = BRANCH_W // 4
MLA_HEADS = 8
QK_NOPE = 128
QK_ROPE = 64
QK_HEAD = QK_NOPE + QK_ROPE
QK_PAD = 256
V_HEAD = 128
XATTN_HEADS = 4
XATTN_HD = BRANCH_W // XATTN_HEADS
ROPE_THETA = 10000.0
EPS = 1e-6
ADAM_LR = 0.001
ADAM_B1 = 0.9
ADAM_B2 = 0.999
ADAM_EPS = 1e-08
ADAM_WD = 0.01
ADAM_STEP = 10
HALO = 16
NEG = -1e30
VMEM_LIMIT = 56 << 20


def _pick(n, pref, align):
    if n <= pref:
        return n
    t = (pref // align) * align
    while t >= align:
        if n % t == 0:
            return t
        t -= align
    return n


def _params(sem):
    return pltpu.CompilerParams(dimension_semantics=sem, vmem_limit_bytes=VMEM_LIMIT)


def _silu(z):
    return z * (1.0 / (1.0 + jnp.exp(-z)))


def _silu_and_grad(z):
    s = 1.0 / (1.0 + jnp.exp(-z))
    return z * s, s * (1.0 + z * (1.0 - s))


def _dot(a, b):
    return jnp.dot(a.astype(BF16), b.astype(BF16), preferred_element_type=F32)


def _dot_nt(a, b):
    return lax.dot_general(a.astype(BF16), b.astype(BF16), (((1,), (1,)), ((), ())), preferred_element_type=F32)


def _dot_tn(a, b):
    return lax.dot_general(a.astype(BF16), b.astype(BF16), (((0,), (0,)), ((), ())), preferred_element_type=F32)


def _rms(x, g, n):
    r = lax.rsqrt(jnp.sum(x * x, axis=-1, keepdims=True) / n + EPS)
    return x * r * g, r


def _rms_bwd(x, g, r, dy, n):
    dyg = dy * g
    dx = r * dyg - x * (r * r * r) * (jnp.sum(dyg * x, axis=-1, keepdims=True) / n)
    dg = jnp.sum(dy * x * r, axis=0, keepdims=True)
    return dx, dg


def _col(tm, w, off):
    assert off % w == 0
    cb = off // w
    return pl.BlockSpec((tm, w), lambda i: (i, cb))


def _whole(shape):
    nd = len(shape)
    return pl.BlockSpec(shape, lambda *_: (0,) * nd)


def _matmul(a, b, mode, out_dtype, name, tm=1024, tn=1024, tk=2048, res=None):
    if mode == "nn":
        (M, K), N = a.shape, b.shape[1]
    elif mode == "nt":
        (M, K), N = a.shape, b.shape[0]
    else:
        (K, M), N = a.shape, b.shape[1]
    tm, tn, tk = _pick(M, tm, 128), _pick(N, tn, 128), _pick(K, tk, 128)
    nk = K // tk
    if mode == "nn":
        a_spec = pl.BlockSpec((tm, tk), lambda i, j, k: (i, k))
        b_spec = pl.BlockSpec((tk, tn), lambda i, j, k: (k, j))
        op = _dot
    elif mode == "nt":
        a_spec = pl.BlockSpec((tm, tk), lambda i, j, k: (i, k))
        b_spec = pl.BlockSpec((tn, tk), lambda i, j, k: (j, k))
        op = _dot_nt
    else:
        a_spec = pl.BlockSpec((tk, tm), lambda i, j, k: (k, i))
        b_spec = pl.BlockSpec((tk, tn), lambda i, j, k: (k, j))
        op = _dot_tn
    o_spec = pl.BlockSpec((tm, tn), lambda i, j, k: (i, j))
    has_res = res is not None

    def body(*refs):
        a_ref, b_ref = refs[0], refs[1]
        res_ref = refs[2] if has_res else None
        o_ref = refs[2 + has_res]
        part = op(a_ref[...], b_ref[...])

        def finish(acc):
            if has_res:
                acc = res_ref[...] + acc
            o_ref[...] = acc.astype(out_dtype)

        if nk == 1:
            finish(part)
        else:
            acc_ref = refs[3 + has_res]
            k = pl.program_id(2)

            @pl.when(k == 0)
            def _():
                acc_ref[...] = part

            @pl.when(k > 0)
            def _():
                acc_ref[...] += part

            @pl.when(k == nk - 1)
            def _():
                finish(acc_ref[...])

    ins = [a, b] + ([res] if has_res else [])
    in_specs = [a_spec, b_spec] + ([o_spec] if has_res else [])
    scratch = [pltpu.VMEM((tm, tn), F32)] if nk > 1 else []
    return pl.pallas_call(
        body, grid=(M // tm, N // tn, nk), in_specs=in_specs, out_specs=o_spec,
        out_shape=jax.ShapeDtypeStruct((M, N), out_dtype), scratch_shapes=scratch, name=name,
        compiler_params=_params(("parallel", "parallel", "arbitrary")),
    )(*ins)


def _rms_rows(x, width, off, g, name):
    S = x.shape[0]
    tm = _pick(S, 512, 16)

    def body(x_ref, g_ref, o_ref):
        y, _ = _rms(x_ref[...], g_ref[...], width)
        o_ref[...] = y.astype(BF16)

    return pl.pallas_call(
        body, grid=(S // tm,), in_specs=[_col(tm, width, off), _whole((1, width))], out_specs=_col(tm, width, 0),
        out_shape=jax.ShapeDtypeStruct((S, width), BF16), name=name, compiler_params=_params(("parallel",)),
    )(x, g)


def _rms_rows_bwd(x, width, off, g, dy, out_dtype, name, res=None):
    S = x.shape[0]
    tm = _pick(S, 256, 16)
    has_res = res is not None

    def body(*refs):
        x_ref, g_ref, dy_ref = refs[:3]
        res_ref = refs[3] if has_res else None
        dx_ref, dg_ref = refs[3 + has_res:]
        xv, gv = x_ref[...], g_ref[...]
        _, r = _rms(xv, gv, width)
        dx, dg = _rms_bwd(xv, gv, r, dy_ref[...], width)
        if has_res:
            dx = res_ref[...] + dx
        dx_ref[...] = dx.astype(out_dtype)

        @pl.when(pl.program_id(0) == 0)
        def _():
            dg_ref[...] = jnp.zeros_like(dg_ref)

        dg_ref[...] += dg

    ins = [x, g, dy] + ([res] if has_res else [])
    in_specs = [_col(tm, width, off), _whole((1, width)), _col(tm, width, 0)] + ([_col(tm, width, 0)] if has_res else [])
    return pl.pallas_call(
        body, grid=(S // tm,), in_specs=in_specs, out_specs=[_col(tm, width, 0), _whole((1, width))],
        out_shape=[jax.ShapeDtypeStruct((S, width), out_dtype), jax.ShapeDtypeStruct((1, width), F32)], name=name,
        compiler_params=_params(("arbitrary",)),
    )(*ins)


def _norm_g_grad(x, g, dy, name):
    S, width = x.shape
    tm = _pick(S, 256, 16)

    def body(x_ref, g_ref, dy_ref, dg_ref):
        xv = x_ref[...]
        _, r = _rms(xv, g_ref[...], width)

        @pl.when(pl.program_id(0) == 0)
        def _():
            dg_ref[...] = jnp.zeros_like(dg_ref)

        dg_ref[...] += jnp.sum(dy_ref[...] * xv * r, axis=0, keepdims=True)

    return pl.pallas_call(
        body, grid=(S // tm,), in_specs=[_col(tm, width, 0), _whole((1, width)), _col(tm, width, 0)],
        out_specs=_whole((1, width)), out_shape=jax.ShapeDtypeStruct((1, width), F32), name=name,
        compiler_params=_params(("arbitrary",)),
    )(x, g, dy)


def _loss_rows(y, target, name):
    S, D = y.shape
    tm = _pick(S, 512, 16)

    def body(y_ref, t_ref, g_ref, l_ref):
        err = y_ref[...] - t_ref[...]
        g_ref[...] = err / D

        @pl.when(pl.program_id(0) == 0)
        def _():
            l_ref[...] = jnp.zeros_like(l_ref)

        row = jnp.sum(err * err, axis=-1, keepdims=True) / D
        l_ref[...] += jnp.sum(row, axis=0, keepdims=True)

    return pl.pallas_call(
        body, grid=(S // tm,), in_specs=[_col(tm, D, 0), _col(tm, D, 0)], out_specs=[_col(tm, D, 0), _whole((1, 1))],
        out_shape=[jax.ShapeDtypeStruct((S, D), F32), jax.ShapeDtypeStruct((1, 1), F32)], name=name,
        compiler_params=_params(("arbitrary",)),
    )(y, target)


def _row_index(i, ts):
    return i * ts + lax.broadcasted_iota(jnp.int32, (ts, 1), 0)


def _pool_mixed(scr, pv_ref, g, t, ts):
    w, lo = POOL_WINDOWS[g], g * POOL_GW
    win = scr[HALO:HALO + ts, lo:lo + POOL_GW]
    for k in range(1, w):
        win = win + scr[HALO - k:HALO - k + ts, lo:lo + POOL_GW]
    cnt = jnp.minimum(t + 1, w).astype(F32)
    return win / cnt - pv_ref[:, lo:lo + POOL_GW], cnt


def _halo_before(ts, off):
    nb = ts // HALO
    cb = off // BRANCH_W
    return pl.BlockSpec((HALO, BRANCH_W), lambda i: (jnp.maximum(i * nb - 1, 0), cb))


def _halo_after(ts, off, n_tiles):
    nb = ts // HALO
    cb = off // BRANCH_W
    last = n_tiles * nb - 1
    return pl.BlockSpec((HALO, BRANCH_W), lambda i: (jnp.minimum((i + 1) * nb, last), cb))


def _pool_fwd(proj, offs, pool_w, pool_scale, name):
    S = proj.shape[0]
    ts = _pick(S, 512, HALO)

    def body(pv_ref, halo_ref, pz_ref, w_ref, sc_ref, y_ref, scr):
        i = pl.program_id(0)
        scr[0:HALO, :] = jnp.where(i > 0, halo_ref[...], 0.0)
        scr[HALO:HALO + ts, :] = pv_ref[...]
        t = _row_index(i, ts)
        for g in range(4):
            lo = g * POOL_GW
            mixed, _ = _pool_mixed(scr, pv_ref, g, t, ts)
            out = _dot(mixed, w_ref[g])
            y_ref[:, lo:lo + POOL_GW] = (out * sc_ref[:, lo:lo + POOL_GW] * _silu(pz_ref[:, lo:lo + POOL_GW])).astype(BF16)

    return pl.pallas_call(
        body, grid=(S // ts,),
        in_specs=[_col(ts, BRANCH_W, offs["pv"]), _halo_before(ts, offs["pv"]), _col(ts, BRANCH_W, offs["pz"]),
                  _whole((4, POOL_GW, POOL_GW)), _whole((1, BRANCH_W))],
        out_specs=_col(ts, BRANCH_W, 0), out_shape=jax.ShapeDtypeStruct((S, BRANCH_W), BF16),
        scratch_shapes=[pltpu.VMEM((HALO + ts, BRANCH_W), F32)], name=name, compiler_params=_params(("parallel",)),
    )(proj, proj, proj, pool_w, pool_scale)


def _pool_bwd_a(proj, offs, pool_w, pool_scale, dy, name):
    S = proj.shape[0]
    ts = _pick(S, 512, HALO)

    def body(pv_ref, halo_ref, pz_ref, w_ref, sc_ref, dy_ref, dpz_ref, e_ref, dw_ref, dsc_ref, scr):
        i = pl.program_id(0)
        scr[0:HALO, :] = jnp.where(i > 0, halo_ref[...], 0.0)
        scr[HALO:HALO + ts, :] = pv_ref[...]
        t = _row_index(i, ts)

        @pl.when(i == 0)
        def _():
            dw_ref[...] = jnp.zeros_like(dw_ref)
            dsc_ref[...] = jnp.zeros_like(dsc_ref)

        for g in range(4):
            lo = g * POOL_GW
            sl = slice(lo, lo + POOL_GW)
            mixed, cnt = _pool_mixed(scr, pv_ref, g, t, ts)
            out = _dot(mixed, w_ref[g])
            sz, dsz = _silu_and_grad(pz_ref[:, sl])
            dyv, sc = dy_ref[:, sl], sc_ref[:, sl]
            d_out = dyv * sc * sz
            dsc_ref[:, sl] += jnp.sum(dyv * out * sz, axis=0, keepdims=True)
            dpz_ref[:, sl] = (dyv * out * sc * dsz).astype(BF16)
            dw_ref[g] += _dot_tn(mixed, d_out)
            e_ref[:, sl] = _dot_nt(d_out, w_ref[g]) / cnt

    return pl.pallas_call(
        body, grid=(S // ts,),
        in_specs=[_col(ts, BRANCH_W, offs["pv"]), _halo_before(ts, offs["pv"]), _col(ts, BRANCH_W, offs["pz"]),
                  _whole((4, POOL_GW, POOL_GW)), _whole((1, BRANCH_W)), _col(ts, BRANCH_W, 0)],
        out_specs=[_col(ts, BRANCH_W, 0), _col(ts, BRANCH_W, 0), _whole((4, POOL_GW, POOL_GW)), _whole((1, BRANCH_W))],
        out_shape=[jax.ShapeDtypeStruct((S, BRANCH_W), BF16), jax.ShapeDtypeStruct((S, BRANCH_W), F32),
                   jax.ShapeDtypeStruct((4, POOL_GW, POOL_GW), F32), jax.ShapeDtypeStruct((1, BRANCH_W), F32)],
        scratch_shapes=[pltpu.VMEM((HALO + ts, BRANCH_W), F32)], name=name, compiler_params=_params(("arbitrary",)),
    )(proj, proj, proj, pool_w, pool_scale, dy)


def _pool_bwd_b(e, name):
    S = e.shape[0]
    ts = _pick(S, 512, HALO)
    n_tiles = S // ts

    def body(e_ref, halo_ref, dpv_ref, scr):
        i = pl.program_id(0)
        scr[0:ts, :] = e_ref[...]
        scr[ts:ts + HALO, :] = jnp.where(i < n_tiles - 1, halo_ref[...], 0.0)
        t = _row_index(i, ts)
        for g in range(4):
            w, lo = POOL_WINDOWS[g], g * POOL_GW
            acc = scr[0:ts, lo:lo + POOL_GW]
            ev = acc
            for k in range(1, w):
                acc = acc + scr[k:k + ts, lo:lo + POOL_GW]
            cnt = jnp.minimum(t + 1, w).astype(F32)
            dpv_ref[:, lo:lo + POOL_GW] = (acc - ev * cnt).astype(BF16)

    return pl.pallas_call(
        body, grid=(n_tiles,), in_specs=[_col(ts, BRANCH_W, 0), _halo_after(ts, 0, n_tiles)], out_specs=_col(ts, BRANCH_W, 0),
        out_shape=jax.ShapeDtypeStruct((S, BRANCH_W), BF16), scratch_shapes=[pltpu.VMEM((ts + HALO, BRANCH_W), F32)],
        name=name, compiler_params=_params(("parallel",)),
    )(e, e)


def _conv_fwd(proj, offs, conv_w, name):
    S = proj.shape[0]
    ts = _pick(S, 512, HALO)

    def body(cb_ref, cc_ref, cx_ref, cz_ref, hc_ref, hx_ref, w_ref, y_ref, scr):
        i = pl.program_id(0)
        scr[0:HALO, :] = jnp.where(i > 0, hc_ref[...] * hx_ref[...], 0.0)
        scr[HALO:HALO + ts, :] = cc_ref[...] * cx_ref[...]
        y = w_ref[0:1, :] * scr[HALO - 2:HALO - 2 + ts, :] + w_ref[1:2, :] * scr[HALO - 1:HALO - 1 + ts, :] + w_ref[2:3, :] * scr[HALO:HALO + ts, :]
        y_ref[...] = (cb_ref[...] * y * _silu(cz_ref[...])).astype(BF16)

    return pl.pallas_call(
        body, grid=(S // ts,),
        in_specs=[_col(ts, BRANCH_W, offs["cb"]), _col(ts, BRANCH_W, offs["cc"]), _col(ts, BRANCH_W, offs["cx"]),
                  _col(ts, BRANCH_W, offs["cz"]), _halo_before(ts, offs["cc"]), _halo_before(ts, offs["cx"]), _whole((3, BRANCH_W))],
        out_specs=_col(ts, BRANCH_W, 0), out_shape=jax.ShapeDtypeStruct((S, BRANCH_W), BF16),
        scratch_shapes=[pltpu.VMEM((HALO + ts, BRANCH_W), F32)], name=name, compiler_params=_params(("parallel",)),
    )(proj, proj, proj, proj, proj, proj, conv_w)


def _conv_bwd(proj, offs, conv_w, dy, name):
    S = proj.shape[0]
    ts = _pick(S, 256, HALO)
    n_tiles = S // ts

    def body(cb_ref, cc_ref, cx_ref, cz_ref, hc_ref, hx_ref, w_ref, dy_ref, ady_ref, acb_ref, acz_ref,
             dcb_ref, dcc_ref, dcx_ref, dcz_ref, dw_ref, scr_u, scr_d):
        i = pl.program_id(0)
        u = cc_ref[...] * cx_ref[...]
        scr_u[0:HALO, :] = jnp.where(i > 0, hc_ref[...] * hx_ref[...], 0.0)
        scr_u[HALO:HALO + ts, :] = u
        u1 = scr_u[HALO - 1:HALO - 1 + ts, :]
        u2 = scr_u[HALO - 2:HALO - 2 + ts, :]
        y = w_ref[0:1, :] * u2 + w_ref[1:2, :] * u1 + w_ref[2:3, :] * u
        sz, dsz = _silu_and_grad(cz_ref[...])
        dyv, cb = dy_ref[...], cb_ref[...]
        dcb_ref[...] = (dyv * y * sz).astype(BF16)
        dcz_ref[...] = (dyv * cb * y * dsz).astype(BF16)
        d_y = dyv * cb * sz
        scr_d[0:ts, :] = d_y
        scr_d[ts:ts + HALO, :] = jnp.where(i < n_tiles - 1, ady_ref[...] * acb_ref[...] * _silu(acz_ref[...]), 0.0)
        du = w_ref[2:3, :] * d_y + w_ref[1:2, :] * scr_d[1:1 + ts, :] + w_ref[0:1, :] * scr_d[2:2 + ts, :]
        dcc_ref[...] = (du * cx_ref[...]).astype(BF16)
        dcx_ref[...] = (du * cc_ref[...]).astype(BF16)

        @pl.when(i == 0)
        def _():
            dw_ref[...] = jnp.zeros_like(dw_ref)

        dw_ref[0:1, :] += jnp.sum(d_y * u2, axis=0, keepdims=True)
        dw_ref[1:2, :] += jnp.sum(d_y * u1, axis=0, keepdims=True)
        dw_ref[2:3, :] += jnp.sum(d_y * u, axis=0, keepdims=True)

    row = lambda off: _col(ts, BRANCH_W, off)
    return pl.pallas_call(
        body, grid=(n_tiles,),
        in_specs=[row(offs["cb"]), row(offs["cc"]), row(offs["cx"]), row(offs["cz"]), _halo_before(ts, offs["cc"]),
                  _halo_before(ts, offs["cx"]), _whole((3, BRANCH_W)), row(0), _halo_after(ts, 0, n_tiles),
                  _halo_after(ts, offs["cb"], n_tiles), _halo_after(ts, offs["cz"], n_tiles)],
        out_specs=[row(0), row(0), row(0), row(0), _whole((3, BRANCH_W))],
        out_shape=[jax.ShapeDtypeStruct((S, BRANCH_W), BF16)] * 4 + [jax.ShapeDtypeStruct((3, BRANCH_W), F32)],
        scratch_shapes=[pltpu.VMEM((HALO + ts, BRANCH_W), F32), pltpu.VMEM((ts + HALO, BRANCH_W), F32)], name=name,
        compiler_params=_params(("arbitrary",)),
    )(proj, proj, proj, proj, proj, proj, conv_w, dy, dy, proj, proj)


def _xattn_head(xq, kraw, v, qg, kg):
    qn, rq = _rms(xq, qg, XATTN_HD)
    kn, rk = _rms(kraw, kg, XATTN_HD)
    s = _dot_nt(qn, kn) * (XATTN_HD ** -0.5)
    e = jnp.exp(s - jnp.max(s, axis=-1, keepdims=True))
    p = e / jnp.sum(e, axis=-1, keepdims=True)
    return qn, rq, kn, rk, p, _dot(p, v)


def _xattn_fwd(proj, offs, mem_kv, qg, kg, name):
    S, M = proj.shape[0], mem_kv.shape[0]
    tm = _pick(S, 512, 16)

    def body(xq_ref, xz_ref, kv_ref, qg_ref, kg_ref, y_ref):
        for h in range(XATTN_HEADS):
            sl = slice(h * XATTN_HD, (h + 1) * XATTN_HD)
            vs = slice(BRANCH_W + h * XATTN_HD, BRANCH_W + (h + 1) * XATTN_HD)
            o = _xattn_head(xq_ref[:, sl], kv_ref[:, sl], kv_ref[:, vs], qg_ref[...], kg_ref[...])[-1]
            y_ref[:, sl] = (o * _silu(xz_ref[:, sl])).astype(BF16)

    return pl.pallas_call(
        body, grid=(S // tm,),
        in_specs=[_col(tm, BRANCH_W, offs["xq"]), _col(tm, BRANCH_W, offs["xz"]), _whole((M, 2 * BRANCH_W)),
                  _whole((1, XATTN_HD)), _whole((1, XATTN_HD))],
        out_specs=_col(tm, BRANCH_W, 0), out_shape=jax.ShapeDtypeStruct((S, BRANCH_W), BF16), name=name,
        compiler_params=_params(("parallel",)),
    )(proj, proj, mem_kv, qg, kg)


def _xattn_bwd(proj, offs, mem_kv, qg, kg, dy, name):
    S, M = proj.shape[0], mem_kv.shape[0]
    tm = _pick(S, 256, 16)
    n_tiles = S // tm

    def body(xq_ref, xz_ref, kv_ref, qg_ref, kg_ref, dy_ref, dxq_ref, dxz_ref, dkv_ref, dqg_ref, dkg_ref, dkn_acc):
        i = pl.program_id(0)

        @pl.when(i == 0)
        def _():
            dkv_ref[...] = jnp.zeros_like(dkv_ref)
            dqg_ref[...] = jnp.zeros_like(dqg_ref)
            dkg_ref[...] = jnp.zeros_like(dkg_ref)
            dkn_acc[...] = jnp.zeros_like(dkn_acc)

        qg_v, kg_v = qg_ref[...], kg_ref[...]
        for h in range(XATTN_HEADS):
            sl = slice(h * XATTN_HD, (h + 1) * XATTN_HD)
            vs = slice(BRANCH_W + h * XATTN_HD, BRANCH_W + (h + 1) * XATTN_HD)
            xq, v = xq_ref[:, sl], kv_ref[:, vs]
            qn, rq, kn, rk, p, o = _xattn_head(xq, kv_ref[:, sl], v, qg_v, kg_v)
            sz, dsz = _silu_and_grad(xz_ref[:, sl])
            dyv = dy_ref[:, sl]
            dxz_ref[:, sl] = (dyv * o * dsz).astype(BF16)
            do = dyv * sz
            dkv_ref[:, vs] += _dot_tn(p, do)
            dp = _dot_nt(do, v)
            ds = p * (dp - jnp.sum(dp * p, axis=-1, keepdims=True)) * (XATTN_HD ** -0.5)
            dqn = _dot(ds, kn)
            dkn_acc[:, sl] += _dot_tn(ds, qn)
            dxq, dqg = _rms_bwd(xq, qg_v, rq, dqn, XATTN_HD)
            dxq_ref[:, sl] = dxq.astype(BF16)
            dqg_ref[...] += dqg

        @pl.when(i == n_tiles - 1)
        def _():
            for h in range(XATTN_HEADS):
                sl = slice(h * XATTN_HD, (h + 1) * XATTN_HD)
                kraw = kv_ref[:, sl]
                _, rk = _rms(kraw, kg_v, XATTN_HD)
                dk, dkg = _rms_bwd(kraw, kg_v, rk, dkn_acc[:, sl], XATTN_HD)
                dkv_ref[:, sl] = dk
                dkg_ref[...] += dkg

    return pl.pallas_call(
        body, grid=(n_tiles,),
        in_specs=[_col(tm, BRANCH_W, offs["xq"]), _col(tm, BRANCH_W, offs["xz"]), _whole((M, 2 * BRANCH_W)),
                  _whole((1, XATTN_HD)), _whole((1, XATTN_HD)), _col(tm, BRANCH_W, 0)],
        out_specs=[_col(tm, BRANCH_W, 0), _col(tm, BRANCH_W, 0), _whole((M, 2 * BRANCH_W)), _whole((1, XATTN_HD)), _whole((1, XATTN_HD))],
        out_shape=[jax.ShapeDtypeStruct((S, BRANCH_W), BF16), jax.ShapeDtypeStruct((S, BRANCH_W), BF16),
                   jax.ShapeDtypeStruct((M, 2 * BRANCH_W), F32), jax.ShapeDtypeStruct((1, XATTN_HD), F32), jax.ShapeDtypeStruct((1, XATTN_HD), F32)],
        scratch_shapes=[pltpu.VMEM((M, BRANCH_W), F32)], name=name, compiler_params=_params(("arbitrary",)),
    )(proj, proj, mem_kv, qg, kg, dy)


def _rope(c, tc, ta, tb):
    return c * tc + pltpu.roll(c, 96, 1) * ta + pltpu.roll(c, 32, 1) * tb


def _rope_t(d, tc, ta, tb):
    return d * tc + pltpu.roll(d * ta, 32, 1) + pltpu.roll(d * tb, 96, 1)


def _mla_qk_fwd(qp, kvp, proj, offs, tabs, qg, kg, name):
    S = qp.shape[0]
    tm = _pick(S, 256, 16)
    W = MLA_HEADS * QK_PAD

    def body(q_ref, kv_ref, kr_ref, tc_ref, ta_ref, tb_ref, qg_ref, kg_ref, qo_ref, ko_ref, vo_ref):
        tc, ta, tb = tc_ref[...], ta_ref[...], tb_ref[...]
        qg_v, kg_v = qg_ref[...], kg_ref[...]
        kr = kr_ref[...]
        kr_ss = jnp.sum(kr * kr, axis=-1, keepdims=True)
        for h in range(MLA_HEADS):
            sl = slice(h * QK_PAD, (h + 1) * QK_PAD)
            qn, _ = _rms(q_ref[:, sl], qg_v, QK_HEAD)
            qo_ref[:, sl] = jnp.concatenate([qn[:, :QK_NOPE], _rope(qn[:, QK_NOPE:], tc, ta, tb)], axis=1).astype(BF16)
            kn = kv_ref[:, h * QK_NOPE:(h + 1) * QK_NOPE]
            r = lax.rsqrt((jnp.sum(kn * kn, axis=-1, keepdims=True) + kr_ss) / QK_HEAD + EPS)
            ko_ref[:, sl] = jnp.concatenate([kn * r * kg_v[:, :QK_NOPE], _rope(kr * r * kg_v[:, QK_NOPE:], tc, ta, tb)], axis=1).astype(BF16)
        vo_ref[...] = kv_ref[:, MLA_HEADS * QK_NOPE:].astype(BF16)

    return pl.pallas_call(
        body, grid=(S // tm,),
        in_specs=[_col(tm, W, 0), _col(tm, W, 0), _col(tm, 128, offs["kr"]), _col(tm, 128, 0), _col(tm, 128, 0), _col(tm, 128, 0),
                  _whole((1, QK_PAD)), _whole((1, QK_PAD))],
        out_specs=[_col(tm, W, 0), _col(tm, W, 0), _col(tm, BRANCH_W, 0)],
        out_shape=[jax.ShapeDtypeStruct((S, W), BF16), jax.ShapeDtypeStruct((S, W), BF16), jax.ShapeDtypeStruct((S, BRANCH_W), BF16)],
        name=name, compiler_params=_params(("parallel",)),
    )(qp, kvp, proj, *tabs, qg, kg)


def _mla_qk_bwd(qp, kvp, proj, offs, tabs, qg, kg, dQ, dK, dV, name):
    S = qp.shape[0]
    tm = _pick(S, 256, 16)
    W = MLA_HEADS * QK_PAD

    def body(q_ref, kv_ref, kr_ref, tc_ref, ta_ref, tb_ref, qg_ref, kg_ref, dQ_ref, dK_ref, dV_ref,
             dq_ref, dkv_ref, dkr_ref, dqg_ref, dkg_ref):
        @pl.when(pl.program_id(0) == 0)
        def _():
            dqg_ref[...] = jnp.zeros_like(dqg_ref)
            dkg_ref[...] = jnp.zeros_like(dkg_ref)

        tc, ta, tb = tc_ref[...], ta_ref[...], tb_ref[...]
        qg_v, kg_v = qg_ref[...], kg_ref[...]
        kg_n, kg_r = kg_v[:, :QK_NOPE], kg_v[:, QK_NOPE:]
        kr = kr_ref[...]
        kr_ss = jnp.sum(kr * kr, axis=-1, keepdims=True)
        dkr = jnp.zeros_like(kr)
        dkg_n = jnp.zeros((1, QK_NOPE), F32)
        dkg_r = jnp.zeros((1, QK_NOPE), F32)
        for h in range(MLA_HEADS):
            sl = slice(h * QK_PAD, (h + 1) * QK_PAD)
            qv = q_ref[:, sl]
            _, rq = _rms(qv, qg_v, QK_HEAD)
            dQv = dQ_ref[:, sl]
            dqn = jnp.concatenate([dQv[:, :QK_NOPE], _rope_t(dQv[:, QK_NOPE:], tc, ta, tb)], axis=1)
            dq, dqg = _rms_bwd(qv, qg_v, rq, dqn, QK_HEAD)
            dq_ref[:, sl] = dq.astype(BF16)
            dqg_ref[...] += dqg
            ns = slice(h * QK_NOPE, (h + 1) * QK_NOPE)
            kn = kv_ref[:, ns]
            r = lax.rsqrt((jnp.sum(kn * kn, axis=-1, keepdims=True) + kr_ss) / QK_HEAD + EPS)
            dKv = dK_ref[:, sl]
            dyn = dKv[:, :QK_NOPE]
            dyr = _rope_t(dKv[:, QK_NOPE:], tc, ta, tb)
            proj_s = (jnp.sum(dyn * kg_n * kn, axis=-1, keepdims=True) + jnp.sum(dyr * kg_r * kr, axis=-1, keepdims=True)) / QK_HEAD
            r3 = r * r * r
            dkv_ref[:, ns] = (r * dyn * kg_n - kn * r3 * proj_s).astype(BF16)
            dkr = dkr + (r * dyr * kg_r - kr * r3 * proj_s)
            dkg_n = dkg_n + jnp.sum(dyn * kn * r, axis=0, keepdims=True)
            dkg_r = dkg_r + jnp.sum(dyr * kr * r, axis=0, keepdims=True)
        dkg_ref[...] += jnp.concatenate([dkg_n, dkg_r], axis=1)
        dkr_ref[...] = dkr.astype(BF16)
        dkv_ref[:, MLA_HEADS * QK_NOPE:] = dV_ref[...].astype(BF16)

    return pl.pallas_call(
        body, grid=(S // tm,),
        in_specs=[_col(tm, W, 0), _col(tm, W, 0), _col(tm, 128, offs["kr"]), _col(tm, 128, 0), _col(tm, 128, 0), _col(tm, 128, 0),
                  _whole((1, QK_PAD)), _whole((1, QK_PAD)), _col(tm, W, 0), _col(tm, W, 0), _col(tm, BRANCH_W, 0)],
        out_specs=[_col(tm, W, 0), _col(tm, W, 0), _col(tm, 128, 0), _whole((1, QK_PAD)), _whole((1, QK_PAD))],
        out_shape=[jax.ShapeDtypeStruct((S, W), BF16), jax.ShapeDtypeStruct((S, W), BF16), jax.ShapeDtypeStruct((S, 128), BF16),
                   jax.ShapeDtypeStruct((1, QK_PAD), F32), jax.ShapeDtypeStruct((1, QK_PAD), F32)],
        name=name, compiler_params=_params(("arbitrary",)),
    )(qp, kvp, proj, *tabs, qg, kg, dQ, dK, dV)


def _causal_scores(q, k, qi, ki, tq, tk):
    s = _dot_nt(q, k) * (QK_HEAD ** -0.5)
    rows = qi * tq + lax.broadcasted_iota(jnp.int32, (tq, tk), 0)
    cols = ki * tk + lax.broadcasted_iota(jnp.int32, (tq, tk), 1)
    return s, cols <= rows


def _flash_fwd(Q, K, V, proj, offs, name):
    S = Q.shape[0]
    tq = _pick(S, 512, 128)
    nq = S // tq
    mz_cb = offs["mz"] // V_HEAD

    def body(q_ref, k_ref, v_ref, mz_ref, o_ref, y_ref, lse_ref, m_scr, l_scr, acc_scr):
        qi, ki = pl.program_id(1), pl.program_id(2)

        @pl.when(ki == 0)
        def _():
            m_scr[...] = jnp.full_like(m_scr, NEG)
            l_scr[...] = jnp.zeros_like(l_scr)
            acc_scr[...] = jnp.zeros_like(acc_scr)

        @pl.when(ki <= qi)
        def _():
            s, mask = _causal_scores(q_ref[...], k_ref[...], qi, ki, tq, tq)
            s = jnp.where(mask, s, NEG)
            m_old = m_scr[...]
            m_new = jnp.maximum(m_old, jnp.max(s, axis=-1, keepdims=True))
            alpha = jnp.exp(m_old - m_new)
            p = jnp.exp(s - m_new)
            l_scr[...] = alpha * l_scr[...] + jnp.sum(p, axis=-1, keepdims=True)
            acc_scr[...] = alpha * acc_scr[...] + _dot(p, v_ref[...])
            m_scr[...] = m_new

        @pl.when(ki == nq - 1)
        def _():
            o = acc_scr[...] / l_scr[...]
            o_ref[...] = o
            y_ref[...] = (o * _silu(mz_ref[...])).astype(BF16)
            lse_ref[0] = m_scr[...] + jnp.log(l_scr[...])

    return pl.pallas_call(
        body, grid=(MLA_HEADS, nq, nq),
        in_specs=[pl.BlockSpec((tq, QK_PAD), lambda h, i, j: (i, h)),
                  pl.BlockSpec((tq, QK_PAD), lambda h, i, j: (jnp.minimum(j, i), h)),
                  pl.BlockSpec((tq, V_HEAD), lambda h, i, j: (jnp.minimum(j, i), h)),
                  pl.BlockSpec((tq, V_HEAD), lambda h, i, j: (i, mz_cb + h))],
        out_specs=[pl.BlockSpec((tq, V_HEAD), lambda h, i, j: (i, h)), pl.BlockSpec((tq, V_HEAD), lambda h, i, j: (i, h)),
                   pl.BlockSpec((1, tq, 1), lambda h, i, j: (h, i, 0))],
        out_shape=[jax.ShapeDtypeStruct((S, BRANCH_W), F32), jax.ShapeDtypeStruct((S, BRANCH_W), BF16),
                   jax.ShapeDtypeStruct((MLA_HEADS, S, 1), F32)],
        scratch_shapes=[pltpu.VMEM((tq, 1), F32), pltpu.VMEM((tq, 1), F32), pltpu.VMEM((tq, V_HEAD), F32)], name=name,
        compiler_params=_params(("parallel", "parallel", "arbitrary")),
    )(Q, K, V, proj)


def _mla_gate_bwd(o, proj, offs, dy, name):
    S = o.shape[0]
    tm = _pick(S, 512, 16)

    def body(o_ref, mz_ref, dy_ref, do_ref, dmz_ref):
        sz, dsz = _silu_and_grad(mz_ref[...])
        dyv = dy_ref[...]
        do_ref[...] = (dyv * sz).astype(BF16)
        dmz_ref[...] = (dyv * o_ref[...] * dsz).astype(BF16)

    return pl.pallas_call(
        body, grid=(S // tm,), in_specs=[_col(tm, BRANCH_W, 0), _col(tm, BRANCH_W, offs["mz"]), _col(tm, BRANCH_W, 0)],
        out_specs=[_col(tm, BRANCH_W, 0), _col(tm, BRANCH_W, 0)],
        out_shape=[jax.ShapeDtypeStruct((S, BRANCH_W), BF16), jax.ShapeDtypeStruct((S, BRANCH_W), BF16)], name=name,
        compiler_params=_params(("parallel",)),
    )(o, proj, dy)


def _flash_bwd(Q, K, V, o, lse, do, name):
    S = Q.shape[0]
    tq = _pick(S, 512, 128)
    nq = S // tq

    def body(q_ref, k_ref, v_ref, o_ref, lse_ref, do_ref, dq_ref, dk_ref, dv_ref, dk_acc, dv_acc):
        ki, qi = pl.program_id(1), pl.program_id(2)

        @pl.when((ki == 0) & (qi == 0))
        def _():
            dq_ref[...] = jnp.zeros_like(dq_ref)

        @pl.when(qi == 0)
        def _():
            dk_acc[...] = jnp.zeros_like(dk_acc)
            dv_acc[...] = jnp.zeros_like(dv_acc)

        @pl.when(qi >= ki)
        def _():
            q, k, dov = q_ref[...], k_ref[...], do_ref[...]
            s, mask = _causal_scores(q, k, qi, ki, tq, tq)
            p = jnp.exp(jnp.where(mask, s, NEG) - lse_ref[0])
            dv_acc[...] += _dot_tn(p, dov)
            dp = _dot_nt(dov, v_ref[...])
            delta = jnp.sum(dov.astype(F32) * o_ref[...], axis=-1, keepdims=True)
            ds = p * (dp - delta) * (QK_HEAD ** -0.5)
            dk_acc[...] += _dot_tn(ds, q)
            rows = pl.ds(pl.multiple_of(qi * tq, tq), tq)
            dq_ref[rows, :] += _dot(ds, k)

        @pl.when(qi == nq - 1)
        def _():
            dk_ref[...] = dk_acc[...]
            dv_ref[...] = dv_acc[...]

    qrow = lambda h, j, i: (jnp.maximum(i, j), h)
    return pl.pallas_call(
        body, grid=(MLA_HEADS, nq, nq),
        in_specs=[pl.BlockSpec((tq, QK_PAD), qrow), pl.BlockSpec((tq, QK_PAD), lambda h, j, i: (j, h)),
                  pl.BlockSpec((tq, V_HEAD), lambda h, j, i: (j, h)), pl.BlockSpec((tq, V_HEAD), qrow),
                  pl.BlockSpec((1, tq, 1), lambda h, j, i: (h, jnp.maximum(i, j), 0)), pl.BlockSpec((tq, V_HEAD), qrow)],
        out_specs=[pl.BlockSpec((S, QK_PAD), lambda h, j, i: (0, h)), pl.BlockSpec((tq, QK_PAD), lambda h, j, i: (j, h)),
                   pl.BlockSpec((tq, V_HEAD), lambda h, j, i: (j, h))],
        out_shape=[jax.ShapeDtypeStruct((S, MLA_HEADS * QK_PAD), F32), jax.ShapeDtypeStruct((S, MLA_HEADS * QK_PAD), F32),
                   jax.ShapeDtypeStruct((S, BRANCH_W), F32)],
        scratch_shapes=[pltpu.VMEM((tq, QK_PAD), F32), pltpu.VMEM((tq, V_HEAD), F32)], name=name,
        compiler_params=_params(("parallel", "arbitrary", "arbitrary")),
    )(Q, K, V, o, lse, do)


def _merge_specs(S, D, offs, tm, tn):
    g_cb = offs["g"] // tn
    nj = D // tn
    y_specs = [pl.BlockSpec((tm, BRANCH_W), lambda j, i: (i, 0)) for _ in range(4)]
    w_spec = pl.BlockSpec((4, BRANCH_W, tn), lambda j, i: (0, 0, j))
    g_specs = [pl.BlockSpec((tm, tn), functools.partial(lambda j, i, b: (i, g_cb + b * nj + j), b=b)) for b in range(4)]
    bias_spec = pl.BlockSpec((4, tn), lambda j, i: (0, j))
    return y_specs, w_spec, g_specs, bias_spec


def _merge_fwd(ys, wb, proj, offs, gate_b, name):
    S, D = ys[0].shape[0], wb.shape[2]
    tm, tn = _pick(S, 512, 16), _pick(D, 512, 128)
    y_specs, w_spec, g_specs, bias_spec = _merge_specs(S, D, offs, tm, tn)

    def body(y0, y1, y2, y3, w_ref, g0, g1, g2, g3, b_ref, o_ref):
        acc = None
        for b, (y_ref, g_ref) in enumerate(zip((y0, y1, y2, y3), (g0, g1, g2, g3))):
            gate = 1.0 / (1.0 + jnp.exp(-(g_ref[...] + b_ref[b:b + 1, :])))
            term = gate * _dot(y_ref[...], w_ref[b])
            acc = term if acc is None else acc + term
        o_ref[...] = acc.astype(BF16)

    return pl.pallas_call(
        body, grid=(D // tn, S // tm), in_specs=y_specs + [w_spec] + g_specs + [bias_spec],
        out_specs=pl.BlockSpec((tm, tn), lambda j, i: (i, j)), out_shape=jax.ShapeDtypeStruct((S, D), BF16), name=name,
        compiler_params=_params(("parallel", "parallel")),
    )(*ys, wb, proj, proj, proj, proj, gate_b)


def _merge_bwd(ys, wb, proj, offs, gate_b, dmerged, name):
    S, D = ys[0].shape[0], wb.shape[2]
    tm, tn = _pick(S, 512, 16), _pick(D, 512, 128)
    nj = D // tn
    y_specs, w_spec, g_specs, bias_spec = _merge_specs(S, D, offs, tm, tn)

    def body(y0, y1, y2, y3, w_ref, g0, g1, g2, g3, b_ref, dm_ref, dt_ref, dg_ref, db_ref):
        @pl.when(pl.program_id(1) == 0)
        def _():
            db_ref[...] = jnp.zeros_like(db_ref)

        dm = dm_ref[...]
        for b, (y_ref, g_ref) in enumerate(zip((y0, y1, y2, y3), (g0, g1, g2, g3))):
            gate = 1.0 / (1.0 + jnp.exp(-(g_ref[...] + b_ref[b:b + 1, :])))
            t = _dot(y_ref[...], w_ref[b])
            dt_ref[b] = (dm * gate).astype(BF16)
            dgp = dm * t * gate * (1.0 - gate)
            dg_ref[b] = dgp.astype(BF16)
            db_ref[b:b + 1, :] += jnp.sum(dgp, axis=0, keepdims=True)

    stacked = pl.BlockSpec((4, tm, tn), lambda j, i: (0, i, j))
    return pl.pallas_call(
        body, grid=(nj, S // tm),
        in_specs=y_specs + [w_spec] + g_specs + [bias_spec, pl.BlockSpec((tm, tn), lambda j, i: (i, j))],
        out_specs=[stacked, stacked, pl.BlockSpec((4, tn), lambda j, i: (0, j))],
        out_shape=[jax.ShapeDtypeStruct((4, S, D), BF16), jax.ShapeDtypeStruct((4, S, D), BF16), jax.ShapeDtypeStruct((4, D), F32)],
        name=name, compiler_params=_params(("parallel", "arbitrary")),
    )(*ys, wb, proj, proj, proj, proj, gate_b, dmerged)


def _peer(r):
    x, y, c = lax.axis_index("x"), lax.axis_index("y"), lax.axis_index("c")
    px = 1 - x if (r >> 2) & 1 else x
    py = 1 - y if (r >> 1) & 1 else y
    pc = 1 - c if r & 1 else c
    return (px, py, pc), 4 * px + 2 * py + pc


def _exchange(arrs, scatter, name):
    n = len(arrs)
    outs = [jax.ShapeDtypeStruct(a.shape if scatter else (N_DEV,) + a.shape, a.dtype) for a in arrs]

    def body(*refs):
        ins, dsts = refs[:n], refs[n:2 * n]
        send_sems, recv_sems, local_sems = refs[2 * n:]
        _, me = _peer(0)
        started = []
        for a in range(n):
            src = ins[a].at[me] if scatter else ins[a]
            cp = pltpu.make_async_copy(src, dsts[a].at[me], local_sems.at[a])
            cp.start()
            started.append(cp)
        remote = []
        for a in range(n):
            for r in range(1, N_DEV):
                dev, idx = _peer(r)
                src = ins[a].at[idx] if scatter else ins[a]
                cp = pltpu.make_async_remote_copy(src_ref=src, dst_ref=dsts[a].at[me], send_sem=send_sems.at[a, r - 1],
                                                  recv_sem=recv_sems.at[a, r - 1], device_id=dev, device_id_type=pl.DeviceIdType.MESH)
                cp.start()
                remote.append(cp)
        for cp in remote:
            cp.wait()
        for cp in started:
            cp.wait()

    hbm = pl.BlockSpec(memory_space=pl.ANY)
    res = pl.pallas_call(
        body, in_specs=[hbm] * n, out_specs=[hbm] * n, out_shape=outs,
        scratch_shapes=[pltpu.SemaphoreType.DMA((n, N_DEV - 1)), pltpu.SemaphoreType.DMA((n, N_DEV - 1)), pltpu.SemaphoreType.DMA((n,))],
        name=name,
    )(*arrs)
    return list(res)


def _adamw(parts, w, m, v, name):
    R, C = w.shape
    tr = _pick(R, max(8, ((1 << 18) // C) // 8 * 8), 8)

    def body(p_ref, w_ref, m_ref, v_ref, g_ref, d_ref, nm_ref, nv_ref):
        g = p_ref[0]
        for s in range(1, N_DEV):
            g = g + p_ref[s]
        mn = ADAM_B1 * m_ref[...] + (1.0 - ADAM_B1) * g
        vn = ADAM_B2 * v_ref[...] + (1.0 - ADAM_B2) * (g * g)
        m_hat = mn / (1.0 - ADAM_B1 ** ADAM_STEP)
        v_hat = vn / (1.0 - ADAM_B2 ** ADAM_STEP)
        g_ref[...] = g
        d_ref[...] = -ADAM_LR * (m_hat / (jnp.sqrt(v_hat) + ADAM_EPS) + ADAM_WD * w_ref[...])
        nm_ref[...] = mn
        nv_ref[...] = vn

    blk = pl.BlockSpec((tr, C), lambda i: (i, 0))
    return pl.pallas_call(
        body, grid=(R // tr,), in_specs=[pl.BlockSpec((N_DEV, tr, C), lambda i: (0, i, 0)), blk, blk, blk],
        out_specs=[blk] * 4, out_shape=[jax.ShapeDtypeStruct((R, C), F32)] * 4, name=name,
        compiler_params=_params(("parallel",)),
    )(parts, w, m, v)


def _offsets(L, D):
    o = {"pv": 0, "pz": BRANCH_W, "cq": 2 * BRANCH_W, "ckv": 2 * BRANCH_W + L, "mz": 2 * BRANCH_W + 2 * L}
    for prev, nxt in (("mz", "cb"), ("cb", "cc"), ("cc", "cx"), ("cx", "cz"), ("cz", "xq"), ("xq", "xz"), ("xz", "g")):
        o[nxt] = o[prev] + BRANCH_W
    o["kr"] = o["g"] + 4 * D
    o["n"] = o["kr"] + 128
    return o


def _unshard(gathered, axis):
    moved = jnp.moveaxis(gathered, 0, axis)
    shape = list(moved.shape)
    shape[axis:axis + 2] = [shape[axis] * shape[axis + 1]]
    return moved.reshape(shape)


def _reshard(full, axis):
    shape = list(full.shape)
    shape[axis:axis + 1] = [N_DEV, shape[axis] // N_DEV]
    return jnp.moveaxis(full.reshape(shape), axis, 0)


def _pad_heads(g, width):
    return jnp.pad(g, ((0, 0), (0, width - g.shape[1])))


def kernel(x, mem, positions, norm_g, w_in, gate_b, pool_w, pool_scale, q_a_norm_g, kv_a_norm_g, w_uq, w_ukv, mla_q_norm_g, mla_k_norm_g, conv_w, mem_norm_g, w_mem_kv, xattn_q_norm_g, xattn_k_norm_g, w_branch, w_out, loss_target, m_norm_g, m_w_in, m_gate_b, m_pool_w, m_pool_scale, m_q_a_norm_g, m_kv_a_norm_g, m_w_uq, m_w_ukv, m_mla_q_norm_g, m_mla_k_norm_g, m_conv_w, m_mem_norm_g, m_w_mem_kv, m_xattn_q_norm_g, m_xattn_k_norm_g, m_w_branch, m_w_out, v_norm_g, v_w_in, v_gate_b, v_pool_w, v_pool_scale, v_q_a_norm_g, v_kv_a_norm_g, v_w_uq, v_w_ukv, v_mla_q_norm_g, v_mla_k_norm_g, v_conv_w, v_mem_norm_g, v_w_mem_kv, v_xattn_q_norm_g, v_xattn_k_norm_g, v_w_branch, v_w_out):
    depth = norm_g.shape[0]
    S, D = x.shape[1], x.shape[2]
    L = w_uq.shape[1]
    offs = _offsets(L, D)
    seg_a = 2 * BRANCH_W + 2 * L
    xs = x.reshape(S, D)
    mems = mem.reshape(mem.shape[1], D)

    sharded = dict(w_in=w_in, pool_w=pool_w, w_uq=w_uq, w_ukv=w_ukv, conv_w=conv_w, w_mem_kv=w_mem_kv, w_branch=w_branch, w_out=w_out)
    shard_axis = dict(w_in=2, pool_w=2, w_uq=2, w_ukv=2, conv_w=2, w_mem_kv=1, w_branch=3, w_out=1)
    keys = list(shard_axis)
    payload = [sharded[k] if k == "conv_w" else sharded[k].astype(BF16) for k in keys]
    gathered = {k: _unshard(g, shard_axis[k]) for k, g in zip(keys, _exchange(payload, False, "gather_weights"))}
    conv_full = gathered["conv_w"]
    w_in_f = gathered["w_in"]
    w_in_p = jnp.concatenate([w_in_f[..., :seg_a], w_in_f[..., seg_a + QK_ROPE:], w_in_f[..., seg_a:seg_a + QK_ROPE],
                              jnp.zeros(w_in_f.shape[:2] + (128 - QK_ROPE,), BF16)], axis=-1)
    pool_w_f = gathered["pool_w"]
    w_uq_f = gathered["w_uq"].reshape(depth, L, MLA_HEADS, QK_HEAD)
    w_uq_p = jnp.pad(w_uq_f, ((0, 0), (0, 0), (0, 0), (0, QK_PAD - QK_HEAD))).reshape(depth, L, MLA_HEADS * QK_PAD)
    w_ukv_f = gathered["w_ukv"].reshape(depth, L, MLA_HEADS, QK_NOPE + V_HEAD)
    w_ukv_p = jnp.concatenate([w_ukv_f[..., :QK_NOPE].reshape(depth, L, -1), w_ukv_f[..., QK_NOPE:].reshape(depth, L, -1)], axis=-1)
    w_mem_f = gathered["w_mem_kv"]
    w_br_f = gathered["w_branch"]
    w_out_f = gathered["w_out"]

    inv = ROPE_THETA ** (-jnp.arange(0, QK_ROPE, 2, dtype=F32) / QK_ROPE)
    ang = positions.reshape(S).astype(F32)[:, None] * inv
    cos, sin, z = jnp.cos(ang), jnp.sin(ang), jnp.zeros_like(ang)
    tabs = (jnp.concatenate([cos, cos, z, z], axis=1), jnp.concatenate([-sin, z, z, z], axis=1), jnp.concatenate([z, sin, z, z], axis=1))

    row = lambda a, l: a[l][None, :]
    saved = []
    cur = xs
    for l in range(depth):
        h = _rms_rows(cur, D, 0, row(norm_g, l), f"norm_fwd{l}")
        proj = _matmul(h, w_in_p[l], "nn", F32, f"proj_fwd{l}", tm=1024, tn=640, tk=2048)
        y_pool = _pool_fwd(proj, offs, pool_w_f[l], row(pool_scale, l), f"pool_fwd{l}")
        cqn = _rms_rows(proj, L, offs["cq"], row(q_a_norm_g, l), f"cq_norm{l}")
        ckvn = _rms_rows(proj, L, offs["ckv"], row(kv_a_norm_g, l), f"ckv_norm{l}")
        qp = _matmul(cqn, w_uq_p[l], "nn", F32, f"uq_fwd{l}")
        kvp = _matmul(ckvn, w_ukv_p[l], "nn", F32, f"ukv_fwd{l}")
        qg = _pad_heads(row(mla_q_norm_g, l), QK_PAD)
        kg = _pad_heads(row(mla_k_norm_g, l), QK_PAD)
        Q, K, V = _mla_qk_fwd(qp, kvp, proj, offs, tabs, qg, kg, f"mla_qk_fwd{l}")
        o, y_mla, lse = _flash_fwd(Q, K, V, proj, offs, f"flash_fwd{l}")
        y_conv = _conv_fwd(proj, offs, conv_full[l], f"conv_fwd{l}")
        memn = _rms_rows(mems, D, 0, row(mem_norm_g, l), f"mem_norm{l}")
        mem_kv = _matmul(memn, w_mem_f[l], "nn", F32, f"mem_kv{l}")
        y_mem = _xattn_fwd(proj, offs, mem_kv, row(xattn_q_norm_g, l), row(xattn_k_norm_g, l), f"xattn_fwd{l}")
        ys = (y_pool, y_mla, y_conv, y_mem)
        gb = gate_b[l].reshape(4, D)
        merged = _merge_fwd(ys, w_br_f[l], proj, offs, gb, f"merge_fwd{l}")
        nxt = _matmul(merged, w_out_f[l], "nn", F32, f"out_fwd{l}", res=cur)
        saved.append(dict(x=cur, h=h, proj=proj, cqn=cqn, ckvn=ckvn, qp=qp, kvp=kvp, qg=qg, kg=kg, Q=Q, K=K, V=V, o=o, lse=lse,
                          memn=memn, mem_kv=mem_kv, ys=ys, gb=gb, merged=merged))
        cur = nxt

    gx, loss_acc = _loss_rows(cur, loss_target.reshape(S, D), "loss")
    loss = lax.psum(0.5 * loss_acc[0, 0], MESH_AXES)

    grads = {k: [None] * depth for k in ("norm_g", "w_in", "gate_b", "pool_w", "pool_scale", "q_a_norm_g", "kv_a_norm_g", "w_uq", "w_ukv",
                                         "mla_q_norm_g", "mla_k_norm_g", "conv_w", "mem_norm_g", "w_mem_kv", "xattn_q_norm_g",
                                         "xattn_k_norm_g", "w_branch", "w_out")}
    for l in reversed(range(depth)):
        sv = saved[l]
        proj, ys = sv["proj"], sv["ys"]
        dmerged = _matmul(gx, w_out_f[l], "nt", F32, f"out_bwd_x{l}")
        grads["w_out"][l] = _matmul(sv["merged"], gx, "tn", F32, f"out_bwd_w{l}", tm=1024, tn=1024, tk=512)
        dt, dgp, dgb = _merge_bwd(ys, w_br_f[l], proj, offs, sv["gb"], dmerged, f"merge_bwd{l}")
        grads["gate_b"][l] = dgb.reshape(4 * D)
        dys, dwb = [], []
        for b in range(4):
            dys.append(_matmul(dt[b], w_br_f[l, b], "nt", F32, f"branch_bwd_x{l}_{b}"))
            dwb.append(_matmul(ys[b], dt[b], "tn", F32, f"branch_bwd_w{l}_{b}", tm=1024, tn=1024, tk=512))
        grads["w_branch"][l] = jnp.stack(dwb)
        d_pz, e, d_pw, d_ps = _pool_bwd_a(proj, offs, pool_w_f[l], row(pool_scale, l), dys[0], f"pool_bwd_a{l}")
        d_pv = _pool_bwd_b(e, f"pool_bwd_b{l}")
        grads["pool_w"][l], grads["pool_scale"][l] = d_pw, d_ps[0]
        do, d_mz = _mla_gate_bwd(sv["o"], proj, offs, dys[1], f"mla_gate_bwd{l}")
        dQ, dK, dV = _flash_bwd(sv["Q"], sv["K"], sv["V"], sv["o"], sv["lse"], do, f"flash_bwd{l}")
        dqp, dkvp, d_kr, dqg, dkg = _mla_qk_bwd(sv["qp"], sv["kvp"], proj, offs, tabs, sv["qg"], sv["kg"], dQ, dK, dV, f"mla_qk_bwd{l}")
        grads["mla_q_norm_g"][l], grads["mla_k_norm_g"][l] = dqg[0, :QK_HEAD], dkg[0, :QK_HEAD]
        dcqn = _matmul(dqp, w_uq_p[l], "nt", F32, f"uq_bwd_x{l}")
        d_wuq = _matmul(sv["cqn"], dqp, "tn", F32, f"uq_bwd_w{l}", tk=512)
        dckvn = _matmul(dkvp, w_ukv_p[l], "nt", F32, f"ukv_bwd_x{l}")
        d_wukv = _matmul(sv["ckvn"], dkvp, "tn", F32, f"ukv_bwd_w{l}", tk=512)
        grads["w_uq"][l] = d_wuq.reshape(L, MLA_HEADS, QK_PAD)[..., :QK_HEAD].reshape(L, MLA_HEADS * QK_HEAD)
        grads["w_ukv"][l] = jnp.concatenate([d_wukv[:, :MLA_HEADS * QK_NOPE].reshape(L, MLA_HEADS, QK_NOPE),
                                             d_wukv[:, MLA_HEADS * QK_NOPE:].reshape(L, MLA_HEADS, V_HEAD)], axis=-1).reshape(L, -1)
        d_cq, dqa = _rms_rows_bwd(proj, L, offs["cq"], row(q_a_norm_g, l), dcqn, BF16, f"cq_norm_bwd{l}")
        d_ckv, dkva = _rms_rows_bwd(proj, L, offs["ckv"], row(kv_a_norm_g, l), dckvn, BF16, f"ckv_norm_bwd{l}")
        grads["q_a_norm_g"][l], grads["kv_a_norm_g"][l] = dqa[0], dkva[0]
        d_cb, d_cc, d_cx, d_cz, d_cw = _conv_bwd(proj, offs, conv_full[l], dys[2], f"conv_bwd{l}")
        grads["conv_w"][l] = d_cw
        d_xq, d_xz, d_memkv, dxqg, dxkg = _xattn_bwd(proj, offs, sv["mem_kv"], row(xattn_q_norm_g, l), row(xattn_k_norm_g, l), dys[3], f"xattn_bwd{l}")
        grads["xattn_q_norm_g"][l], grads["xattn_k_norm_g"][l] = dxqg[0], dxkg[0]
        grads["w_mem_kv"][l] = _matmul(sv["memn"], d_memkv, "tn", F32, f"mem_kv_bwd_w{l}")
        dmemn = _matmul(d_memkv, w_mem_f[l], "nt", F32, f"mem_kv_bwd_x{l}")
        grads["mem_norm_g"][l] = _norm_g_grad(mems, row(mem_norm_g, l), dmemn, f"mem_norm_bwd{l}")[0]
        dproj = jnp.concatenate([d_pv, d_pz, d_cq, d_ckv, d_mz, d_cb, d_cc, d_cx, d_cz, d_xq, d_xz, dgp[0], dgp[1], dgp[2], dgp[3], d_kr], axis=1)
        dh = _matmul(dproj, w_in_p[l], "nt", F32, f"proj_bwd_x{l}", tm=1024, tn=1024, tk=640)
        d_win_p = _matmul(sv["h"], dproj, "tn", F32, f"proj_bwd_w{l}", tm=2048, tn=640, tk=512)
        n_p = offs["n"]
        grads["w_in"][l] = jnp.concatenate([d_win_p[:, :seg_a], d_win_p[:, n_p - 128:n_p - 128 + QK_ROPE], d_win_p[:, seg_a:n_p - 128]], axis=1)
        gx, dng = _rms_rows_bwd(sv["x"], D, 0, row(norm_g, l), dh, F32, f"norm_bwd{l}", res=gx)
        grads["norm_g"][l] = dng[0]
    grad_x = gx.reshape(1, S, D)
    full = {k: jnp.stack(v) for k, v in grads.items()}

    moments = dict(w_in=(m_w_in, v_w_in), pool_w=(m_pool_w, v_pool_w), w_uq=(m_w_uq, v_w_uq), w_ukv=(m_w_ukv, v_w_ukv),
                   conv_w=(m_conv_w, v_conv_w), w_mem_kv=(m_w_mem_kv, v_w_mem_kv), w_branch=(m_w_branch, v_w_branch), w_out=(m_w_out, v_w_out))
    names_sh = list(shard_axis)
    staged = []
    for k in names_sh:
        w = sharded[k]
        staged.append(_reshard(full[k], shard_axis[k]).reshape(N_DEV, -1, w.shape[-1]))
    received = _exchange(staged, True, "scatter_grads")
    results = {}
    for k, parts in zip(names_sh, received):
        w = sharded[k]
        m, v = moments[k]
        flat = lambda a: a.reshape(-1, w.shape[-1])
        outs = _adamw(parts, flat(w), flat(m), flat(v), f"adamw_{k}")
        results[k] = [o.reshape(w.shape) for o in outs]

    small = dict(norm_g=(norm_g, m_norm_g, v_norm_g), gate_b=(gate_b, m_gate_b, v_gate_b), pool_scale=(pool_scale, m_pool_scale, v_pool_scale),
                 q_a_norm_g=(q_a_norm_g, m_q_a_norm_g, v_q_a_norm_g), kv_a_norm_g=(kv_a_norm_g, m_kv_a_norm_g, v_kv_a_norm_g),
                 mla_q_norm_g=(mla_q_norm_g, m_mla_q_norm_g, v_mla_q_norm_g), mla_k_norm_g=(mla_k_norm_g, m_mla_k_norm_g, v_mla_k_norm_g),
                 mem_norm_g=(mem_norm_g, m_mem_norm_g, v_mem_norm_g), xattn_q_norm_g=(xattn_q_norm_g, m_xattn_q_norm_g, v_xattn_q_norm_g),
                 xattn_k_norm_g=(xattn_k_norm_g, m_xattn_k_norm_g, v_xattn_k_norm_g))
    names_sm = list(small)
    sizes = [small[k][0].size for k in names_sm]
    total = sum(sizes)
    rows = -(-total // (8 * 128)) * 8

    def pack(arrs):
        flat = jnp.concatenate([a.reshape(-1) for a in arrs])
        return jnp.pad(flat, (0, rows * 128 - total)).reshape(rows, 128)

    g_small = _exchange([pack([full[k] for k in names_sm])], False, "gather_small_grads")[0]
    outs = _adamw(g_small, pack([small[k][0] for k in names_sm]), pack([small[k][1] for k in names_sm]),
                  pack([small[k][2] for k in names_sm]), "adamw_small")
    for kind, o in enumerate(outs):
        flat, pos = o.reshape(-1), 0
        for k, n in zip(names_sm, sizes):
            results.setdefault(k, [None] * 4)[kind] = flat[pos:pos + n].reshape(small[k][0].shape)
            pos += n

    order = ["norm_g", "w_in", "gate_b", "pool_w", "pool_scale", "q_a_norm_g", "kv_a_norm_g", "w_uq", "w_ukv", "mla_q_norm_g",
             "mla_k_norm_g", "conv_w", "mem_norm_g", "w_mem_kv", "xattn_q_norm_g", "xattn_k_norm_g", "w_branch", "w_out"]
    return (loss, grad_x, *[results[k][0] for k in order], *[results[k][1] for k in order],
            *[results[k][2] for k in order], *[results[k][3] for k in order])
```

```python
import functools

import jax
import jax.numpy as jnp
from jax import lax
from jax.experimental import pallas as pl
from jax.experimental.pallas import tpu as pltpu

F32 = jnp.float32
BF16 = jnp.bfloat16
MESH_AXES = ("x", "y", "c")
N_DEV = 8

BRANCH_W = 1024
POOL_WINDOWS = (2, 4, 8, 16)
POOL_GW = BRANCH_W // 4
MLA_HEADS = 8
QK_NOPE = 128
QK_ROPE = 64
QK_HEAD = QK_NOPE + QK_ROPE
QK_PAD = 256
V_HEAD = 128
XATTN_HEADS = 4
XATTN_HD = BRANCH_W // XATTN_HEADS
ROPE_THETA = 10000.0
EPS = 1e-6
ADAM_LR = 0.001
ADAM_B1 = 0.9
ADAM_B2 = 0.999
ADAM_EPS = 1e-08
ADAM_WD = 0.01
ADAM_STEP = 10
HALO = 16
NEG = -1e30
SCORE_SCALE_LOG2 = (QK_HEAD ** -0.5) * 1.4426950408889634
VMEM_LIMIT = 56 << 20


def _pick(n, pref, align):
    if n <= pref:
        return n
    t = (pref // align) * align
    while t >= align:
        if n % t == 0:
            return t
        t -= align
    return n


def _params(sem):
    return pltpu.CompilerParams(dimension_semantics=sem, vmem_limit_bytes=VMEM_LIMIT)


def _silu(z):
    return z * (1.0 / (1.0 + jnp.exp(-z)))


def _silu_and_grad(z):
    s = 1.0 / (1.0 + jnp.exp(-z))
    return z * s, s * (1.0 + z * (1.0 - s))


def _dot(a, b):
    return jnp.dot(a.astype(BF16), b.astype(BF16), preferred_element_type=F32)


def _dot_nt(a, b):
    return lax.dot_general(a.astype(BF16), b.astype(BF16), (((1,), (1,)), ((), ())), preferred_element_type=F32)


def _dot_tn(a, b):
    return lax.dot_general(a.astype(BF16), b.astype(BF16), (((0,), (0,)), ((), ())), preferred_element_type=F32)


def _rms(x, g, n):
    r = lax.rsqrt(jnp.sum(x * x, axis=-1, keepdims=True) / n + EPS)
    return x * r * g, r


def _rms_bwd(x, g, r, dy, n):
    dyg = dy * g
    dx = r * dyg - x * (r * r * r) * (jnp.sum(dyg * x, axis=-1, keepdims=True) / n)
    dg = jnp.sum(dy * x * r, axis=0, keepdims=True)
    return dx, dg


def _col(tm, w, off):
    assert off % w == 0
    cb = off // w
    return pl.BlockSpec((tm, w), lambda i: (i, cb))


def _whole(shape):
    nd = len(shape)
    return pl.BlockSpec(shape, lambda *_: (0,) * nd)


def _lead_spec(block, fn, idx):
    if not idx:
        return pl.BlockSpec(block, fn)
    return pl.BlockSpec((None,) * len(idx) + block, lambda i, j, k: tuple(idx) + fn(i, j, k))


def _matmul(a, b, mode, out_dtype, name, tm=1024, tn=1024, tk=2048, res=None, a_idx=(), b_idx=()):
    a_shape, b_shape = a.shape[len(a_idx):], b.shape[len(b_idx):]
    if mode == "nn":
        (M, K), N = a_shape, b_shape[1]
    elif mode == "nt":
        (M, K), N = a_shape, b_shape[0]
    else:
        (K, M), N = a_shape, b_shape[1]
    tm, tn, tk = _pick(M, tm, 128), _pick(N, tn, 128), _pick(K, tk, 128)
    nk = K // tk
    if mode == "nn":
        a_spec = _lead_spec((tm, tk), lambda i, j, k: (i, k), a_idx)
        b_spec = _lead_spec((tk, tn), lambda i, j, k: (k, j), b_idx)
        op = _dot
    elif mode == "nt":
        a_spec = _lead_spec((tm, tk), lambda i, j, k: (i, k), a_idx)
        b_spec = _lead_spec((tn, tk), lambda i, j, k: (j, k), b_idx)
        op = _dot_nt
    else:
        a_spec = _lead_spec((tk, tm), lambda i, j, k: (k, i), a_idx)
        b_spec = _lead_spec((tk, tn), lambda i, j, k: (k, j), b_idx)
        op = _dot_tn
    o_spec = pl.BlockSpec((tm, tn), lambda i, j, k: (i, j))
    has_res = res is not None

    def body(*refs):
        a_ref, b_ref = refs[0], refs[1]
        res_ref = refs[2] if has_res else None
        o_ref = refs[2 + has_res]
        part = op(a_ref[...], b_ref[...])

        def finish(acc):
            if has_res:
                acc = res_ref[...] + acc
            o_ref[...] = acc.astype(out_dtype)

        if nk == 1:
            finish(part)
        else:
            acc_ref = refs[3 + has_res]
            k = pl.program_id(2)

            @pl.when(k == 0)
            def _():
                acc_ref[...] = part

            @pl.when(k > 0)
            def _():
                acc_ref[...] += part

            @pl.when(k == nk - 1)
            def _():
                finish(acc_ref[...])

    ins = [a, b] + ([res] if has_res else [])
    in_specs = [a_spec, b_spec] + ([o_spec] if has_res else [])
    scratch = [pltpu.VMEM((tm, tn), F32)] if nk > 1 else []
    return pl.pallas_call(
        body, grid=(M // tm, N // tn, nk), in_specs=in_specs, out_specs=o_spec,
        out_shape=jax.ShapeDtypeStruct((M, N), out_dtype), scratch_shapes=scratch, name=name,
        compiler_params=_params(("parallel", "parallel", "arbitrary")),
    )(*ins)


def _rms_rows(x, width, off, g, name):
    S = x.shape[0]
    tm = _pick(S, 512, 16)

    def body(x_ref, g_ref, o_ref):
        y, _ = _rms(x_ref[...], g_ref[...], width)
        o_ref[...] = y.astype(BF16)

    return pl.pallas_call(
        body, grid=(S // tm,), in_specs=[_col(tm, width, off), _whole((1, width))], out_specs=_col(tm, width, 0),
        out_shape=jax.ShapeDtypeStruct((S, width), BF16), name=name, compiler_params=_params(("parallel",)),
    )(x, g)


def _rms_rows_bwd(x, width, off, g, dy, out_dtype, name, res=None):
    S = x.shape[0]
    tm = _pick(S, 256, 16)
    has_res = res is not None

    def body(*refs):
        x_ref, g_ref, dy_ref = refs[:3]
        res_ref = refs[3] if has_res else None
        dx_ref, dg_ref = refs[3 + has_res:]
        xv, gv = x_ref[...], g_ref[...]
        _, r = _rms(xv, gv, width)
        dx, dg = _rms_bwd(xv, gv, r, dy_ref[...], width)
        if has_res:
            dx = res_ref[...] + dx
        dx_ref[...] = dx.astype(out_dtype)

        @pl.when(pl.program_id(0) == 0)
        def _():
            dg_ref[...] = jnp.zeros_like(dg_ref)

        dg_ref[...] += dg

    ins = [x, g, dy] + ([res] if has_res else [])
    in_specs = [_col(tm, width, off), _whole((1, width)), _col(tm, width, 0)] + ([_col(tm, width, 0)] if has_res else [])
    return pl.pallas_call(
        body, grid=(S // tm,), in_specs=in_specs, out_specs=[_col(tm, width, 0), _whole((1, width))],
        out_shape=[jax.ShapeDtypeStruct((S, width), out_dtype), jax.ShapeDtypeStruct((1, width), F32)], name=name,
        compiler_params=_params(("arbitrary",)),
    )(*ins)


def _norm_g_grad(x, g, dy, name):
    S, width = x.shape
    tm = _pick(S, 256, 16)

    def body(x_ref, g_ref, dy_ref, dg_ref):
        xv = x_ref[...]
        _, r = _rms(xv, g_ref[...], width)

        @pl.when(pl.program_id(0) == 0)
        def _():
            dg_ref[...] = jnp.zeros_like(dg_ref)

        dg_ref[...] += jnp.sum(dy_ref[...] * xv * r, axis=0, keepdims=True)

    return pl.pallas_call(
        body, grid=(S // tm,), in_specs=[_col(tm, width, 0), _whole((1, width)), _col(tm, width, 0)],
        out_specs=_whole((1, width)), out_shape=jax.ShapeDtypeStruct((1, width), F32), name=name,
        compiler_params=_params(("arbitrary",)),
    )(x, g, dy)


def _loss_rows(y, target, name):
    S, D = y.shape
    tm = _pick(S, 512, 16)

    def body(y_ref, t_ref, g_ref, l_ref):
        err = y_ref[...] - t_ref[...]
        g_ref[...] = err / D

        @pl.when(pl.program_id(0) == 0)
        def _():
            l_ref[...] = jnp.zeros_like(l_ref)

        row = jnp.sum(err * err, axis=-1, keepdims=True) / D
        l_ref[...] += jnp.sum(row, axis=0, keepdims=True)

    return pl.pallas_call(
        body, grid=(S // tm,), in_specs=[_col(tm, D, 0), _col(tm, D, 0)], out_specs=[_col(tm, D, 0), _whole((1, 1))],
        out_shape=[jax.ShapeDtypeStruct((S, D), F32), jax.ShapeDtypeStruct((1, 1), F32)], name=name,
        compiler_params=_params(("arbitrary",)),
    )(y, target)


def _row_index(i, ts):
    return i * ts + lax.broadcasted_iota(jnp.int32, (ts, 1), 0)


def _pool_mixed(scr, pv_ref, g, t, ts):
    w, lo = POOL_WINDOWS[g], g * POOL_GW
    win = scr[HALO:HALO + ts, lo:lo + POOL_GW]
    for k in range(1, w):
        win = win + scr[HALO - k:HALO - k + ts, lo:lo + POOL_GW]
    cnt = jnp.minimum(t + 1, w).astype(F32)
    return win / cnt - pv_ref[:, lo:lo + POOL_GW], cnt


def _halo_before(ts, off):
    nb = ts // HALO
    cb = off // BRANCH_W
    return pl.BlockSpec((HALO, BRANCH_W), lambda i: (jnp.maximum(i * nb - 1, 0), cb))


def _halo_after(ts, off, n_tiles):
    nb = ts // HALO
    cb = off // BRANCH_W
    last = n_tiles * nb - 1
    return pl.BlockSpec((HALO, BRANCH_W), lambda i: (jnp.minimum((i + 1) * nb, last), cb))


def _pool_fwd(proj, offs, pool_w, pool_scale, name):
    S = proj.shape[0]
    ts = _pick(S, 512, HALO)

    def body(pv_ref, halo_ref, pz_ref, w_ref, sc_ref, y_ref, scr):
        i = pl.program_id(0)
        scr[0:HALO, :] = jnp.where(i > 0, halo_ref[...], 0.0)
        scr[HALO:HALO + ts, :] = pv_ref[...]
        t = _row_index(i, ts)
        for g in range(4):
            lo = g * POOL_GW
            mixed, _ = _pool_mixed(scr, pv_ref, g, t, ts)
            out = _dot(mixed, w_ref[g])
            y_ref[:, lo:lo + POOL_GW] = (out * sc_ref[:, lo:lo + POOL_GW] * _silu(pz_ref[:, lo:lo + POOL_GW])).astype(BF16)

    return pl.pallas_call(
        body, grid=(S // ts,),
        in_specs=[_col(ts, BRANCH_W, offs["pv"]), _halo_before(ts, offs["pv"]), _col(ts, BRANCH_W, offs["pz"]),
                  _whole((4, POOL_GW, POOL_GW)), _whole((1, BRANCH_W))],
        out_specs=_col(ts, BRANCH_W, 0), out_shape=jax.ShapeDtypeStruct((S, BRANCH_W), BF16),
        scratch_shapes=[pltpu.VMEM((HALO + ts, BRANCH_W), F32)], name=name, compiler_params=_params(("parallel",)),
    )(proj, proj, proj, pool_w, pool_scale)


def _pool_bwd_a(proj, offs, pool_w, pool_scale, dy, name):
    S = proj.shape[0]
    ts = _pick(S, 512, HALO)

    def body(pv_ref, halo_ref, pz_ref, w_ref, sc_ref, dy_ref, dpz_ref, e_ref, dw_ref, dsc_ref, scr):
        i = pl.program_id(0)
        scr[0:HALO, :] = jnp.where(i > 0, halo_ref[...], 0.0)
        scr[HALO:HALO + ts, :] = pv_ref[...]
        t = _row_index(i, ts)

        @pl.when(i == 0)
        def _():
            dw_ref[...] = jnp.zeros_like(dw_ref)
            dsc_ref[...] = jnp.zeros_like(dsc_ref)

        for g in range(4):
            lo = g * POOL_GW
            sl = slice(lo, lo + POOL_GW)
            mixed, cnt = _pool_mixed(scr, pv_ref, g, t, ts)
            out = _dot(mixed, w_ref[g])
            sz, dsz = _silu_and_grad(pz_ref[:, sl])
            dyv, sc = dy_ref[:, sl], sc_ref[:, sl]
            d_out = dyv * sc * sz
            dsc_ref[:, sl] += jnp.sum(dyv * out * sz, axis=0, keepdims=True)
            dpz_ref[:, sl] = (dyv * out * sc * dsz).astype(BF16)
            dw_ref[g] += _dot_tn(mixed, d_out)
            e_ref[:, sl] = _dot_nt(d_out, w_ref[g]) / cnt

    return pl.pallas_call(
        body, grid=(S // ts,),
        in_specs=[_col(ts, BRANCH_W, offs["pv"]), _halo_before(ts, offs["pv"]), _col(ts, BRANCH_W, offs["pz"]),
                  _whole((4, POOL_GW, POOL_GW)), _whole((1, BRANCH_W)), _col(ts, BRANCH_W, 0)],
        out_specs=[_col(ts, BRANCH_W, 0), _col(ts, BRANCH_W, 0), _whole((4, POOL_GW, POOL_GW)), _whole((1, BRANCH_W))],
        out_shape=[jax.ShapeDtypeStruct((S, BRANCH_W), BF16), jax.ShapeDtypeStruct((S, BRANCH_W), F32),
                   jax.ShapeDtypeStruct((4, POOL_GW, POOL_GW), F32), jax.ShapeDtypeStruct((1, BRANCH_W), F32)],
        scratch_shapes=[pltpu.VMEM((HALO + ts, BRANCH_W), F32)], name=name, compiler_params=_params(("arbitrary",)),
    )(proj, proj, proj, pool_w, pool_scale, dy)


def _pool_bwd_b(e, name):
    S = e.shape[0]
    ts = _pick(S, 512, HALO)
    n_tiles = S // ts

    def body(e_ref, halo_ref, dpv_ref, scr):
        i = pl.program_id(0)
        scr[0:ts, :] = e_ref[...]
        scr[ts:ts + HALO, :] = jnp.where(i < n_tiles - 1, halo_ref[...], 0.0)
        t = _row_index(i, ts)
        for g in range(4):
            w, lo = POOL_WINDOWS[g], g * POOL_GW
            acc = scr[0:ts, lo:lo + POOL_GW]
            ev = acc
            for k in range(1, w):
                acc = acc + scr[k:k + ts, lo:lo + POOL_GW]
            cnt = jnp.minimum(t + 1, w).astype(F32)
            dpv_ref[:, lo:lo + POOL_GW] = (acc - ev * cnt).astype(BF16)

    return pl.pallas_call(
        body, grid=(n_tiles,), in_specs=[_col(ts, BRANCH_W, 0), _halo_after(ts, 0, n_tiles)], out_specs=_col(ts, BRANCH_W, 0),
        out_shape=jax.ShapeDtypeStruct((S, BRANCH_W), BF16), scratch_shapes=[pltpu.VMEM((ts + HALO, BRANCH_W), F32)],
        name=name, compiler_params=_params(("parallel",)),
    )(e, e)


def _conv_fwd(proj, offs, conv_w, name):
    S = proj.shape[0]
    ts = _pick(S, 512, HALO)

    def body(cb_ref, cc_ref, cx_ref, cz_ref, hc_ref, hx_ref, w_ref, y_ref, scr):
        i = pl.program_id(0)
        scr[0:HALO, :] = jnp.where(i > 0, hc_ref[...] * hx_ref[...], 0.0)
        scr[HALO:HALO + ts, :] = cc_ref[...] * cx_ref[...]
        y = w_ref[0:1, :] * scr[HALO - 2:HALO - 2 + ts, :] + w_ref[1:2, :] * scr[HALO - 1:HALO - 1 + ts, :] + w_ref[2:3, :] * scr[HALO:HALO + ts, :]
        y_ref[...] = (cb_ref[...] * y * _silu(cz_ref[...])).astype(BF16)

    return pl.pallas_call(
        body, grid=(S // ts,),
        in_specs=[_col(ts, BRANCH_W, offs["cb"]), _col(ts, BRANCH_W, offs["cc"]), _col(ts, BRANCH_W, offs["cx"]),
                  _col(ts, BRANCH_W, offs["cz"]), _halo_before(ts, offs["cc"]), _halo_before(ts, offs["cx"]), _whole((3, BRANCH_W))],
        out_specs=_col(ts, BRANCH_W, 0), out_shape=jax.ShapeDtypeStruct((S, BRANCH_W), BF16),
        scratch_shapes=[pltpu.VMEM((HALO + ts, BRANCH_W), F32)], name=name, compiler_params=_params(("parallel",)),
    )(proj, proj, proj, proj, proj, proj, conv_w)


def _conv_bwd(proj, offs, conv_w, dy, name):
    S = proj.shape[0]
    ts = _pick(S, 256, HALO)
    n_tiles = S // ts

    def body(cb_ref, cc_ref, cx_ref, cz_ref, hc_ref, hx_ref, w_ref, dy_ref, ady_ref, acb_ref, acz_ref,
             dcb_ref, dcc_ref, dcx_ref, dcz_ref, dw_ref, scr_u, scr_d):
        i = pl.program_id(0)
        u = cc_ref[...] * cx_ref[...]
        scr_u[0:HALO, :] = jnp.where(i > 0, hc_ref[...] * hx_ref[...], 0.0)
        scr_u[HALO:HALO + ts, :] = u
        u1 = scr_u[HALO - 1:HALO - 1 + ts, :]
        u2 = scr_u[HALO - 2:HALO - 2 + ts, :]
        y = w_ref[0:1, :] * u2 + w_ref[1:2, :] * u1 + w_ref[2:3, :] * u
        sz, dsz = _silu_and_grad(cz_ref[...])
        dyv, cb = dy_ref[...], cb_ref[...]
        dcb_ref[...] = (dyv * y * sz).astype(BF16)
        dcz_ref[...] = (dyv * cb * y * dsz).astype(BF16)
        d_y = dyv * cb * sz
        scr_d[0:ts, :] = d_y
        scr_d[ts:ts + HALO, :] = jnp.where(i < n_tiles - 1, ady_ref[...] * acb_ref[...] * _silu(acz_ref[...]), 0.0)
        du = w_ref[2:3, :] * d_y + w_ref[1:2, :] * scr_d[1:1 + ts, :] + w_ref[0:1, :] * scr_d[2:2 + ts, :]
        dcc_ref[...] = (du * cx_ref[...]).astype(BF16)
        dcx_ref[...] = (du * cc_ref[...]).astype(BF16)

        @pl.when(i == 0)
        def _():
            dw_ref[...] = jnp.zeros_like(dw_ref)

        dw_ref[0:1, :] += jnp.sum(d_y * u2, axis=0, keepdims=True)
        dw_ref[1:2, :] += jnp.sum(d_y * u1, axis=0, keepdims=True)
        dw_ref[2:3, :] += jnp.sum(d_y * u, axis=0, keepdims=True)

    row = lambda off: _col(ts, BRANCH_W, off)
    return pl.pallas_call(
        body, grid=(n_tiles,),
        in_specs=[row(offs["cb"]), row(offs["cc"]), row(offs["cx"]), row(offs["cz"]), _halo_before(ts, offs["cc"]),
                  _halo_before(ts, offs["cx"]), _whole((3, BRANCH_W)), row(0), _halo_after(ts, 0, n_tiles),
                  _halo_after(ts, offs["cb"], n_tiles), _halo_after(ts, offs["cz"], n_tiles)],
        out_specs=[row(0), row(0), row(0), row(0), _whole((3, BRANCH_W))],
        out_shape=[jax.ShapeDtypeStruct((S, BRANCH_W), BF16)] * 4 + [jax.ShapeDtypeStruct((3, BRANCH_W), F32)],
        scratch_shapes=[pltpu.VMEM((HALO + ts, BRANCH_W), F32), pltpu.VMEM((ts + HALO, BRANCH_W), F32)], name=name,
        compiler_params=_params(("arbitrary",)),
    )(proj, proj, proj, proj, proj, proj, conv_w, dy, dy, proj, proj)


def _xattn_head(xq, kraw, v, qg, kg):
    qn, rq = _rms(xq, qg, XATTN_HD)
    kn, rk = _rms(kraw, kg, XATTN_HD)
    s = _dot_nt(qn, kn) * (XATTN_HD ** -0.5)
    e = jnp.exp(s - jnp.max(s, axis=-1, keepdims=True))
    p = e / jnp.sum(e, axis=-1, keepdims=True)
    return qn, rq, kn, rk, p, _dot(p, v)


def _xattn_fwd(proj, offs, mem_kv, qg, kg, name):
    S, M = proj.shape[0], mem_kv.shape[0]
    tm = _pick(S, 512, 16)

    def body(xq_ref, xz_ref, kv_ref, qg_ref, kg_ref, y_ref):
        for h in range(XATTN_HEADS):
            sl = slice(h * XATTN_HD, (h + 1) * XATTN_HD)
            vs = slice(BRANCH_W + h * XATTN_HD, BRANCH_W + (h + 1) * XATTN_HD)
            o = _xattn_head(xq_ref[:, sl], kv_ref[:, sl], kv_ref[:, vs], qg_ref[...], kg_ref[...])[-1]
            y_ref[:, sl] = (o * _silu(xz_ref[:, sl])).astype(BF16)

    return pl.pallas_call(
        body, grid=(S // tm,),
        in_specs=[_col(tm, BRANCH_W, offs["xq"]), _col(tm, BRANCH_W, offs["xz"]), _whole((M, 2 * BRANCH_W)),
                  _whole((1, XATTN_HD)), _whole((1, XATTN_HD))],
        out_specs=_col(tm, BRANCH_W, 0), out_shape=jax.ShapeDtypeStruct((S, BRANCH_W), BF16), name=name,
        compiler_params=_params(("parallel",)),
    )(proj, proj, mem_kv, qg, kg)


def _xattn_bwd(proj, offs, mem_kv, qg, kg, dy, name):
    S, M = proj.shape[0], mem_kv.shape[0]
    tm = _pick(S, 256, 16)
    n_tiles = S // tm

    def body(xq_ref, xz_ref, kv_ref, qg_ref, kg_ref, dy_ref, dxq_ref, dxz_ref, dkv_ref, dqg_ref, dkg_ref, dkn_acc):
        i = pl.program_id(0)

        @pl.when(i == 0)
        def _():
            dkv_ref[...] = jnp.zeros_like(dkv_ref)
            dqg_ref[...] = jnp.zeros_like(dqg_ref)
            dkg_ref[...] = jnp.zeros_like(dkg_ref)
            dkn_acc[...] = jnp.zeros_like(dkn_acc)

        qg_v, kg_v = qg_ref[...], kg_ref[...]
        for h in range(XATTN_HEADS):
            sl = slice(h * XATTN_HD, (h + 1) * XATTN_HD)
            vs = slice(BRANCH_W + h * XATTN_HD, BRANCH_W + (h + 1) * XATTN_HD)
            xq, v = xq_ref[:, sl], kv_ref[:, vs]
            qn, rq, kn, rk, p, o = _xattn_head(xq, kv_ref[:, sl], v, qg_v, kg_v)
            sz, dsz = _silu_and_grad(xz_ref[:, sl])
            dyv = dy_ref[:, sl]
            dxz_ref[:, sl] = (dyv * o * dsz).astype(BF16)
            do = dyv * sz
            dkv_ref[:, vs] += _dot_tn(p, do)
            dp = _dot_nt(do, v)
            ds = p * (dp - jnp.sum(dp * p, axis=-1, keepdims=True)) * (XATTN_HD ** -0.5)
            dqn = _dot(ds, kn)
            dkn_acc[:, sl] += _dot_tn(ds, qn)
            dxq, dqg = _rms_bwd(xq, qg_v, rq, dqn, XATTN_HD)
            dxq_ref[:, sl] = dxq.astype(BF16)
            dqg_ref[...] += dqg

        @pl.when(i == n_tiles - 1)
        def _():
            for h in range(XATTN_HEADS):
                sl = slice(h * XATTN_HD, (h + 1) * XATTN_HD)
                kraw = kv_ref[:, sl]
                _, rk = _rms(kraw, kg_v, XATTN_HD)
                dk, dkg = _rms_bwd(kraw, kg_v, rk, dkn_acc[:, sl], XATTN_HD)
                dkv_ref[:, sl] = dk
                dkg_ref[...] += dkg

    return pl.pallas_call(
        body, grid=(n_tiles,),
        in_specs=[_col(tm, BRANCH_W, offs["xq"]), _col(tm, BRANCH_W, offs["xz"]), _whole((M, 2 * BRANCH_W)),
                  _whole((1, XATTN_HD)), _whole((1, XATTN_HD)), _col(tm, BRANCH_W, 0)],
        out_specs=[_col(tm, BRANCH_W, 0), _col(tm, BRANCH_W, 0), _whole((M, 2 * BRANCH_W)), _whole((1, XATTN_HD)), _whole((1, XATTN_HD))],
        out_shape=[jax.ShapeDtypeStruct((S, BRANCH_W), BF16), jax.ShapeDtypeStruct((S, BRANCH_W), BF16),
                   jax.ShapeDtypeStruct((M, 2 * BRANCH_W), F32), jax.ShapeDtypeStruct((1, XATTN_HD), F32), jax.ShapeDtypeStruct((1, XATTN_HD), F32)],
        scratch_shapes=[pltpu.VMEM((M, BRANCH_W), F32)], name=name, compiler_params=_params(("arbitrary",)),
    )(proj, proj, mem_kv, qg, kg, dy)


def _rope(c, tc, ta, tb):
    return c * tc + pltpu.roll(c, 96, 1) * ta + pltpu.roll(c, 32, 1) * tb


def _rope_t(d, tc, ta, tb):
    return d * tc + pltpu.roll(d * ta, 32, 1) + pltpu.roll(d * tb, 96, 1)


def _mla_qk_fwd(qp, kvp, proj, offs, tabs, qg, kg, name):
    S = qp.shape[0]
    tm = _pick(S, 256, 16)
    W = MLA_HEADS * QK_PAD

    def body(q_ref, kv_ref, kr_ref, tc_ref, ta_ref, tb_ref, qg_ref, kg_ref, qo_ref, ko_ref, vo_ref):
        tc, ta, tb = tc_ref[...], ta_ref[...], tb_ref[...]
        qg_v, kg_v = qg_ref[...], kg_ref[...]
        kr = kr_ref[...]
        kr_ss = jnp.sum(kr * kr, axis=-1, keepdims=True)
        for h in range(MLA_HEADS):
            sl = slice(h * QK_PAD, (h + 1) * QK_PAD)
            qn, _ = _rms(q_ref[:, sl], qg_v, QK_HEAD)
            qo_ref[:, sl] = jnp.concatenate([qn[:, :QK_NOPE], _rope(qn[:, QK_NOPE:], tc, ta, tb)], axis=1).astype(BF16)
            kn = kv_ref[:, h * QK_NOPE:(h + 1) * QK_NOPE]
            r = lax.rsqrt((jnp.sum(kn * kn, axis=-1, keepdims=True) + kr_ss) / QK_HEAD + EPS)
            ko_ref[:, sl] = jnp.concatenate([kn * r * kg_v[:, :QK_NOPE], _rope(kr * r * kg_v[:, QK_NOPE:], tc, ta, tb)], axis=1).astype(BF16)
        vo_ref[...] = kv_ref[:, MLA_HEADS * QK_NOPE:].astype(BF16)

    return pl.pallas_call(
        body, grid=(S // tm,),
        in_specs=[_col(tm, W, 0), _col(tm, W, 0), _col(tm, 128, offs["kr"]), _col(tm, 128, 0), _col(tm, 128, 0), _col(tm, 128, 0),
                  _whole((1, QK_PAD)), _whole((1, QK_PAD))],
        out_specs=[_col(tm, W, 0), _col(tm, W, 0), _col(tm, BRANCH_W, 0)],
        out_shape=[jax.ShapeDtypeStruct((S, W), BF16), jax.ShapeDtypeStruct((S, W), BF16), jax.ShapeDtypeStruct((S, BRANCH_W), BF16)],
        name=name, compiler_params=_params(("parallel",)),
    )(qp, kvp, proj, *tabs, qg, kg)


def _mla_qk_bwd(qp, kvp, proj, offs, tabs, qg, kg, dQ, dK, dV, name):
    S = qp.shape[0]
    tm = _pick(S, 256, 16)
    W = MLA_HEADS * QK_PAD

    def body(q_ref, kv_ref, kr_ref, tc_ref, ta_ref, tb_ref, qg_ref, kg_ref, dQ_ref, dK_ref, dV_ref,
             dq_ref, dkv_ref, dkr_ref, dqg_ref, dkg_ref):
        @pl.when(pl.program_id(0) == 0)
        def _():
            dqg_ref[...] = jnp.zeros_like(dqg_ref)
            dkg_ref[...] = jnp.zeros_like(dkg_ref)

        tc, ta, tb = tc_ref[...], ta_ref[...], tb_ref[...]
        qg_v, kg_v = qg_ref[...], kg_ref[...]
        kg_n, kg_r = kg_v[:, :QK_NOPE], kg_v[:, QK_NOPE:]
        kr = kr_ref[...]
        kr_ss = jnp.sum(kr * kr, axis=-1, keepdims=True)
        dkr = jnp.zeros_like(kr)
        dkg_n = jnp.zeros((1, QK_NOPE), F32)
        dkg_r = jnp.zeros((1, QK_NOPE), F32)
        for h in range(MLA_HEADS):
            sl = slice(h * QK_PAD, (h + 1) * QK_PAD)
            qv = q_ref[:, sl]
            _, rq = _rms(qv, qg_v, QK_HEAD)
            dQv = dQ_ref[:, sl]
            dqn = jnp.concatenate([dQv[:, :QK_NOPE], _rope_t(dQv[:, QK_NOPE:], tc, ta, tb)], axis=1)
            dq, dqg = _rms_bwd(qv, qg_v, rq, dqn, QK_HEAD)
            dq_ref[:, sl] = dq.astype(BF16)
            dqg_ref[...] += dqg
            ns = slice(h * QK_NOPE, (h + 1) * QK_NOPE)
            kn = kv_ref[:, ns]
            r = lax.rsqrt((jnp.sum(kn * kn, axis=-1, keepdims=True) + kr_ss) / QK_HEAD + EPS)
            dKv = dK_ref[:, sl]
            dyn = dKv[:, :QK_NOPE]
            dyr = _rope_t(dKv[:, QK_NOPE:], tc, ta, tb)
            proj_s = (jnp.sum(dyn * kg_n * kn, axis=-1, keepdims=True) + jnp.sum(dyr * kg_r * kr, axis=-1, keepdims=True)) / QK_HEAD
            r3 = r * r * r
            dkv_ref[:, ns] = (r * dyn * kg_n - kn * r3 * proj_s).astype(BF16)
            dkr = dkr + (r * dyr * kg_r - kr * r3 * proj_s)
            dkg_n = dkg_n + jnp.sum(dyn * kn * r, axis=0, keepdims=True)
            dkg_r = dkg_r + jnp.sum(dyr * kr * r, axis=0, keepdims=True)
        dkg_ref[...] += jnp.concatenate([dkg_n, dkg_r], axis=1)
        dkr_ref[...] = dkr.astype(BF16)
        dkv_ref[:, MLA_HEADS * QK_NOPE:] = dV_ref[...].astype(BF16)

    return pl.pallas_call(
        body, grid=(S // tm,),
        in_specs=[_col(tm, W, 0), _col(tm, W, 0), _col(tm, 128, offs["kr"]), _col(tm, 128, 0), _col(tm, 128, 0), _col(tm, 128, 0),
                  _whole((1, QK_PAD)), _whole((1, QK_PAD)), _col(tm, W, 0), _col(tm, W, 0), _col(tm, BRANCH_W, 0)],
        out_specs=[_col(tm, W, 0), _col(tm, W, 0), _col(tm, 128, 0), _whole((1, QK_PAD)), _whole((1, QK_PAD))],
        out_shape=[jax.ShapeDtypeStruct((S, W), BF16), jax.ShapeDtypeStruct((S, W), BF16), jax.ShapeDtypeStruct((S, 128), BF16),
                   jax.ShapeDtypeStruct((1, QK_PAD), F32), jax.ShapeDtypeStruct((1, QK_PAD), F32)],
        name=name, compiler_params=_params(("arbitrary",)),
    )(qp, kvp, proj, *tabs, qg, kg, dQ, dK, dV)


def _causal_scores(q, k, qi, ki, tq, tk):
    s = _dot_nt(q, k) * SCORE_SCALE_LOG2
    rows = qi * tq + lax.broadcasted_iota(jnp.int32, (tq, tk), 0)
    cols = ki * tk + lax.broadcasted_iota(jnp.int32, (tq, tk), 1)
    return s, cols <= rows


def _flash_fwd(Q, K, V, proj, offs, name):
    S = Q.shape[0]
    tq = _pick(S, 1024, 128)
    tk = _pick(tq, 512, 128)
    per = tq // tk
    mz_cb = offs["mz"] // V_HEAD

    def body(q_ref, k_ref, v_ref, mz_ref, o_ref, y_ref, lse_ref, m_scr, l_scr, acc_scr):
        qi = pl.program_id(1)
        q = q_ref[...]
        m_scr[...] = jnp.full_like(m_scr, NEG)
        l_scr[...] = jnp.zeros_like(l_scr)
        acc_scr[...] = jnp.zeros_like(acc_scr)

        def block(j, masked):
            rows_k = pl.ds(pl.multiple_of(j * tk, tk), tk)
            s, mask = _causal_scores(q, k_ref[rows_k, :], qi, j, tq, tk)
            if masked:
                s = jnp.where(mask, s, NEG)
            m_old = m_scr[...]
            m_new = jnp.maximum(m_old, jnp.max(s, axis=-1, keepdims=True))
            alpha = jnp.exp2(m_old - m_new)
            p = jnp.exp2(s - m_new)
            l_scr[...] = alpha * l_scr[...] + jnp.sum(p, axis=-1, keepdims=True)
            acc_scr[...] = alpha * acc_scr[...] + _dot(p, v_ref[rows_k, :])
            m_scr[...] = m_new

        def below_diagonal(j, carry):
            block(j, False)
            return carry

        lax.fori_loop(0, qi * per, below_diagonal, 0)
        for d in range(per):
            block(qi * per + d, True)
        o = acc_scr[...] / l_scr[...]
        o_ref[...] = o
        y_ref[...] = (o * _silu(mz_ref[...])).astype(BF16)
        lse_ref[0] = m_scr[...] + jnp.log2(l_scr[...])

    return pl.pallas_call(
        body, grid=(MLA_HEADS, S // tq),
        in_specs=[pl.BlockSpec((tq, QK_PAD), lambda h, i: (i, h)), pl.BlockSpec((S, QK_PAD), lambda h, i: (0, h)),
                  pl.BlockSpec((S, V_HEAD), lambda h, i: (0, h)), pl.BlockSpec((tq, V_HEAD), lambda h, i: (i, mz_cb + h))],
        out_specs=[pl.BlockSpec((tq, V_HEAD), lambda h, i: (i, h)), pl.BlockSpec((tq, V_HEAD), lambda h, i: (i, h)),
                   pl.BlockSpec((1, tq, 1), lambda h, i: (h, i, 0))],
        out_shape=[jax.ShapeDtypeStruct((S, BRANCH_W), F32), jax.ShapeDtypeStruct((S, BRANCH_W), BF16),
                   jax.ShapeDtypeStruct((MLA_HEADS, S, 1), F32)],
        scratch_shapes=[pltpu.VMEM((tq, 1), F32), pltpu.VMEM((tq, 1), F32), pltpu.VMEM((tq, V_HEAD), F32)], name=name,
        compiler_params=_params(("parallel", "parallel")),
    )(Q, K, V, proj)


def _mla_gate_bwd(o, proj, offs, dy, name):
    S = o.shape[0]
    tm = _pick(S, 512, 16)

    def body(o_ref, mz_ref, dy_ref, do_ref, dmz_ref, dl_ref):
        sz, dsz = _silu_and_grad(mz_ref[...])
        dyv, ov = dy_ref[...], o_ref[...]
        do = (dyv * sz).astype(BF16)
        do_ref[...] = do
        dmz_ref[...] = (dyv * ov * dsz).astype(BF16)
        prod = do.astype(F32) * ov
        for h in range(MLA_HEADS):
            dl_ref[h] = jnp.sum(prod[:, h * V_HEAD:(h + 1) * V_HEAD], axis=-1, keepdims=True)

    return pl.pallas_call(
        body, grid=(S // tm,), in_specs=[_col(tm, BRANCH_W, 0), _col(tm, BRANCH_W, offs["mz"]), _col(tm, BRANCH_W, 0)],
        out_specs=[_col(tm, BRANCH_W, 0), _col(tm, BRANCH_W, 0), pl.BlockSpec((MLA_HEADS, tm, 1), lambda i: (0, i, 0))],
        out_shape=[jax.ShapeDtypeStruct((S, BRANCH_W), BF16), jax.ShapeDtypeStruct((S, BRANCH_W), BF16),
                   jax.ShapeDtypeStruct((MLA_HEADS, S, 1), F32)], name=name,
        compiler_params=_params(("parallel",)),
    )(o, proj, dy)


def _flash_bwd(Q, K, V, lse, delta, do, name):
    S = Q.shape[0]
    tk = _pick(S, 512, 128)
    nq = S // tk
    once = pl.Buffered(1)

    def body(q_ref, k_ref, v_ref, lse_ref, dl_ref, do_ref, dq_ref, dk_ref, dv_ref, dk_acc, dv_acc):
        ki = pl.program_id(1)

        @pl.when(ki == 0)
        def _():
            dq_ref[...] = jnp.zeros_like(dq_ref)

        dk_acc[...] = jnp.zeros_like(dk_acc)
        dv_acc[...] = jnp.zeros_like(dv_acc)
        k, v = k_ref[...], v_ref[...]

        def block(i, masked):
            rows = pl.ds(pl.multiple_of(i * tk, tk), tk)
            q, dov = q_ref[rows, :], do_ref[rows, :]
            s, mask = _causal_scores(q, k, i, ki, tk, tk)
            if masked:
                s = jnp.where(mask, s, NEG)
            p = jnp.exp2(s - lse_ref[0, rows, :])
            dv_acc[...] += _dot_tn(p, dov)
            dp = _dot_nt(dov, v)
            ds = p * (dp - dl_ref[0, rows, :]) * (QK_HEAD ** -0.5)
            dk_acc[...] += _dot_tn(ds, q)
            dq_ref[rows, :] += _dot(ds, k)

        def below_diagonal(i, carry):
            block(i, False)
            return carry

        block(ki, True)
        lax.fori_loop(ki + 1, nq, below_diagonal, 0)
        dk_ref[...] = dk_acc[...]
        dv_ref[...] = dv_acc[...]

    return pl.pallas_call(
        body, grid=(MLA_HEADS, nq),
        in_specs=[pl.BlockSpec((S, QK_PAD), lambda h, j: (0, h), pipeline_mode=once), pl.BlockSpec((tk, QK_PAD), lambda h, j: (j, h)),
                  pl.BlockSpec((tk, V_HEAD), lambda h, j: (j, h)), pl.BlockSpec((1, S, 1), lambda h, j: (h, 0, 0), pipeline_mode=once),
                  pl.BlockSpec((1, S, 1), lambda h, j: (h, 0, 0), pipeline_mode=once),
                  pl.BlockSpec((S, V_HEAD), lambda h, j: (0, h), pipeline_mode=once)],
        out_specs=[pl.BlockSpec((S, QK_PAD), lambda h, j: (0, h)), pl.BlockSpec((tk, QK_PAD), lambda h, j: (j, h)),
                   pl.BlockSpec((tk, V_HEAD), lambda h, j: (j, h))],
        out_shape=[jax.ShapeDtypeStruct((S, MLA_HEADS * QK_PAD), F32), jax.ShapeDtypeStruct((S, MLA_HEADS * QK_PAD), F32),
                   jax.ShapeDtypeStruct((S, BRANCH_W), F32)],
        scratch_shapes=[pltpu.VMEM((tk, QK_PAD), F32), pltpu.VMEM((tk, V_HEAD), F32)], name=name,
        compiler_params=_params(("parallel", "arbitrary")),
    )(Q, K, V, lse, delta, do)


def _merge_specs(S, D, offs, tm, tn, layer):
    g_cb = offs["g"] // tn
    nj = D // tn
    y_specs = [pl.BlockSpec((tm, BRANCH_W), lambda j, i: (i, 0)) for _ in range(4)]
    w_spec = pl.BlockSpec((None, 4, BRANCH_W, tn), lambda j, i: (layer, 0, 0, j))
    g_specs = [pl.BlockSpec((tm, tn), functools.partial(lambda j, i, b: (i, g_cb + b * nj + j), b=b)) for b in range(4)]
    bias_spec = pl.BlockSpec((4, tn), lambda j, i: (0, j))
    return y_specs, w_spec, g_specs, bias_spec


def _merge_fwd(ys, wb, layer, proj, offs, gate_b, name):
    S, D = ys[0].shape[0], wb.shape[3]
    tm, tn = _pick(S, 512, 16), _pick(D, 512, 128)
    y_specs, w_spec, g_specs, bias_spec = _merge_specs(S, D, offs, tm, tn, layer)

    def body(y0, y1, y2, y3, w_ref, g0, g1, g2, g3, b_ref, o_ref):
        acc = None
        for b, (y_ref, g_ref) in enumerate(zip((y0, y1, y2, y3), (g0, g1, g2, g3))):
            gate = 1.0 / (1.0 + jnp.exp(-(g_ref[...] + b_ref[b:b + 1, :])))
            term = gate * _dot(y_ref[...], w_ref[b])
            acc = term if acc is None else acc + term
        o_ref[...] = acc.astype(BF16)

    return pl.pallas_call(
        body, grid=(D // tn, S // tm), in_specs=y_specs + [w_spec] + g_specs + [bias_spec],
        out_specs=pl.BlockSpec((tm, tn), lambda j, i: (i, j)), out_shape=jax.ShapeDtypeStruct((S, D), BF16), name=name,
        compiler_params=_params(("parallel", "parallel")),
    )(*ys, wb, proj, proj, proj, proj, gate_b)


def _merge_bwd(ys, wb, layer, proj, offs, gate_b, dmerged, name):
    S, D = ys[0].shape[0], wb.shape[3]
    tm, tn = _pick(S, 512, 16), _pick(D, 512, 128)
    nj = D // tn
    y_specs, w_spec, g_specs, bias_spec = _merge_specs(S, D, offs, tm, tn, layer)

    def body(y0, y1, y2, y3, w_ref, g0, g1, g2, g3, b_ref, dm_ref, dt_ref, dg_ref, db_ref):
        @pl.when(pl.program_id(1) == 0)
        def _():
            db_ref[...] = jnp.zeros_like(db_ref)

        dm = dm_ref[...]
        for b, (y_ref, g_ref) in enumerate(zip((y0, y1, y2, y3), (g0, g1, g2, g3))):
            gate = 1.0 / (1.0 + jnp.exp(-(g_ref[...] + b_ref[b:b + 1, :])))
            t = _dot(y_ref[...], w_ref[b])
            dt_ref[b] = (dm * gate).astype(BF16)
            dgp = dm * t * gate * (1.0 - gate)
            dg_ref[b] = dgp.astype(BF16)
            db_ref[b:b + 1, :] += jnp.sum(dgp, axis=0, keepdims=True)

    stacked = pl.BlockSpec((4, tm, tn), lambda j, i: (0, i, j))
    return pl.pallas_call(
        body, grid=(nj, S // tm),
        in_specs=y_specs + [w_spec] + g_specs + [bias_spec, pl.BlockSpec((tm, tn), lambda j, i: (i, j))],
        out_specs=[stacked, stacked, pl.BlockSpec((4, tn), lambda j, i: (0, j))],
        out_shape=[jax.ShapeDtypeStruct((4, S, D), BF16), jax.ShapeDtypeStruct((4, S, D), BF16), jax.ShapeDtypeStruct((4, D), F32)],
        name=name, compiler_params=_params(("parallel", "arbitrary")),
    )(*ys, wb, proj, proj, proj, proj, gate_b, dmerged)


HBM_SPEC = pl.BlockSpec(memory_space=pl.ANY)


def _mesh_place():
    x, y, c = lax.axis_index("x"), lax.axis_index("y"), lax.axis_index("c")
    return x, y, c, [(1 - x, y), (x, 1 - y), (1 - x, 1 - y)]


def _remote(src, dst, send_sem, recv_sem, to):
    return pltpu.make_async_remote_copy(src_ref=src, dst_ref=dst, send_sem=send_sem, recv_sem=recv_sem, device_id=to,
                                        device_id_type=pl.DeviceIdType.MESH)


def _gather_two_level(arrs, name):
    n = len(arrs)

    def body(*refs):
        ins, dsts = refs[:n], refs[n:2 * n]
        send_sems, recv_sems, local_sems = refs[2 * n:]
        x, y, c, chips = _mesh_place()
        slot = lambda px, py, pc: 4 * px + 2 * py + pc
        me, sibling = slot(x, y, c), (x, y, 1 - c)

        def copy(a, k, src, block, to):
            return _remote(src, dsts[a].at[block], send_sems.at[a, k], recv_sems.at[a, k], to)

        local = [pltpu.make_async_copy(ins[a], dsts[a].at[me], local_sems.at[a]) for a in range(n)]
        for cp in local:
            cp.start()
        first = []
        for a in range(n):
            first.append(copy(a, 0, ins[a], me, sibling))
            first += [copy(a, 1 + j, ins[a], me, (*chip, c)) for j, chip in enumerate(chips)]
        for cp in first:
            cp.start()
        passed = []
        for a in range(n):
            for j, chip in enumerate(chips):
                landed = slot(*chip, c)
                copy(a, 1 + j, ins[a], landed, sibling).wait_recv()
                fwd = copy(a, 4 + j, dsts[a].at[landed], landed, sibling)
                fwd.start()
                passed.append(fwd)
        for a in range(n):
            copy(a, 0, ins[a], slot(x, y, 1 - c), sibling).wait_recv()
            for j, chip in enumerate(chips):
                copy(a, 4 + j, ins[a], slot(*chip, 1 - c), sibling).wait_recv()
        for cp in first + passed:
            cp.wait_send()
        for cp in local:
            cp.wait()

    res = pl.pallas_call(
        body, in_specs=[HBM_SPEC] * n, out_specs=[HBM_SPEC] * n,
        out_shape=[jax.ShapeDtypeStruct((N_DEV,) + a.shape, a.dtype) for a in arrs],
        scratch_shapes=[pltpu.SemaphoreType.DMA((n, 7)), pltpu.SemaphoreType.DMA((n, 7)), pltpu.SemaphoreType.DMA((n,))], name=name,
    )(*arrs)
    return list(res)


def _sibling_swap(arrs, name):
    n = len(arrs)

    def body(*refs):
        ins, dsts = refs[:n], refs[n:2 * n]
        send_sems, recv_sems = refs[2 * n:]
        x, y, c, _ = _mesh_place()
        cps = [_remote(ins[a], dsts[a], send_sems.at[a], recv_sems.at[a], (x, y, 1 - c)) for a in range(n)]
        for cp in cps:
            cp.start()
        for cp in cps:
            cp.wait()

    res = pl.pallas_call(
        body, in_specs=[HBM_SPEC] * n, out_specs=[HBM_SPEC] * n, out_shape=[jax.ShapeDtypeStruct(a.shape, a.dtype) for a in arrs],
        scratch_shapes=[pltpu.SemaphoreType.DMA((n,)), pltpu.SemaphoreType.DMA((n,))], name=name,
    )(*arrs)
    return list(res)


def _chip_exchange(arrs, name):
    n = len(arrs)

    def body(*refs):
        ins, dsts = refs[:n], refs[n:2 * n]
        send_sems, recv_sems, local_sems = refs[2 * n:]
        x, y, c, chips = _mesh_place()
        mine = 2 * x + y
        local = [pltpu.make_async_copy(ins[a].at[mine], dsts[a].at[mine], local_sems.at[a]) for a in range(n)]
        for cp in local:
            cp.start()
        cps = []
        for a in range(n):
            for j, (px, py) in enumerate(chips):
                cps.append(_remote(ins[a].at[2 * px + py], dsts[a].at[mine], send_sems.at[a, j], recv_sems.at[a, j], (px, py, c)))
        for cp in cps:
            cp.start()
        for cp in cps:
            cp.wait()
        for cp in local:
            cp.wait()

    res = pl.pallas_call(
        body, in_specs=[HBM_SPEC] * n, out_specs=[HBM_SPEC] * n, out_shape=[jax.ShapeDtypeStruct(a.shape, a.dtype) for a in arrs],
        scratch_shapes=[pltpu.SemaphoreType.DMA((n, 3)), pltpu.SemaphoreType.DMA((n, 3)), pltpu.SemaphoreType.DMA((n,))], name=name,
    )(*arrs)
    return list(res)


def _pair_sum(mine, theirs, name):
    R, C = mine.shape
    tr = _pick(R, max(16, ((1 << 19) // C) // 16 * 16), 16)

    def body(a_ref, b_ref, o_ref):
        o_ref[...] = (a_ref[...].astype(F32) + b_ref[...].astype(F32)).astype(BF16)

    blk = pl.BlockSpec((tr, C), lambda i: (i, 0))
    return pl.pallas_call(
        body, grid=(R // tr,), in_specs=[blk, blk], out_specs=blk, out_shape=jax.ShapeDtypeStruct((R, C), BF16), name=name,
        compiler_params=_params(("parallel",)),
    )(mine, theirs)


def _adamw(parts, w, m, v, name):
    R, C = w.shape
    n_parts = parts.shape[0]
    tr = _pick(R, max(16, ((1 << 18) // C) // 16 * 16), 16)

    def body(p_ref, w_ref, m_ref, v_ref, g_ref, d_ref, nm_ref, nv_ref):
        g = p_ref[0].astype(F32)
        for s in range(1, n_parts):
            g = g + p_ref[s].astype(F32)
        mn = ADAM_B1 * m_ref[...] + (1.0 - ADAM_B1) * g
        vn = ADAM_B2 * v_ref[...] + (1.0 - ADAM_B2) * (g * g)
        m_hat = mn / (1.0 - ADAM_B1 ** ADAM_STEP)
        v_hat = vn / (1.0 - ADAM_B2 ** ADAM_STEP)
        g_ref[...] = g
        d_ref[...] = -ADAM_LR * (m_hat / (jnp.sqrt(v_hat) + ADAM_EPS) + ADAM_WD * w_ref[...])
        nm_ref[...] = mn
        nv_ref[...] = vn

    blk = pl.BlockSpec((tr, C), lambda i: (i, 0))
    return pl.pallas_call(
        body, grid=(R // tr,), in_specs=[pl.BlockSpec((n_parts, tr, C), lambda i: (0, i, 0)), blk, blk, blk],
        out_specs=[blk] * 4, out_shape=[jax.ShapeDtypeStruct((R, C), F32)] * 4, name=name,
        compiler_params=_params(("parallel",)),
    )(parts, w, m, v)


def _offsets(L, D):
    o = {"pv": 0, "pz": BRANCH_W, "cq": 2 * BRANCH_W, "ckv": 2 * BRANCH_W + L, "mz": 2 * BRANCH_W + 2 * L}
    for prev, nxt in (("mz", "cb"), ("cb", "cc"), ("cc", "cx"), ("cx", "cz"), ("cz", "xq"), ("xq", "xz"), ("xz", "g")):
        o[nxt] = o[prev] + BRANCH_W
    o["kr"] = o["g"] + 4 * D
    o["n"] = o["kr"] + 128
    return o


def _unshard(gathered, axis):
    moved = jnp.moveaxis(gathered, 0, axis)
    shape = list(moved.shape)
    shape[axis:axis + 2] = [shape[axis] * shape[axis + 1]]
    return moved.reshape(shape)


def _reshard(full, axis):
    shape = list(full.shape)
    shape[axis:axis + 1] = [N_DEV, shape[axis] // N_DEV]
    return jnp.moveaxis(full.reshape(shape), axis, 0)


def _pad_heads(g, width):
    return jnp.pad(g, ((0, 0), (0, width - g.shape[1])))


def kernel(x, mem, positions, norm_g, w_in, gate_b, pool_w, pool_scale, q_a_norm_g, kv_a_norm_g, w_uq, w_ukv, mla_q_norm_g, mla_k_norm_g, conv_w, mem_norm_g, w_mem_kv, xattn_q_norm_g, xattn_k_norm_g, w_branch, w_out, loss_target, m_norm_g, m_w_in, m_gate_b, m_pool_w, m_pool_scale, m_q_a_norm_g, m_kv_a_norm_g, m_w_uq, m_w_ukv, m_mla_q_norm_g, m_mla_k_norm_g, m_conv_w, m_mem_norm_g, m_w_mem_kv, m_xattn_q_norm_g, m_xattn_k_norm_g, m_w_branch, m_w_out, v_norm_g, v_w_in, v_gate_b, v_pool_w, v_pool_scale, v_q_a_norm_g, v_kv_a_norm_g, v_w_uq, v_w_ukv, v_mla_q_norm_g, v_mla_k_norm_g, v_conv_w, v_mem_norm_g, v_w_mem_kv, v_xattn_q_norm_g, v_xattn_k_norm_g, v_w_branch, v_w_out):
    depth = norm_g.shape[0]
    S, D = x.shape[1], x.shape[2]
    L = w_uq.shape[1]
    offs = _offsets(L, D)
    seg_a = 2 * BRANCH_W + 2 * L
    xs = x.reshape(S, D)
    mems = mem.reshape(mem.shape[1], D)

    sharded = dict(w_in=w_in, pool_w=pool_w, w_uq=w_uq, w_ukv=w_ukv, conv_w=conv_w, w_mem_kv=w_mem_kv, w_branch=w_branch, w_out=w_out)
    shard_axis = dict(w_in=2, pool_w=2, w_uq=2, w_ukv=2, conv_w=2, w_mem_kv=1, w_branch=3, w_out=1)
    keys = list(shard_axis)
    payload = [sharded[k] if k == "conv_w" else sharded[k].astype(BF16) for k in keys]
    gathered = {k: _unshard(g, shard_axis[k]) for k, g in zip(keys, _gather_two_level(payload, "gather_weights"))}
    conv_full = gathered["conv_w"]
    w_in_f = gathered["w_in"]
    w_in_p = [jnp.concatenate([w_in_f[l, :, :seg_a], w_in_f[l, :, seg_a + QK_ROPE:], w_in_f[l, :, seg_a:seg_a + QK_ROPE],
                               jnp.zeros((D, 128 - QK_ROPE), BF16)], axis=-1) for l in range(depth)]
    pool_w_f = gathered["pool_w"]
    w_uq_f = gathered["w_uq"].reshape(depth, L, MLA_HEADS, QK_HEAD)
    w_uq_p = jnp.pad(w_uq_f, ((0, 0), (0, 0), (0, 0), (0, QK_PAD - QK_HEAD))).reshape(depth, L, MLA_HEADS * QK_PAD)
    w_ukv_f = gathered["w_ukv"].reshape(depth, L, MLA_HEADS, QK_NOPE + V_HEAD)
    w_ukv_p = jnp.concatenate([w_ukv_f[..., :QK_NOPE].reshape(depth, L, -1), w_ukv_f[..., QK_NOPE:].reshape(depth, L, -1)], axis=-1)
    w_mem_f = gathered["w_mem_kv"]
    w_br_f = gathered["w_branch"]
    w_out_f = gathered["w_out"]

    inv = ROPE_THETA ** (-jnp.arange(0, QK_ROPE, 2, dtype=F32) / QK_ROPE)
    ang = positions.reshape(S).astype(F32)[:, None] * inv
    cos, sin, z = jnp.cos(ang), jnp.sin(ang), jnp.zeros_like(ang)
    tabs = (jnp.concatenate([cos, cos, z, z], axis=1), jnp.concatenate([-sin, z, z, z], axis=1), jnp.concatenate([z, sin, z, z], axis=1))

    row = lambda a, l: a[l][None, :]
    saved = []
    cur = xs
    for l in range(depth):
        h = _rms_rows(cur, D, 0, row(norm_g, l), f"norm_fwd{l}")
        proj = _matmul(h, w_in_p[l], "nn", F32, f"proj_fwd{l}", tm=1024, tn=640, tk=2048)
        y_pool = _pool_fwd(proj, offs, pool_w_f[l], row(pool_scale, l), f"pool_fwd{l}")
        cqn = _rms_rows(proj, L, offs["cq"], row(q_a_norm_g, l), f"cq_norm{l}")
        ckvn = _rms_rows(proj, L, offs["ckv"], row(kv_a_norm_g, l), f"ckv_norm{l}")
        qp = _matmul(cqn, w_uq_p[l], "nn", F32, f"uq_fwd{l}")
        kvp = _matmul(ckvn, w_ukv_p[l], "nn", F32, f"ukv_fwd{l}")
        qg = _pad_heads(row(mla_q_norm_g, l), QK_PAD)
        kg = _pad_heads(row(mla_k_norm_g, l), QK_PAD)
        Q, K, V = _mla_qk_fwd(qp, kvp, proj, offs, tabs, qg, kg, f"mla_qk_fwd{l}")
        o, y_mla, lse = _flash_fwd(Q, K, V, proj, offs, f"flash_fwd{l}")
        y_conv = _conv_fwd(proj, offs, conv_full[l], f"conv_fwd{l}")
        memn = _rms_rows(mems, D, 0, row(mem_norm_g, l), f"mem_norm{l}")
        mem_kv = _matmul(memn, w_mem_f, "nn", F32, f"mem_kv{l}", b_idx=(l,))
        y_mem = _xattn_fwd(proj, offs, mem_kv, row(xattn_q_norm_g, l), row(xattn_k_norm_g, l), f"xattn_fwd{l}")
        ys = (y_pool, y_mla, y_conv, y_mem)
        gb = gate_b[l].reshape(4, D)
        merged = _merge_fwd(ys, w_br_f, l, proj, offs, gb, f"merge_fwd{l}")
        nxt = _matmul(merged, w_out_f, "nn", F32, f"out_fwd{l}", res=cur, b_idx=(l,))
        saved.append(dict(x=cur, h=h, proj=proj, cqn=cqn, ckvn=ckvn, qp=qp, kvp=kvp, qg=qg, kg=kg, Q=Q, K=K, V=V, o=o, lse=lse,
                          memn=memn, mem_kv=mem_kv, ys=ys, gb=gb, merged=merged))
        cur = nxt

    gx, loss_acc = _loss_rows(cur, loss_target.reshape(S, D), "loss")
    loss = lax.psum(0.5 * loss_acc[0, 0], MESH_AXES)

    grads = {k: [None] * depth for k in ("norm_g", "w_in", "gate_b", "pool_w", "pool_scale", "q_a_norm_g", "kv_a_norm_g", "w_uq", "w_ukv",
                                         "mla_q_norm_g", "mla_k_norm_g", "conv_w", "mem_norm_g", "w_mem_kv", "xattn_q_norm_g",
                                         "xattn_k_norm_g", "w_branch", "w_out")}
    for l in reversed(range(depth)):
        sv = saved[l]
        proj, ys = sv["proj"], sv["ys"]
        dmerged = _matmul(gx, w_out_f, "nt", F32, f"out_bwd_x{l}", b_idx=(l,))
        grads["w_out"][l] = _matmul(sv["merged"], gx, "tn", BF16, f"out_bwd_w{l}")
        dt, dgp, dgb = _merge_bwd(ys, w_br_f, l, proj, offs, sv["gb"], dmerged, f"merge_bwd{l}")
        grads["gate_b"][l] = dgb.reshape(4 * D)
        dys, dwb = [], []
        for b in range(4):
            dys.append(_matmul(dt, w_br_f, "nt", F32, f"branch_bwd_x{l}_{b}", a_idx=(b,), b_idx=(l, b)))
            dwb.append(_matmul(ys[b], dt, "tn", BF16, f"branch_bwd_w{l}_{b}", b_idx=(b,)))
        grads["w_branch"][l] = jnp.stack(dwb)
        d_pz, e, d_pw, d_ps = _pool_bwd_a(proj, offs, pool_w_f[l], row(pool_scale, l), dys[0], f"pool_bwd_a{l}")
        d_pv = _pool_bwd_b(e, f"pool_bwd_b{l}")
        grads["pool_w"][l], grads["pool_scale"][l] = d_pw, d_ps[0]
        do, d_mz, delta = _mla_gate_bwd(sv["o"], proj, offs, dys[1], f"mla_gate_bwd{l}")
        dQ, dK, dV = _flash_bwd(sv["Q"], sv["K"], sv["V"], sv["lse"], delta, do, f"flash_bwd{l}")
        dqp, dkvp, d_kr, dqg, dkg = _mla_qk_bwd(sv["qp"], sv["kvp"], proj, offs, tabs, sv["qg"], sv["kg"], dQ, dK, dV, f"mla_qk_bwd{l}")
        grads["mla_q_norm_g"][l], grads["mla_k_norm_g"][l] = dqg[0, :QK_HEAD], dkg[0, :QK_HEAD]
        dcqn = _matmul(dqp, w_uq_p[l], "nt", F32, f"uq_bwd_x{l}")
        d_wuq = _matmul(sv["cqn"], dqp, "tn", BF16, f"uq_bwd_w{l}")
        dckvn = _matmul(dkvp, w_ukv_p[l], "nt", F32, f"ukv_bwd_x{l}")
        d_wukv = _matmul(sv["ckvn"], dkvp, "tn", BF16, f"ukv_bwd_w{l}")
        grads["w_uq"][l] = d_wuq.reshape(L, MLA_HEADS, QK_PAD)[..., :QK_HEAD].reshape(L, MLA_HEADS * QK_HEAD)
        grads["w_ukv"][l] = jnp.concatenate([d_wukv[:, :MLA_HEADS * QK_NOPE].reshape(L, MLA_HEADS, QK_NOPE),
                                             d_wukv[:, MLA_HEADS * QK_NOPE:].reshape(L, MLA_HEADS, V_HEAD)], axis=-1).reshape(L, -1)
        d_cq, dqa = _rms_rows_bwd(proj, L, offs["cq"], row(q_a_norm_g, l), dcqn, BF16, f"cq_norm_bwd{l}")
        d_ckv, dkva = _rms_rows_bwd(proj, L, offs["ckv"], row(kv_a_norm_g, l), dckvn, BF16, f"ckv_norm_bwd{l}")
        grads["q_a_norm_g"][l], grads["kv_a_norm_g"][l] = dqa[0], dkva[0]
        d_cb, d_cc, d_cx, d_cz, d_cw = _conv_bwd(proj, offs, conv_full[l], dys[2], f"conv_bwd{l}")
        grads["conv_w"][l] = d_cw
        d_xq, d_xz, d_memkv, dxqg, dxkg = _xattn_bwd(proj, offs, sv["mem_kv"], row(xattn_q_norm_g, l), row(xattn_k_norm_g, l), dys[3], f"xattn_bwd{l}")
        grads["xattn_q_norm_g"][l], grads["xattn_k_norm_g"][l] = dxqg[0], dxkg[0]
        grads["w_mem_kv"][l] = _matmul(sv["memn"], d_memkv, "tn", BF16, f"mem_kv_bwd_w{l}")
        dmemn = _matmul(d_memkv, w_mem_f, "nt", F32, f"mem_kv_bwd_x{l}", b_idx=(l,))
        grads["mem_norm_g"][l] = _norm_g_grad(mems, row(mem_norm_g, l), dmemn, f"mem_norm_bwd{l}")[0]
        dproj = jnp.concatenate([d_pv, d_pz, d_cq, d_ckv, d_mz, d_cb, d_cc, d_cx, d_cz, d_xq, d_xz, dgp[0], dgp[1], dgp[2], dgp[3], d_kr], axis=1)
        dh = _matmul(dproj, w_in_p[l], "nt", F32, f"proj_bwd_x{l}", tm=1024, tn=1024, tk=3712)
        d_win_p = _matmul(sv["h"], dproj, "tn", BF16, f"proj_bwd_w{l}", tm=2048, tn=640, tk=2048)
        n_p = offs["n"]
        grads["w_in"][l] = jnp.concatenate([d_win_p[:, :seg_a], d_win_p[:, n_p - 128:n_p - 128 + QK_ROPE], d_win_p[:, seg_a:n_p - 128]], axis=1)
        gx, dng = _rms_rows_bwd(sv["x"], D, 0, row(norm_g, l), dh, F32, f"norm_bwd{l}", res=gx)
        grads["norm_g"][l] = dng[0]
    grad_x = gx.reshape(1, S, D)
    full = {k: jnp.stack(v) for k, v in grads.items()}

    moments = dict(w_in=(m_w_in, v_w_in), pool_w=(m_pool_w, v_pool_w), w_uq=(m_w_uq, v_w_uq), w_ukv=(m_w_ukv, v_w_ukv),
                   conv_w=(m_conv_w, v_conv_w), w_mem_kv=(m_w_mem_kv, v_w_mem_kv), w_branch=(m_w_branch, v_w_branch), w_out=(m_w_out, v_w_out))
    names_sh = list(shard_axis)
    my_c = lax.axis_index("c")
    keep, give = [], []
    for k in names_sh:
        C = sharded[k].shape[-1]
        by_core = _reshard(full[k], shard_axis[k]).reshape(4, 2, -1, C).astype(BF16)
        keep.append(lax.dynamic_index_in_dim(by_core, my_c, axis=1, keepdims=False))
        give.append(lax.dynamic_index_in_dim(by_core, 1 - my_c, axis=1, keepdims=False))
    given = _sibling_swap(give, "pair_swap_grads")
    pair = []
    for k, mine, theirs in zip(names_sh, keep, given):
        R, C = mine.shape[1], mine.shape[2]
        pair.append(_pair_sum(mine.reshape(4 * R, C), theirs.reshape(4 * R, C), f"pair_sum_{k}").reshape(4, R, C))
    received = _chip_exchange(pair, "chip_exchange_grads")
    results = {}
    for k, parts in zip(names_sh, received):
        w = sharded[k]
        m, v = moments[k]
        flat = lambda a: a.reshape(-1, w.shape[-1])
        outs = _adamw(parts, flat(w), flat(m), flat(v), f"adamw_{k}")
        results[k] = [o.reshape(w.shape) for o in outs]

    small = dict(norm_g=(norm_g, m_norm_g, v_norm_g), gate_b=(gate_b, m_gate_b, v_gate_b), pool_scale=(pool_scale, m_pool_scale, v_pool_scale),
                 q_a_norm_g=(q_a_norm_g, m_q_a_norm_g, v_q_a_norm_g), kv_a_norm_g=(kv_a_norm_g, m_kv_a_norm_g, v_kv_a_norm_g),
                 mla_q_norm_g=(mla_q_norm_g, m_mla_q_norm_g, v_mla_q_norm_g), mla_k_norm_g=(mla_k_norm_g, m_mla_k_norm_g, v_mla_k_norm_g),
                 mem_norm_g=(mem_norm_g, m_mem_norm_g, v_mem_norm_g), xattn_q_norm_g=(xattn_q_norm_g, m_xattn_q_norm_g, v_xattn_q_norm_g),
                 xattn_k_norm_g=(xattn_k_norm_g, m_xattn_k_norm_g, v_xattn_k_norm_g))
    names_sm = list(small)
    sizes = [small[k][0].size for k in names_sm]
    total = sum(sizes)
    rows = -(-total // (8 * 128)) * 8

    def pack(arrs):
        flat = jnp.concatenate([a.reshape(-1) for a in arrs])
        return jnp.pad(flat, (0, rows * 128 - total)).reshape(rows, 128)

    g_small = _gather_two_level([pack([full[k] for k in names_sm])], "gather_small_grads")[0]
    outs = _adamw(g_small, pack([small[k][0] for k in names_sm]), pack([small[k][1] for k in names_sm]),
                  pack([small[k][2] for k in names_sm]), "adamw_small")
    for kind, o in enumerate(outs):
        flat, pos = o.reshape(-1), 0
        for k, n in zip(names_sm, sizes):
            results.setdefault(k, [None] * 4)[kind] = flat[pos:pos + n].reshape(small[k][0].shape)
            pos += n

    order = ["norm_g", "w_in", "gate_b", "pool_w", "pool_scale", "q_a_norm_g", "kv_a_norm_g", "w_uq", "w_ukv", "mla_q_norm_g",
             "mla_k_norm_g", "conv_w", "mem_norm_g", "w_mem_kv", "xattn_q_norm_g", "xattn_k_norm_g", "w_branch", "w_out"]
    return (loss, grad_x, *[results[k][0] for k in order], *[results[k][1] for k in order],
            *[results[k][2] for k in order], *[results[k][3] for k in order])
```

```python
import functools

import jax
import jax.numpy as jnp
from jax import lax
from jax.experimental import pallas as pl
from jax.experimental.pallas import tpu as pltpu

F32 = jnp.float32
BF16 = jnp.bfloat16
MESH_AXES = ("x", "y", "c")
N_DEV = 8

BRANCH_W = 1024
POOL_WINDOWS = (2, 4, 8, 16)
POOL_GW = BRANCH_W // 4
MLA_HEADS = 8
QK_NOPE = 128
QK_ROPE = 64
QK_HEAD = QK_NOPE + QK_ROPE
QK_PAD = 256
V_HEAD = 128
XATTN_HEADS = 4
XATTN_HD = BRANCH_W // XATTN_HEADS
ROPE_THETA = 10000.0
EPS = 1e-6
ADAM_LR = 0.001
ADAM_B1 = 0.9
ADAM_B2 = 0.999
ADAM_EPS = 1e-08
ADAM_WD = 0.01
ADAM_STEP = 10
HALO = 16
NEG = -1e30
SCORE_SCALE_LOG2 = (QK_HEAD ** -0.5) * 1.4426950408889634
VMEM_LIMIT = 56 << 20


def _pick(n, pref, align):
    if n <= pref:
        return n
    t = (pref // align) * align
    while t >= align:
        if n % t == 0:
            return t
        t -= align
    return n


def _params(sem):
    return pltpu.CompilerParams(dimension_semantics=sem, vmem_limit_bytes=VMEM_LIMIT)


def _silu(z):
    return z * (1.0 / (1.0 + jnp.exp(-z)))


def _silu_and_grad(z):
    s = 1.0 / (1.0 + jnp.exp(-z))
    return z * s, s * (1.0 + z * (1.0 - s))


def _dot(a, b):
    return jnp.dot(a.astype(BF16), b.astype(BF16), preferred_element_type=F32)


def _dot_nt(a, b):
    return lax.dot_general(a.astype(BF16), b.astype(BF16), (((1,), (1,)), ((), ())), preferred_element_type=F32)


def _dot_tn(a, b):
    return lax.dot_general(a.astype(BF16), b.astype(BF16), (((0,), (0,)), ((), ())), preferred_element_type=F32)


def _rms(x, g, n):
    r = lax.rsqrt(jnp.sum(x * x, axis=-1, keepdims=True) / n + EPS)
    return x * r * g, r


def _rms_bwd(x, g, r, dy, n):
    dyg = dy * g
    dx = r * dyg - x * (r * r * r) * (jnp.sum(dyg * x, axis=-1, keepdims=True) / n)
    dg = jnp.sum(dy * x * r, axis=0, keepdims=True)
    return dx, dg


def _col(tm, w, off):
    assert off % w == 0
    cb = off // w
    return pl.BlockSpec((tm, w), lambda i: (i, cb))


def _whole(shape):
    nd = len(shape)
    return pl.BlockSpec(shape, lambda *_: (0,) * nd)


def _lead_spec(block, fn, idx):
    if not idx:
        return pl.BlockSpec(block, fn)
    return pl.BlockSpec((None,) * len(idx) + block, lambda i, j, k: tuple(idx) + fn(i, j, k))


def _matmul(a, b, mode, out_dtype, name, tm=1024, tn=1024, tk=2048, res=None, a_idx=(), b_idx=()):
    a_shape, b_shape = a.shape[len(a_idx):], b.shape[len(b_idx):]
    if mode == "nn":
        (M, K), N = a_shape, b_shape[1]
    elif mode == "nt":
        (M, K), N = a_shape, b_shape[0]
    else:
        (K, M), N = a_shape, b_shape[1]
    tm, tn, tk = _pick(M, tm, 128), _pick(N, tn, 128), _pick(K, tk, 128)
    nk = K // tk
    if mode == "nn":
        a_spec = _lead_spec((tm, tk), lambda i, j, k: (i, k), a_idx)
        b_spec = _lead_spec((tk, tn), lambda i, j, k: (k, j), b_idx)
        op = _dot
    elif mode == "nt":
        a_spec = _lead_spec((tm, tk), lambda i, j, k: (i, k), a_idx)
        b_spec = _lead_spec((tn, tk), lambda i, j, k: (j, k), b_idx)
        op = _dot_nt
    else:
        a_spec = _lead_spec((tk, tm), lambda i, j, k: (k, i), a_idx)
        b_spec = _lead_spec((tk, tn), lambda i, j, k: (k, j), b_idx)
        op = _dot_tn
    o_spec = pl.BlockSpec((tm, tn), lambda i, j, k: (i, j))
    has_res = res is not None

    def body(*refs):
        a_ref, b_ref = refs[0], refs[1]
        res_ref = refs[2] if has_res else None
        o_ref = refs[2 + has_res]
        part = op(a_ref[...], b_ref[...])

        def finish(acc):
            if has_res:
                acc = res_ref[...] + acc
            o_ref[...] = acc.astype(out_dtype)

        if nk == 1:
            finish(part)
        else:
            acc_ref = refs[3 + has_res]
            k = pl.program_id(2)

            @pl.when(k == 0)
            def _():
                acc_ref[...] = part

            @pl.when(k > 0)
            def _():
                acc_ref[...] += part

            @pl.when(k == nk - 1)
            def _():
                finish(acc_ref[...])

    ins = [a, b] + ([res] if has_res else [])
    in_specs = [a_spec, b_spec] + ([o_spec] if has_res else [])
    scratch = [pltpu.VMEM((tm, tn), F32)] if nk > 1 else []
    return pl.pallas_call(
        body, grid=(M // tm, N // tn, nk), in_specs=in_specs, out_specs=o_spec,
        out_shape=jax.ShapeDtypeStruct((M, N), out_dtype), scratch_shapes=scratch, name=name,
        compiler_params=_params(("parallel", "parallel", "arbitrary")),
    )(*ins)


def _rms_rows(x, width, off, g, name):
    S = x.shape[0]
    tm = _pick(S, 512, 16)

    def body(x_ref, g_ref, o_ref):
        y, _ = _rms(x_ref[...], g_ref[...], width)
        o_ref[...] = y.astype(BF16)

    return pl.pallas_call(
        body, grid=(S // tm,), in_specs=[_col(tm, width, off), _whole((1, width))], out_specs=_col(tm, width, 0),
        out_shape=jax.ShapeDtypeStruct((S, width), BF16), name=name, compiler_params=_params(("parallel",)),
    )(x, g)


def _rms_rows_bwd(x, width, off, g, dy, out_dtype, name, res=None):
    S = x.shape[0]
    tm = _pick(S, 256, 16)
    has_res = res is not None

    def body(*refs):
        x_ref, g_ref, dy_ref = refs[:3]
        res_ref = refs[3] if has_res else None
        dx_ref, dg_ref = refs[3 + has_res:]
        xv, gv = x_ref[...], g_ref[...]
        _, r = _rms(xv, gv, width)
        dx, dg = _rms_bwd(xv, gv, r, dy_ref[...], width)
        if has_res:
            dx = res_ref[...] + dx
        dx_ref[...] = dx.astype(out_dtype)

        @pl.when(pl.program_id(0) == 0)
        def _():
            dg_ref[...] = jnp.zeros_like(dg_ref)

        dg_ref[...] += dg

    ins = [x, g, dy] + ([res] if has_res else [])
    in_specs = [_col(tm, width, off), _whole((1, width)), _col(tm, width, 0)] + ([_col(tm, width, 0)] if has_res else [])
    return pl.pallas_call(
        body, grid=(S // tm,), in_specs=in_specs, out_specs=[_col(tm, width, 0), _whole((1, width))],
        out_shape=[jax.ShapeDtypeStruct((S, width), out_dtype), jax.ShapeDtypeStruct((1, width), F32)], name=name,
        compiler_params=_params(("arbitrary",)),
    )(*ins)


def _norm_g_grad(x, g, dy, name):
    S, width = x.shape
    tm = _pick(S, 256, 16)

    def body(x_ref, g_ref, dy_ref, dg_ref):
        xv = x_ref[...]
        _, r = _rms(xv, g_ref[...], width)

        @pl.when(pl.program_id(0) == 0)
        def _():
            dg_ref[...] = jnp.zeros_like(dg_ref)

        dg_ref[...] += jnp.sum(dy_ref[...] * xv * r, axis=0, keepdims=True)

    return pl.pallas_call(
        body, grid=(S // tm,), in_specs=[_col(tm, width, 0), _whole((1, width)), _col(tm, width, 0)],
        out_specs=_whole((1, width)), out_shape=jax.ShapeDtypeStruct((1, width), F32), name=name,
        compiler_params=_params(("arbitrary",)),
    )(x, g, dy)


def _loss_rows(y, target, name):
    S, D = y.shape
    tm = _pick(S, 512, 16)

    def body(y_ref, t_ref, g_ref, l_ref):
        err = y_ref[...] - t_ref[...]
        g_ref[...] = err / D

        @pl.when(pl.program_id(0) == 0)
        def _():
            l_ref[...] = jnp.zeros_like(l_ref)

        row = jnp.sum(err * err, axis=-1, keepdims=True) / D
        l_ref[...] += jnp.sum(row, axis=0, keepdims=True)

    return pl.pallas_call(
        body, grid=(S // tm,), in_specs=[_col(tm, D, 0), _col(tm, D, 0)], out_specs=[_col(tm, D, 0), _whole((1, 1))],
        out_shape=[jax.ShapeDtypeStruct((S, D), F32), jax.ShapeDtypeStruct((1, 1), F32)], name=name,
        compiler_params=_params(("arbitrary",)),
    )(y, target)


def _row_index(i, ts):
    return i * ts + lax.broadcasted_iota(jnp.int32, (ts, 1), 0)


def _pool_mixed(scr, pv_ref, g, t, ts):
    w, lo = POOL_WINDOWS[g], g * POOL_GW
    win = scr[HALO:HALO + ts, lo:lo + POOL_GW]
    for k in range(1, w):
        win = win + scr[HALO - k:HALO - k + ts, lo:lo + POOL_GW]
    cnt = jnp.minimum(t + 1, w).astype(F32)
    return win / cnt - pv_ref[:, lo:lo + POOL_GW], cnt


def _halo_before(ts, off):
    nb = ts // HALO
    cb = off // BRANCH_W
    return pl.BlockSpec((HALO, BRANCH_W), lambda i: (jnp.maximum(i * nb - 1, 0), cb))


def _halo_after(ts, off, n_tiles):
    nb = ts // HALO
    cb = off // BRANCH_W
    last = n_tiles * nb - 1
    return pl.BlockSpec((HALO, BRANCH_W), lambda i: (jnp.minimum((i + 1) * nb, last), cb))


def _pool_fwd(proj, offs, pool_w, pool_scale, name):
    S = proj.shape[0]
    ts = _pick(S, 512, HALO)

    def body(pv_ref, halo_ref, pz_ref, w_ref, sc_ref, y_ref, scr):
        i = pl.program_id(0)
        scr[0:HALO, :] = jnp.where(i > 0, halo_ref[...], 0.0)
        scr[HALO:HALO + ts, :] = pv_ref[...]
        t = _row_index(i, ts)
        for g in range(4):
            lo = g * POOL_GW
            mixed, _ = _pool_mixed(scr, pv_ref, g, t, ts)
            out = _dot(mixed, w_ref[g])
            y_ref[:, lo:lo + POOL_GW] = (out * sc_ref[:, lo:lo + POOL_GW] * _silu(pz_ref[:, lo:lo + POOL_GW])).astype(BF16)

    return pl.pallas_call(
        body, grid=(S // ts,),
        in_specs=[_col(ts, BRANCH_W, offs["pv"]), _halo_before(ts, offs["pv"]), _col(ts, BRANCH_W, offs["pz"]),
                  _whole((4, POOL_GW, POOL_GW)), _whole((1, BRANCH_W))],
        out_specs=_col(ts, BRANCH_W, 0), out_shape=jax.ShapeDtypeStruct((S, BRANCH_W), BF16),
        scratch_shapes=[pltpu.VMEM((HALO + ts, BRANCH_W), F32)], name=name, compiler_params=_params(("parallel",)),
    )(proj, proj, proj, pool_w, pool_scale)


def _pool_bwd_a(proj, offs, pool_w, pool_scale, dy, name):
    S = proj.shape[0]
    ts = _pick(S, 512, HALO)

    def body(pv_ref, halo_ref, pz_ref, w_ref, sc_ref, dy_ref, dpz_ref, e_ref, dw_ref, dsc_ref, scr):
        i = pl.program_id(0)
        scr[0:HALO, :] = jnp.where(i > 0, halo_ref[...], 0.0)
        scr[HALO:HALO + ts, :] = pv_ref[...]
        t = _row_index(i, ts)

        @pl.when(i == 0)
        def _():
            dw_ref[...] = jnp.zeros_like(dw_ref)
            dsc_ref[...] = jnp.zeros_like(dsc_ref)

        for g in range(4):
            lo = g * POOL_GW
            sl = slice(lo, lo + POOL_GW)
            mixed, cnt = _pool_mixed(scr, pv_ref, g, t, ts)
            out = _dot(mixed, w_ref[g])
            sz, dsz = _silu_and_grad(pz_ref[:, sl])
            dyv, sc = dy_ref[:, sl], sc_ref[:, sl]
            d_out = dyv * sc * sz
            dsc_ref[:, sl] += jnp.sum(dyv * out * sz, axis=0, keepdims=True)
            dpz_ref[:, sl] = (dyv * out * sc * dsz).astype(BF16)
            dw_ref[g] += _dot_tn(mixed, d_out)
            e_ref[:, sl] = _dot_nt(d_out, w_ref[g]) / cnt

    return pl.pallas_call(
        body, grid=(S // ts,),
        in_specs=[_col(ts, BRANCH_W, offs["pv"]), _halo_before(ts, offs["pv"]), _col(ts, BRANCH_W, offs["pz"]),
                  _whole((4, POOL_GW, POOL_GW)), _whole((1, BRANCH_W)), _col(ts, BRANCH_W, 0)],
        out_specs=[_col(ts, BRANCH_W, 0), _col(ts, BRANCH_W, 0), _whole((4, POOL_GW, POOL_GW)), _whole((1, BRANCH_W))],
        out_shape=[jax.ShapeDtypeStruct((S, BRANCH_W), BF16), jax.ShapeDtypeStruct((S, BRANCH_W), F32),
                   jax.ShapeDtypeStruct((4, POOL_GW, POOL_GW), F32), jax.ShapeDtypeStruct((1, BRANCH_W), F32)],
        scratch_shapes=[pltpu.VMEM((HALO + ts, BRANCH_W), F32)], name=name, compiler_params=_params(("arbitrary",)),
    )(proj, proj, proj, pool_w, pool_scale, dy)


def _pool_bwd_b(e, name):
    S = e.shape[0]
    ts = _pick(S, 512, HALO)
    n_tiles = S // ts

    def body(e_ref, halo_ref, dpv_ref, scr):
        i = pl.program_id(0)
        scr[0:ts, :] = e_ref[...]
        scr[ts:ts + HALO, :] = jnp.where(i < n_tiles - 1, halo_ref[...], 0.0)
        t = _row_index(i, ts)
        for g in range(4):
            w, lo = POOL_WINDOWS[g], g * POOL_GW
            acc = scr[0:ts, lo:lo + POOL_GW]
            ev = acc
            for k in range(1, w):
                acc = acc + scr[k:k + ts, lo:lo + POOL_GW]
            cnt = jnp.minimum(t + 1, w).astype(F32)
            dpv_ref[:, lo:lo + POOL_GW] = (acc - ev * cnt).astype(BF16)

    return pl.pallas_call(
        body, grid=(n_tiles,), in_specs=[_col(ts, BRANCH_W, 0), _halo_after(ts, 0, n_tiles)], out_specs=_col(ts, BRANCH_W, 0),
        out_shape=jax.ShapeDtypeStruct((S, BRANCH_W), BF16), scratch_shapes=[pltpu.VMEM((ts + HALO, BRANCH_W), F32)],
        name=name, compiler_params=_params(("parallel",)),
    )(e, e)


def _conv_fwd(proj, offs, conv_w, name):
    S = proj.shape[0]
    ts = _pick(S, 512, HALO)

    def body(cb_ref, cc_ref, cx_ref, cz_ref, hc_ref, hx_ref, w_ref, y_ref, scr):
        i = pl.program_id(0)
        scr[0:HALO, :] = jnp.where(i > 0, hc_ref[...] * hx_ref[...], 0.0)
        scr[HALO:HALO + ts, :] = cc_ref[...] * cx_ref[...]
        y = w_ref[0:1, :] * scr[HALO - 2:HALO - 2 + ts, :] + w_ref[1:2, :] * scr[HALO - 1:HALO - 1 + ts, :] + w_ref[2:3, :] * scr[HALO:HALO + ts, :]
        y_ref[...] = (cb_ref[...] * y * _silu(cz_ref[...])).astype(BF16)

    return pl.pallas_call(
        body, grid=(S // ts,),
        in_specs=[_col(ts, BRANCH_W, offs["cb"]), _col(ts, BRANCH_W, offs["cc"]), _col(ts, BRANCH_W, offs["cx"]),
                  _col(ts, BRANCH_W, offs["cz"]), _halo_before(ts, offs["cc"]), _halo_before(ts, offs["cx"]), _whole((3, BRANCH_W))],
        out_specs=_col(ts, BRANCH_W, 0), out_shape=jax.ShapeDtypeStruct((S, BRANCH_W), BF16),
        scratch_shapes=[pltpu.VMEM((HALO + ts, BRANCH_W), F32)], name=name, compiler_params=_params(("parallel",)),
    )(proj, proj, proj, proj, proj, proj, conv_w)


def _conv_bwd(proj, offs, conv_w, dy, name):
    S = proj.shape[0]
    ts = _pick(S, 256, HALO)
    n_tiles = S // ts

    def body(cb_ref, cc_ref, cx_ref, cz_ref, hc_ref, hx_ref, w_ref, dy_ref, ady_ref, acb_ref, acz_ref,
             dcb_ref, dcc_ref, dcx_ref, dcz_ref, dw_ref, scr_u, scr_d):
        i = pl.program_id(0)
        u = cc_ref[...] * cx_ref[...]
        scr_u[0:HALO, :] = jnp.where(i > 0, hc_ref[...] * hx_ref[...], 0.0)
        scr_u[HALO:HALO + ts, :] = u
        u1 = scr_u[HALO - 1:HALO - 1 + ts, :]
        u2 = scr_u[HALO - 2:HALO - 2 + ts, :]
        y = w_ref[0:1, :] * u2 + w_ref[1:2, :] * u1 + w_ref[2:3, :] * u
        sz, dsz = _silu_and_grad(cz_ref[...])
        dyv, cb = dy_ref[...], cb_ref[...]
        dcb_ref[...] = (dyv * y * sz).astype(BF16)
        dcz_ref[...] = (dyv * cb * y * dsz).astype(BF16)
        d_y = dyv * cb * sz
        scr_d[0:ts, :] = d_y
        scr_d[ts:ts + HALO, :] = jnp.where(i < n_tiles - 1, ady_ref[...] * acb_ref[...] * _silu(acz_ref[...]), 0.0)
        du = w_ref[2:3, :] * d_y + w_ref[1:2, :] * scr_d[1:1 + ts, :] + w_ref[0:1, :] * scr_d[2:2 + ts, :]
        dcc_ref[...] = (du * cx_ref[...]).astype(BF16)
        dcx_ref[...] = (du * cc_ref[...]).astype(BF16)

        @pl.when(i == 0)
        def _():
            dw_ref[...] = jnp.zeros_like(dw_ref)

        dw_ref[0:1, :] += jnp.sum(d_y * u2, axis=0, keepdims=True)
        dw_ref[1:2, :] += jnp.sum(d_y * u1, axis=0, keepdims=True)
        dw_ref[2:3, :] += jnp.sum(d_y * u, axis=0, keepdims=True)

    row = lambda off: _col(ts, BRANCH_W, off)
    return pl.pallas_call(
        body, grid=(n_tiles,),
        in_specs=[row(offs["cb"]), row(offs["cc"]), row(offs["cx"]), row(offs["cz"]), _halo_before(ts, offs["cc"]),
                  _halo_before(ts, offs["cx"]), _whole((3, BRANCH_W)), row(0), _halo_after(ts, 0, n_tiles),
                  _halo_after(ts, offs["cb"], n_tiles), _halo_after(ts, offs["cz"], n_tiles)],
        out_specs=[row(0), row(0), row(0), row(0), _whole((3, BRANCH_W))],
        out_shape=[jax.ShapeDtypeStruct((S, BRANCH_W), BF16)] * 4 + [jax.ShapeDtypeStruct((3, BRANCH_W), F32)],
        scratch_shapes=[pltpu.VMEM((HALO + ts, BRANCH_W), F32), pltpu.VMEM((ts + HALO, BRANCH_W), F32)], name=name,
        compiler_params=_params(("arbitrary",)),
    )(proj, proj, proj, proj, proj, proj, conv_w, dy, dy, proj, proj)


def _xattn_head(xq, kraw, v, qg, kg):
    qn, rq = _rms(xq, qg, XATTN_HD)
    kn, rk = _rms(kraw, kg, XATTN_HD)
    s = _dot_nt(qn, kn) * (XATTN_HD ** -0.5)
    e = jnp.exp(s - jnp.max(s, axis=-1, keepdims=True))
    p = e / jnp.sum(e, axis=-1, keepdims=True)
    return qn, rq, kn, rk, p, _dot(p, v)


def _xattn_fwd(proj, offs, mem_kv, qg, kg, name):
    S, M = proj.shape[0], mem_kv.shape[0]
    tm = _pick(S, 512, 16)

    def body(xq_ref, xz_ref, kv_ref, qg_ref, kg_ref, y_ref):
        for h in range(XATTN_HEADS):
            sl = slice(h * XATTN_HD, (h + 1) * XATTN_HD)
            vs = slice(BRANCH_W + h * XATTN_HD, BRANCH_W + (h + 1) * XATTN_HD)
            o = _xattn_head(xq_ref[:, sl], kv_ref[:, sl], kv_ref[:, vs], qg_ref[...], kg_ref[...])[-1]
            y_ref[:, sl] = (o * _silu(xz_ref[:, sl])).astype(BF16)

    return pl.pallas_call(
        body, grid=(S // tm,),
        in_specs=[_col(tm, BRANCH_W, offs["xq"]), _col(tm, BRANCH_W, offs["xz"]), _whole((M, 2 * BRANCH_W)),
                  _whole((1, XATTN_HD)), _whole((1, XATTN_HD))],
        out_specs=_col(tm, BRANCH_W, 0), out_shape=jax.ShapeDtypeStruct((S, BRANCH_W), BF16), name=name,
        compiler_params=_params(("parallel",)),
    )(proj, proj, mem_kv, qg, kg)


def _xattn_bwd(proj, offs, mem_kv, qg, kg, dy, name):
    S, M = proj.shape[0], mem_kv.shape[0]
    tm = _pick(S, 256, 16)
    n_tiles = S // tm

    def body(xq_ref, xz_ref, kv_ref, qg_ref, kg_ref, dy_ref, dxq_ref, dxz_ref, dkv_ref, dqg_ref, dkg_ref, dkn_acc):
        i = pl.program_id(0)

        @pl.when(i == 0)
        def _():
            dkv_ref[...] = jnp.zeros_like(dkv_ref)
            dqg_ref[...] = jnp.zeros_like(dqg_ref)
            dkg_ref[...] = jnp.zeros_like(dkg_ref)
            dkn_acc[...] = jnp.zeros_like(dkn_acc)

        qg_v, kg_v = qg_ref[...], kg_ref[...]
        for h in range(XATTN_HEADS):
            sl = slice(h * XATTN_HD, (h + 1) * XATTN_HD)
            vs = slice(BRANCH_W + h * XATTN_HD, BRANCH_W + (h + 1) * XATTN_HD)
            xq, v = xq_ref[:, sl], kv_ref[:, vs]
            qn, rq, kn, rk, p, o = _xattn_head(xq, kv_ref[:, sl], v, qg_v, kg_v)
            sz, dsz = _silu_and_grad(xz_ref[:, sl])
            dyv = dy_ref[:, sl]
            dxz_ref[:, sl] = (dyv * o * dsz).astype(BF16)
            do = dyv * sz
            dkv_ref[:, vs] += _dot_tn(p, do)
            dp = _dot_nt(do, v)
            ds = p * (dp - jnp.sum(dp * p, axis=-1, keepdims=True)) * (XATTN_HD ** -0.5)
            dqn = _dot(ds, kn)
            dkn_acc[:, sl] += _dot_tn(ds, qn)
            dxq, dqg = _rms_bwd(xq, qg_v, rq, dqn, XATTN_HD)
            dxq_ref[:, sl] = dxq.astype(BF16)
            dqg_ref[...] += dqg

        @pl.when(i == n_tiles - 1)
        def _():
            for h in range(XATTN_HEADS):
                sl = slice(h * XATTN_HD, (h + 1) * XATTN_HD)
                kraw = kv_ref[:, sl]
                _, rk = _rms(kraw, kg_v, XATTN_HD)
                dk, dkg = _rms_bwd(kraw, kg_v, rk, dkn_acc[:, sl], XATTN_HD)
                dkv_ref[:, sl] = dk
                dkg_ref[...] += dkg

    return pl.pallas_call(
        body, grid=(n_tiles,),
        in_specs=[_col(tm, BRANCH_W, offs["xq"]), _col(tm, BRANCH_W, offs["xz"]), _whole((M, 2 * BRANCH_W)),
                  _whole((1, XATTN_HD)), _whole((1, XATTN_HD)), _col(tm, BRANCH_W, 0)],
        out_specs=[_col(tm, BRANCH_W, 0), _col(tm, BRANCH_W, 0), _whole((M, 2 * BRANCH_W)), _whole((1, XATTN_HD)), _whole((1, XATTN_HD))],
        out_shape=[jax.ShapeDtypeStruct((S, BRANCH_W), BF16), jax.ShapeDtypeStruct((S, BRANCH_W), BF16),
                   jax.ShapeDtypeStruct((M, 2 * BRANCH_W), F32), jax.ShapeDtypeStruct((1, XATTN_HD), F32), jax.ShapeDtypeStruct((1, XATTN_HD), F32)],
        scratch_shapes=[pltpu.VMEM((M, BRANCH_W), F32)], name=name, compiler_params=_params(("arbitrary",)),
    )(proj, proj, mem_kv, qg, kg, dy)


def _rope(c, tc, ta, tb):
    return c * tc + pltpu.roll(c, 96, 1) * ta + pltpu.roll(c, 32, 1) * tb


def _rope_t(d, tc, ta, tb):
    return d * tc + pltpu.roll(d * ta, 32, 1) + pltpu.roll(d * tb, 96, 1)


def _mla_qk_fwd(qp, kvp, proj, offs, tabs, qg, kg, name):
    S = qp.shape[0]
    tm = _pick(S, 256, 16)
    W = MLA_HEADS * QK_PAD

    def body(q_ref, kv_ref, kr_ref, tc_ref, ta_ref, tb_ref, qg_ref, kg_ref, qo_ref, ko_ref, vo_ref):
        tc, ta, tb = tc_ref[...], ta_ref[...], tb_ref[...]
        qg_v, kg_v = qg_ref[...], kg_ref[...]
        kr = kr_ref[...]
        kr_ss = jnp.sum(kr * kr, axis=-1, keepdims=True)
        for h in range(MLA_HEADS):
            sl = slice(h * QK_PAD, (h + 1) * QK_PAD)
            qn, _ = _rms(q_ref[:, sl], qg_v, QK_HEAD)
            qo_ref[:, sl] = jnp.concatenate([qn[:, :QK_NOPE], _rope(qn[:, QK_NOPE:], tc, ta, tb)], axis=1).astype(BF16)
            kn = kv_ref[:, h * QK_NOPE:(h + 1) * QK_NOPE]
            r = lax.rsqrt((jnp.sum(kn * kn, axis=-1, keepdims=True) + kr_ss) / QK_HEAD + EPS)
            ko_ref[:, sl] = jnp.concatenate([kn * r * kg_v[:, :QK_NOPE], _rope(kr * r * kg_v[:, QK_NOPE:], tc, ta, tb)], axis=1).astype(BF16)
        vo_ref[...] = kv_ref[:, MLA_HEADS * QK_NOPE:].astype(BF16)

    return pl.pallas_call(
        body, grid=(S // tm,),
        in_specs=[_col(tm, W, 0), _col(tm, W, 0), _col(tm, 128, offs["kr"]), _col(tm, 128, 0), _col(tm, 128, 0), _col(tm, 128, 0),
                  _whole((1, QK_PAD)), _whole((1, QK_PAD))],
        out_specs=[_col(tm, W, 0), _col(tm, W, 0), _col(tm, BRANCH_W, 0)],
        out_shape=[jax.ShapeDtypeStruct((S, W), BF16), jax.ShapeDtypeStruct((S, W), BF16), jax.ShapeDtypeStruct((S, BRANCH_W), BF16)],
        name=name, compiler_params=_params(("parallel",)),
    )(qp, kvp, proj, *tabs, qg, kg)


def _mla_qk_bwd(qp, kvp, proj, offs, tabs, qg, kg, dQ, dK, dV, name):
    S = qp.shape[0]
    tm = _pick(S, 256, 16)
    W = MLA_HEADS * QK_PAD

    def body(q_ref, kv_ref, kr_ref, tc_ref, ta_ref, tb_ref, qg_ref, kg_ref, dQ_ref, dK_ref, dV_ref,
             dq_ref, dkv_ref, dkr_ref, dqg_ref, dkg_ref):
        @pl.when(pl.program_id(0) == 0)
        def _():
            dqg_ref[...] = jnp.zeros_like(dqg_ref)
            dkg_ref[...] = jnp.zeros_like(dkg_ref)

        tc, ta, tb = tc_ref[...], ta_ref[...], tb_ref[...]
        qg_v, kg_v = qg_ref[...], kg_ref[...]
        kg_n, kg_r = kg_v[:, :QK_NOPE], kg_v[:, QK_NOPE:]
        kr = kr_ref[...]
        kr_ss = jnp.sum(kr * kr, axis=-1, keepdims=True)
        dkr = jnp.zeros_like(kr)
        dkg_n = jnp.zeros((1, QK_NOPE), F32)
        dkg_r = jnp.zeros((1, QK_NOPE), F32)
        for h in range(MLA_HEADS):
            sl = slice(h * QK_PAD, (h + 1) * QK_PAD)
            qv = q_ref[:, sl]
            _, rq = _rms(qv, qg_v, QK_HEAD)
            dQv = dQ_ref[:, sl]
            dqn = jnp.concatenate([dQv[:, :QK_NOPE], _rope_t(dQv[:, QK_NOPE:], tc, ta, tb)], axis=1)
            dq, dqg = _rms_bwd(qv, qg_v, rq, dqn, QK_HEAD)
            dq_ref[:, sl] = dq.astype(BF16)
            dqg_ref[...] += dqg
            ns = slice(h * QK_NOPE, (h + 1) * QK_NOPE)
            kn = kv_ref[:, ns]
            r = lax.rsqrt((jnp.sum(kn * kn, axis=-1, keepdims=True) + kr_ss) / QK_HEAD + EPS)
            dKv = dK_ref[:, sl]
            dyn = dKv[:, :QK_NOPE]
            dyr = _rope_t(dKv[:, QK_NOPE:], tc, ta, tb)
            proj_s = (jnp.sum(dyn * kg_n * kn, axis=-1, keepdims=True) + jnp.sum(dyr * kg_r * kr, axis=-1, keepdims=True)) / QK_HEAD
            r3 = r * r * r
            dkv_ref[:, ns] = (r * dyn * kg_n - kn * r3 * proj_s).astype(BF16)
            dkr = dkr + (r * dyr * kg_r - kr * r3 * proj_s)
            dkg_n = dkg_n + jnp.sum(dyn * kn * r, axis=0, keepdims=True)
            dkg_r = dkg_r + jnp.sum(dyr * kr * r, axis=0, keepdims=True)
        dkg_ref[...] += jnp.concatenate([dkg_n, dkg_r], axis=1)
        dkr_ref[...] = dkr.astype(BF16)
        dkv_ref[:, MLA_HEADS * QK_NOPE:] = dV_ref[...].astype(BF16)

    return pl.pallas_call(
        body, grid=(S // tm,),
        in_specs=[_col(tm, W, 0), _col(tm, W, 0), _col(tm, 128, offs["kr"]), _col(tm, 128, 0), _col(tm, 128, 0), _col(tm, 128, 0),
                  _whole((1, QK_PAD)), _whole((1, QK_PAD)), _col(tm, W, 0), _col(tm, W, 0), _col(tm, BRANCH_W, 0)],
        out_specs=[_col(tm, W, 0), _col(tm, W, 0), _col(tm, 128, 0), _whole((1, QK_PAD)), _whole((1, QK_PAD))],
        out_shape=[jax.ShapeDtypeStruct((S, W), BF16), jax.ShapeDtypeStruct((S, W), BF16), jax.ShapeDtypeStruct((S, 128), BF16),
                   jax.ShapeDtypeStruct((1, QK_PAD), F32), jax.ShapeDtypeStruct((1, QK_PAD), F32)],
        name=name, compiler_params=_params(("arbitrary",)),
    )(qp, kvp, proj, *tabs, qg, kg, dQ, dK, dV)


def _causal_scores(q, k, qi, ki, tq, tk):
    s = _dot_nt(q, k) * SCORE_SCALE_LOG2
    rows = qi * tq + lax.broadcasted_iota(jnp.int32, (tq, tk), 0)
    cols = ki * tk + lax.broadcasted_iota(jnp.int32, (tq, tk), 1)
    return s, cols <= rows


def _flash_fwd(Q, K, V, proj, offs, name):
    S = Q.shape[0]
    t = _pick(S, 1024, 128)
    chunks = t // 128
    mz_cb = offs["mz"] // V_HEAD

    def fold(x, op):
        r = x[:, 0:128]
        for ch in range(1, chunks):
            r = op(r, x[:, ch * 128:(ch + 1) * 128])
        return r

    def body(q_ref, k_ref, v_ref, mz_ref, o_ref, y_ref, lse_ref, mp_scr, lp_scr, acc_scr):
        qi = pl.program_id(1)
        q = q_ref[...]

        def raw_scores(j, masked):
            rows_k = pl.ds(pl.multiple_of(j * t, t), t)
            s = _dot_nt(q, k_ref[rows_k, :])
            if masked:
                rows = lax.broadcasted_iota(jnp.int32, (t, t), 0)
                cols = lax.broadcasted_iota(jnp.int32, (t, t), 1)
                s = jnp.where(cols <= rows, s, NEG)
            return s, rows_k

        mp_scr[...] = jnp.full_like(mp_scr, NEG)

        def max_sweep(j, carry):
            s, _ = raw_scores(j, False)
            mp_scr[...] = jnp.maximum(mp_scr[...], fold(s, jnp.maximum))
            return carry

        lax.fori_loop(0, qi, max_sweep, 0)
        s_diag, _ = raw_scores(qi, True)
        m = jnp.max(jnp.maximum(mp_scr[...], fold(s_diag, jnp.maximum)), axis=-1, keepdims=True) * SCORE_SCALE_LOG2

        lp_scr[...] = jnp.zeros_like(lp_scr)
        acc_scr[...] = jnp.zeros_like(acc_scr)

        def accumulate(j, masked):
            s, rows_k = raw_scores(j, masked)
            p = jnp.exp2(s * SCORE_SCALE_LOG2 - m)
            lp_scr[...] += fold(p, jnp.add)
            acc_scr[...] += _dot(p, v_ref[rows_k, :])

        def sum_sweep(j, carry):
            accumulate(j, False)
            return carry

        lax.fori_loop(0, qi, sum_sweep, 0)
        accumulate(qi, True)
        l = jnp.sum(lp_scr[...], axis=-1, keepdims=True)
        o = acc_scr[...] / l
        o_ref[...] = o
        y_ref[...] = (o * _silu(mz_ref[...])).astype(BF16)
        lse_ref[0] = m + jnp.log2(l)

    return pl.pallas_call(
        body, grid=(MLA_HEADS, S // t),
        in_specs=[pl.BlockSpec((t, QK_PAD), lambda h, i: (i, h)), pl.BlockSpec((S, QK_PAD), lambda h, i: (0, h)),
                  pl.BlockSpec((S, V_HEAD), lambda h, i: (0, h)), pl.BlockSpec((t, V_HEAD), lambda h, i: (i, mz_cb + h))],
        out_specs=[pl.BlockSpec((t, V_HEAD), lambda h, i: (i, h)), pl.BlockSpec((t, V_HEAD), lambda h, i: (i, h)),
                   pl.BlockSpec((1, t, 1), lambda h, i: (h, i, 0))],
        out_shape=[jax.ShapeDtypeStruct((S, BRANCH_W), F32), jax.ShapeDtypeStruct((S, BRANCH_W), BF16),
                   jax.ShapeDtypeStruct((MLA_HEADS, S, 1), F32)],
        scratch_shapes=[pltpu.VMEM((t, 128), F32), pltpu.VMEM((t, 128), F32), pltpu.VMEM((t, V_HEAD), F32)], name=name,
        compiler_params=_params(("parallel", "parallel")),
    )(Q, K, V, proj)


def _mla_gate_bwd(o, proj, offs, dy, name):
    S = o.shape[0]
    tm = _pick(S, 512, 16)

    def body(o_ref, mz_ref, dy_ref, do_ref, dmz_ref, dl_ref):
        sz, dsz = _silu_and_grad(mz_ref[...])
        dyv, ov = dy_ref[...], o_ref[...]
        do = (dyv * sz).astype(BF16)
        do_ref[...] = do
        dmz_ref[...] = (dyv * ov * dsz).astype(BF16)
        prod = do.astype(F32) * ov
        for h in range(MLA_HEADS):
            dl_ref[h] = jnp.sum(prod[:, h * V_HEAD:(h + 1) * V_HEAD], axis=-1, keepdims=True)

    return pl.pallas_call(
        body, grid=(S // tm,), in_specs=[_col(tm, BRANCH_W, 0), _col(tm, BRANCH_W, offs["mz"]), _col(tm, BRANCH_W, 0)],
        out_specs=[_col(tm, BRANCH_W, 0), _col(tm, BRANCH_W, 0), pl.BlockSpec((MLA_HEADS, tm, 1), lambda i: (0, i, 0))],
        out_shape=[jax.ShapeDtypeStruct((S, BRANCH_W), BF16), jax.ShapeDtypeStruct((S, BRANCH_W), BF16),
                   jax.ShapeDtypeStruct((MLA_HEADS, S, 1), F32)], name=name,
        compiler_params=_params(("parallel",)),
    )(o, proj, dy)


def _flash_bwd(Q, K, V, lse, delta, do, name):
    S = Q.shape[0]
    tk = _pick(S, 1024, 128)
    nq = S // tk
    once = pl.Buffered(1)

    def body(q_ref, k_ref, v_ref, lse_ref, dl_ref, do_ref, dq_ref, dk_ref, dv_ref, dk_acc, dv_acc):
        ki = pl.program_id(1)

        @pl.when(ki == 0)
        def _():
            dq_ref[...] = jnp.zeros_like(dq_ref)

        dk_acc[...] = jnp.zeros_like(dk_acc)
        dv_acc[...] = jnp.zeros_like(dv_acc)
        k, v = k_ref[...], v_ref[...]

        def block(i, masked):
            rows = pl.ds(pl.multiple_of(i * tk, tk), tk)
            q, dov = q_ref[rows, :], do_ref[rows, :]
            s, mask = _causal_scores(q, k, i, ki, tk, tk)
            if masked:
                s = jnp.where(mask, s, NEG)
            p = jnp.exp2(s - lse_ref[0, rows, :])
            dv_acc[...] += _dot_tn(p, dov)
            dp = _dot_nt(dov, v)
            ds = p * (dp - dl_ref[0, rows, :]) * (QK_HEAD ** -0.5)
            dk_acc[...] += _dot_tn(ds, q)
            dq_ref[rows, :] += _dot(ds, k)

        def below_diagonal(i, carry):
            block(i, False)
            return carry

        block(ki, True)
        lax.fori_loop(ki + 1, nq, below_diagonal, 0)
        dk_ref[...] = dk_acc[...]
        dv_ref[...] = dv_acc[...]

    return pl.pallas_call(
        body, grid=(MLA_HEADS, nq),
        in_specs=[pl.BlockSpec((S, QK_PAD), lambda h, j: (0, h), pipeline_mode=once), pl.BlockSpec((tk, QK_PAD), lambda h, j: (j, h)),
                  pl.BlockSpec((tk, V_HEAD), lambda h, j: (j, h)), pl.BlockSpec((1, S, 1), lambda h, j: (h, 0, 0), pipeline_mode=once),
                  pl.BlockSpec((1, S, 1), lambda h, j: (h, 0, 0), pipeline_mode=once),
                  pl.BlockSpec((S, V_HEAD), lambda h, j: (0, h), pipeline_mode=once)],
        out_specs=[pl.BlockSpec((S, QK_PAD), lambda h, j: (0, h)), pl.BlockSpec((tk, QK_PAD), lambda h, j: (j, h)),
                   pl.BlockSpec((tk, V_HEAD), lambda h, j: (j, h))],
        out_shape=[jax.ShapeDtypeStruct((S, MLA_HEADS * QK_PAD), F32), jax.ShapeDtypeStruct((S, MLA_HEADS * QK_PAD), F32),
                   jax.ShapeDtypeStruct((S, BRANCH_W), F32)],
        scratch_shapes=[pltpu.VMEM((tk, QK_PAD), F32), pltpu.VMEM((tk, V_HEAD), F32)], name=name,
        compiler_params=_params(("parallel", "arbitrary")),
    )(Q, K, V, lse, delta, do)


def _merge_specs(S, D, offs, tm, tn, layer):
    g_cb = offs["g"] // tn
    nj = D // tn
    y_specs = [pl.BlockSpec((tm, BRANCH_W), lambda j, i: (i, 0)) for _ in range(4)]
    w_spec = pl.BlockSpec((None, 4, BRANCH_W, tn), lambda j, i: (layer, 0, 0, j))
    g_specs = [pl.BlockSpec((tm, tn), functools.partial(lambda j, i, b: (i, g_cb + b * nj + j), b=b)) for b in range(4)]
    bias_spec = pl.BlockSpec((4, tn), lambda j, i: (0, j))
    return y_specs, w_spec, g_specs, bias_spec


def _merge_fwd(ys, wb, layer, proj, offs, gate_b, name):
    S, D = ys[0].shape[0], wb.shape[3]
    tm, tn = _pick(S, 512, 16), _pick(D, 512, 128)
    y_specs, w_spec, g_specs, bias_spec = _merge_specs(S, D, offs, tm, tn, layer)

    def body(y0, y1, y2, y3, w_ref, g0, g1, g2, g3, b_ref, o_ref):
        acc = None
        for b, (y_ref, g_ref) in enumerate(zip((y0, y1, y2, y3), (g0, g1, g2, g3))):
            gate = 1.0 / (1.0 + jnp.exp(-(g_ref[...] + b_ref[b:b + 1, :])))
            term = gate * _dot(y_ref[...], w_ref[b])
            acc = term if acc is None else acc + term
        o_ref[...] = acc.astype(BF16)

    return pl.pallas_call(
        body, grid=(D // tn, S // tm), in_specs=y_specs + [w_spec] + g_specs + [bias_spec],
        out_specs=pl.BlockSpec((tm, tn), lambda j, i: (i, j)), out_shape=jax.ShapeDtypeStruct((S, D), BF16), name=name,
        compiler_params=_params(("parallel", "parallel")),
    )(*ys, wb, proj, proj, proj, proj, gate_b)


def _merge_bwd(ys, wb, layer, proj, offs, gate_b, dmerged, name):
    S, D = ys[0].shape[0], wb.shape[3]
    tm, tn = _pick(S, 512, 16), _pick(D, 512, 128)
    nj = D // tn
    y_specs, w_spec, g_specs, bias_spec = _merge_specs(S, D, offs, tm, tn, layer)

    def body(y0, y1, y2, y3, w_ref, g0, g1, g2, g3, b_ref, dm_ref, dt_ref, dg_ref, db_ref):
        @pl.when(pl.program_id(1) == 0)
        def _():
            db_ref[...] = jnp.zeros_like(db_ref)

        dm = dm_ref[...]
        for b, (y_ref, g_ref) in enumerate(zip((y0, y1, y2, y3), (g0, g1, g2, g3))):
            gate = 1.0 / (1.0 + jnp.exp(-(g_ref[...] + b_ref[b:b + 1, :])))
            t = _dot(y_ref[...], w_ref[b])
            dt_ref[b] = (dm * gate).astype(BF16)
            dgp = dm * t * gate * (1.0 - gate)
            dg_ref[b] = dgp.astype(BF16)
            db_ref[b:b + 1, :] += jnp.sum(dgp, axis=0, keepdims=True)

    stacked = pl.BlockSpec((4, tm, tn), lambda j, i: (0, i, j))
    return pl.pallas_call(
        body, grid=(nj, S // tm),
        in_specs=y_specs + [w_spec] + g_specs + [bias_spec, pl.BlockSpec((tm, tn), lambda j, i: (i, j))],
        out_specs=[stacked, stacked, pl.BlockSpec((4, tn), lambda j, i: (0, j))],
        out_shape=[jax.ShapeDtypeStruct((4, S, D), BF16), jax.ShapeDtypeStruct((4, S, D), BF16), jax.ShapeDtypeStruct((4, D), F32)],
        name=name, compiler_params=_params(("parallel", "arbitrary")),
    )(*ys, wb, proj, proj, proj, proj, gate_b, dmerged)


HBM_SPEC = pl.BlockSpec(memory_space=pl.ANY)


def _mesh_place():
    x, y, c = lax.axis_index("x"), lax.axis_index("y"), lax.axis_index("c")
    return x, y, c, [(1 - x, y), (x, 1 - y), (1 - x, 1 - y)]


def _remote(src, dst, send_sem, recv_sem, to):
    return pltpu.make_async_remote_copy(src_ref=src, dst_ref=dst, send_sem=send_sem, recv_sem=recv_sem, device_id=to,
                                        device_id_type=pl.DeviceIdType.MESH)


def _gather_two_level(arrs, name):
    n = len(arrs)

    def body(*refs):
        ins, dsts = refs[:n], refs[n:2 * n]
        send_sems, recv_sems, local_sems = refs[2 * n:]
        x, y, c, chips = _mesh_place()
        slot = lambda px, py, pc: 4 * px + 2 * py + pc
        me, sibling = slot(x, y, c), (x, y, 1 - c)

        def copy(a, k, src, block, to):
            return _remote(src, dsts[a].at[block], send_sems.at[a, k], recv_sems.at[a, k], to)

        local = [pltpu.make_async_copy(ins[a], dsts[a].at[me], local_sems.at[a]) for a in range(n)]
        for cp in local:
            cp.start()
        first = []
        for a in range(n):
            first.append(copy(a, 0, ins[a], me, sibling))
            first += [copy(a, 1 + j, ins[a], me, (*chip, c)) for j, chip in enumerate(chips)]
        for cp in first:
            cp.start()
        passed = []
        for a in range(n):
            for j, chip in enumerate(chips):
                landed = slot(*chip, c)
                copy(a, 1 + j, ins[a], landed, sibling).wait_recv()
                fwd = copy(a, 4 + j, dsts[a].at[landed], landed, sibling)
                fwd.start()
                passed.append(fwd)
        for a in range(n):
            copy(a, 0, ins[a], slot(x, y, 1 - c), sibling).wait_recv()
            for j, chip in enumerate(chips):
                copy(a, 4 + j, ins[a], slot(*chip, 1 - c), sibling).wait_recv()
        for cp in first + passed:
            cp.wait_send()
        for cp in local:
            cp.wait()

    res = pl.pallas_call(
        body, in_specs=[HBM_SPEC] * n, out_specs=[HBM_SPEC] * n,
        out_shape=[jax.ShapeDtypeStruct((N_DEV,) + a.shape, a.dtype) for a in arrs],
        scratch_shapes=[pltpu.SemaphoreType.DMA((n, 7)), pltpu.SemaphoreType.DMA((n, 7)), pltpu.SemaphoreType.DMA((n,))], name=name,
    )(*arrs)
    return list(res)


def _sibling_swap(arrs, name):
    n = len(arrs)

    def body(*refs):
        ins, dsts = refs[:n], refs[n:2 * n]
        send_sems, recv_sems = refs[2 * n:]
        x, y, c, _ = _mesh_place()
        cps = [_remote(ins[a].at[q, 1 - c], dsts[a].at[q], send_sems.at[a, q], recv_sems.at[a, q], (x, y, 1 - c))
               for a in range(n) for q in range(4)]
        for cp in cps:
            cp.start()
        for cp in cps:
            cp.wait()

    res = pl.pallas_call(
        body, in_specs=[HBM_SPEC] * n, out_specs=[HBM_SPEC] * n,
        out_shape=[jax.ShapeDtypeStruct((4,) + a.shape[2:], a.dtype) for a in arrs],
        scratch_shapes=[pltpu.SemaphoreType.DMA((n, 4)), pltpu.SemaphoreType.DMA((n, 4))], name=name,
    )(*arrs)
    return list(res)


def _chip_exchange(arrs, name):
    n = len(arrs)

    def body(*refs):
        ins, dsts = refs[:n], refs[n:2 * n]
        send_sems, recv_sems, local_sems = refs[2 * n:]
        x, y, c, chips = _mesh_place()
        mine = 2 * x + y
        local = [pltpu.make_async_copy(ins[a].at[mine], dsts[a].at[mine], local_sems.at[a]) for a in range(n)]
        for cp in local:
            cp.start()
        cps = []
        for a in range(n):
            for j, (px, py) in enumerate(chips):
                cps.append(_remote(ins[a].at[2 * px + py], dsts[a].at[mine], send_sems.at[a, j], recv_sems.at[a, j], (px, py, c)))
        for cp in cps:
            cp.start()
        for cp in cps:
            cp.wait()
        for cp in local:
            cp.wait()

    res = pl.pallas_call(
        body, in_specs=[HBM_SPEC] * n, out_specs=[HBM_SPEC] * n, out_shape=[jax.ShapeDtypeStruct(a.shape, a.dtype) for a in arrs],
        scratch_shapes=[pltpu.SemaphoreType.DMA((n, 3)), pltpu.SemaphoreType.DMA((n, 3)), pltpu.SemaphoreType.DMA((n,))], name=name,
    )(*arrs)
    return list(res)


def _pair_sum(staged, theirs, core, name):
    _, _, R, C = staged.shape
    tr = _pick(R, max(16, ((1 << 19) // C) // 16 * 16), 16)

    def body(c_ref, a_ref, b_ref, o_ref):
        o_ref[...] = (a_ref[...].astype(F32) + b_ref[...].astype(F32)).astype(BF16)

    blk = pl.BlockSpec((None, tr, C), lambda q, i, c: (q, i, 0))
    grid_spec = pltpu.PrefetchScalarGridSpec(
        num_scalar_prefetch=1, grid=(4, R // tr),
        in_specs=[pl.BlockSpec((None, None, tr, C), lambda q, i, c: (q, c[0], i, 0)), blk], out_specs=blk)
    return pl.pallas_call(
        body, grid_spec=grid_spec, out_shape=jax.ShapeDtypeStruct((4, R, C), BF16), name=name,
        compiler_params=_params(("parallel", "parallel")),
    )(core, staged, theirs)


def _adamw(parts, w, m, v, name):
    R, C = w.shape
    n_parts = parts.shape[0]
    tr = _pick(R, max(16, ((1 << 18) // C) // 16 * 16), 16)

    def body(p_ref, w_ref, m_ref, v_ref, g_ref, d_ref, nm_ref, nv_ref):
        g = p_ref[0].astype(F32)
        for s in range(1, n_parts):
            g = g + p_ref[s].astype(F32)
        mn = ADAM_B1 * m_ref[...] + (1.0 - ADAM_B1) * g
        vn = ADAM_B2 * v_ref[...] + (1.0 - ADAM_B2) * (g * g)
        m_hat = mn / (1.0 - ADAM_B1 ** ADAM_STEP)
        v_hat = vn / (1.0 - ADAM_B2 ** ADAM_STEP)
        g_ref[...] = g
        d_ref[...] = -ADAM_LR * (m_hat / (jnp.sqrt(v_hat) + ADAM_EPS) + ADAM_WD * w_ref[...])
        nm_ref[...] = mn
        nv_ref[...] = vn

    blk = pl.BlockSpec((tr, C), lambda i: (i, 0))
    return pl.pallas_call(
        body, grid=(R // tr,), in_specs=[pl.BlockSpec((n_parts, tr, C), lambda i: (0, i, 0)), blk, blk, blk],
        out_specs=[blk] * 4, out_shape=[jax.ShapeDtypeStruct((R, C), F32)] * 4, name=name,
        compiler_params=_params(("parallel",)),
    )(parts, w, m, v)


def _offsets(L, D):
    o = {"pv": 0, "pz": BRANCH_W, "cq": 2 * BRANCH_W, "ckv": 2 * BRANCH_W + L, "mz": 2 * BRANCH_W + 2 * L}
    for prev, nxt in (("mz", "cb"), ("cb", "cc"), ("cc", "cx"), ("cx", "cz"), ("cz", "xq"), ("xq", "xz"), ("xz", "g")):
        o[nxt] = o[prev] + BRANCH_W
    o["kr"] = o["g"] + 4 * D
    o["n"] = o["kr"] + 128
    return o


def _unshard(gathered, axis):
    moved = jnp.moveaxis(gathered, 0, axis)
    shape = list(moved.shape)
    shape[axis:axis + 2] = [shape[axis] * shape[axis + 1]]
    return moved.reshape(shape)


def _reshard(full, axis):
    shape = list(full.shape)
    shape[axis:axis + 1] = [N_DEV, shape[axis] // N_DEV]
    return jnp.moveaxis(full.reshape(shape), axis, 0)


def _pad_heads(g, width):
    return jnp.pad(g, ((0, 0), (0, width - g.shape[1])))


def kernel(x, mem, positions, norm_g, w_in, gate_b, pool_w, pool_scale, q_a_norm_g, kv_a_norm_g, w_uq, w_ukv, mla_q_norm_g, mla_k_norm_g, conv_w, mem_norm_g, w_mem_kv, xattn_q_norm_g, xattn_k_norm_g, w_branch, w_out, loss_target, m_norm_g, m_w_in, m_gate_b, m_pool_w, m_pool_scale, m_q_a_norm_g, m_kv_a_norm_g, m_w_uq, m_w_ukv, m_mla_q_norm_g, m_mla_k_norm_g, m_conv_w, m_mem_norm_g, m_w_mem_kv, m_xattn_q_norm_g, m_xattn_k_norm_g, m_w_branch, m_w_out, v_norm_g, v_w_in, v_gate_b, v_pool_w, v_pool_scale, v_q_a_norm_g, v_kv_a_norm_g, v_w_uq, v_w_ukv, v_mla_q_norm_g, v_mla_k_norm_g, v_conv_w, v_mem_norm_g, v_w_mem_kv, v_xattn_q_norm_g, v_xattn_k_norm_g, v_w_branch, v_w_out):
    depth = norm_g.shape[0]
    S, D = x.shape[1], x.shape[2]
    L = w_uq.shape[1]
    offs = _offsets(L, D)
    seg_a = 2 * BRANCH_W + 2 * L
    xs = x.reshape(S, D)
    mems = mem.reshape(mem.shape[1], D)

    sharded = dict(w_in=w_in, pool_w=pool_w, w_uq=w_uq, w_ukv=w_ukv, conv_w=conv_w, w_mem_kv=w_mem_kv, w_branch=w_branch, w_out=w_out)
    shard_axis = dict(w_in=2, pool_w=2, w_uq=2, w_ukv=2, conv_w=2, w_mem_kv=1, w_branch=3, w_out=1)
    keys = list(shard_axis)
    payload = [sharded[k] if k == "conv_w" else sharded[k].astype(BF16) for k in keys]
    gathered = {k: _unshard(g, shard_axis[k]) for k, g in zip(keys, _gather_two_level(payload, "gather_weights"))}
    conv_full = gathered["conv_w"]
    w_in_f = gathered["w_in"]
    w_in_p = [jnp.concatenate([w_in_f[l, :, :seg_a], w_in_f[l, :, seg_a + QK_ROPE:], w_in_f[l, :, seg_a:seg_a + QK_ROPE],
                               jnp.zeros((D, 128 - QK_ROPE), BF16)], axis=-1) for l in range(depth)]
    pool_w_f = gathered["pool_w"]
    w_uq_f = gathered["w_uq"].reshape(depth, L, MLA_HEADS, QK_HEAD)
    w_uq_p = jnp.pad(w_uq_f, ((0, 0), (0, 0), (0, 0), (0, QK_PAD - QK_HEAD))).reshape(depth, L, MLA_HEADS * QK_PAD)
    w_ukv_f = gathered["w_ukv"].reshape(depth, L, MLA_HEADS, QK_NOPE + V_HEAD)
    w_ukv_p = jnp.concatenate([w_ukv_f[..., :QK_NOPE].reshape(depth, L, -1), w_ukv_f[..., QK_NOPE:].reshape(depth, L, -1)], axis=-1)
    w_mem_f = gathered["w_mem_kv"]
    w_br_f = gathered["w_branch"]
    w_out_f = gathered["w_out"]

    inv = ROPE_THETA ** (-jnp.arange(0, QK_ROPE, 2, dtype=F32) / QK_ROPE)
    ang = positions.reshape(S).astype(F32)[:, None] * inv
    cos, sin, z = jnp.cos(ang), jnp.sin(ang), jnp.zeros_like(ang)
    tabs = (jnp.concatenate([cos, cos, z, z], axis=1), jnp.concatenate([-sin, z, z, z], axis=1), jnp.concatenate([z, sin, z, z], axis=1))

    row = lambda a, l: a[l][None, :]
    saved = []
    cur = xs
    for l in range(depth):
        h = _rms_rows(cur, D, 0, row(norm_g, l), f"norm_fwd{l}")
        proj = _matmul(h, w_in_p[l], "nn", F32, f"proj_fwd{l}", tm=1024, tn=640, tk=2048)
        y_pool = _pool_fwd(proj, offs, pool_w_f[l], row(pool_scale, l), f"pool_fwd{l}")
        cqn = _rms_rows(proj, L, offs["cq"], row(q_a_norm_g, l), f"cq_norm{l}")
        ckvn = _rms_rows(proj, L, offs["ckv"], row(kv_a_norm_g, l), f"ckv_norm{l}")
        qp = _matmul(cqn, w_uq_p[l], "nn", F32, f"uq_fwd{l}")
        kvp = _matmul(ckvn, w_ukv_p[l], "nn", F32, f"ukv_fwd{l}")
        qg = _pad_heads(row(mla_q_norm_g, l), QK_PAD)
        kg = _pad_heads(row(mla_k_norm_g, l), QK_PAD)
        Q, K, V = _mla_qk_fwd(qp, kvp, proj, offs, tabs, qg, kg, f"mla_qk_fwd{l}")
        o, y_mla, lse = _flash_fwd(Q, K, V, proj, offs, f"flash_fwd{l}")
        y_conv = _conv_fwd(proj, offs, conv_full[l], f"conv_fwd{l}")
        memn = _rms_rows(mems, D, 0, row(mem_norm_g, l), f"mem_norm{l}")
        mem_kv = _matmul(memn, w_mem_f, "nn", F32, f"mem_kv{l}", b_idx=(l,))
        y_mem = _xattn_fwd(proj, offs, mem_kv, row(xattn_q_norm_g, l), row(xattn_k_norm_g, l), f"xattn_fwd{l}")
        ys = (y_pool, y_mla, y_conv, y_mem)
        gb = gate_b[l].reshape(4, D)
        merged = _merge_fwd(ys, w_br_f, l, proj, offs, gb, f"merge_fwd{l}")
        nxt = _matmul(merged, w_out_f, "nn", F32, f"out_fwd{l}", res=cur, b_idx=(l,))
        saved.append(dict(x=cur, h=h, proj=proj, cqn=cqn, ckvn=ckvn, qp=qp, kvp=kvp, qg=qg, kg=kg, Q=Q, K=K, V=V, o=o, lse=lse,
                          memn=memn, mem_kv=mem_kv, ys=ys, gb=gb, merged=merged))
        cur = nxt

    gx, loss_acc = _loss_rows(cur, loss_target.reshape(S, D), "loss")
    loss = lax.psum(0.5 * loss_acc[0, 0], MESH_AXES)

    grads = {k: [None] * depth for k in ("norm_g", "w_in", "gate_b", "pool_w", "pool_scale", "q_a_norm_g", "kv_a_norm_g", "w_uq", "w_ukv",
                                         "mla_q_norm_g", "mla_k_norm_g", "conv_w", "mem_norm_g", "w_mem_kv", "xattn_q_norm_g",
                                         "xattn_k_norm_g", "w_branch", "w_out")}
    for l in reversed(range(depth)):
        sv = saved[l]
        proj, ys = sv["proj"], sv["ys"]
        dmerged = _matmul(gx, w_out_f, "nt", F32, f"out_bwd_x{l}", b_idx=(l,))
        grads["w_out"][l] = _matmul(sv["merged"], gx, "tn", BF16, f"out_bwd_w{l}")
        dt, dgp, dgb = _merge_bwd(ys, w_br_f, l, proj, offs, sv["gb"], dmerged, f"merge_bwd{l}")
        grads["gate_b"][l] = dgb.reshape(4 * D)
        dys, dwb = [], []
        for b in range(4):
            dys.append(_matmul(dt, w_br_f, "nt", F32, f"branch_bwd_x{l}_{b}", a_idx=(b,), b_idx=(l, b)))
            dwb.append(_matmul(ys[b], dt, "tn", BF16, f"branch_bwd_w{l}_{b}", b_idx=(b,)))
        grads["w_branch"][l] = jnp.stack(dwb)
        d_pz, e, d_pw, d_ps = _pool_bwd_a(proj, offs, pool_w_f[l], row(pool_scale, l), dys[0], f"pool_bwd_a{l}")
        d_pv = _pool_bwd_b(e, f"pool_bwd_b{l}")
        grads["pool_w"][l], grads["pool_scale"][l] = d_pw, d_ps[0]
        do, d_mz, delta = _mla_gate_bwd(sv["o"], proj, offs, dys[1], f"mla_gate_bwd{l}")
        dQ, dK, dV = _flash_bwd(sv["Q"], sv["K"], sv["V"], sv["lse"], delta, do, f"flash_bwd{l}")
        dqp, dkvp, d_kr, dqg, dkg = _mla_qk_bwd(sv["qp"], sv["kvp"], proj, offs, tabs, sv["qg"], sv["kg"], dQ, dK, dV, f"mla_qk_bwd{l}")
        grads["mla_q_norm_g"][l], grads["mla_k_norm_g"][l] = dqg[0, :QK_HEAD], dkg[0, :QK_HEAD]
        dcqn = _matmul(dqp, w_uq_p[l], "nt", F32, f"uq_bwd_x{l}")
        d_wuq = _matmul(sv["cqn"], dqp, "tn", BF16, f"uq_bwd_w{l}")
        dckvn = _matmul(dkvp, w_ukv_p[l], "nt", F32, f"ukv_bwd_x{l}")
        d_wukv = _matmul(sv["ckvn"], dkvp, "tn", BF16, f"ukv_bwd_w{l}")
        grads["w_uq"][l] = d_wuq.reshape(L, MLA_HEADS, QK_PAD)[..., :QK_HEAD].reshape(L, MLA_HEADS * QK_HEAD)
        grads["w_ukv"][l] = jnp.concatenate([d_wukv[:, :MLA_HEADS * QK_NOPE].reshape(L, MLA_HEADS, QK_NOPE),
                                             d_wukv[:, MLA_HEADS * QK_NOPE:].reshape(L, MLA_HEADS, V_HEAD)], axis=-1).reshape(L, -1)
        d_cq, dqa = _rms_rows_bwd(proj, L, offs["cq"], row(q_a_norm_g, l), dcqn, BF16, f"cq_norm_bwd{l}")
        d_ckv, dkva = _rms_rows_bwd(proj, L, offs["ckv"], row(kv_a_norm_g, l), dckvn, BF16, f"ckv_norm_bwd{l}")
        grads["q_a_norm_g"][l], grads["kv_a_norm_g"][l] = dqa[0], dkva[0]
        d_cb, d_cc, d_cx, d_cz, d_cw = _conv_bwd(proj, offs, conv_full[l], dys[2], f"conv_bwd{l}")
        grads["conv_w"][l] = d_cw
        d_xq, d_xz, d_memkv, dxqg, dxkg = _xattn_bwd(proj, offs, sv["mem_kv"], row(xattn_q_norm_g, l), row(xattn_k_norm_g, l), dys[3], f"xattn_bwd{l}")
        grads["xattn_q_norm_g"][l], grads["xattn_k_norm_g"][l] = dxqg[0], dxkg[0]
        grads["w_mem_kv"][l] = _matmul(sv["memn"], d_memkv, "tn", BF16, f"mem_kv_bwd_w{l}")
        dmemn = _matmul(d_memkv, w_mem_f, "nt", F32, f"mem_kv_bwd_x{l}", b_idx=(l,))
        grads["mem_norm_g"][l] = _norm_g_grad(mems, row(mem_norm_g, l), dmemn, f"mem_norm_bwd{l}")[0]
        dproj = jnp.concatenate([d_pv, d_pz, d_cq, d_ckv, d_mz, d_cb, d_cc, d_cx, d_cz, d_xq, d_xz, dgp[0], dgp[1], dgp[2], dgp[3], d_kr], axis=1)
        dh = _matmul(dproj, w_in_p[l], "nt", F32, f"proj_bwd_x{l}", tm=1024, tn=1024, tk=3712)
        d_win_p = _matmul(sv["h"], dproj, "tn", BF16, f"proj_bwd_w{l}", tm=2048, tn=640, tk=2048)
        n_p = offs["n"]
        grads["w_in"][l] = jnp.concatenate([d_win_p[:, :seg_a], d_win_p[:, n_p - 128:n_p - 128 + QK_ROPE], d_win_p[:, seg_a:n_p - 128]], axis=1)
        gx, dng = _rms_rows_bwd(sv["x"], D, 0, row(norm_g, l), dh, F32, f"norm_bwd{l}", res=gx)
        grads["norm_g"][l] = dng[0]
    grad_x = gx.reshape(1, S, D)
    full = {k: jnp.stack(v) for k, v in grads.items()}

    moments = dict(w_in=(m_w_in, v_w_in), pool_w=(m_pool_w, v_pool_w), w_uq=(m_w_uq, v_w_uq), w_ukv=(m_w_ukv, v_w_ukv),
                   conv_w=(m_conv_w, v_conv_w), w_mem_kv=(m_w_mem_kv, v_w_mem_kv), w_branch=(m_w_branch, v_w_branch), w_out=(m_w_out, v_w_out))
    names_sh = list(shard_axis)
    core = lax.axis_index("c").astype(jnp.int32).reshape(1)
    staged = [_reshard(full[k], shard_axis[k]).reshape(4, 2, -1, sharded[k].shape[-1]).astype(BF16) for k in names_sh]
    given = _sibling_swap(staged, "pair_swap_grads")
    pair = [_pair_sum(st, theirs, core, f"pair_sum_{k}") for k, st, theirs in zip(names_sh, staged, given)]
    received = _chip_exchange(pair, "chip_exchange_grads")
    results = {}
    for k, parts in zip(names_sh, received):
        w = sharded[k]
        m, v = moments[k]
        flat = lambda a: a.reshape(-1, w.shape[-1])
        outs = _adamw(parts, flat(w), flat(m), flat(v), f"adamw_{k}")
        results[k] = [o.reshape(w.shape) for o in outs]

    small = dict(norm_g=(norm_g, m_norm_g, v_norm_g), gate_b=(gate_b, m_gate_b, v_gate_b), pool_scale=(pool_scale, m_pool_scale, v_pool_scale),
                 q_a_norm_g=(q_a_norm_g, m_q_a_norm_g, v_q_a_norm_g), kv_a_norm_g=(kv_a_norm_g, m_kv_a_norm_g, v_kv_a_norm_g),
                 mla_q_norm_g=(mla_q_norm_g, m_mla_q_norm_g, v_mla_q_norm_g), mla_k_norm_g=(mla_k_norm_g, m_mla_k_norm_g, v_mla_k_norm_g),
                 mem_norm_g=(mem_norm_g, m_mem_norm_g, v_mem_norm_g), xattn_q_norm_g=(xattn_q_norm_g, m_xattn_q_norm_g, v_xattn_q_norm_g),
                 xattn_k_norm_g=(xattn_k_norm_g, m_xattn_k_norm_g, v_xattn_k_norm_g))
    names_sm = list(small)
    sizes = [small[k][0].size for k in names_sm]
    total = sum(sizes)
    rows = -(-total // (8 * 128)) * 8

    def pack(arrs):
        flat = jnp.concatenate([a.reshape(-1) for a in arrs])
        return jnp.pad(flat, (0, rows * 128 - total)).reshape(rows, 128)

    g_small = _gather_two_level([pack([full[k] for k in names_sm])], "gather_small_grads")[0]
    outs = _adamw(g_small, pack([small[k][0] for k in names_sm]), pack([small[k][1] for k in names_sm]),
                  pack([small[k][2] for k in names_sm]), "adamw_small")
    for kind, o in enumerate(outs):
        flat, pos = o.reshape(-1), 0
        for k, n in zip(names_sm, sizes):
            results.setdefault(k, [None] * 4)[kind] = flat[pos:pos + n].reshape(small[k][0].shape)
            pos += n

    order = ["norm_g", "w_in", "gate_b", "pool_w", "pool_scale", "q_a_norm_g", "kv_a_norm_g", "w_uq", "w_ukv", "mla_q_norm_g",
             "mla_k_norm_g", "conv_w", "mem_norm_g", "w_mem_kv", "xattn_q_norm_g", "xattn_k_norm_g", "w_branch", "w_out"]
    return (loss, grad_x, *[results[k][0] for k in order], *[results[k][1] for k in order],
            *[results[k][2] for k in order], *[results[k][3] for k in order])
```

```python
import functools

import jax
import jax.numpy as jnp
from jax import lax
from jax.experimental import pallas as pl
from jax.experimental.pallas import tpu as pltpu

F32 = jnp.float32
BF16 = jnp.bfloat16
MESH_AXES = ("x", "y", "c")
N_DEV = 8

BRANCH_W = 1024
POOL_WINDOWS = (2, 4, 8, 16)
POOL_GW = BRANCH_W // 4
MLA_HEADS = 8
QK_NOPE = 128
QK_ROPE = 64
QK_HEAD = QK_NOPE + QK_ROPE
QK_PAD = 256
V_HEAD = 128
XATTN_HEADS = 4
XATTN_HD = BRANCH_W // XATTN_HEADS
ROPE_THETA = 10000.0
EPS = 1e-6
ADAM_LR = 0.001
ADAM_B1 = 0.9
ADAM_B2 = 0.999
ADAM_EPS = 1e-08
ADAM_WD = 0.01
ADAM_STEP = 10
HALO = 16
NEG = -1e30
SCORE_SCALE_LOG2 = (QK_HEAD ** -0.5) * 1.4426950408889634
VMEM_LIMIT = 56 << 20


def _pick(n, pref, align):
    if n <= pref:
        return n
    t = (pref // align) * align
    while t >= align:
        if n % t == 0:
            return t
        t -= align
    return n


def _params(sem):
    return pltpu.CompilerParams(dimension_semantics=sem, vmem_limit_bytes=VMEM_LIMIT)


def _silu(z):
    return z * (1.0 / (1.0 + jnp.exp(-z)))


def _silu_and_grad(z):
    s = 1.0 / (1.0 + jnp.exp(-z))
    return z * s, s * (1.0 + z * (1.0 - s))


def _dot(a, b):
    return jnp.dot(a.astype(BF16), b.astype(BF16), preferred_element_type=F32)


def _dot_nt(a, b):
    return lax.dot_general(a.astype(BF16), b.astype(BF16), (((1,), (1,)), ((), ())), preferred_element_type=F32)


def _dot_tn(a, b):
    return lax.dot_general(a.astype(BF16), b.astype(BF16), (((0,), (0,)), ((), ())), preferred_element_type=F32)


def _rms(x, g, n):
    r = lax.rsqrt(jnp.sum(x * x, axis=-1, keepdims=True) / n + EPS)
    return x * r * g, r


def _rms_bwd(x, g, r, dy, n):
    dyg = dy * g
    dx = r * dyg - x * (r * r * r) * (jnp.sum(dyg * x, axis=-1, keepdims=True) / n)
    dg = jnp.sum(dy * x * r, axis=0, keepdims=True)
    return dx, dg


def _col(tm, w, off):
    assert off % w == 0
    cb = off // w
    return pl.BlockSpec((tm, w), lambda i: (i, cb))


def _whole(shape):
    nd = len(shape)
    return pl.BlockSpec(shape, lambda *_: (0,) * nd)


def _lead_spec(block, fn, idx):
    if not idx:
        return pl.BlockSpec(block, fn)
    return pl.BlockSpec((None,) * len(idx) + block, lambda i, j, k: tuple(idx) + fn(i, j, k))


def _hosted_call(plan, body, ins, in_specs, out_specs, out_shape, scratch, grid, name):
    n_in, n_out, n_scr, n = len(ins), len(out_specs), len(scratch), plan.n
    n_steps = 1
    for g in grid:
        n_steps *= g

    def wrapped(*refs):
        a, c_in = refs[:n_in], refs[n_in:n_in + n]
        o, c_out = refs[n_in + n:n_in + n + n_out], refs[n_in + n + n_out:n_in + 2 * n + n_out]
        rest = refs[n_in + 2 * n + n_out:]
        scr, sems = rest[:n_scr], rest[n_scr:]
        step = 0
        for d, g in enumerate(grid):
            step = step * g + pl.program_id(d)

        @pl.when(step == 0)
        def _():
            plan.start(c_in, c_out, sems)

        body(*a, *o, *scr)

        @pl.when(step == n_steps // 2)
        def _():
            plan.relay(c_in, c_out, sems)

        @pl.when(step == n_steps - 1)
        def _():
            plan.finish(c_in, c_out, sems)

    res = pl.pallas_call(
        wrapped, grid=grid, in_specs=list(in_specs) + [HBM_SPEC] * n, out_specs=list(out_specs) + [HBM_SPEC] * n,
        out_shape=list(out_shape) + plan.out_shape, scratch_shapes=list(scratch) + plan.scratch, name=name,
        compiler_params=_params(("arbitrary",) * len(grid)),
    )(*ins, *plan.arrs)
    return list(res[:n_out]), list(res[n_out:])


def _matmul(a, b, mode, out_dtype, name, tm=1024, tn=1024, tk=2048, res=None, a_idx=(), b_idx=(), n_outer=False, comm=None):
    a_shape, b_shape = a.shape[len(a_idx):], b.shape[len(b_idx):]
    if mode == "nn":
        (M, K), N = a_shape, b_shape[1]
    elif mode == "nt":
        (M, K), N = a_shape, b_shape[0]
    else:
        (K, M), N = a_shape, b_shape[1]
    tm, tn, tk = _pick(M, tm, 128), _pick(N, tn, 128), _pick(K, tk, 128)
    nk = K // tk
    order = (lambda fn: (lambda j, i, k: fn(i, j, k))) if n_outer else (lambda fn: fn)
    if mode == "nn":
        a_spec = _lead_spec((tm, tk), order(lambda i, j, k: (i, k)), a_idx)
        b_spec = _lead_spec((tk, tn), order(lambda i, j, k: (k, j)), b_idx)
        op = _dot
    elif mode == "nt":
        a_spec = _lead_spec((tm, tk), order(lambda i, j, k: (i, k)), a_idx)
        b_spec = _lead_spec((tn, tk), order(lambda i, j, k: (j, k)), b_idx)
        op = _dot_nt
    else:
        a_spec = _lead_spec((tk, tm), order(lambda i, j, k: (k, i)), a_idx)
        b_spec = _lead_spec((tk, tn), order(lambda i, j, k: (k, j)), b_idx)
        op = _dot_tn
    o_spec = pl.BlockSpec((tm, tn), order(lambda i, j, k: (i, j)))
    has_res = res is not None

    def body(*refs):
        a_ref, b_ref = refs[0], refs[1]
        res_ref = refs[2] if has_res else None
        o_ref = refs[2 + has_res]
        part = op(a_ref[...], b_ref[...])

        def finish(acc):
            if has_res:
                acc = res_ref[...] + acc
            o_ref[...] = acc.astype(out_dtype)

        if nk == 1:
            finish(part)
        else:
            acc_ref = refs[3 + has_res]
            k = pl.program_id(2)

            @pl.when(k == 0)
            def _():
                acc_ref[...] = part

            @pl.when(k > 0)
            def _():
                acc_ref[...] += part

            @pl.when(k == nk - 1)
            def _():
                finish(acc_ref[...])

    ins = [a, b] + ([res] if has_res else [])
    in_specs = [a_spec, b_spec] + ([o_spec] if has_res else [])
    scratch = [pltpu.VMEM((tm, tn), F32)] if nk > 1 else []
    grid = (N // tn, M // tm, nk) if n_outer else (M // tm, N // tn, nk)
    out_shape = jax.ShapeDtypeStruct((M, N), out_dtype)
    if comm is not None:
        (out,), carried = _hosted_call(comm, body, ins, in_specs, [o_spec], [out_shape], scratch, grid, name)
        return out, carried
    return pl.pallas_call(
        body, grid=grid, in_specs=in_specs, out_specs=o_spec, out_shape=out_shape, scratch_shapes=scratch, name=name,
        compiler_params=_params(("parallel", "parallel", "arbitrary")),
    )(*ins)


def _rms_rows(x, width, off, g, name, with_transpose=False):
    S = x.shape[0]
    tm = _pick(S, 512, 128 if with_transpose else 16)

    def body(x_ref, g_ref, o_ref, *t_ref):
        y, _ = _rms(x_ref[...], g_ref[...], width)
        o_ref[...] = y.astype(BF16)
        if with_transpose:
            t_ref[0][...] = y.T.astype(BF16)

    out_specs = [_col(tm, width, 0)] + ([pl.BlockSpec((width, tm), lambda i: (0, i))] if with_transpose else [])
    out_shape = [jax.ShapeDtypeStruct((S, width), BF16)] + ([jax.ShapeDtypeStruct((width, S), BF16)] if with_transpose else [])
    res = pl.pallas_call(
        body, grid=(S // tm,), in_specs=[_col(tm, width, off), _whole((1, width))], out_specs=out_specs,
        out_shape=out_shape, name=name, compiler_params=_params(("parallel",)),
    )(x, g)
    return res if with_transpose else res[0]


def _rms_rows_bwd(x, width, off, g, dy, out_dtype, name, res=None):
    S = x.shape[0]
    tm = _pick(S, 256, 16)
    has_res = res is not None

    def body(*refs):
        x_ref, g_ref, dy_ref = refs[:3]
        res_ref = refs[3] if has_res else None
        dx_ref, dg_ref = refs[3 + has_res:]
        xv, gv = x_ref[...], g_ref[...]
        _, r = _rms(xv, gv, width)
        dx, dg = _rms_bwd(xv, gv, r, dy_ref[...], width)
        if has_res:
            dx = res_ref[...] + dx
        dx_ref[...] = dx.astype(out_dtype)

        @pl.when(pl.program_id(0) == 0)
        def _():
            dg_ref[...] = jnp.zeros_like(dg_ref)

        dg_ref[...] += dg

    ins = [x, g, dy] + ([res] if has_res else [])
    in_specs = [_col(tm, width, off), _whole((1, width)), _col(tm, width, 0)] + ([_col(tm, width, 0)] if has_res else [])
    return pl.pallas_call(
        body, grid=(S // tm,), in_specs=in_specs, out_specs=[_col(tm, width, 0), _whole((1, width))],
        out_shape=[jax.ShapeDtypeStruct((S, width), out_dtype), jax.ShapeDtypeStruct((1, width), F32)], name=name,
        compiler_params=_params(("arbitrary",)),
    )(*ins)


def _norm_g_grad(x, g, dy, name):
    S, width = x.shape
    tm = _pick(S, 256, 16)

    def body(x_ref, g_ref, dy_ref, dg_ref):
        xv = x_ref[...]
        _, r = _rms(xv, g_ref[...], width)

        @pl.when(pl.program_id(0) == 0)
        def _():
            dg_ref[...] = jnp.zeros_like(dg_ref)

        dg_ref[...] += jnp.sum(dy_ref[...] * xv * r, axis=0, keepdims=True)

    return pl.pallas_call(
        body, grid=(S // tm,), in_specs=[_col(tm, width, 0), _whole((1, width)), _col(tm, width, 0)],
        out_specs=_whole((1, width)), out_shape=jax.ShapeDtypeStruct((1, width), F32), name=name,
        compiler_params=_params(("arbitrary",)),
    )(x, g, dy)


def _loss_rows(y, target, name):
    S, D = y.shape
    tm = _pick(S, 512, 16)

    def body(y_ref, t_ref, g_ref, l_ref):
        err = y_ref[...] - t_ref[...]
        g_ref[...] = err / D

        @pl.when(pl.program_id(0) == 0)
        def _():
            l_ref[...] = jnp.zeros_like(l_ref)

        row = jnp.sum(err * err, axis=-1, keepdims=True) / D
        l_ref[...] += jnp.sum(row, axis=0, keepdims=True)

    return pl.pallas_call(
        body, grid=(S // tm,), in_specs=[_col(tm, D, 0), _col(tm, D, 0)], out_specs=[_col(tm, D, 0), _whole((1, 1))],
        out_shape=[jax.ShapeDtypeStruct((S, D), F32), jax.ShapeDtypeStruct((1, 1), F32)], name=name,
        compiler_params=_params(("arbitrary",)),
    )(y, target)


def _row_index(i, ts):
    return i * ts + lax.broadcasted_iota(jnp.int32, (ts, 1), 0)


def _pool_mixed(scr, pv_ref, g, t, ts):
    w, lo = POOL_WINDOWS[g], g * POOL_GW
    win = scr[HALO:HALO + ts, lo:lo + POOL_GW]
    for k in range(1, w):
        win = win + scr[HALO - k:HALO - k + ts, lo:lo + POOL_GW]
    cnt = jnp.minimum(t + 1, w).astype(F32)
    return win / cnt - pv_ref[:, lo:lo + POOL_GW], cnt


def _halo_before(ts, off):
    nb = ts // HALO
    cb = off // BRANCH_W
    return pl.BlockSpec((HALO, BRANCH_W), lambda i: (jnp.maximum(i * nb - 1, 0), cb))


def _halo_after(ts, off, n_tiles):
    nb = ts // HALO
    cb = off // BRANCH_W
    last = n_tiles * nb - 1
    return pl.BlockSpec((HALO, BRANCH_W), lambda i: (jnp.minimum((i + 1) * nb, last), cb))


def _pool_fwd(proj, offs, pool_w, pool_scale, name):
    S = proj.shape[0]
    ts = _pick(S, 512, HALO)

    def body(pv_ref, halo_ref, pz_ref, w_ref, sc_ref, y_ref, scr):
        i = pl.program_id(0)
        scr[0:HALO, :] = jnp.where(i > 0, halo_ref[...], 0.0)
        scr[HALO:HALO + ts, :] = pv_ref[...]
        t = _row_index(i, ts)
        for g in range(4):
            lo = g * POOL_GW
            mixed, _ = _pool_mixed(scr, pv_ref, g, t, ts)
            out = _dot(mixed, w_ref[g])
            y_ref[:, lo:lo + POOL_GW] = (out * sc_ref[:, lo:lo + POOL_GW] * _silu(pz_ref[:, lo:lo + POOL_GW])).astype(BF16)

    return pl.pallas_call(
        body, grid=(S // ts,),
        in_specs=[_col(ts, BRANCH_W, offs["pv"]), _halo_before(ts, offs["pv"]), _col(ts, BRANCH_W, offs["pz"]),
                  _whole((4, POOL_GW, POOL_GW)), _whole((1, BRANCH_W))],
        out_specs=_col(ts, BRANCH_W, 0), out_shape=jax.ShapeDtypeStruct((S, BRANCH_W), BF16),
        scratch_shapes=[pltpu.VMEM((HALO + ts, BRANCH_W), F32)], name=name, compiler_params=_params(("parallel",)),
    )(proj, proj, proj, pool_w, pool_scale)


def _pool_bwd_a(proj, offs, pool_w, pool_scale, dy, name):
    S = proj.shape[0]
    ts = _pick(S, 512, HALO)

    def body(pv_ref, halo_ref, pz_ref, w_ref, sc_ref, dy_ref, dpz_ref, e_ref, dw_ref, dsc_ref, scr):
        i = pl.program_id(0)
        scr[0:HALO, :] = jnp.where(i > 0, halo_ref[...], 0.0)
        scr[HALO:HALO + ts, :] = pv_ref[...]
        t = _row_index(i, ts)

        @pl.when(i == 0)
        def _():
            dw_ref[...] = jnp.zeros_like(dw_ref)
            dsc_ref[...] = jnp.zeros_like(dsc_ref)

        for g in range(4):
            lo = g * POOL_GW
            sl = slice(lo, lo + POOL_GW)
            mixed, cnt = _pool_mixed(scr, pv_ref, g, t, ts)
            out = _dot(mixed, w_ref[g])
            sz, dsz = _silu_and_grad(pz_ref[:, sl])
            dyv, sc = dy_ref[:, sl], sc_ref[:, sl]
            d_out = dyv * sc * sz
            dsc_ref[:, sl] += jnp.sum(dyv * out * sz, axis=0, keepdims=True)
            dpz_ref[:, sl] = (dyv * out * sc * dsz).astype(BF16)
            dw_ref[g] += _dot_tn(mixed, d_out)
            e_ref[:, sl] = _dot_nt(d_out, w_ref[g]) / cnt

    return pl.pallas_call(
        body, grid=(S // ts,),
        in_specs=[_col(ts, BRANCH_W, offs["pv"]), _halo_before(ts, offs["pv"]), _col(ts, BRANCH_W, offs["pz"]),
                  _whole((4, POOL_GW, POOL_GW)), _whole((1, BRANCH_W)), _col(ts, BRANCH_W, 0)],
        out_specs=[_col(ts, BRANCH_W, 0), _col(ts, BRANCH_W, 0), _whole((4, POOL_GW, POOL_GW)), _whole((1, BRANCH_W))],
        out_shape=[jax.ShapeDtypeStruct((S, BRANCH_W), BF16), jax.ShapeDtypeStruct((S, BRANCH_W), F32),
                   jax.ShapeDtypeStruct((4, POOL_GW, POOL_GW), F32), jax.ShapeDtypeStruct((1, BRANCH_W), F32)],
        scratch_shapes=[pltpu.VMEM((HALO + ts, BRANCH_W), F32)], name=name, compiler_params=_params(("arbitrary",)),
    )(proj, proj, proj, pool_w, pool_scale, dy)


def _pool_bwd_b(e, name):
    S = e.shape[0]
    ts = _pick(S, 512, HALO)
    n_tiles = S // ts

    def body(e_ref, halo_ref, dpv_ref, scr):
        i = pl.program_id(0)
        scr[0:ts, :] = e_ref[...]
        scr[ts:ts + HALO, :] = jnp.where(i < n_tiles - 1, halo_ref[...], 0.0)
        t = _row_index(i, ts)
        for g in range(4):
            w, lo = POOL_WINDOWS[g], g * POOL_GW
            acc = scr[0:ts, lo:lo + POOL_GW]
            ev = acc
            for k in range(1, w):
                acc = acc + scr[k:k + ts, lo:lo + POOL_GW]
            cnt = jnp.minimum(t + 1, w).astype(F32)
            dpv_ref[:, lo:lo + POOL_GW] = (acc - ev * cnt).astype(BF16)

    return pl.pallas_call(
        body, grid=(n_tiles,), in_specs=[_col(ts, BRANCH_W, 0), _halo_after(ts, 0, n_tiles)], out_specs=_col(ts, BRANCH_W, 0),
        out_shape=jax.ShapeDtypeStruct((S, BRANCH_W), BF16), scratch_shapes=[pltpu.VMEM((ts + HALO, BRANCH_W), F32)],
        name=name, compiler_params=_params(("parallel",)),
    )(e, e)


def _conv_fwd(proj, offs, conv_w, name):
    S = proj.shape[0]
    ts = _pick(S, 512, HALO)

    def body(cb_ref, cc_ref, cx_ref, cz_ref, hc_ref, hx_ref, w_ref, y_ref, scr):
        i = pl.program_id(0)
        scr[0:HALO, :] = jnp.where(i > 0, hc_ref[...] * hx_ref[...], 0.0)
        scr[HALO:HALO + ts, :] = cc_ref[...] * cx_ref[...]
        y = w_ref[0:1, :] * scr[HALO - 2:HALO - 2 + ts, :] + w_ref[1:2, :] * scr[HALO - 1:HALO - 1 + ts, :] + w_ref[2:3, :] * scr[HALO:HALO + ts, :]
        y_ref[...] = (cb_ref[...] * y * _silu(cz_ref[...])).astype(BF16)

    return pl.pallas_call(
        body, grid=(S // ts,),
        in_specs=[_col(ts, BRANCH_W, offs["cb"]), _col(ts, BRANCH_W, offs["cc"]), _col(ts, BRANCH_W, offs["cx"]),
                  _col(ts, BRANCH_W, offs["cz"]), _halo_before(ts, offs["cc"]), _halo_before(ts, offs["cx"]), _whole((3, BRANCH_W))],
        out_specs=_col(ts, BRANCH_W, 0), out_shape=jax.ShapeDtypeStruct((S, BRANCH_W), BF16),
        scratch_shapes=[pltpu.VMEM((HALO + ts, BRANCH_W), F32)], name=name, compiler_params=_params(("parallel",)),
    )(proj, proj, proj, proj, proj, proj, conv_w)


def _conv_bwd(proj, offs, conv_w, dy, name):
    S = proj.shape[0]
    ts = _pick(S, 256, HALO)
    n_tiles = S // ts

    def body(cb_ref, cc_ref, cx_ref, cz_ref, hc_ref, hx_ref, w_ref, dy_ref, ady_ref, acb_ref, acz_ref,
             dcb_ref, dcc_ref, dcx_ref, dcz_ref, dw_ref, scr_u, scr_d):
        i = pl.program_id(0)
        u = cc_ref[...] * cx_ref[...]
        scr_u[0:HALO, :] = jnp.where(i > 0, hc_ref[...] * hx_ref[...], 0.0)
        scr_u[HALO:HALO + ts, :] = u
        u1 = scr_u[HALO - 1:HALO - 1 + ts, :]
        u2 = scr_u[HALO - 2:HALO - 2 + ts, :]
        y = w_ref[0:1, :] * u2 + w_ref[1:2, :] * u1 + w_ref[2:3, :] * u
        sz, dsz = _silu_and_grad(cz_ref[...])
        dyv, cb = dy_ref[...], cb_ref[...]
        dcb_ref[...] = (dyv * y * sz).astype(BF16)
        dcz_ref[...] = (dyv * cb * y * dsz).astype(BF16)
        d_y = dyv * cb * sz
        scr_d[0:ts, :] = d_y
        scr_d[ts:ts + HALO, :] = jnp.where(i < n_tiles - 1, ady_ref[...] * acb_ref[...] * _silu(acz_ref[...]), 0.0)
        du = w_ref[2:3, :] * d_y + w_ref[1:2, :] * scr_d[1:1 + ts, :] + w_ref[0:1, :] * scr_d[2:2 + ts, :]
        dcc_ref[...] = (du * cx_ref[...]).astype(BF16)
        dcx_ref[...] = (du * cc_ref[...]).astype(BF16)

        @pl.when(i == 0)
        def _():
            dw_ref[...] = jnp.zeros_like(dw_ref)

        dw_ref[0:1, :] += jnp.sum(d_y * u2, axis=0, keepdims=True)
        dw_ref[1:2, :] += jnp.sum(d_y * u1, axis=0, keepdims=True)
        dw_ref[2:3, :] += jnp.sum(d_y * u, axis=0, keepdims=True)

    row = lambda off: _col(ts, BRANCH_W, off)
    return pl.pallas_call(
        body, grid=(n_tiles,),
        in_specs=[row(offs["cb"]), row(offs["cc"]), row(offs["cx"]), row(offs["cz"]), _halo_before(ts, offs["cc"]),
                  _halo_before(ts, offs["cx"]), _whole((3, BRANCH_W)), row(0), _halo_after(ts, 0, n_tiles),
                  _halo_after(ts, offs["cb"], n_tiles), _halo_after(ts, offs["cz"], n_tiles)],
        out_specs=[row(0), row(0), row(0), row(0), _whole((3, BRANCH_W))],
        out_shape=[jax.ShapeDtypeStruct((S, BRANCH_W), BF16)] * 4 + [jax.ShapeDtypeStruct((3, BRANCH_W), F32)],
        scratch_shapes=[pltpu.VMEM((HALO + ts, BRANCH_W), F32), pltpu.VMEM((ts + HALO, BRANCH_W), F32)], name=name,
        compiler_params=_params(("arbitrary",)),
    )(proj, proj, proj, proj, proj, proj, conv_w, dy, dy, proj, proj)


def _xattn_head(xq, kraw, v, qg, kg):
    qn, rq = _rms(xq, qg, XATTN_HD)
    kn, rk = _rms(kraw, kg, XATTN_HD)
    s = _dot_nt(qn, kn) * (XATTN_HD ** -0.5)
    e = jnp.exp(s - jnp.max(s, axis=-1, keepdims=True))
    p = e / jnp.sum(e, axis=-1, keepdims=True)
    return qn, rq, kn, rk, p, _dot(p, v)


def _xattn_fwd(proj, offs, mem_kv, qg, kg, name):
    S, M = proj.shape[0], mem_kv.shape[0]
    tm = _pick(S, 512, 16)

    def body(xq_ref, xz_ref, kv_ref, qg_ref, kg_ref, y_ref):
        for h in range(XATTN_HEADS):
            sl = slice(h * XATTN_HD, (h + 1) * XATTN_HD)
            vs = slice(BRANCH_W + h * XATTN_HD, BRANCH_W + (h + 1) * XATTN_HD)
            o = _xattn_head(xq_ref[:, sl], kv_ref[:, sl], kv_ref[:, vs], qg_ref[...], kg_ref[...])[-1]
            y_ref[:, sl] = (o * _silu(xz_ref[:, sl])).astype(BF16)

    return pl.pallas_call(
        body, grid=(S // tm,),
        in_specs=[_col(tm, BRANCH_W, offs["xq"]), _col(tm, BRANCH_W, offs["xz"]), _whole((M, 2 * BRANCH_W)),
                  _whole((1, XATTN_HD)), _whole((1, XATTN_HD))],
        out_specs=_col(tm, BRANCH_W, 0), out_shape=jax.ShapeDtypeStruct((S, BRANCH_W), BF16), name=name,
        compiler_params=_params(("parallel",)),
    )(proj, proj, mem_kv, qg, kg)


def _xattn_bwd(proj, offs, mem_kv, qg, kg, dy, name):
    S, M = proj.shape[0], mem_kv.shape[0]
    tm = _pick(S, 256, 16)
    n_tiles = S // tm

    def body(xq_ref, xz_ref, kv_ref, qg_ref, kg_ref, dy_ref, dxq_ref, dxz_ref, dkv_ref, dqg_ref, dkg_ref, dkn_acc):
        i = pl.program_id(0)

        @pl.when(i == 0)
        def _():
            dkv_ref[...] = jnp.zeros_like(dkv_ref)
            dqg_ref[...] = jnp.zeros_like(dqg_ref)
            dkg_ref[...] = jnp.zeros_like(dkg_ref)
            dkn_acc[...] = jnp.zeros_like(dkn_acc)

        qg_v, kg_v = qg_ref[...], kg_ref[...]
        for h in range(XATTN_HEADS):
            sl = slice(h * XATTN_HD, (h + 1) * XATTN_HD)
            vs = slice(BRANCH_W + h * XATTN_HD, BRANCH_W + (h + 1) * XATTN_HD)
            xq, v = xq_ref[:, sl], kv_ref[:, vs]
            qn, rq, kn, rk, p, o = _xattn_head(xq, kv_ref[:, sl], v, qg_v, kg_v)
            sz, dsz = _silu_and_grad(xz_ref[:, sl])
            dyv = dy_ref[:, sl]
            dxz_ref[:, sl] = (dyv * o * dsz).astype(BF16)
            do = dyv * sz
            dkv_ref[:, vs] += _dot_tn(p, do)
            dp = _dot_nt(do, v)
            ds = p * (dp - jnp.sum(dp * p, axis=-1, keepdims=True)) * (XATTN_HD ** -0.5)
            dqn = _dot(ds, kn)
            dkn_acc[:, sl] += _dot_tn(ds, qn)
            dxq, dqg = _rms_bwd(xq, qg_v, rq, dqn, XATTN_HD)
            dxq_ref[:, sl] = dxq.astype(BF16)
            dqg_ref[...] += dqg

        @pl.when(i == n_tiles - 1)
        def _():
            for h in range(XATTN_HEADS):
                sl = slice(h * XATTN_HD, (h + 1) * XATTN_HD)
                kraw = kv_ref[:, sl]
                _, rk = _rms(kraw, kg_v, XATTN_HD)
                dk, dkg = _rms_bwd(kraw, kg_v, rk, dkn_acc[:, sl], XATTN_HD)
                dkv_ref[:, sl] = dk
                dkg_ref[...] += dkg

    return pl.pallas_call(
        body, grid=(n_tiles,),
        in_specs=[_col(tm, BRANCH_W, offs["xq"]), _col(tm, BRANCH_W, offs["xz"]), _whole((M, 2 * BRANCH_W)),
                  _whole((1, XATTN_HD)), _whole((1, XATTN_HD)), _col(tm, BRANCH_W, 0)],
        out_specs=[_col(tm, BRANCH_W, 0), _col(tm, BRANCH_W, 0), _whole((M, 2 * BRANCH_W)), _whole((1, XATTN_HD)), _whole((1, XATTN_HD))],
        out_shape=[jax.ShapeDtypeStruct((S, BRANCH_W), BF16), jax.ShapeDtypeStruct((S, BRANCH_W), BF16),
                   jax.ShapeDtypeStruct((M, 2 * BRANCH_W), F32), jax.ShapeDtypeStruct((1, XATTN_HD), F32), jax.ShapeDtypeStruct((1, XATTN_HD), F32)],
        scratch_shapes=[pltpu.VMEM((M, BRANCH_W), F32)], name=name, compiler_params=_params(("arbitrary",)),
    )(proj, proj, mem_kv, qg, kg, dy)


def _rope(c, tc, ta, tb):
    return c * tc + pltpu.roll(c, 96, 1) * ta + pltpu.roll(c, 32, 1) * tb


def _rope_t(d, tc, ta, tb):
    return d * tc + pltpu.roll(d * ta, 32, 1) + pltpu.roll(d * tb, 96, 1)


def _mla_qk_fwd(qp, kvp, proj, offs, tabs, qg, kg, name):
    S = qp.shape[0]
    tm = _pick(S, 256, 16)
    W = MLA_HEADS * QK_PAD

    def body(q_ref, kv_ref, kr_ref, tc_ref, ta_ref, tb_ref, qg_ref, kg_ref, qo_ref, ko_ref, vo_ref):
        tc, ta, tb = tc_ref[...], ta_ref[...], tb_ref[...]
        qg_v, kg_v = qg_ref[...], kg_ref[...]
        kr = kr_ref[...]
        kr_ss = jnp.sum(kr * kr, axis=-1, keepdims=True)
        for h in range(MLA_HEADS):
            sl = slice(h * QK_PAD, (h + 1) * QK_PAD)
            qn, _ = _rms(q_ref[:, sl], qg_v, QK_HEAD)
            qo_ref[:, sl] = jnp.concatenate([qn[:, :QK_NOPE], _rope(qn[:, QK_NOPE:], tc, ta, tb)], axis=1).astype(BF16)
            kn = kv_ref[:, h * QK_NOPE:(h + 1) * QK_NOPE]
            r = lax.rsqrt((jnp.sum(kn * kn, axis=-1, keepdims=True) + kr_ss) / QK_HEAD + EPS)
            ko_ref[:, sl] = jnp.concatenate([kn * r * kg_v[:, :QK_NOPE], _rope(kr * r * kg_v[:, QK_NOPE:], tc, ta, tb)], axis=1).astype(BF16)
        vo_ref[...] = kv_ref[:, MLA_HEADS * QK_NOPE:].astype(BF16)

    return pl.pallas_call(
        body, grid=(S // tm,),
        in_specs=[_col(tm, W, 0), _col(tm, W, 0), _col(tm, 128, offs["kr"]), _col(tm, 128, 0), _col(tm, 128, 0), _col(tm, 128, 0),
                  _whole((1, QK_PAD)), _whole((1, QK_PAD))],
        out_specs=[_col(tm, W, 0), _col(tm, W, 0), _col(tm, BRANCH_W, 0)],
        out_shape=[jax.ShapeDtypeStruct((S, W), BF16), jax.ShapeDtypeStruct((S, W), BF16), jax.ShapeDtypeStruct((S, BRANCH_W), BF16)],
        name=name, compiler_params=_params(("parallel",)),
    )(qp, kvp, proj, *tabs, qg, kg)


def _mla_qk_bwd(qp, kvp, proj, offs, tabs, qg, kg, dQ, dK, dV, name):
    S = qp.shape[0]
    tm = _pick(S, 256, 16)
    W = MLA_HEADS * QK_PAD

    def body(q_ref, kv_ref, kr_ref, tc_ref, ta_ref, tb_ref, qg_ref, kg_ref, dQ_ref, dK_ref, dV_ref,
             dq_ref, dkv_ref, dkr_ref, dqg_ref, dkg_ref):
        @pl.when(pl.program_id(0) == 0)
        def _():
            dqg_ref[...] = jnp.zeros_like(dqg_ref)
            dkg_ref[...] = jnp.zeros_like(dkg_ref)

        tc, ta, tb = tc_ref[...], ta_ref[...], tb_ref[...]
        qg_v, kg_v = qg_ref[...], kg_ref[...]
        kg_n, kg_r = kg_v[:, :QK_NOPE], kg_v[:, QK_NOPE:]
        kr = kr_ref[...]
        kr_ss = jnp.sum(kr * kr, axis=-1, keepdims=True)
        dkr = jnp.zeros_like(kr)
        dkg_n = jnp.zeros((1, QK_NOPE), F32)
        dkg_r = jnp.zeros((1, QK_NOPE), F32)
        for h in range(MLA_HEADS):
            sl = slice(h * QK_PAD, (h + 1) * QK_PAD)
            qv = q_ref[:, sl]
            _, rq = _rms(qv, qg_v, QK_HEAD)
            dQv = dQ_ref[:, sl]
            dqn = jnp.concatenate([dQv[:, :QK_NOPE], _rope_t(dQv[:, QK_NOPE:], tc, ta, tb)], axis=1)
            dq, dqg = _rms_bwd(qv, qg_v, rq, dqn, QK_HEAD)
            dq_ref[:, sl] = dq.astype(BF16)
            dqg_ref[...] += dqg
            ns = slice(h * QK_NOPE, (h + 1) * QK_NOPE)
            kn = kv_ref[:, ns]
            r = lax.rsqrt((jnp.sum(kn * kn, axis=-1, keepdims=True) + kr_ss) / QK_HEAD + EPS)
            dKv = dK_ref[:, sl]
            dyn = dKv[:, :QK_NOPE]
            dyr = _rope_t(dKv[:, QK_NOPE:], tc, ta, tb)
            proj_s = (jnp.sum(dyn * kg_n * kn, axis=-1, keepdims=True) + jnp.sum(dyr * kg_r * kr, axis=-1, keepdims=True)) / QK_HEAD
            r3 = r * r * r
            dkv_ref[:, ns] = (r * dyn * kg_n - kn * r3 * proj_s).astype(BF16)
            dkr = dkr + (r * dyr * kg_r - kr * r3 * proj_s)
            dkg_n = dkg_n + jnp.sum(dyn * kn * r, axis=0, keepdims=True)
            dkg_r = dkg_r + jnp.sum(dyr * kr * r, axis=0, keepdims=True)
        dkg_ref[...] += jnp.concatenate([dkg_n, dkg_r], axis=1)
        dkr_ref[...] = dkr.astype(BF16)
        dkv_ref[:, MLA_HEADS * QK_NOPE:] = dV_ref[...].astype(BF16)

    return pl.pallas_call(
        body, grid=(S // tm,),
        in_specs=[_col(tm, W, 0), _col(tm, W, 0), _col(tm, 128, offs["kr"]), _col(tm, 128, 0), _col(tm, 128, 0), _col(tm, 128, 0),
                  _whole((1, QK_PAD)), _whole((1, QK_PAD)), _col(tm, W, 0), _col(tm, W, 0), _col(tm, BRANCH_W, 0)],
        out_specs=[_col(tm, W, 0), _col(tm, W, 0), _col(tm, 128, 0), _whole((1, QK_PAD)), _whole((1, QK_PAD))],
        out_shape=[jax.ShapeDtypeStruct((S, W), BF16), jax.ShapeDtypeStruct((S, W), BF16), jax.ShapeDtypeStruct((S, 128), BF16),
                   jax.ShapeDtypeStruct((1, QK_PAD), F32), jax.ShapeDtypeStruct((1, QK_PAD), F32)],
        name=name, compiler_params=_params(("arbitrary",)),
    )(qp, kvp, proj, *tabs, qg, kg, dQ, dK, dV)


def _causal_scores(q, k, qi, ki, tq, tk):
    s = _dot_nt(q, k) * SCORE_SCALE_LOG2
    rows = qi * tq + lax.broadcasted_iota(jnp.int32, (tq, tk), 0)
    cols = ki * tk + lax.broadcasted_iota(jnp.int32, (tq, tk), 1)
    return s, cols <= rows


def _flash_fwd(Q, K, V, proj, offs, name):
    S = Q.shape[0]
    t = _pick(S, 1024, 128)
    chunks = t // 128
    mz_cb = offs["mz"] // V_HEAD

    def fold(x, op):
        r = x[:, 0:128]
        for ch in range(1, chunks):
            r = op(r, x[:, ch * 128:(ch + 1) * 128])
        return r

    def body(q_ref, k_ref, v_ref, mz_ref, o_ref, y_ref, lse_ref, mp_scr, lp_scr, acc_scr):
        qi = pl.program_id(1)
        q = q_ref[...]

        def raw_scores(j, masked):
            rows_k = pl.ds(pl.multiple_of(j * t, t), t)
            s = _dot_nt(q, k_ref[rows_k, :])
            if masked:
                rows = lax.broadcasted_iota(jnp.int32, (t, t), 0)
                cols = lax.broadcasted_iota(jnp.int32, (t, t), 1)
                s = jnp.where(cols <= rows, s, NEG)
            return s, rows_k

        mp_scr[...] = jnp.full_like(mp_scr, NEG)

        def max_sweep(j, carry):
            s, _ = raw_scores(j, False)
            mp_scr[...] = jnp.maximum(mp_scr[...], fold(s, jnp.maximum))
            return carry

        lax.fori_loop(0, qi, max_sweep, 0)
        s_diag, _ = raw_scores(qi, True)
        m = jnp.max(jnp.maximum(mp_scr[...], fold(s_diag, jnp.maximum)), axis=-1, keepdims=True) * SCORE_SCALE_LOG2

        lp_scr[...] = jnp.zeros_like(lp_scr)
        acc_scr[...] = jnp.zeros_like(acc_scr)

        def accumulate(j, masked):
            s, rows_k = raw_scores(j, masked)
            p = jnp.exp2(s * SCORE_SCALE_LOG2 - m)
            lp_scr[...] += fold(p, jnp.add)
            acc_scr[...] += _dot(p, v_ref[rows_k, :])

        def sum_sweep(j, carry):
            accumulate(j, False)
            return carry

        lax.fori_loop(0, qi, sum_sweep, 0)
        accumulate(qi, True)
        l = jnp.sum(lp_scr[...], axis=-1, keepdims=True)
        o = acc_scr[...] / l
        o_ref[...] = o
        y_ref[...] = (o * _silu(mz_ref[...])).astype(BF16)
        lse_ref[0] = m + jnp.log2(l)

    return pl.pallas_call(
        body, grid=(MLA_HEADS, S // t),
        in_specs=[pl.BlockSpec((t, QK_PAD), lambda h, i: (i, h)), pl.BlockSpec((S, QK_PAD), lambda h, i: (0, h)),
                  pl.BlockSpec((S, V_HEAD), lambda h, i: (0, h)), pl.BlockSpec((t, V_HEAD), lambda h, i: (i, mz_cb + h))],
        out_specs=[pl.BlockSpec((t, V_HEAD), lambda h, i: (i, h)), pl.BlockSpec((t, V_HEAD), lambda h, i: (i, h)),
                   pl.BlockSpec((1, t, 1), lambda h, i: (h, i, 0))],
        out_shape=[jax.ShapeDtypeStruct((S, BRANCH_W), F32), jax.ShapeDtypeStruct((S, BRANCH_W), BF16),
                   jax.ShapeDtypeStruct((MLA_HEADS, S, 1), F32)],
        scratch_shapes=[pltpu.VMEM((t, 128), F32), pltpu.VMEM((t, 128), F32), pltpu.VMEM((t, V_HEAD), F32)], name=name,
        compiler_params=_params(("parallel", "parallel")),
    )(Q, K, V, proj)


def _mla_gate_bwd(o, proj, offs, dy, name):
    S = o.shape[0]
    tm = _pick(S, 512, 16)

    def body(o_ref, mz_ref, dy_ref, do_ref, dmz_ref, dl_ref):
        sz, dsz = _silu_and_grad(mz_ref[...])
        dyv, ov = dy_ref[...], o_ref[...]
        do = (dyv * sz).astype(BF16)
        do_ref[...] = do
        dmz_ref[...] = (dyv * ov * dsz).astype(BF16)
        prod = do.astype(F32) * ov
        for h in range(MLA_HEADS):
            dl_ref[h] = jnp.sum(prod[:, h * V_HEAD:(h + 1) * V_HEAD], axis=-1, keepdims=True)

    return pl.pallas_call(
        body, grid=(S // tm,), in_specs=[_col(tm, BRANCH_W, 0), _col(tm, BRANCH_W, offs["mz"]), _col(tm, BRANCH_W, 0)],
        out_specs=[_col(tm, BRANCH_W, 0), _col(tm, BRANCH_W, 0), pl.BlockSpec((MLA_HEADS, tm, 1), lambda i: (0, i, 0))],
        out_shape=[jax.ShapeDtypeStruct((S, BRANCH_W), BF16), jax.ShapeDtypeStruct((S, BRANCH_W), BF16),
                   jax.ShapeDtypeStruct((MLA_HEADS, S, 1), F32)], name=name,
        compiler_params=_params(("parallel",)),
    )(o, proj, dy)


def _flash_bwd(Q, K, V, lse, delta, do, name, comm=None):
    S = Q.shape[0]
    tk = _pick(S, 1024, 128)
    nq = S // tk
    once = pl.Buffered(1)

    def body(q_ref, k_ref, v_ref, lse_ref, dl_ref, do_ref, dq_ref, dk_ref, dv_ref, dk_acc, dv_acc):
        ki = pl.program_id(1)

        @pl.when(ki == 0)
        def _():
            dq_ref[...] = jnp.zeros_like(dq_ref)

        dk_acc[...] = jnp.zeros_like(dk_acc)
        dv_acc[...] = jnp.zeros_like(dv_acc)
        k, v = k_ref[...], v_ref[...]

        def block(i, masked):
            rows = pl.ds(pl.multiple_of(i * tk, tk), tk)
            q, dov = q_ref[rows, :], do_ref[rows, :]
            s, mask = _causal_scores(q, k, i, ki, tk, tk)
            if masked:
                s = jnp.where(mask, s, NEG)
            p = jnp.exp2(s - lse_ref[0, rows, :])
            dv_acc[...] += _dot_tn(p, dov)
            dp = _dot_nt(dov, v)
            ds = p * (dp - dl_ref[0, rows, :]) * (QK_HEAD ** -0.5)
            dk_acc[...] += _dot_tn(ds, q)
            dq_ref[rows, :] += _dot(ds, k)

        def below_diagonal(i, carry):
            block(i, False)
            return carry

        block(ki, True)
        lax.fori_loop(ki + 1, nq, below_diagonal, 0)
        dk_ref[...] = dk_acc[...]
        dv_ref[...] = dv_acc[...]

    in_specs = [pl.BlockSpec((S, QK_PAD), lambda h, j: (0, h), pipeline_mode=once), pl.BlockSpec((tk, QK_PAD), lambda h, j: (j, h)),
                pl.BlockSpec((tk, V_HEAD), lambda h, j: (j, h)), pl.BlockSpec((1, S, 1), lambda h, j: (h, 0, 0), pipeline_mode=once),
                pl.BlockSpec((1, S, 1), lambda h, j: (h, 0, 0), pipeline_mode=once),
                pl.BlockSpec((S, V_HEAD), lambda h, j: (0, h), pipeline_mode=once)]
    out_specs = [pl.BlockSpec((S, QK_PAD), lambda h, j: (0, h)), pl.BlockSpec((tk, QK_PAD), lambda h, j: (j, h)),
                 pl.BlockSpec((tk, V_HEAD), lambda h, j: (j, h))]
    out_shape = [jax.ShapeDtypeStruct((S, MLA_HEADS * QK_PAD), F32), jax.ShapeDtypeStruct((S, MLA_HEADS * QK_PAD), F32),
                 jax.ShapeDtypeStruct((S, BRANCH_W), F32)]
    scratch = [pltpu.VMEM((tk, QK_PAD), F32), pltpu.VMEM((tk, V_HEAD), F32)]
    ins = [Q, K, V, lse, delta, do]
    if comm is not None:
        return _hosted_call(comm, body, ins, in_specs, out_specs, out_shape, scratch, (MLA_HEADS, nq), name)
    res = pl.pallas_call(
        body, grid=(MLA_HEADS, nq), in_specs=in_specs, out_specs=out_specs, out_shape=out_shape, scratch_shapes=scratch, name=name,
        compiler_params=_params(("parallel", "arbitrary")),
    )(*ins)
    return list(res), []


def _merge_specs(S, D, offs, tm, tn, layer):
    g_cb = offs["g"] // tn
    nj = D // tn
    y_specs = [pl.BlockSpec((tm, BRANCH_W), lambda j, i: (i, 0)) for _ in range(4)]
    w_spec = pl.BlockSpec((None, 4, BRANCH_W, tn), lambda j, i: (layer, 0, 0, j))
    g_specs = [pl.BlockSpec((tm, tn), functools.partial(lambda j, i, b: (i, g_cb + b * nj + j), b=b)) for b in range(4)]
    bias_spec = pl.BlockSpec((4, tn), lambda j, i: (0, j))
    return y_specs, w_spec, g_specs, bias_spec


def _merge_fwd(ys, wb, layer, proj, offs, gate_b, name):
    S, D = ys[0].shape[0], wb.shape[3]
    tm, tn = _pick(S, 512, 16), _pick(D, 512, 128)
    y_specs, w_spec, g_specs, bias_spec = _merge_specs(S, D, offs, tm, tn, layer)

    def body(y0, y1, y2, y3, w_ref, g0, g1, g2, g3, b_ref, o_ref):
        acc = None
        for b, (y_ref, g_ref) in enumerate(zip((y0, y1, y2, y3), (g0, g1, g2, g3))):
            gate = 1.0 / (1.0 + jnp.exp(-(g_ref[...] + b_ref[b:b + 1, :])))
            term = gate * _dot(y_ref[...], w_ref[b])
            acc = term if acc is None else acc + term
        o_ref[...] = acc.astype(BF16)

    return pl.pallas_call(
        body, grid=(D // tn, S // tm), in_specs=y_specs + [w_spec] + g_specs + [bias_spec],
        out_specs=pl.BlockSpec((tm, tn), lambda j, i: (i, j)), out_shape=jax.ShapeDtypeStruct((S, D), BF16), name=name,
        compiler_params=_params(("parallel", "parallel")),
    )(*ys, wb, proj, proj, proj, proj, gate_b)


def _merge_bwd(ys, wb, layer, proj, offs, gate_b, dmerged, name):
    S, D = ys[0].shape[0], wb.shape[3]
    tm, tn = _pick(S, 512, 16), _pick(D, 512, 128)
    nj = D // tn
    y_specs, w_spec, g_specs, bias_spec = _merge_specs(S, D, offs, tm, tn, layer)

    def body(y0, y1, y2, y3, w_ref, g0, g1, g2, g3, b_ref, dm_ref, dt_ref, dg_ref, db_ref):
        @pl.when(pl.program_id(1) == 0)
        def _():
            db_ref[...] = jnp.zeros_like(db_ref)

        dm = dm_ref[...]
        for b, (y_ref, g_ref) in enumerate(zip((y0, y1, y2, y3), (g0, g1, g2, g3))):
            gate = 1.0 / (1.0 + jnp.exp(-(g_ref[...] + b_ref[b:b + 1, :])))
            t = _dot(y_ref[...], w_ref[b])
            dt_ref[b] = (dm * gate).astype(BF16)
            dgp = dm * t * gate * (1.0 - gate)
            dg_ref[b] = dgp.astype(BF16)
            db_ref[b:b + 1, :] += jnp.sum(dgp, axis=0, keepdims=True)

    stacked = pl.BlockSpec((4, tm, tn), lambda j, i: (0, i, j))
    return pl.pallas_call(
        body, grid=(nj, S // tm),
        in_specs=y_specs + [w_spec] + g_specs + [bias_spec, pl.BlockSpec((tm, tn), lambda j, i: (i, j))],
        out_specs=[stacked, stacked, pl.BlockSpec((4, tn), lambda j, i: (0, j))],
        out_shape=[jax.ShapeDtypeStruct((4, S, D), BF16), jax.ShapeDtypeStruct((4, S, D), BF16), jax.ShapeDtypeStruct((4, D), F32)],
        name=name, compiler_params=_params(("parallel", "arbitrary")),
    )(*ys, wb, proj, proj, proj, proj, gate_b, dmerged)


HBM_SPEC = pl.BlockSpec(memory_space=pl.ANY)


def _mesh_place():
    x, y, c = lax.axis_index("x"), lax.axis_index("y"), lax.axis_index("c")
    return x, y, c, [(1 - x, y), (x, 1 - y), (1 - x, 1 - y)]


def _remote(src, dst, send_sem, recv_sem, to):
    return pltpu.make_async_remote_copy(src_ref=src, dst_ref=dst, send_sem=send_sem, recv_sem=recv_sem, device_id=to,
                                        device_id_type=pl.DeviceIdType.MESH)


class _GatherPlan:
    def __init__(self, arrs):
        self.arrs, self.n = list(arrs), len(arrs)
        self.out_shape = [jax.ShapeDtypeStruct((N_DEV,) + a.shape, a.dtype) for a in arrs]
        self.scratch = [pltpu.SemaphoreType.DMA((self.n, 7)), pltpu.SemaphoreType.DMA((self.n, 7)), pltpu.SemaphoreType.DMA((self.n,))]

    def _copies(self, ins, dsts, sems):
        send_sems, recv_sems, local_sems = sems
        x, y, c, chips = _mesh_place()
        slot = lambda px, py, pc: 4 * px + 2 * py + pc
        me, sibling = slot(x, y, c), (x, y, 1 - c)

        def copy(a, k, src, block, to):
            return lambda: _remote(src(), dsts[a].at[block], send_sems.at[a, k], recv_sems.at[a, k], to)

        rng = range(self.n)
        shard = lambda a: (lambda: ins[a])
        local = [(lambda a=a: pltpu.make_async_copy(ins[a], dsts[a].at[me], local_sems.at[a])) for a in rng]
        own = [copy(a, 0, shard(a), me, sibling) for a in rng]
        own += [copy(a, 1 + j, shard(a), me, (*chip, c)) for a in rng for j, chip in enumerate(chips)]
        landed = [copy(a, 1 + j, shard(a), slot(*chip, c), sibling) for a in rng for j, chip in enumerate(chips)]
        passed = [copy(a, 4 + j, (lambda a=a, chip=chip: dsts[a].at[slot(*chip, c)]), slot(*chip, c), sibling)
                  for a in rng for j, chip in enumerate(chips)]
        from_sibling = [copy(a, 0, shard(a), slot(x, y, 1 - c), sibling) for a in rng]
        from_sibling += [copy(a, 4 + j, shard(a), slot(*chip, 1 - c), sibling) for a in rng for j, chip in enumerate(chips)]
        return local, own, landed, passed, from_sibling

    def start(self, ins, dsts, sems):
        local, own, _, _, _ = self._copies(ins, dsts, sems)
        for make in local + own:
            make().start()

    def relay(self, ins, dsts, sems):
        _, _, landed, passed, _ = self._copies(ins, dsts, sems)
        for arrived, onward in zip(landed, passed):
            arrived().wait_recv()
            onward().start()

    def finish(self, ins, dsts, sems):
        local, own, _, passed, from_sibling = self._copies(ins, dsts, sems)
        for make in from_sibling:
            make().wait_recv()
        for make in own + passed:
            make().wait_send()
        for make in local:
            make().wait()


class _ChipExchangePlan:
    def __init__(self, arrs):
        self.arrs, self.n = list(arrs), len(arrs)
        self.out_shape = [jax.ShapeDtypeStruct(a.shape, a.dtype) for a in arrs]
        self.scratch = [pltpu.SemaphoreType.DMA((self.n, 3)), pltpu.SemaphoreType.DMA((self.n, 3)), pltpu.SemaphoreType.DMA((self.n,))]

    def _copies(self, ins, dsts, sems):
        send_sems, recv_sems, local_sems = sems
        x, y, c, chips = _mesh_place()
        mine = 2 * x + y
        local = [pltpu.make_async_copy(ins[a].at[mine], dsts[a].at[mine], local_sems.at[a]) for a in range(self.n)]
        remote = [_remote(ins[a].at[2 * px + py], dsts[a].at[mine], send_sems.at[a, j], recv_sems.at[a, j], (px, py, c))
                  for a in range(self.n) for j, (px, py) in enumerate(chips)]
        return local, remote

    def start(self, ins, dsts, sems):
        local, remote = self._copies(ins, dsts, sems)
        for cp in local + remote:
            cp.start()

    def relay(self, ins, dsts, sems):
        pass

    def finish(self, ins, dsts, sems):
        local, remote = self._copies(ins, dsts, sems)
        for cp in remote + local:
            cp.wait()


def _run_plan(plan, name):
    n = plan.n

    def body(*refs):
        ins, dsts, sems = refs[:n], refs[n:2 * n], refs[2 * n:]
        plan.start(ins, dsts, sems)
        plan.relay(ins, dsts, sems)
        plan.finish(ins, dsts, sems)

    res = pl.pallas_call(body, in_specs=[HBM_SPEC] * n, out_specs=[HBM_SPEC] * n, out_shape=plan.out_shape,
                         scratch_shapes=plan.scratch, name=name)(*plan.arrs)
    return list(res)


def _sibling_swap(arrs, name):
    n = len(arrs)

    def body(*refs):
        ins, dsts = refs[:n], refs[n:2 * n]
        send_sems, recv_sems = refs[2 * n:]
        x, y, c, _ = _mesh_place()
        cps = [_remote(ins[a].at[q, 1 - c], dsts[a].at[q], send_sems.at[a, q], recv_sems.at[a, q], (x, y, 1 - c))
               for a in range(n) for q in range(4)]
        for cp in cps:
            cp.start()
        for cp in cps:
            cp.wait()

    res = pl.pallas_call(
        body, in_specs=[HBM_SPEC] * n, out_specs=[HBM_SPEC] * n,
        out_shape=[jax.ShapeDtypeStruct((4,) + a.shape[2:], a.dtype) for a in arrs],
        scratch_shapes=[pltpu.SemaphoreType.DMA((n, 4)), pltpu.SemaphoreType.DMA((n, 4))], name=name,
    )(*arrs)
    return list(res)


def _pair_sum(staged, theirs, core, name):
    _, _, R, C = staged.shape
    tr = _pick(R, max(16, ((1 << 19) // C) // 16 * 16), 16)

    def body(c_ref, a_ref, b_ref, o_ref):
        o_ref[...] = (a_ref[...].astype(F32) + b_ref[...].astype(F32)).astype(BF16)

    blk = pl.BlockSpec((None, tr, C), lambda q, i, c: (q, i, 0))
    grid_spec = pltpu.PrefetchScalarGridSpec(
        num_scalar_prefetch=1, grid=(4, R // tr),
        in_specs=[pl.BlockSpec((None, None, tr, C), lambda q, i, c: (q, c[0], i, 0)), blk], out_specs=blk)
    return pl.pallas_call(
        body, grid_spec=grid_spec, out_shape=jax.ShapeDtypeStruct((4, R, C), BF16), name=name,
        compiler_params=_params(("parallel", "parallel")),
    )(core, staged, theirs)


def _adamw(parts, w, m, v, name):
    R, C = w.shape
    n_parts = parts.shape[0]
    tr = _pick(R, max(16, ((1 << 18) // C) // 16 * 16), 16)

    def body(p_ref, w_ref, m_ref, v_ref, g_ref, d_ref, nm_ref, nv_ref):
        g = p_ref[0].astype(F32)
        for s in range(1, n_parts):
            g = g + p_ref[s].astype(F32)
        mn = ADAM_B1 * m_ref[...] + (1.0 - ADAM_B1) * g
        vn = ADAM_B2 * v_ref[...] + (1.0 - ADAM_B2) * (g * g)
        m_hat = mn / (1.0 - ADAM_B1 ** ADAM_STEP)
        v_hat = vn / (1.0 - ADAM_B2 ** ADAM_STEP)
        g_ref[...] = g
        d_ref[...] = -ADAM_LR * (m_hat / (jnp.sqrt(v_hat) + ADAM_EPS) + ADAM_WD * w_ref[...])
        nm_ref[...] = mn
        nv_ref[...] = vn

    blk = pl.BlockSpec((tr, C), lambda i: (i, 0))
    return pl.pallas_call(
        body, grid=(R // tr,), in_specs=[pl.BlockSpec((n_parts, tr, C), lambda i: (0, i, 0)), blk, blk, blk],
        out_specs=[blk] * 4, out_shape=[jax.ShapeDtypeStruct((R, C), F32)] * 4, name=name,
        compiler_params=_params(("parallel",)),
    )(parts, w, m, v)


def _offsets(L, D):
    o = {"pv": 0, "pz": BRANCH_W, "cq": 2 * BRANCH_W, "ckv": 2 * BRANCH_W + L, "mz": 2 * BRANCH_W + 2 * L}
    for prev, nxt in (("mz", "cb"), ("cb", "cc"), ("cc", "cx"), ("cx", "cz"), ("cz", "xq"), ("xq", "xz"), ("xz", "g")):
        o[nxt] = o[prev] + BRANCH_W
    o["kr"] = o["g"] + 4 * D
    o["n"] = o["kr"] + 128
    return o


def _unshard(gathered, axis):
    moved = jnp.moveaxis(gathered, 0, axis)
    shape = list(moved.shape)
    shape[axis:axis + 2] = [shape[axis] * shape[axis + 1]]
    return moved.reshape(shape)


def _reshard(full, axis):
    shape = list(full.shape)
    shape[axis:axis + 1] = [N_DEV, shape[axis] // N_DEV]
    return jnp.moveaxis(full.reshape(shape), axis, 0)


def _pad_heads(g, width):
    return jnp.pad(g, ((0, 0), (0, width - g.shape[1])))


def kernel(x, mem, positions, norm_g, w_in, gate_b, pool_w, pool_scale, q_a_norm_g, kv_a_norm_g, w_uq, w_ukv, mla_q_norm_g, mla_k_norm_g, conv_w, mem_norm_g, w_mem_kv, xattn_q_norm_g, xattn_k_norm_g, w_branch, w_out, loss_target, m_norm_g, m_w_in, m_gate_b, m_pool_w, m_pool_scale, m_q_a_norm_g, m_kv_a_norm_g, m_w_uq, m_w_ukv, m_mla_q_norm_g, m_mla_k_norm_g, m_conv_w, m_mem_norm_g, m_w_mem_kv, m_xattn_q_norm_g, m_xattn_k_norm_g, m_w_branch, m_w_out, v_norm_g, v_w_in, v_gate_b, v_pool_w, v_pool_scale, v_q_a_norm_g, v_kv_a_norm_g, v_w_uq, v_w_ukv, v_mla_q_norm_g, v_mla_k_norm_g, v_conv_w, v_mem_norm_g, v_w_mem_kv, v_xattn_q_norm_g, v_xattn_k_norm_g, v_w_branch, v_w_out):
    depth = norm_g.shape[0]
    S, D = x.shape[1], x.shape[2]
    L = w_uq.shape[1]
    offs = _offsets(L, D)
    seg_a = 2 * BRANCH_W + 2 * L
    xs = x.reshape(S, D)
    mems = mem.reshape(mem.shape[1], D)

    sharded = dict(w_in=w_in, pool_w=pool_w, w_uq=w_uq, w_ukv=w_ukv, conv_w=conv_w, w_mem_kv=w_mem_kv, w_branch=w_branch, w_out=w_out)
    shard_axis = dict(w_in=2, pool_w=2, w_uq=2, w_ukv=2, conv_w=2, w_mem_kv=1, w_branch=3, w_out=1)
    def permuted_w_in(g):
        f = _unshard(g, 1)
        return jnp.concatenate([f[:, :seg_a], f[:, seg_a + QK_ROPE:], f[:, seg_a:seg_a + QK_ROPE], jnp.zeros((D, 128 - QK_ROPE), BF16)], axis=-1)

    w_in_local = [w_in[l].astype(BF16) for l in range(depth)]
    w_in_p = [permuted_w_in(_run_plan(_GatherPlan([w_in_local[0]]), "gather_w_in0")[0])] + [None] * (depth - 1)
    rest_keys = [k for k in shard_axis if k != "w_in"]
    rest_plan = _GatherPlan([sharded[k] if k == "conv_w" else sharded[k].astype(BF16) for k in rest_keys] + w_in_local[1:])

    inv = ROPE_THETA ** (-jnp.arange(0, QK_ROPE, 2, dtype=F32) / QK_ROPE)
    ang = positions.reshape(S).astype(F32)[:, None] * inv
    cos, sin, z = jnp.cos(ang), jnp.sin(ang), jnp.zeros_like(ang)
    tabs = (jnp.concatenate([cos, cos, z, z], axis=1), jnp.concatenate([-sin, z, z, z], axis=1), jnp.concatenate([z, sin, z, z], axis=1))

    row = lambda a, l: a[l][None, :]
    saved = []
    cur = xs
    for l in range(depth):
        h, hT = _rms_rows(cur, D, 0, row(norm_g, l), f"norm_fwd{l}", with_transpose=True)
        proj_tiles = dict(tm=512, tn=3712, tk=2048, n_outer=True)
        if l == 0:
            proj, carried = _matmul(h, w_in_p[0], "nn", F32, "proj_fwd0", comm=rest_plan, **proj_tiles)
            gathered = {k: _unshard(g, shard_axis[k]) for k, g in zip(rest_keys, carried)}
            for l2 in range(1, depth):
                w_in_p[l2] = permuted_w_in(carried[len(rest_keys) + l2 - 1])
            conv_full = gathered["conv_w"]
            pool_w_f = gathered["pool_w"]
            w_uq_f = gathered["w_uq"].reshape(depth, L, MLA_HEADS, QK_HEAD)
            w_uq_p = jnp.pad(w_uq_f, ((0, 0), (0, 0), (0, 0), (0, QK_PAD - QK_HEAD))).reshape(depth, L, MLA_HEADS * QK_PAD)
            w_ukv_f = gathered["w_ukv"].reshape(depth, L, MLA_HEADS, QK_NOPE + V_HEAD)
            w_ukv_p = jnp.concatenate([w_ukv_f[..., :QK_NOPE].reshape(depth, L, -1), w_ukv_f[..., QK_NOPE:].reshape(depth, L, -1)], axis=-1)
            w_mem_f = gathered["w_mem_kv"]
            w_br_f = gathered["w_branch"]
            w_out_f = gathered["w_out"]
        else:
            proj = _matmul(h, w_in_p[l], "nn", F32, f"proj_fwd{l}", **proj_tiles)
        y_pool = _pool_fwd(proj, offs, pool_w_f[l], row(pool_scale, l), f"pool_fwd{l}")
        cqn = _rms_rows(proj, L, offs["cq"], row(q_a_norm_g, l), f"cq_norm{l}")
        ckvn = _rms_rows(proj, L, offs["ckv"], row(kv_a_norm_g, l), f"ckv_norm{l}")
        qp = _matmul(cqn, w_uq_p[l], "nn", F32, f"uq_fwd{l}")
        kvp = _matmul(ckvn, w_ukv_p[l], "nn", F32, f"ukv_fwd{l}")
        qg = _pad_heads(row(mla_q_norm_g, l), QK_PAD)
        kg = _pad_heads(row(mla_k_norm_g, l), QK_PAD)
        Q, K, V = _mla_qk_fwd(qp, kvp, proj, offs, tabs, qg, kg, f"mla_qk_fwd{l}")
        o, y_mla, lse = _flash_fwd(Q, K, V, proj, offs, f"flash_fwd{l}")
        y_conv = _conv_fwd(proj, offs, conv_full[l], f"conv_fwd{l}")
        memn = _rms_rows(mems, D, 0, row(mem_norm_g, l), f"mem_norm{l}")
        mem_kv = _matmul(memn, w_mem_f, "nn", F32, f"mem_kv{l}", b_idx=(l,))
        y_mem = _xattn_fwd(proj, offs, mem_kv, row(xattn_q_norm_g, l), row(xattn_k_norm_g, l), f"xattn_fwd{l}")
        ys = (y_pool, y_mla, y_conv, y_mem)
        gb = gate_b[l].reshape(4, D)
        merged = _merge_fwd(ys, w_br_f, l, proj, offs, gb, f"merge_fwd{l}")
        nxt = _matmul(merged, w_out_f, "nn", F32, f"out_fwd{l}", res=cur, b_idx=(l,))
        saved.append(dict(x=cur, hT=hT, proj=proj, cqn=cqn, ckvn=ckvn, qp=qp, kvp=kvp, qg=qg, kg=kg, Q=Q, K=K, V=V, o=o, lse=lse,
                          memn=memn, mem_kv=mem_kv, ys=ys, gb=gb, merged=merged))
        cur = nxt

    gx, loss_acc = _loss_rows(cur, loss_target.reshape(S, D), "loss")
    loss = lax.psum(0.5 * loss_acc[0, 0], MESH_AXES)

    grads = {k: [None] * depth for k in ("norm_g", "w_in", "gate_b", "pool_w", "pool_scale", "q_a_norm_g", "kv_a_norm_g", "w_uq", "w_ukv",
                                         "mla_q_norm_g", "mla_k_norm_g", "conv_w", "mem_norm_g", "w_mem_kv", "xattn_q_norm_g",
                                         "xattn_k_norm_g", "w_branch", "w_out")}
    names_sh = list(shard_axis)
    core = lax.axis_index("c").astype(jnp.int32).reshape(1)
    received = [None] * depth
    pending = None
    for l in reversed(range(depth)):
        sv = saved[l]
        proj, ys = sv["proj"], sv["ys"]
        dmerged = _matmul(gx, w_out_f, "nt", F32, f"out_bwd_x{l}", b_idx=(l,))
        grads["w_out"][l] = _matmul(sv["merged"], gx, "tn", BF16, f"out_bwd_w{l}")
        dt, dgp, dgb = _merge_bwd(ys, w_br_f, l, proj, offs, sv["gb"], dmerged, f"merge_bwd{l}")
        grads["gate_b"][l] = dgb.reshape(4 * D)
        dys, dwb = [], []
        for b in range(4):
            dys.append(_matmul(dt, w_br_f, "nt", F32, f"branch_bwd_x{l}_{b}", a_idx=(b,), b_idx=(l, b)))
            dwb.append(_matmul(ys[b], dt, "tn", BF16, f"branch_bwd_w{l}_{b}", b_idx=(b,)))
        grads["w_branch"][l] = jnp.stack(dwb)
        d_pz, e, d_pw, d_ps = _pool_bwd_a(proj, offs, pool_w_f[l], row(pool_scale, l), dys[0], f"pool_bwd_a{l}")
        d_pv = _pool_bwd_b(e, f"pool_bwd_b{l}")
        grads["pool_w"][l], grads["pool_scale"][l] = d_pw, d_ps[0]
        do, d_mz, delta = _mla_gate_bwd(sv["o"], proj, offs, dys[1], f"mla_gate_bwd{l}")
        (dQ, dK, dV), carried = _flash_bwd(sv["Q"], sv["K"], sv["V"], sv["lse"], delta, do, f"flash_bwd{l}",
                                           comm=None if pending is None else pending[1])
        if pending is not None:
            received[pending[0]], pending = carried, None
        dqp, dkvp, d_kr, dqg, dkg = _mla_qk_bwd(sv["qp"], sv["kvp"], proj, offs, tabs, sv["qg"], sv["kg"], dQ, dK, dV, f"mla_qk_bwd{l}")
        grads["mla_q_norm_g"][l], grads["mla_k_norm_g"][l] = dqg[0, :QK_HEAD], dkg[0, :QK_HEAD]
        dcqn = _matmul(dqp, w_uq_p[l], "nt", F32, f"uq_bwd_x{l}")
        d_wuq = _matmul(sv["cqn"], dqp, "tn", BF16, f"uq_bwd_w{l}")
        dckvn = _matmul(dkvp, w_ukv_p[l], "nt", F32, f"ukv_bwd_x{l}")
        d_wukv = _matmul(sv["ckvn"], dkvp, "tn", BF16, f"ukv_bwd_w{l}")
        grads["w_uq"][l] = d_wuq.reshape(L, MLA_HEADS, QK_PAD)[..., :QK_HEAD].reshape(L, MLA_HEADS * QK_HEAD)
        grads["w_ukv"][l] = jnp.concatenate([d_wukv[:, :MLA_HEADS * QK_NOPE].reshape(L, MLA_HEADS, QK_NOPE),
                                             d_wukv[:, MLA_HEADS * QK_NOPE:].reshape(L, MLA_HEADS, V_HEAD)], axis=-1).reshape(L, -1)
        d_cq, dqa = _rms_rows_bwd(proj, L, offs["cq"], row(q_a_norm_g, l), dcqn, BF16, f"cq_norm_bwd{l}")
        d_ckv, dkva = _rms_rows_bwd(proj, L, offs["ckv"], row(kv_a_norm_g, l), dckvn, BF16, f"ckv_norm_bwd{l}")
        grads["q_a_norm_g"][l], grads["kv_a_norm_g"][l] = dqa[0], dkva[0]
        d_cb, d_cc, d_cx, d_cz, d_cw = _conv_bwd(proj, offs, conv_full[l], dys[2], f"conv_bwd{l}")
        grads["conv_w"][l] = d_cw
        d_xq, d_xz, d_memkv, dxqg, dxkg = _xattn_bwd(proj, offs, sv["mem_kv"], row(xattn_q_norm_g, l), row(xattn_k_norm_g, l), dys[3], f"xattn_bwd{l}")
        grads["xattn_q_norm_g"][l], grads["xattn_k_norm_g"][l] = dxqg[0], dxkg[0]
        grads["w_mem_kv"][l] = _matmul(sv["memn"], d_memkv, "tn", BF16, f"mem_kv_bwd_w{l}")
        dmemn = _matmul(d_memkv, w_mem_f, "nt", F32, f"mem_kv_bwd_x{l}", b_idx=(l,))
        grads["mem_norm_g"][l] = _norm_g_grad(mems, row(mem_norm_g, l), dmemn, f"mem_norm_bwd{l}")[0]
        dproj = jnp.concatenate([d_pv, d_pz, d_cq, d_ckv, d_mz, d_cb, d_cc, d_cx, d_cz, d_xq, d_xz, dgp[0], dgp[1], dgp[2], dgp[3], d_kr], axis=1)
        dh = _matmul(dproj, w_in_p[l], "nt", F32, f"proj_bwd_x{l}", tm=1024, tn=1024, tk=3712)
        d_win_p = _matmul(sv["hT"], dproj, "nn", BF16, f"proj_bwd_w{l}", tm=2048, tn=640, tk=2048)
        n_p = offs["n"]
        grads["w_in"][l] = jnp.concatenate([d_win_p[:, :seg_a], d_win_p[:, n_p - 128:n_p - 128 + QK_ROPE], d_win_p[:, seg_a:n_p - 128]], axis=1)
        gx, dng = _rms_rows_bwd(sv["x"], D, 0, row(norm_g, l), dh, F32, f"norm_bwd{l}", res=gx)
        grads["norm_g"][l] = dng[0]
        staged = [_reshard(grads[k][l], shard_axis[k] - 1).reshape(4, 2, -1, sharded[k].shape[-1]).astype(BF16) for k in names_sh]
        given = _sibling_swap(staged, f"pair_swap_grads{l}")
        plan = _ChipExchangePlan([_pair_sum(st, theirs, core, f"pair_sum_{k}{l}") for k, st, theirs in zip(names_sh, staged, given)])
        if l > 0:
            pending = (l, plan)
        else:
            received[0] = _run_plan(plan, "chip_exchange_grads0")
    grad_x = gx.reshape(1, S, D)
    full = {k: jnp.stack(v) for k, v in grads.items() if k not in shard_axis}

    moments = dict(w_in=(m_w_in, v_w_in), pool_w=(m_pool_w, v_pool_w), w_uq=(m_w_uq, v_w_uq), w_ukv=(m_w_ukv, v_w_ukv),
                   conv_w=(m_conv_w, v_conv_w), w_mem_kv=(m_w_mem_kv, v_w_mem_kv), w_branch=(m_w_branch, v_w_branch), w_out=(m_w_out, v_w_out))
    results = {}
    for ki, k in enumerate(names_sh):
        w = sharded[k]
        m, v = moments[k]
        flat = lambda a: a.reshape(-1, w.shape[-1])
        parts = jnp.concatenate([received[l][ki] for l in range(depth)], axis=1)
        outs = _adamw(parts, flat(w), flat(m), flat(v), f"adamw_{k}")
        results[k] = [o.reshape(w.shape) for o in outs]

    small = dict(norm_g=(norm_g, m_norm_g, v_norm_g), gate_b=(gate_b, m_gate_b, v_gate_b), pool_scale=(pool_scale, m_pool_scale, v_pool_scale),
                 q_a_norm_g=(q_a_norm_g, m_q_a_norm_g, v_q_a_norm_g), kv_a_norm_g=(kv_a_norm_g, m_kv_a_norm_g, v_kv_a_norm_g),
                 mla_q_norm_g=(mla_q_norm_g, m_mla_q_norm_g, v_mla_q_norm_g), mla_k_norm_g=(mla_k_norm_g, m_mla_k_norm_g, v_mla_k_norm_g),
                 mem_norm_g=(mem_norm_g, m_mem_norm_g, v_mem_norm_g), xattn_q_norm_g=(xattn_q_norm_g, m_xattn_q_norm_g, v_xattn_q_norm_g),
                 xattn_k_norm_g=(xattn_k_norm_g, m_xattn_k_norm_g, v_xattn_k_norm_g))
    names_sm = list(small)
    sizes = [small[k][0].size for k in names_sm]
    total = sum(sizes)
    rows = -(-total // (8 * 128)) * 8

    def pack(arrs):
        flat = jnp.concatenate([a.reshape(-1) for a in arrs])
        return jnp.pad(flat, (0, rows * 128 - total)).reshape(rows, 128)

    g_small = _run_plan(_GatherPlan([pack([full[k] for k in names_sm])]), "gather_small_grads")[0]
    outs = _adamw(g_small, pack([small[k][0] for k in names_sm]), pack([small[k][1] for k in names_sm]),
                  pack([small[k][2] for k in names_sm]), "adamw_small")
    for kind, o in enumerate(outs):
        flat, pos = o.reshape(-1), 0
        for k, n in zip(names_sm, sizes):
            results.setdefault(k, [None] * 4)[kind] = flat[pos:pos + n].reshape(small[k][0].shape)
            pos += n

    order = ["norm_g", "w_in", "gate_b", "pool_w", "pool_scale", "q_a_norm_g", "kv_a_norm_g", "w_uq", "w_ukv", "mla_q_norm_g",
             "mla_k_norm_g", "conv_w", "mem_norm_g", "w_mem_kv", "xattn_q_norm_g", "xattn_k_norm_g", "w_branch", "w_out"]
    return (loss, grad_x, *[results[k][0] for k in order], *[results[k][1] for k in order],
            *[results[k][2] for k in order], *[results[k][3] for k in order])
```

```python
import functools

import jax
import jax.numpy as jnp
from jax import lax
from jax.experimental import pallas as pl
from jax.experimental.pallas import tpu as pltpu

F32 = jnp.float32
BF16 = jnp.bfloat16
MESH_AXES = ("x", "y", "c")
N_DEV = 8

BRANCH_W = 1024
POOL_WINDOWS = (2, 4, 8, 16)
POOL_GW = BRANCH_W // 4
MLA_HEADS = 8
QK_NOPE = 128
QK_ROPE = 64
QK_HEAD = QK_NOPE + QK_ROPE
QK_PAD = 256
V_HEAD = 128
XATTN_HEADS = 4
XATTN_HD = BRANCH_W // XATTN_HEADS
ROPE_THETA = 10000.0
EPS = 1e-6
ADAM_LR = 0.001
ADAM_B1 = 0.9
ADAM_B2 = 0.999
ADAM_EPS = 1e-08
ADAM_WD = 0.01
ADAM_STEP = 10
HALO = 16
NEG = -1e30
SCORE_SCALE_LOG2 = (QK_HEAD ** -0.5) * 1.4426950408889634
VMEM_LIMIT = 56 << 20


def _pick(n, pref, align):
    if n <= pref:
        return n
    t = (pref // align) * align
    while t >= align:
        if n % t == 0:
            return t
        t -= align
    return n


def _params(sem):
    return pltpu.CompilerParams(dimension_semantics=sem, vmem_limit_bytes=VMEM_LIMIT)


def _silu(z):
    return z * (1.0 / (1.0 + jnp.exp(-z)))


def _silu_and_grad(z):
    s = 1.0 / (1.0 + jnp.exp(-z))
    return z * s, s * (1.0 + z * (1.0 - s))


def _dot(a, b):
    return jnp.dot(a.astype(BF16), b.astype(BF16), preferred_element_type=F32)


def _dot_nt(a, b):
    return lax.dot_general(a.astype(BF16), b.astype(BF16), (((1,), (1,)), ((), ())), preferred_element_type=F32)


def _dot_tn(a, b):
    return lax.dot_general(a.astype(BF16), b.astype(BF16), (((0,), (0,)), ((), ())), preferred_element_type=F32)


def _rms(x, g, n):
    r = lax.rsqrt(jnp.sum(x * x, axis=-1, keepdims=True) / n + EPS)
    return x * r * g, r


def _rms_bwd(x, g, r, dy, n):
    dyg = dy * g
    dx = r * dyg - x * (r * r * r) * (jnp.sum(dyg * x, axis=-1, keepdims=True) / n)
    dg = jnp.sum(dy * x * r, axis=0, keepdims=True)
    return dx, dg


def _col(tm, w, off):
    assert off % w == 0
    cb = off // w
    return pl.BlockSpec((tm, w), lambda i: (i, cb))


def _whole(shape):
    nd = len(shape)
    return pl.BlockSpec(shape, lambda *_: (0,) * nd)


def _lead_spec(block, fn, idx):
    if not idx:
        return pl.BlockSpec(block, fn)
    return pl.BlockSpec((None,) * len(idx) + block, lambda i, j, k: tuple(idx) + fn(i, j, k))


def _hosted_call(plan, body, ins, in_specs, out_specs, out_shape, scratch, grid, name):
    n_in, n_out, n_scr, n = len(ins), len(out_specs), len(scratch), plan.n
    n_steps = 1
    for g in grid:
        n_steps *= g

    def wrapped(*refs):
        a, c_in = refs[:n_in], refs[n_in:n_in + n]
        o, c_out = refs[n_in + n:n_in + n + n_out], refs[n_in + n + n_out:n_in + 2 * n + n_out]
        rest = refs[n_in + 2 * n + n_out:]
        scr, sems = rest[:n_scr], rest[n_scr:]
        step = 0
        for d, g in enumerate(grid):
            step = step * g + pl.program_id(d)

        @pl.when(step == 0)
        def _():
            plan.start(c_in, c_out, sems)

        body(*a, *o, *scr)

        @pl.when(step == n_steps // 2)
        def _():
            plan.relay(c_in, c_out, sems)

        @pl.when(step == n_steps - 1)
        def _():
            plan.finish(c_in, c_out, sems)

    res = pl.pallas_call(
        wrapped, grid=grid, in_specs=list(in_specs) + [HBM_SPEC] * n, out_specs=list(out_specs) + [HBM_SPEC] * n,
        out_shape=list(out_shape) + plan.out_shape, scratch_shapes=list(scratch) + plan.scratch, name=name,
        compiler_params=_params(("arbitrary",) * len(grid)),
    )(*ins, *plan.arrs)
    return list(res[:n_out]), list(res[n_out:])


def _matmul(a, b, mode, out_dtype, name, tm=1024, tn=1024, tk=2048, res=None, a_idx=(), b_idx=(), n_outer=False, comm=None):
    a_shape, b_shape = a.shape[len(a_idx):], b.shape[len(b_idx):]
    if mode == "nn":
        (M, K), N = a_shape, b_shape[1]
    elif mode == "nt":
        (M, K), N = a_shape, b_shape[0]
    else:
        (K, M), N = a_shape, b_shape[1]
    tm, tn, tk = _pick(M, tm, 128), _pick(N, tn, 128), _pick(K, tk, 128)
    nk = K // tk
    order = (lambda fn: (lambda j, i, k: fn(i, j, k))) if n_outer else (lambda fn: fn)
    if mode == "nn":
        a_spec = _lead_spec((tm, tk), order(lambda i, j, k: (i, k)), a_idx)
        b_spec = _lead_spec((tk, tn), order(lambda i, j, k: (k, j)), b_idx)
        op = _dot
    elif mode == "nt":
        a_spec = _lead_spec((tm, tk), order(lambda i, j, k: (i, k)), a_idx)
        b_spec = _lead_spec((tn, tk), order(lambda i, j, k: (j, k)), b_idx)
        op = _dot_nt
    else:
        a_spec = _lead_spec((tk, tm), order(lambda i, j, k: (k, i)), a_idx)
        b_spec = _lead_spec((tk, tn), order(lambda i, j, k: (k, j)), b_idx)
        op = _dot_tn
    o_spec = pl.BlockSpec((tm, tn), order(lambda i, j, k: (i, j)))
    has_res = res is not None

    def body(*refs):
        a_ref, b_ref = refs[0], refs[1]
        res_ref = refs[2] if has_res else None
        o_ref = refs[2 + has_res]
        part = op(a_ref[...], b_ref[...])

        def finish(acc):
            if has_res:
                acc = res_ref[...] + acc
            o_ref[...] = acc.astype(out_dtype)

        if nk == 1:
            finish(part)
        else:
            acc_ref = refs[3 + has_res]
            k = pl.program_id(2)

            @pl.when(k == 0)
            def _():
                acc_ref[...] = part

            @pl.when(k > 0)
            def _():
                acc_ref[...] += part

            @pl.when(k == nk - 1)
            def _():
                finish(acc_ref[...])

    ins = [a, b] + ([res] if has_res else [])
    in_specs = [a_spec, b_spec] + ([o_spec] if has_res else [])
    scratch = [pltpu.VMEM((tm, tn), F32)] if nk > 1 else []
    grid = (N // tn, M // tm, nk) if n_outer else (M // tm, N // tn, nk)
    out_shape = jax.ShapeDtypeStruct((M, N), out_dtype)
    if comm is not None:
        (out,), carried = _hosted_call(comm, body, ins, in_specs, [o_spec], [out_shape], scratch, grid, name)
        return out, carried
    return pl.pallas_call(
        body, grid=grid, in_specs=in_specs, out_specs=o_spec, out_shape=out_shape, scratch_shapes=scratch, name=name,
        compiler_params=_params(("parallel", "parallel", "arbitrary")),
    )(*ins)


def _rms_rows(x, width, off, g, name, with_transpose=False):
    S = x.shape[0]
    tm = _pick(S, 512, 128 if with_transpose else 16)

    def body(x_ref, g_ref, o_ref, *t_ref):
        y, _ = _rms(x_ref[...], g_ref[...], width)
        o_ref[...] = y.astype(BF16)
        if with_transpose:
            t_ref[0][...] = y.T.astype(BF16)

    out_specs = [_col(tm, width, 0)] + ([pl.BlockSpec((width, tm), lambda i: (0, i))] if with_transpose else [])
    out_shape = [jax.ShapeDtypeStruct((S, width), BF16)] + ([jax.ShapeDtypeStruct((width, S), BF16)] if with_transpose else [])
    res = pl.pallas_call(
        body, grid=(S // tm,), in_specs=[_col(tm, width, off), _whole((1, width))], out_specs=out_specs,
        out_shape=out_shape, name=name, compiler_params=_params(("parallel",)),
    )(x, g)
    return res if with_transpose else res[0]


def _rms_rows_bwd(x, width, off, g, dy, out_dtype, name, res=None):
    S = x.shape[0]
    tm = _pick(S, 256, 16)
    has_res = res is not None

    def body(*refs):
        x_ref, g_ref, dy_ref = refs[:3]
        res_ref = refs[3] if has_res else None
        dx_ref, dg_ref = refs[3 + has_res:]
        xv, gv = x_ref[...], g_ref[...]
        _, r = _rms(xv, gv, width)
        dx, dg = _rms_bwd(xv, gv, r, dy_ref[...], width)
        if has_res:
            dx = res_ref[...] + dx
        dx_ref[...] = dx.astype(out_dtype)

        @pl.when(pl.program_id(0) == 0)
        def _():
            dg_ref[...] = jnp.zeros_like(dg_ref)

        dg_ref[...] += dg

    ins = [x, g, dy] + ([res] if has_res else [])
    in_specs = [_col(tm, width, off), _whole((1, width)), _col(tm, width, 0)] + ([_col(tm, width, 0)] if has_res else [])
    return pl.pallas_call(
        body, grid=(S // tm,), in_specs=in_specs, out_specs=[_col(tm, width, 0), _whole((1, width))],
        out_shape=[jax.ShapeDtypeStruct((S, width), out_dtype), jax.ShapeDtypeStruct((1, width), F32)], name=name,
        compiler_params=_params(("arbitrary",)),
    )(*ins)


def _norm_g_grad(x, g, dy, name):
    S, width = x.shape
    tm = _pick(S, 256, 16)

    def body(x_ref, g_ref, dy_ref, dg_ref):
        xv = x_ref[...]
        _, r = _rms(xv, g_ref[...], width)

        @pl.when(pl.program_id(0) == 0)
        def _():
            dg_ref[...] = jnp.zeros_like(dg_ref)

        dg_ref[...] += jnp.sum(dy_ref[...] * xv * r, axis=0, keepdims=True)

    return pl.pallas_call(
        body, grid=(S // tm,), in_specs=[_col(tm, width, 0), _whole((1, width)), _col(tm, width, 0)],
        out_specs=_whole((1, width)), out_shape=jax.ShapeDtypeStruct((1, width), F32), name=name,
        compiler_params=_params(("arbitrary",)),
    )(x, g, dy)


def _loss_rows(y, target, name):
    S, D = y.shape
    tm = _pick(S, 512, 16)

    def body(y_ref, t_ref, g_ref, l_ref):
        err = y_ref[...] - t_ref[...]
        g_ref[...] = err / D

        @pl.when(pl.program_id(0) == 0)
        def _():
            l_ref[...] = jnp.zeros_like(l_ref)

        row = jnp.sum(err * err, axis=-1, keepdims=True) / D
        l_ref[...] += jnp.sum(row, axis=0, keepdims=True)

    return pl.pallas_call(
        body, grid=(S // tm,), in_specs=[_col(tm, D, 0), _col(tm, D, 0)], out_specs=[_col(tm, D, 0), _whole((1, 1))],
        out_shape=[jax.ShapeDtypeStruct((S, D), F32), jax.ShapeDtypeStruct((1, 1), F32)], name=name,
        compiler_params=_params(("arbitrary",)),
    )(y, target)


def _row_index(i, ts):
    return i * ts + lax.broadcasted_iota(jnp.int32, (ts, 1), 0)


def _pool_mixed(scr, pv_ref, g, t, ts):
    w, lo = POOL_WINDOWS[g], g * POOL_GW
    win = scr[HALO:HALO + ts, lo:lo + POOL_GW]
    for k in range(1, w):
        win = win + scr[HALO - k:HALO - k + ts, lo:lo + POOL_GW]
    cnt = jnp.minimum(t + 1, w).astype(F32)
    return win / cnt - pv_ref[:, lo:lo + POOL_GW], cnt


def _halo_before(ts, off):
    nb = ts // HALO
    cb = off // BRANCH_W
    return pl.BlockSpec((HALO, BRANCH_W), lambda i: (jnp.maximum(i * nb - 1, 0), cb))


def _halo_after(ts, off, n_tiles):
    nb = ts // HALO
    cb = off // BRANCH_W
    last = n_tiles * nb - 1
    return pl.BlockSpec((HALO, BRANCH_W), lambda i: (jnp.minimum((i + 1) * nb, last), cb))


def _pool_fwd(proj, offs, pool_w, pool_scale, name):
    S = proj.shape[0]
    ts = _pick(S, 512, HALO)

    def body(pv_ref, halo_ref, pz_ref, w_ref, sc_ref, y_ref, scr):
        i = pl.program_id(0)
        scr[0:HALO, :] = jnp.where(i > 0, halo_ref[...], 0.0)
        scr[HALO:HALO + ts, :] = pv_ref[...]
        t = _row_index(i, ts)
        for g in range(4):
            lo = g * POOL_GW
            mixed, _ = _pool_mixed(scr, pv_ref, g, t, ts)
            out = _dot(mixed, w_ref[g])
            y_ref[:, lo:lo + POOL_GW] = (out * sc_ref[:, lo:lo + POOL_GW] * _silu(pz_ref[:, lo:lo + POOL_GW])).astype(BF16)

    return pl.pallas_call(
        body, grid=(S // ts,),
        in_specs=[_col(ts, BRANCH_W, offs["pv"]), _halo_before(ts, offs["pv"]), _col(ts, BRANCH_W, offs["pz"]),
                  _whole((4, POOL_GW, POOL_GW)), _whole((1, BRANCH_W))],
        out_specs=_col(ts, BRANCH_W, 0), out_shape=jax.ShapeDtypeStruct((S, BRANCH_W), BF16),
        scratch_shapes=[pltpu.VMEM((HALO + ts, BRANCH_W), F32)], name=name, compiler_params=_params(("parallel",)),
    )(proj, proj, proj, pool_w, pool_scale)


def _pool_bwd_a(proj, offs, pool_w, pool_scale, dy, name):
    S = proj.shape[0]
    ts = _pick(S, 512, HALO)

    def body(pv_ref, halo_ref, pz_ref, w_ref, sc_ref, dy_ref, dpz_ref, e_ref, dw_ref, dsc_ref, scr):
        i = pl.program_id(0)
        scr[0:HALO, :] = jnp.where(i > 0, halo_ref[...], 0.0)
        scr[HALO:HALO + ts, :] = pv_ref[...]
        t = _row_index(i, ts)

        @pl.when(i == 0)
        def _():
            dw_ref[...] = jnp.zeros_like(dw_ref)
            dsc_ref[...] = jnp.zeros_like(dsc_ref)

        for g in range(4):
            lo = g * POOL_GW
            sl = slice(lo, lo + POOL_GW)
            mixed, cnt = _pool_mixed(scr, pv_ref, g, t, ts)
            out = _dot(mixed, w_ref[g])
            sz, dsz = _silu_and_grad(pz_ref[:, sl])
            dyv, sc = dy_ref[:, sl], sc_ref[:, sl]
            d_out = dyv * sc * sz
            dsc_ref[:, sl] += jnp.sum(dyv * out * sz, axis=0, keepdims=True)
            dpz_ref[:, sl] = (dyv * out * sc * dsz).astype(BF16)
            dw_ref[g] += _dot_tn(mixed, d_out)
            e_ref[:, sl] = _dot_nt(d_out, w_ref[g]) / cnt

    return pl.pallas_call(
        body, grid=(S // ts,),
        in_specs=[_col(ts, BRANCH_W, offs["pv"]), _halo_before(ts, offs["pv"]), _col(ts, BRANCH_W, offs["pz"]),
                  _whole((4, POOL_GW, POOL_GW)), _whole((1, BRANCH_W)), _col(ts, BRANCH_W, 0)],
        out_specs=[_col(ts, BRANCH_W, 0), _col(ts, BRANCH_W, 0), _whole((4, POOL_GW, POOL_GW)), _whole((1, BRANCH_W))],
        out_shape=[jax.ShapeDtypeStruct((S, BRANCH_W), BF16), jax.ShapeDtypeStruct((S, BRANCH_W), F32),
                   jax.ShapeDtypeStruct((4, POOL_GW, POOL_GW), F32), jax.ShapeDtypeStruct((1, BRANCH_W), F32)],
        scratch_shapes=[pltpu.VMEM((HALO + ts, BRANCH_W), F32)], name=name, compiler_params=_params(("arbitrary",)),
    )(proj, proj, proj, pool_w, pool_scale, dy)


def _pool_bwd_b(e, name):
    S = e.shape[0]
    ts = _pick(S, 512, HALO)
    n_tiles = S // ts

    def body(e_ref, halo_ref, dpv_ref, scr):
        i = pl.program_id(0)
        scr[0:ts, :] = e_ref[...]
        scr[ts:ts + HALO, :] = jnp.where(i < n_tiles - 1, halo_ref[...], 0.0)
        t = _row_index(i, ts)
        for g in range(4):
            w, lo = POOL_WINDOWS[g], g * POOL_GW
            acc = scr[0:ts, lo:lo + POOL_GW]
            ev = acc
            for k in range(1, w):
                acc = acc + scr[k:k + ts, lo:lo + POOL_GW]
            cnt = jnp.minimum(t + 1, w).astype(F32)
            dpv_ref[:, lo:lo + POOL_GW] = (acc - ev * cnt).astype(BF16)

    return pl.pallas_call(
        body, grid=(n_tiles,), in_specs=[_col(ts, BRANCH_W, 0), _halo_after(ts, 0, n_tiles)], out_specs=_col(ts, BRANCH_W, 0),
        out_shape=jax.ShapeDtypeStruct((S, BRANCH_W), BF16), scratch_shapes=[pltpu.VMEM((ts + HALO, BRANCH_W), F32)],
        name=name, compiler_params=_params(("parallel",)),
    )(e, e)


def _conv_fwd(proj, offs, conv_w, name):
    S = proj.shape[0]
    ts = _pick(S, 512, HALO)

    def body(cb_ref, cc_ref, cx_ref, cz_ref, hc_ref, hx_ref, w_ref, y_ref, scr):
        i = pl.program_id(0)
        scr[0:HALO, :] = jnp.where(i > 0, hc_ref[...] * hx_ref[...], 0.0)
        scr[HALO:HALO + ts, :] = cc_ref[...] * cx_ref[...]
        y = w_ref[0:1, :] * scr[HALO - 2:HALO - 2 + ts, :] + w_ref[1:2, :] * scr[HALO - 1:HALO - 1 + ts, :] + w_ref[2:3, :] * scr[HALO:HALO + ts, :]
        y_ref[...] = (cb_ref[...] * y * _silu(cz_ref[...])).astype(BF16)

    return pl.pallas_call(
        body, grid=(S // ts,),
        in_specs=[_col(ts, BRANCH_W, offs["cb"]), _col(ts, BRANCH_W, offs["cc"]), _col(ts, BRANCH_W, offs["cx"]),
                  _col(ts, BRANCH_W, offs["cz"]), _halo_before(ts, offs["cc"]), _halo_before(ts, offs["cx"]), _whole((3, BRANCH_W))],
        out_specs=_col(ts, BRANCH_W, 0), out_shape=jax.ShapeDtypeStruct((S, BRANCH_W), BF16),
        scratch_shapes=[pltpu.VMEM((HALO + ts, BRANCH_W), F32)], name=name, compiler_params=_params(("parallel",)),
    )(proj, proj, proj, proj, proj, proj, conv_w)


def _conv_bwd(proj, offs, conv_w, dy, name):
    S = proj.shape[0]
    ts = _pick(S, 256, HALO)
    n_tiles = S // ts

    def body(cb_ref, cc_ref, cx_ref, cz_ref, hc_ref, hx_ref, w_ref, dy_ref, ady_ref, acb_ref, acz_ref,
             dcb_ref, dcc_ref, dcx_ref, dcz_ref, dw_ref, scr_u, scr_d):
        i = pl.program_id(0)
        u = cc_ref[...] * cx_ref[...]
        scr_u[0:HALO, :] = jnp.where(i > 0, hc_ref[...] * hx_ref[...], 0.0)
        scr_u[HALO:HALO + ts, :] = u
        u1 = scr_u[HALO - 1:HALO - 1 + ts, :]
        u2 = scr_u[HALO - 2:HALO - 2 + ts, :]
        y = w_ref[0:1, :] * u2 + w_ref[1:2, :] * u1 + w_ref[2:3, :] * u
        sz, dsz = _silu_and_grad(cz_ref[...])
        dyv, cb = dy_ref[...], cb_ref[...]
        dcb_ref[...] = (dyv * y * sz).astype(BF16)
        dcz_ref[...] = (dyv * cb * y * dsz).astype(BF16)
        d_y = dyv * cb * sz
        scr_d[0:ts, :] = d_y
        scr_d[ts:ts + HALO, :] = jnp.where(i < n_tiles - 1, ady_ref[...] * acb_ref[...] * _silu(acz_ref[...]), 0.0)
        du = w_ref[2:3, :] * d_y + w_ref[1:2, :] * scr_d[1:1 + ts, :] + w_ref[0:1, :] * scr_d[2:2 + ts, :]
        dcc_ref[...] = (du * cx_ref[...]).astype(BF16)
        dcx_ref[...] = (du * cc_ref[...]).astype(BF16)

        @pl.when(i == 0)
        def _():
            dw_ref[...] = jnp.zeros_like(dw_ref)

        dw_ref[0:1, :] += jnp.sum(d_y * u2, axis=0, keepdims=True)
        dw_ref[1:2, :] += jnp.sum(d_y * u1, axis=0, keepdims=True)
        dw_ref[2:3, :] += jnp.sum(d_y * u, axis=0, keepdims=True)

    row = lambda off: _col(ts, BRANCH_W, off)
    return pl.pallas_call(
        body, grid=(n_tiles,),
        in_specs=[row(offs["cb"]), row(offs["cc"]), row(offs["cx"]), row(offs["cz"]), _halo_before(ts, offs["cc"]),
                  _halo_before(ts, offs["cx"]), _whole((3, BRANCH_W)), row(0), _halo_after(ts, 0, n_tiles),
                  _halo_after(ts, offs["cb"], n_tiles), _halo_after(ts, offs["cz"], n_tiles)],
        out_specs=[row(0), row(0), row(0), row(0), _whole((3, BRANCH_W))],
        out_shape=[jax.ShapeDtypeStruct((S, BRANCH_W), BF16)] * 4 + [jax.ShapeDtypeStruct((3, BRANCH_W), F32)],
        scratch_shapes=[pltpu.VMEM((HALO + ts, BRANCH_W), F32), pltpu.VMEM((ts + HALO, BRANCH_W), F32)], name=name,
        compiler_params=_params(("arbitrary",)),
    )(proj, proj, proj, proj, proj, proj, conv_w, dy, dy, proj, proj)


def _xattn_head(xq, kraw, v, qg, kg):
    qn, rq = _rms(xq, qg, XATTN_HD)
    kn, rk = _rms(kraw, kg, XATTN_HD)
    s = _dot_nt(qn, kn) * (XATTN_HD ** -0.5)
    e = jnp.exp(s - jnp.max(s, axis=-1, keepdims=True))
    p = e / jnp.sum(e, axis=-1, keepdims=True)
    return qn, rq, kn, rk, p, _dot(p, v)


def _xattn_fwd(proj, offs, mem_kv, qg, kg, name):
    S, M = proj.shape[0], mem_kv.shape[0]
    tm = _pick(S, 512, 16)

    def body(xq_ref, xz_ref, kv_ref, qg_ref, kg_ref, y_ref):
        for h in range(XATTN_HEADS):
            sl = slice(h * XATTN_HD, (h + 1) * XATTN_HD)
            vs = slice(BRANCH_W + h * XATTN_HD, BRANCH_W + (h + 1) * XATTN_HD)
            o = _xattn_head(xq_ref[:, sl], kv_ref[:, sl], kv_ref[:, vs], qg_ref[...], kg_ref[...])[-1]
            y_ref[:, sl] = (o * _silu(xz_ref[:, sl])).astype(BF16)

    return pl.pallas_call(
        body, grid=(S // tm,),
        in_specs=[_col(tm, BRANCH_W, offs["xq"]), _col(tm, BRANCH_W, offs["xz"]), _whole((M, 2 * BRANCH_W)),
                  _whole((1, XATTN_HD)), _whole((1, XATTN_HD))],
        out_specs=_col(tm, BRANCH_W, 0), out_shape=jax.ShapeDtypeStruct((S, BRANCH_W), BF16), name=name,
        compiler_params=_params(("parallel",)),
    )(proj, proj, mem_kv, qg, kg)


def _xattn_bwd(proj, offs, mem_kv, qg, kg, dy, name):
    S, M = proj.shape[0], mem_kv.shape[0]
    tm = _pick(S, 256, 16)
    n_tiles = S // tm

    def body(xq_ref, xz_ref, kv_ref, qg_ref, kg_ref, dy_ref, dxq_ref, dxz_ref, dkv_ref, dqg_ref, dkg_ref, dkn_acc):
        i = pl.program_id(0)

        @pl.when(i == 0)
        def _():
            dkv_ref[...] = jnp.zeros_like(dkv_ref)
            dqg_ref[...] = jnp.zeros_like(dqg_ref)
            dkg_ref[...] = jnp.zeros_like(dkg_ref)
            dkn_acc[...] = jnp.zeros_like(dkn_acc)

        qg_v, kg_v = qg_ref[...], kg_ref[...]
        for h in range(XATTN_HEADS):
            sl = slice(h * XATTN_HD, (h + 1) * XATTN_HD)
            vs = slice(BRANCH_W + h * XATTN_HD, BRANCH_W + (h + 1) * XATTN_HD)
            xq, v = xq_ref[:, sl], kv_ref[:, vs]
            qn, rq, kn, rk, p, o = _xattn_head(xq, kv_ref[:, sl], v, qg_v, kg_v)
            sz, dsz = _silu_and_grad(xz_ref[:, sl])
            dyv = dy_ref[:, sl]
            dxz_ref[:, sl] = (dyv * o * dsz).astype(BF16)
            do = dyv * sz
            dkv_ref[:, vs] += _dot_tn(p, do)
            dp = _dot_nt(do, v)
            ds = p * (dp - jnp.sum(dp * p, axis=-1, keepdims=True)) * (XATTN_HD ** -0.5)
            dqn = _dot(ds, kn)
            dkn_acc[:, sl] += _dot_tn(ds, qn)
            dxq, dqg = _rms_bwd(xq, qg_v, rq, dqn, XATTN_HD)
            dxq_ref[:, sl] = dxq.astype(BF16)
            dqg_ref[...] += dqg

        @pl.when(i == n_tiles - 1)
        def _():
            for h in range(XATTN_HEADS):
                sl = slice(h * XATTN_HD, (h + 1) * XATTN_HD)
                kraw = kv_ref[:, sl]
                _, rk = _rms(kraw, kg_v, XATTN_HD)
                dk, dkg = _rms_bwd(kraw, kg_v, rk, dkn_acc[:, sl], XATTN_HD)
                dkv_ref[:, sl] = dk
                dkg_ref[...] += dkg

    return pl.pallas_call(
        body, grid=(n_tiles,),
        in_specs=[_col(tm, BRANCH_W, offs["xq"]), _col(tm, BRANCH_W, offs["xz"]), _whole((M, 2 * BRANCH_W)),
                  _whole((1, XATTN_HD)), _whole((1, XATTN_HD)), _col(tm, BRANCH_W, 0)],
        out_specs=[_col(tm, BRANCH_W, 0), _col(tm, BRANCH_W, 0), _whole((M, 2 * BRANCH_W)), _whole((1, XATTN_HD)), _whole((1, XATTN_HD))],
        out_shape=[jax.ShapeDtypeStruct((S, BRANCH_W), BF16), jax.ShapeDtypeStruct((S, BRANCH_W), BF16),
                   jax.ShapeDtypeStruct((M, 2 * BRANCH_W), F32), jax.ShapeDtypeStruct((1, XATTN_HD), F32), jax.ShapeDtypeStruct((1, XATTN_HD), F32)],
        scratch_shapes=[pltpu.VMEM((M, BRANCH_W), F32)], name=name, compiler_params=_params(("arbitrary",)),
    )(proj, proj, mem_kv, qg, kg, dy)


def _rope(c, tc, ta, tb):
    return c * tc + pltpu.roll(c, 96, 1) * ta + pltpu.roll(c, 32, 1) * tb


def _rope_t(d, tc, ta, tb):
    return d * tc + pltpu.roll(d * ta, 32, 1) + pltpu.roll(d * tb, 96, 1)


def _mla_qk_fwd(qp, kvp, proj, offs, tabs, qg, kg, name):
    S = qp.shape[0]
    tm = _pick(S, 256, 16)
    W = MLA_HEADS * QK_PAD

    def body(q_ref, kv_ref, kr_ref, tc_ref, ta_ref, tb_ref, qg_ref, kg_ref, qo_ref, ko_ref, vo_ref):
        tc, ta, tb = tc_ref[...], ta_ref[...], tb_ref[...]
        qg_v, kg_v = qg_ref[...], kg_ref[...]
        kr = kr_ref[...]
        kr_ss = jnp.sum(kr * kr, axis=-1, keepdims=True)
        for h in range(MLA_HEADS):
            sl = slice(h * QK_PAD, (h + 1) * QK_PAD)
            qn, _ = _rms(q_ref[:, sl], qg_v, QK_HEAD)
            qo_ref[:, sl] = jnp.concatenate([qn[:, :QK_NOPE], _rope(qn[:, QK_NOPE:], tc, ta, tb)], axis=1).astype(BF16)
            kn = kv_ref[:, h * QK_NOPE:(h + 1) * QK_NOPE]
            r = lax.rsqrt((jnp.sum(kn * kn, axis=-1, keepdims=True) + kr_ss) / QK_HEAD + EPS)
            ko_ref[:, sl] = jnp.concatenate([kn * r * kg_v[:, :QK_NOPE], _rope(kr * r * kg_v[:, QK_NOPE:], tc, ta, tb)], axis=1).astype(BF16)
        vo_ref[...] = kv_ref[:, MLA_HEADS * QK_NOPE:].astype(BF16)

    return pl.pallas_call(
        body, grid=(S // tm,),
        in_specs=[_col(tm, W, 0), _col(tm, W, 0), _col(tm, 128, offs["kr"]), _col(tm, 128, 0), _col(tm, 128, 0), _col(tm, 128, 0),
                  _whole((1, QK_PAD)), _whole((1, QK_PAD))],
        out_specs=[_col(tm, W, 0), _col(tm, W, 0), _col(tm, BRANCH_W, 0)],
        out_shape=[jax.ShapeDtypeStruct((S, W), BF16), jax.ShapeDtypeStruct((S, W), BF16), jax.ShapeDtypeStruct((S, BRANCH_W), BF16)],
        name=name, compiler_params=_params(("parallel",)),
    )(qp, kvp, proj, *tabs, qg, kg)


def _mla_qk_bwd(qp, kvp, proj, offs, tabs, qg, kg, dQ, dK, dV, name):
    S = qp.shape[0]
    tm = _pick(S, 256, 16)
    W = MLA_HEADS * QK_PAD

    def body(q_ref, kv_ref, kr_ref, tc_ref, ta_ref, tb_ref, qg_ref, kg_ref, dQ_ref, dK_ref, dV_ref,
             dq_ref, dkv_ref, dkr_ref, dqg_ref, dkg_ref):
        @pl.when(pl.program_id(0) == 0)
        def _():
            dqg_ref[...] = jnp.zeros_like(dqg_ref)
            dkg_ref[...] = jnp.zeros_like(dkg_ref)

        tc, ta, tb = tc_ref[...], ta_ref[...], tb_ref[...]
        qg_v, kg_v = qg_ref[...], kg_ref[...]
        kg_n, kg_r = kg_v[:, :QK_NOPE], kg_v[:, QK_NOPE:]
        kr = kr_ref[...]
        kr_ss = jnp.sum(kr * kr, axis=-1, keepdims=True)
        dkr = jnp.zeros_like(kr)
        dkg_n = jnp.zeros((1, QK_NOPE), F32)
        dkg_r = jnp.zeros((1, QK_NOPE), F32)
        for h in range(MLA_HEADS):
            sl = slice(h * QK_PAD, (h + 1) * QK_PAD)
            qv = q_ref[:, sl]
            _, rq = _rms(qv, qg_v, QK_HEAD)
            dQv = dQ_ref[:, sl]
            dqn = jnp.concatenate([dQv[:, :QK_NOPE], _rope_t(dQv[:, QK_NOPE:], tc, ta, tb)], axis=1)
            dq, dqg = _rms_bwd(qv, qg_v, rq, dqn, QK_HEAD)
            dq_ref[:, sl] = dq.astype(BF16)
            dqg_ref[...] += dqg
            ns = slice(h * QK_NOPE, (h + 1) * QK_NOPE)
            kn = kv_ref[:, ns]
            r = lax.rsqrt((jnp.sum(kn * kn, axis=-1, keepdims=True) + kr_ss) / QK_HEAD + EPS)
            dKv = dK_ref[:, sl]
            dyn = dKv[:, :QK_NOPE]
            dyr = _rope_t(dKv[:, QK_NOPE:], tc, ta, tb)
            proj_s = (jnp.sum(dyn * kg_n * kn, axis=-1, keepdims=True) + jnp.sum(dyr * kg_r * kr, axis=-1, keepdims=True)) / QK_HEAD
            r3 = r * r * r
            dkv_ref[:, ns] = (r * dyn * kg_n - kn * r3 * proj_s).astype(BF16)
            dkr = dkr + (r * dyr * kg_r - kr * r3 * proj_s)
            dkg_n = dkg_n + jnp.sum(dyn * kn * r, axis=0, keepdims=True)
            dkg_r = dkg_r + jnp.sum(dyr * kr * r, axis=0, keepdims=True)
        dkg_ref[...] += jnp.concatenate([dkg_n, dkg_r], axis=1)
        dkr_ref[...] = dkr.astype(BF16)
        dkv_ref[:, MLA_HEADS * QK_NOPE:] = dV_ref[...].astype(BF16)

    return pl.pallas_call(
        body, grid=(S // tm,),
        in_specs=[_col(tm, W, 0), _col(tm, W, 0), _col(tm, 128, offs["kr"]), _col(tm, 128, 0), _col(tm, 128, 0), _col(tm, 128, 0),
                  _whole((1, QK_PAD)), _whole((1, QK_PAD)), _col(tm, W, 0), _col(tm, W, 0), _col(tm, BRANCH_W, 0)],
        out_specs=[_col(tm, W, 0), _col(tm, W, 0), _col(tm, 128, 0), _whole((1, QK_PAD)), _whole((1, QK_PAD))],
        out_shape=[jax.ShapeDtypeStruct((S, W), BF16), jax.ShapeDtypeStruct((S, W), BF16), jax.ShapeDtypeStruct((S, 128), BF16),
                   jax.ShapeDtypeStruct((1, QK_PAD), F32), jax.ShapeDtypeStruct((1, QK_PAD), F32)],
        name=name, compiler_params=_params(("arbitrary",)),
    )(qp, kvp, proj, *tabs, qg, kg, dQ, dK, dV)


def _causal_scores(q, k, qi, ki, tq, tk):
    s = _dot_nt(q, k) * SCORE_SCALE_LOG2
    rows = qi * tq + lax.broadcasted_iota(jnp.int32, (tq, tk), 0)
    cols = ki * tk + lax.broadcasted_iota(jnp.int32, (tq, tk), 1)
    return s, cols <= rows


def _flash_fwd(Q, K, V, proj, offs, name, comm=None):
    S = Q.shape[0]
    t = _pick(S, 1024, 128)
    chunks = t // 128
    mz_cb = offs["mz"] // V_HEAD

    def fold(x, op):
        r = x[:, 0:128]
        for ch in range(1, chunks):
            r = op(r, x[:, ch * 128:(ch + 1) * 128])
        return r

    def body(q_ref, k_ref, v_ref, mz_ref, o_ref, y_ref, lse_ref, mp_scr, lp_scr, acc_scr):
        qi = pl.program_id(1)
        q = q_ref[...]

        def raw_scores(j, masked):
            rows_k = pl.ds(pl.multiple_of(j * t, t), t)
            s = _dot_nt(q, k_ref[rows_k, :])
            if masked:
                rows = lax.broadcasted_iota(jnp.int32, (t, t), 0)
                cols = lax.broadcasted_iota(jnp.int32, (t, t), 1)
                s = jnp.where(cols <= rows, s, NEG)
            return s, rows_k

        mp_scr[...] = jnp.full_like(mp_scr, NEG)

        def max_sweep(j, carry):
            s, _ = raw_scores(j, False)
            mp_scr[...] = jnp.maximum(mp_scr[...], fold(s, jnp.maximum))
            return carry

        lax.fori_loop(0, qi, max_sweep, 0)
        s_diag, _ = raw_scores(qi, True)
        m = jnp.max(jnp.maximum(mp_scr[...], fold(s_diag, jnp.maximum)), axis=-1, keepdims=True) * SCORE_SCALE_LOG2

        lp_scr[...] = jnp.zeros_like(lp_scr)
        acc_scr[...] = jnp.zeros_like(acc_scr)

        def accumulate(j, masked):
            s, rows_k = raw_scores(j, masked)
            p = jnp.exp2(s * SCORE_SCALE_LOG2 - m)
            lp_scr[...] += fold(p, jnp.add)
            acc_scr[...] += _dot(p, v_ref[rows_k, :])

        def sum_sweep(j, carry):
            accumulate(j, False)
            return carry

        lax.fori_loop(0, qi, sum_sweep, 0)
        accumulate(qi, True)
        l = jnp.sum(lp_scr[...], axis=-1, keepdims=True)
        o = acc_scr[...] / l
        o_ref[...] = o
        y_ref[...] = (o * _silu(mz_ref[...])).astype(BF16)
        lse_ref[0] = m + jnp.log2(l)

    in_specs = [pl.BlockSpec((t, QK_PAD), lambda h, i: (i, h)), pl.BlockSpec((S, QK_PAD), lambda h, i: (0, h)),
                pl.BlockSpec((S, V_HEAD), lambda h, i: (0, h)), pl.BlockSpec((t, V_HEAD), lambda h, i: (i, mz_cb + h))]
    out_specs = [pl.BlockSpec((t, V_HEAD), lambda h, i: (i, h)), pl.BlockSpec((t, V_HEAD), lambda h, i: (i, h)),
                 pl.BlockSpec((1, t, 1), lambda h, i: (h, i, 0))]
    out_shape = [jax.ShapeDtypeStruct((S, BRANCH_W), F32), jax.ShapeDtypeStruct((S, BRANCH_W), BF16),
                 jax.ShapeDtypeStruct((MLA_HEADS, S, 1), F32)]
    scratch = [pltpu.VMEM((t, 128), F32), pltpu.VMEM((t, 128), F32), pltpu.VMEM((t, V_HEAD), F32)]
    ins = [Q, K, V, proj]
    if comm is not None:
        return _hosted_call(comm, body, ins, in_specs, out_specs, out_shape, scratch, (MLA_HEADS, S // t), name)
    res = pl.pallas_call(
        body, grid=(MLA_HEADS, S // t), in_specs=in_specs, out_specs=out_specs, out_shape=out_shape, scratch_shapes=scratch,
        name=name, compiler_params=_params(("parallel", "parallel")),
    )(*ins)
    return list(res), []


def _mla_gate_bwd(o, proj, offs, dy, name):
    S = o.shape[0]
    tm = _pick(S, 512, 16)

    def body(o_ref, mz_ref, dy_ref, do_ref, dmz_ref, dl_ref):
        sz, dsz = _silu_and_grad(mz_ref[...])
        dyv, ov = dy_ref[...], o_ref[...]
        do = (dyv * sz).astype(BF16)
        do_ref[...] = do
        dmz_ref[...] = (dyv * ov * dsz).astype(BF16)
        prod = do.astype(F32) * ov
        for h in range(MLA_HEADS):
            dl_ref[h] = jnp.sum(prod[:, h * V_HEAD:(h + 1) * V_HEAD], axis=-1, keepdims=True)

    return pl.pallas_call(
        body, grid=(S // tm,), in_specs=[_col(tm, BRANCH_W, 0), _col(tm, BRANCH_W, offs["mz"]), _col(tm, BRANCH_W, 0)],
        out_specs=[_col(tm, BRANCH_W, 0), _col(tm, BRANCH_W, 0), pl.BlockSpec((MLA_HEADS, tm, 1), lambda i: (0, i, 0))],
        out_shape=[jax.ShapeDtypeStruct((S, BRANCH_W), BF16), jax.ShapeDtypeStruct((S, BRANCH_W), BF16),
                   jax.ShapeDtypeStruct((MLA_HEADS, S, 1), F32)], name=name,
        compiler_params=_params(("parallel",)),
    )(o, proj, dy)


def _flash_bwd(Q, K, V, lse, delta, do, name, comm=None):
    S = Q.shape[0]
    tk = _pick(S, 1024, 128)
    nq = S // tk
    once = pl.Buffered(1)

    def body(q_ref, k_ref, v_ref, lse_ref, dl_ref, do_ref, dq_ref, dk_ref, dv_ref, dk_acc, dv_acc):
        ki = pl.program_id(1)

        @pl.when(ki == 0)
        def _():
            dq_ref[...] = jnp.zeros_like(dq_ref)

        dk_acc[...] = jnp.zeros_like(dk_acc)
        dv_acc[...] = jnp.zeros_like(dv_acc)
        k, v = k_ref[...], v_ref[...]

        def block(i, masked):
            rows = pl.ds(pl.multiple_of(i * tk, tk), tk)
            q, dov = q_ref[rows, :], do_ref[rows, :]
            s, mask = _causal_scores(q, k, i, ki, tk, tk)
            if masked:
                s = jnp.where(mask, s, NEG)
            p = jnp.exp2(s - lse_ref[0, rows, :])
            dv_acc[...] += _dot_tn(p, dov)
            dp = _dot_nt(dov, v)
            ds = p * (dp - dl_ref[0, rows, :]) * (QK_HEAD ** -0.5)
            dk_acc[...] += _dot_tn(ds, q)
            dq_ref[rows, :] += _dot(ds, k)

        def below_diagonal(i, carry):
            block(i, False)
            return carry

        block(ki, True)
        lax.fori_loop(ki + 1, nq, below_diagonal, 0)
        dk_ref[...] = dk_acc[...]
        dv_ref[...] = dv_acc[...]

    in_specs = [pl.BlockSpec((S, QK_PAD), lambda h, j: (0, h), pipeline_mode=once), pl.BlockSpec((tk, QK_PAD), lambda h, j: (j, h)),
                pl.BlockSpec((tk, V_HEAD), lambda h, j: (j, h)), pl.BlockSpec((1, S, 1), lambda h, j: (h, 0, 0), pipeline_mode=once),
                pl.BlockSpec((1, S, 1), lambda h, j: (h, 0, 0), pipeline_mode=once),
                pl.BlockSpec((S, V_HEAD), lambda h, j: (0, h), pipeline_mode=once)]
    out_specs = [pl.BlockSpec((S, QK_PAD), lambda h, j: (0, h)), pl.BlockSpec((tk, QK_PAD), lambda h, j: (j, h)),
                 pl.BlockSpec((tk, V_HEAD), lambda h, j: (j, h))]
    out_shape = [jax.ShapeDtypeStruct((S, MLA_HEADS * QK_PAD), F32), jax.ShapeDtypeStruct((S, MLA_HEADS * QK_PAD), F32),
                 jax.ShapeDtypeStruct((S, BRANCH_W), F32)]
    scratch = [pltpu.VMEM((tk, QK_PAD), F32), pltpu.VMEM((tk, V_HEAD), F32)]
    ins = [Q, K, V, lse, delta, do]
    if comm is not None:
        return _hosted_call(comm, body, ins, in_specs, out_specs, out_shape, scratch, (MLA_HEADS, nq), name)
    res = pl.pallas_call(
        body, grid=(MLA_HEADS, nq), in_specs=in_specs, out_specs=out_specs, out_shape=out_shape, scratch_shapes=scratch, name=name,
        compiler_params=_params(("parallel", "arbitrary")),
    )(*ins)
    return list(res), []


def _merge_specs(S, D, offs, tm, tn, layer):
    g_cb = offs["g"] // tn
    nj = D // tn
    y_specs = [pl.BlockSpec((tm, BRANCH_W), lambda j, i: (i, 0)) for _ in range(4)]
    w_spec = pl.BlockSpec((None, 4, BRANCH_W, tn), lambda j, i: (layer, 0, 0, j))
    g_specs = [pl.BlockSpec((tm, tn), functools.partial(lambda j, i, b: (i, g_cb + b * nj + j), b=b)) for b in range(4)]
    bias_spec = pl.BlockSpec((4, tn), lambda j, i: (0, j))
    return y_specs, w_spec, g_specs, bias_spec


def _merge_fwd(ys, wb, layer, proj, offs, gate_b, name):
    S, D = ys[0].shape[0], wb.shape[3]
    tm, tn = _pick(S, 512, 16), _pick(D, 512, 128)
    y_specs, w_spec, g_specs, bias_spec = _merge_specs(S, D, offs, tm, tn, layer)

    def body(y0, y1, y2, y3, w_ref, g0, g1, g2, g3, b_ref, o_ref):
        acc = None
        for b, (y_ref, g_ref) in enumerate(zip((y0, y1, y2, y3), (g0, g1, g2, g3))):
            gate = 1.0 / (1.0 + jnp.exp(-(g_ref[...] + b_ref[b:b + 1, :])))
            term = gate * _dot(y_ref[...], w_ref[b])
            acc = term if acc is None else acc + term
        o_ref[...] = acc.astype(BF16)

    return pl.pallas_call(
        body, grid=(D // tn, S // tm), in_specs=y_specs + [w_spec] + g_specs + [bias_spec],
        out_specs=pl.BlockSpec((tm, tn), lambda j, i: (i, j)), out_shape=jax.ShapeDtypeStruct((S, D), BF16), name=name,
        compiler_params=_params(("parallel", "parallel")),
    )(*ys, wb, proj, proj, proj, proj, gate_b)


def _merge_bwd(ys, wb, layer, proj, offs, gate_b, dmerged, name):
    S, D = ys[0].shape[0], wb.shape[3]
    tm, tn = _pick(S, 512, 16), _pick(D, 512, 128)
    nj = D // tn
    y_specs, w_spec, g_specs, bias_spec = _merge_specs(S, D, offs, tm, tn, layer)

    def body(y0, y1, y2, y3, w_ref, g0, g1, g2, g3, b_ref, dm_ref, dt_ref, dg_ref, db_ref):
        @pl.when(pl.program_id(1) == 0)
        def _():
            db_ref[...] = jnp.zeros_like(db_ref)

        dm = dm_ref[...]
        for b, (y_ref, g_ref) in enumerate(zip((y0, y1, y2, y3), (g0, g1, g2, g3))):
            gate = 1.0 / (1.0 + jnp.exp(-(g_ref[...] + b_ref[b:b + 1, :])))
            t = _dot(y_ref[...], w_ref[b])
            dt_ref[b] = (dm * gate).astype(BF16)
            dgp = dm * t * gate * (1.0 - gate)
            dg_ref[b] = dgp.astype(BF16)
            db_ref[b:b + 1, :] += jnp.sum(dgp, axis=0, keepdims=True)

    stacked = pl.BlockSpec((4, tm, tn), lambda j, i: (0, i, j))
    return pl.pallas_call(
        body, grid=(nj, S // tm),
        in_specs=y_specs + [w_spec] + g_specs + [bias_spec, pl.BlockSpec((tm, tn), lambda j, i: (i, j))],
        out_specs=[stacked, stacked, pl.BlockSpec((4, tn), lambda j, i: (0, j))],
        out_shape=[jax.ShapeDtypeStruct((4, S, D), BF16), jax.ShapeDtypeStruct((4, S, D), BF16), jax.ShapeDtypeStruct((4, D), F32)],
        name=name, compiler_params=_params(("parallel", "arbitrary")),
    )(*ys, wb, proj, proj, proj, proj, gate_b, dmerged)


HBM_SPEC = pl.BlockSpec(memory_space=pl.ANY)


def _mesh_place():
    x, y, c = lax.axis_index("x"), lax.axis_index("y"), lax.axis_index("c")
    return x, y, c, [(1 - x, y), (x, 1 - y), (1 - x, 1 - y)]


def _remote(src, dst, send_sem, recv_sem, to):
    return pltpu.make_async_remote_copy(src_ref=src, dst_ref=dst, send_sem=send_sem, recv_sem=recv_sem, device_id=to,
                                        device_id_type=pl.DeviceIdType.MESH)


class _GatherPlan:
    def __init__(self, arrs):
        self.arrs, self.n = list(arrs), len(arrs)
        self.out_shape = [jax.ShapeDtypeStruct((N_DEV,) + a.shape, a.dtype) for a in arrs]
        self.scratch = [pltpu.SemaphoreType.DMA((self.n, 7)), pltpu.SemaphoreType.DMA((self.n, 7)), pltpu.SemaphoreType.DMA((self.n,))]

    def _copies(self, ins, dsts, sems):
        send_sems, recv_sems, local_sems = sems
        x, y, c, chips = _mesh_place()
        slot = lambda px, py, pc: 4 * px + 2 * py + pc
        me, sibling = slot(x, y, c), (x, y, 1 - c)

        def copy(a, k, src, block, to):
            return lambda: _remote(src(), dsts[a].at[block], send_sems.at[a, k], recv_sems.at[a, k], to)

        rng = range(self.n)
        shard = lambda a: (lambda: ins[a])
        local = [(lambda a=a: pltpu.make_async_copy(ins[a], dsts[a].at[me], local_sems.at[a])) for a in rng]
        own = [copy(a, 0, shard(a), me, sibling) for a in rng]
        own += [copy(a, 1 + j, shard(a), me, (*chip, c)) for a in rng for j, chip in enumerate(chips)]
        landed = [copy(a, 1 + j, shard(a), slot(*chip, c), sibling) for a in rng for j, chip in enumerate(chips)]
        passed = [copy(a, 4 + j, (lambda a=a, chip=chip: dsts[a].at[slot(*chip, c)]), slot(*chip, c), sibling)
                  for a in rng for j, chip in enumerate(chips)]
        from_sibling = [copy(a, 0, shard(a), slot(x, y, 1 - c), sibling) for a in rng]
        from_sibling += [copy(a, 4 + j, shard(a), slot(*chip, 1 - c), sibling) for a in rng for j, chip in enumerate(chips)]
        return local, own, landed, passed, from_sibling

    def start(self, ins, dsts, sems):
        local, own, _, _, _ = self._copies(ins, dsts, sems)
        for make in local + own:
            make().start()

    def relay(self, ins, dsts, sems):
        _, _, landed, passed, _ = self._copies(ins, dsts, sems)
        for arrived, onward in zip(landed, passed):
            arrived().wait_recv()
            onward().start()

    def finish(self, ins, dsts, sems):
        local, own, _, passed, from_sibling = self._copies(ins, dsts, sems)
        for make in from_sibling:
            make().wait_recv()
        for make in own + passed:
            make().wait_send()
        for make in local:
            make().wait()


class _ChipExchangePlan:
    def __init__(self, arrs):
        self.arrs, self.n = list(arrs), len(arrs)
        self.out_shape = [jax.ShapeDtypeStruct(a.shape, a.dtype) for a in arrs]
        self.scratch = [pltpu.SemaphoreType.DMA((self.n, 3)), pltpu.SemaphoreType.DMA((self.n, 3)), pltpu.SemaphoreType.DMA((self.n,))]

    def _copies(self, ins, dsts, sems):
        send_sems, recv_sems, local_sems = sems
        x, y, c, chips = _mesh_place()
        mine = 2 * x + y
        local = [pltpu.make_async_copy(ins[a].at[mine], dsts[a].at[mine], local_sems.at[a]) for a in range(self.n)]
        remote = [_remote(ins[a].at[2 * px + py], dsts[a].at[mine], send_sems.at[a, j], recv_sems.at[a, j], (px, py, c))
                  for a in range(self.n) for j, (px, py) in enumerate(chips)]
        return local, remote

    def start(self, ins, dsts, sems):
        local, remote = self._copies(ins, dsts, sems)
        for cp in local + remote:
            cp.start()

    def relay(self, ins, dsts, sems):
        pass

    def finish(self, ins, dsts, sems):
        local, remote = self._copies(ins, dsts, sems)
        for cp in remote + local:
            cp.wait()


def _run_plan(plan, name):
    n = plan.n

    def body(*refs):
        ins, dsts, sems = refs[:n], refs[n:2 * n], refs[2 * n:]
        plan.start(ins, dsts, sems)
        plan.relay(ins, dsts, sems)
        plan.finish(ins, dsts, sems)

    res = pl.pallas_call(body, in_specs=[HBM_SPEC] * n, out_specs=[HBM_SPEC] * n, out_shape=plan.out_shape,
                         scratch_shapes=plan.scratch, name=name)(*plan.arrs)
    return list(res)


def _sibling_swap(arrs, name):
    n = len(arrs)

    def body(*refs):
        ins, dsts = refs[:n], refs[n:2 * n]
        send_sems, recv_sems = refs[2 * n:]
        x, y, c, _ = _mesh_place()
        cps = [_remote(ins[a].at[q, 1 - c], dsts[a].at[q], send_sems.at[a, q], recv_sems.at[a, q], (x, y, 1 - c))
               for a in range(n) for q in range(4)]
        for cp in cps:
            cp.start()
        for cp in cps:
            cp.wait()

    res = pl.pallas_call(
        body, in_specs=[HBM_SPEC] * n, out_specs=[HBM_SPEC] * n,
        out_shape=[jax.ShapeDtypeStruct((4,) + a.shape[2:], a.dtype) for a in arrs],
        scratch_shapes=[pltpu.SemaphoreType.DMA((n, 4)), pltpu.SemaphoreType.DMA((n, 4))], name=name,
    )(*arrs)
    return list(res)


def _pair_sum(staged, theirs, core, name):
    _, _, R, C = staged.shape
    tr = _pick(R, max(16, ((1 << 19) // C) // 16 * 16), 16)

    def body(c_ref, a_ref, b_ref, o_ref):
        o_ref[...] = (a_ref[...].astype(F32) + b_ref[...].astype(F32)).astype(BF16)

    blk = pl.BlockSpec((None, tr, C), lambda q, i, c: (q, i, 0))
    grid_spec = pltpu.PrefetchScalarGridSpec(
        num_scalar_prefetch=1, grid=(4, R // tr),
        in_specs=[pl.BlockSpec((None, None, tr, C), lambda q, i, c: (q, c[0], i, 0)), blk], out_specs=blk)
    return pl.pallas_call(
        body, grid_spec=grid_spec, out_shape=jax.ShapeDtypeStruct((4, R, C), BF16), name=name,
        compiler_params=_params(("parallel", "parallel")),
    )(core, staged, theirs)


def _adamw(parts, w, m, v, name):
    R, C = w.shape
    n_parts = parts.shape[0]
    tr = _pick(R, max(16, ((1 << 18) // C) // 16 * 16), 16)

    def body(p_ref, w_ref, m_ref, v_ref, g_ref, d_ref, nm_ref, nv_ref):
        g = p_ref[0].astype(F32)
        for s in range(1, n_parts):
            g = g + p_ref[s].astype(F32)
        mn = ADAM_B1 * m_ref[...] + (1.0 - ADAM_B1) * g
        vn = ADAM_B2 * v_ref[...] + (1.0 - ADAM_B2) * (g * g)
        m_hat = mn / (1.0 - ADAM_B1 ** ADAM_STEP)
        v_hat = vn / (1.0 - ADAM_B2 ** ADAM_STEP)
        g_ref[...] = g
        d_ref[...] = -ADAM_LR * (m_hat / (jnp.sqrt(v_hat) + ADAM_EPS) + ADAM_WD * w_ref[...])
        nm_ref[...] = mn
        nv_ref[...] = vn

    blk = pl.BlockSpec((tr, C), lambda i: (i, 0))
    return pl.pallas_call(
        body, grid=(R // tr,), in_specs=[pl.BlockSpec((n_parts, tr, C), lambda i: (0, i, 0)), blk, blk, blk],
        out_specs=[blk] * 4, out_shape=[jax.ShapeDtypeStruct((R, C), F32)] * 4, name=name,
        compiler_params=_params(("parallel",)),
    )(parts, w, m, v)


def _offsets(L, D):
    o = {"pv": 0, "pz": BRANCH_W, "cq": 2 * BRANCH_W, "ckv": 2 * BRANCH_W + L, "mz": 2 * BRANCH_W + 2 * L}
    for prev, nxt in (("mz", "cb"), ("cb", "cc"), ("cc", "cx"), ("cx", "cz"), ("cz", "xq"), ("xq", "xz"), ("xz", "g")):
        o[nxt] = o[prev] + BRANCH_W
    o["kr"] = o["g"] + 4 * D
    o["n"] = o["kr"] + 128
    return o


def _unshard(gathered, axis):
    moved = jnp.moveaxis(gathered, 0, axis)
    shape = list(moved.shape)
    shape[axis:axis + 2] = [shape[axis] * shape[axis + 1]]
    return moved.reshape(shape)


def _reshard(full, axis):
    shape = list(full.shape)
    shape[axis:axis + 1] = [N_DEV, shape[axis] // N_DEV]
    return jnp.moveaxis(full.reshape(shape), axis, 0)


def _pad_heads(g, width):
    return jnp.pad(g, ((0, 0), (0, width - g.shape[1])))


def kernel(x, mem, positions, norm_g, w_in, gate_b, pool_w, pool_scale, q_a_norm_g, kv_a_norm_g, w_uq, w_ukv, mla_q_norm_g, mla_k_norm_g, conv_w, mem_norm_g, w_mem_kv, xattn_q_norm_g, xattn_k_norm_g, w_branch, w_out, loss_target, m_norm_g, m_w_in, m_gate_b, m_pool_w, m_pool_scale, m_q_a_norm_g, m_kv_a_norm_g, m_w_uq, m_w_ukv, m_mla_q_norm_g, m_mla_k_norm_g, m_conv_w, m_mem_norm_g, m_w_mem_kv, m_xattn_q_norm_g, m_xattn_k_norm_g, m_w_branch, m_w_out, v_norm_g, v_w_in, v_gate_b, v_pool_w, v_pool_scale, v_q_a_norm_g, v_kv_a_norm_g, v_w_uq, v_w_ukv, v_mla_q_norm_g, v_mla_k_norm_g, v_conv_w, v_mem_norm_g, v_w_mem_kv, v_xattn_q_norm_g, v_xattn_k_norm_g, v_w_branch, v_w_out):
    depth = norm_g.shape[0]
    S, D = x.shape[1], x.shape[2]
    L = w_uq.shape[1]
    offs = _offsets(L, D)
    seg_a = 2 * BRANCH_W + 2 * L
    xs = x.reshape(S, D)
    mems = mem.reshape(mem.shape[1], D)

    sharded = dict(w_in=w_in, pool_w=pool_w, w_uq=w_uq, w_ukv=w_ukv, conv_w=conv_w, w_mem_kv=w_mem_kv, w_branch=w_branch, w_out=w_out)
    shard_axis = dict(w_in=2, pool_w=2, w_uq=2, w_ukv=2, conv_w=2, w_mem_kv=1, w_branch=3, w_out=1)
    def permuted_w_in(g):
        f = _unshard(g, 1)
        return jnp.concatenate([f[:, :seg_a], f[:, seg_a + QK_ROPE:], f[:, seg_a:seg_a + QK_ROPE], jnp.zeros((D, 128 - QK_ROPE), BF16)], axis=-1)

    w_in_local = [w_in[l].astype(BF16) for l in range(depth)]
    w_in_p = [permuted_w_in(_run_plan(_GatherPlan([w_in_local[0]]), "gather_w_in0")[0])] + [None] * (depth - 1)
    rest_keys = [k for k in shard_axis if k != "w_in"]
    rest_plan = _GatherPlan([sharded[k] if k == "conv_w" else sharded[k].astype(BF16) for k in rest_keys])
    later_w_in_plan = _GatherPlan(w_in_local[1:]) if depth > 1 else None

    inv = ROPE_THETA ** (-jnp.arange(0, QK_ROPE, 2, dtype=F32) / QK_ROPE)
    ang = positions.reshape(S).astype(F32)[:, None] * inv
    cos, sin, z = jnp.cos(ang), jnp.sin(ang), jnp.zeros_like(ang)
    tabs = (jnp.concatenate([cos, cos, z, z], axis=1), jnp.concatenate([-sin, z, z, z], axis=1), jnp.concatenate([z, sin, z, z], axis=1))

    row = lambda a, l: a[l][None, :]
    saved = []
    cur = xs
    for l in range(depth):
        h, hT = _rms_rows(cur, D, 0, row(norm_g, l), f"norm_fwd{l}", with_transpose=True)
        proj_tiles = dict(tm=512, tn=3712, tk=2048, n_outer=True)
        if l == 0:
            proj, carried = _matmul(h, w_in_p[0], "nn", F32, "proj_fwd0", comm=rest_plan, **proj_tiles)
            gathered = {k: _unshard(g, shard_axis[k]) for k, g in zip(rest_keys, carried)}
            conv_full = gathered["conv_w"]
            pool_w_f = gathered["pool_w"]
            w_uq_f = gathered["w_uq"].reshape(depth, L, MLA_HEADS, QK_HEAD)
            w_uq_p = jnp.pad(w_uq_f, ((0, 0), (0, 0), (0, 0), (0, QK_PAD - QK_HEAD))).reshape(depth, L, MLA_HEADS * QK_PAD)
            w_ukv_f = gathered["w_ukv"].reshape(depth, L, MLA_HEADS, QK_NOPE + V_HEAD)
            w_ukv_p = jnp.concatenate([w_ukv_f[..., :QK_NOPE].reshape(depth, L, -1), w_ukv_f[..., QK_NOPE:].reshape(depth, L, -1)], axis=-1)
            w_mem_f = gathered["w_mem_kv"]
            w_br_f = gathered["w_branch"]
            w_out_f = gathered["w_out"]
        else:
            proj = _matmul(h, w_in_p[l], "nn", F32, f"proj_fwd{l}", **proj_tiles)
        y_pool = _pool_fwd(proj, offs, pool_w_f[l], row(pool_scale, l), f"pool_fwd{l}")
        cqn = _rms_rows(proj, L, offs["cq"], row(q_a_norm_g, l), f"cq_norm{l}")
        ckvn = _rms_rows(proj, L, offs["ckv"], row(kv_a_norm_g, l), f"ckv_norm{l}")
        qp = _matmul(cqn, w_uq_p[l], "nn", F32, f"uq_fwd{l}")
        kvp = _matmul(ckvn, w_ukv_p[l], "nn", F32, f"ukv_fwd{l}")
        qg = _pad_heads(row(mla_q_norm_g, l), QK_PAD)
        kg = _pad_heads(row(mla_k_norm_g, l), QK_PAD)
        Q, K, V = _mla_qk_fwd(qp, kvp, proj, offs, tabs, qg, kg, f"mla_qk_fwd{l}")
        (o, y_mla, lse), carried = _flash_fwd(Q, K, V, proj, offs, f"flash_fwd{l}", comm=later_w_in_plan if l == 0 else None)
        if l == 0 and later_w_in_plan is not None:
            w_in_p[1:] = [permuted_w_in(g) for g in carried]
        y_conv = _conv_fwd(proj, offs, conv_full[l], f"conv_fwd{l}")
        memn = _rms_rows(mems, D, 0, row(mem_norm_g, l), f"mem_norm{l}")
        mem_kv = _matmul(memn, w_mem_f, "nn", F32, f"mem_kv{l}", b_idx=(l,))
        y_mem = _xattn_fwd(proj, offs, mem_kv, row(xattn_q_norm_g, l), row(xattn_k_norm_g, l), f"xattn_fwd{l}")
        ys = (y_pool, y_mla, y_conv, y_mem)
        gb = gate_b[l].reshape(4, D)
        merged = _merge_fwd(ys, w_br_f, l, proj, offs, gb, f"merge_fwd{l}")
        nxt = _matmul(merged, w_out_f, "nn", F32, f"out_fwd{l}", res=cur, b_idx=(l,))
        saved.append(dict(x=cur, hT=hT, proj=proj, cqn=cqn, ckvn=ckvn, qp=qp, kvp=kvp, qg=qg, kg=kg, Q=Q, K=K, V=V, o=o, lse=lse,
                          memn=memn, mem_kv=mem_kv, ys=ys, gb=gb, merged=merged))
        cur = nxt

    gx, loss_acc = _loss_rows(cur, loss_target.reshape(S, D), "loss")
    loss = lax.psum(0.5 * loss_acc[0, 0], MESH_AXES)

    grads = {k: [None] * depth for k in ("norm_g", "w_in", "gate_b", "pool_w", "pool_scale", "q_a_norm_g", "kv_a_norm_g", "w_uq", "w_ukv",
                                         "mla_q_norm_g", "mla_k_norm_g", "conv_w", "mem_norm_g", "w_mem_kv", "xattn_q_norm_g",
                                         "xattn_k_norm_g", "w_branch", "w_out")}
    names_sh = list(shard_axis)
    core = lax.axis_index("c").astype(jnp.int32).reshape(1)
    received = [None] * depth
    pending = None

    def pair_sums(names, l):
        staged = [_reshard(grads[k][l], shard_axis[k] - 1).reshape(4, 2, -1, sharded[k].shape[-1]).astype(BF16) for k in names]
        tag = f"{names[0]}{l}"
        given = _sibling_swap(staged, f"pair_swap_{tag}")
        return [_pair_sum(st, theirs, core, f"pair_sum_{k}{l}") for k, st, theirs in zip(names, staged, given)]

    for l in reversed(range(depth)):
        sv = saved[l]
        proj, ys = sv["proj"], sv["ys"]
        dmerged = _matmul(gx, w_out_f, "nt", F32, f"out_bwd_x{l}", b_idx=(l,))
        grads["w_out"][l] = _matmul(sv["merged"], gx, "tn", BF16, f"out_bwd_w{l}")
        dt, dgp, dgb = _merge_bwd(ys, w_br_f, l, proj, offs, sv["gb"], dmerged, f"merge_bwd{l}")
        grads["gate_b"][l] = dgb.reshape(4 * D)
        dys, dwb = [], []
        for b in range(4):
            dys.append(_matmul(dt, w_br_f, "nt", F32, f"branch_bwd_x{l}_{b}", a_idx=(b,), b_idx=(l, b)))
            dwb.append(_matmul(ys[b], dt, "tn", BF16, f"branch_bwd_w{l}_{b}", b_idx=(b,)))
        grads["w_branch"][l] = jnp.stack(dwb)
        d_pz, e, d_pw, d_ps = _pool_bwd_a(proj, offs, pool_w_f[l], row(pool_scale, l), dys[0], f"pool_bwd_a{l}")
        d_pv = _pool_bwd_b(e, f"pool_bwd_b{l}")
        grads["pool_w"][l], grads["pool_scale"][l] = d_pw, d_ps[0]
        do, d_mz, delta = _mla_gate_bwd(sv["o"], proj, offs, dys[1], f"mla_gate_bwd{l}")
        (dQ, dK, dV), carried = _flash_bwd(sv["Q"], sv["K"], sv["V"], sv["lse"], delta, do, f"flash_bwd{l}",
                                           comm=None if pending is None else pending[1])
        if pending is not None:
            received[pending[0]][0], pending = carried[0], None
        dqp, dkvp, d_kr, dqg, dkg = _mla_qk_bwd(sv["qp"], sv["kvp"], proj, offs, tabs, sv["qg"], sv["kg"], dQ, dK, dV, f"mla_qk_bwd{l}")
        grads["mla_q_norm_g"][l], grads["mla_k_norm_g"][l] = dqg[0, :QK_HEAD], dkg[0, :QK_HEAD]
        dcqn = _matmul(dqp, w_uq_p[l], "nt", F32, f"uq_bwd_x{l}")
        d_wuq = _matmul(sv["cqn"], dqp, "tn", BF16, f"uq_bwd_w{l}")
        dckvn = _matmul(dkvp, w_ukv_p[l], "nt", F32, f"ukv_bwd_x{l}")
        d_wukv = _matmul(sv["ckvn"], dkvp, "tn", BF16, f"ukv_bwd_w{l}")
        grads["w_uq"][l] = d_wuq.reshape(L, MLA_HEADS, QK_PAD)[..., :QK_HEAD].reshape(L, MLA_HEADS * QK_HEAD)
        grads["w_ukv"][l] = jnp.concatenate([d_wukv[:, :MLA_HEADS * QK_NOPE].reshape(L, MLA_HEADS, QK_NOPE),
                                             d_wukv[:, MLA_HEADS * QK_NOPE:].reshape(L, MLA_HEADS, V_HEAD)], axis=-1).reshape(L, -1)
        d_cq, dqa = _rms_rows_bwd(proj, L, offs["cq"], row(q_a_norm_g, l), dcqn, BF16, f"cq_norm_bwd{l}")
        d_ckv, dkva = _rms_rows_bwd(proj, L, offs["ckv"], row(kv_a_norm_g, l), dckvn, BF16, f"ckv_norm_bwd{l}")
        grads["q_a_norm_g"][l], grads["kv_a_norm_g"][l] = dqa[0], dkva[0]
        d_cb, d_cc, d_cx, d_cz, d_cw = _conv_bwd(proj, offs, conv_full[l], dys[2], f"conv_bwd{l}")
        grads["conv_w"][l] = d_cw
        d_xq, d_xz, d_memkv, dxqg, dxkg = _xattn_bwd(proj, offs, sv["mem_kv"], row(xattn_q_norm_g, l), row(xattn_k_norm_g, l), dys[3], f"xattn_bwd{l}")
        grads["xattn_q_norm_g"][l], grads["xattn_k_norm_g"][l] = dxqg[0], dxkg[0]
        grads["w_mem_kv"][l] = _matmul(sv["memn"], d_memkv, "tn", BF16, f"mem_kv_bwd_w{l}")
        dmemn = _matmul(d_memkv, w_mem_f, "nt", F32, f"mem_kv_bwd_x{l}", b_idx=(l,))
        grads["mem_norm_g"][l] = _norm_g_grad(mems, row(mem_norm_g, l), dmemn, f"mem_norm_bwd{l}")[0]
        dproj = jnp.concatenate([d_pv, d_pz, d_cq, d_ckv, d_mz, d_cb, d_cc, d_cx, d_cz, d_xq, d_xz, dgp[0], dgp[1], dgp[2], dgp[3], d_kr], axis=1)
        early = pair_sums([k for k in names_sh if k != "w_in"], l)
        dh, early_received = _matmul(dproj, w_in_p[l], "nt", F32, f"proj_bwd_x{l}", tm=1024, tn=1024, tk=3712, comm=_ChipExchangePlan(early))
        d_win_p = _matmul(sv["hT"], dproj, "nn", BF16, f"proj_bwd_w{l}", tm=512, tn=3712, tk=2048, n_outer=True)
        n_p = offs["n"]
        grads["w_in"][l] = jnp.concatenate([d_win_p[:, :seg_a], d_win_p[:, n_p - 128:n_p - 128 + QK_ROPE], d_win_p[:, seg_a:n_p - 128]], axis=1)
        gx, dng = _rms_rows_bwd(sv["x"], D, 0, row(norm_g, l), dh, F32, f"norm_bwd{l}", res=gx)
        grads["norm_g"][l] = dng[0]
        received[l] = [None] + early_received
        plan = _ChipExchangePlan(pair_sums(["w_in"], l))
        if l > 0:
            pending = (l, plan)
        else:
            received[0][0] = _run_plan(plan, "chip_exchange_w_in0")[0]
    grad_x = gx.reshape(1, S, D)
    full = {k: jnp.stack(v) for k, v in grads.items() if k not in shard_axis}

    moments = dict(w_in=(m_w_in, v_w_in), pool_w=(m_pool_w, v_pool_w), w_uq=(m_w_uq, v_w_uq), w_ukv=(m_w_ukv, v_w_ukv),
                   conv_w=(m_conv_w, v_conv_w), w_mem_kv=(m_w_mem_kv, v_w_mem_kv), w_branch=(m_w_branch, v_w_branch), w_out=(m_w_out, v_w_out))
    results = {}
    for ki, k in enumerate(names_sh):
        w = sharded[k]
        m, v = moments[k]
        flat = lambda a: a.reshape(-1, w.shape[-1])
        parts = jnp.concatenate([received[l][ki] for l in range(depth)], axis=1)
        outs = _adamw(parts, flat(w), flat(m), flat(v), f"adamw_{k}")
        results[k] = [o.reshape(w.shape) for o in outs]

    small = dict(norm_g=(norm_g, m_norm_g, v_norm_g), gate_b=(gate_b, m_gate_b, v_gate_b), pool_scale=(pool_scale, m_pool_scale, v_pool_scale),
                 q_a_norm_g=(q_a_norm_g, m_q_a_norm_g, v_q_a_norm_g), kv_a_norm_g=(kv_a_norm_g, m_kv_a_norm_g, v_kv_a_norm_g),
                 mla_q_norm_g=(mla_q_norm_g, m_mla_q_norm_g, v_mla_q_norm_g), mla_k_norm_g=(mla_k_norm_g, m_mla_k_norm_g, v_mla_k_norm_g),
                 mem_norm_g=(mem_norm_g, m_mem_norm_g, v_mem_norm_g), xattn_q_norm_g=(xattn_q_norm_g, m_xattn_q_norm_g, v_xattn_q_norm_g),
                 xattn_k_norm_g=(xattn_k_norm_g, m_xattn_k_norm_g, v_xattn_k_norm_g))
    names_sm = list(small)
    sizes = [small[k][0].size for k in names_sm]
    total = sum(sizes)
    rows = -(-total // (8 * 128)) * 8

    def pack(arrs):
        flat = jnp.concatenate([a.reshape(-1) for a in arrs])
        return jnp.pad(flat, (0, rows * 128 - total)).reshape(rows, 128)

    g_small = _run_plan(_GatherPlan([pack([full[k] for k in names_sm])]), "gather_small_grads")[0]
    outs = _adamw(g_small, pack([small[k][0] for k in names_sm]), pack([small[k][1] for k in names_sm]),
                  pack([small[k][2] for k in names_sm]), "adamw_small")
    for kind, o in enumerate(outs):
        flat, pos = o.reshape(-1), 0
        for k, n in zip(names_sm, sizes):
            results.setdefault(k, [None] * 4)[kind] = flat[pos:pos + n].reshape(small[k][0].shape)
            pos += n

    order = ["norm_g", "w_in", "gate_b", "pool_w", "pool_scale", "q_a_norm_g", "kv_a_norm_g", "w_uq", "w_ukv", "mla_q_norm_g",
             "mla_k_norm_g", "conv_w", "mem_norm_g", "w_mem_kv", "xattn_q_norm_g", "xattn_k_norm_g", "w_branch", "w_out"]
    return (loss, grad_x, *[results[k][0] for k in order], *[results[k][1] for k in order],
            *[results[k][2] for k in order], *[results[k][3] for k in order])
```

```python
import functools

import jax
import jax.numpy as jnp
from jax import lax
from jax.experimental import pallas as pl
from jax.experimental.pallas import tpu as pltpu

F32 = jnp.float32
BF16 = jnp.bfloat16
MESH_AXES = ("x", "y", "c")
N_DEV = 8

BRANCH_W = 1024
POOL_WINDOWS = (2, 4, 8, 16)
POOL_GW = BRANCH_W // 4
MLA_HEADS = 8
QK_NOPE = 128
QK_ROPE = 64
QK_HEAD = QK_NOPE + QK_ROPE
QK_PAD = 256
V_HEAD = 128
XATTN_HEADS = 4
XATTN_HD = BRANCH_W // XATTN_HEADS
ROPE_THETA = 10000.0
EPS = 1e-6
ADAM_LR = 0.001
ADAM_B1 = 0.9
ADAM_B2 = 0.999
ADAM_EPS = 1e-08
ADAM_WD = 0.01
ADAM_STEP = 10
HALO = 16
NEG = -1e30
SCORE_SCALE_LOG2 = (QK_HEAD ** -0.5) * 1.4426950408889634
VMEM_LIMIT = 56 << 20


def _pick(n, pref, align):
    if n <= pref:
        return n
    t = (pref // align) * align
    while t >= align:
        if n % t == 0:
            return t
        t -= align
    return n


def _params(sem):
    return pltpu.CompilerParams(dimension_semantics=sem, vmem_limit_bytes=VMEM_LIMIT)


def _silu(z):
    return z * (1.0 / (1.0 + jnp.exp(-z)))


def _silu_and_grad(z):
    s = 1.0 / (1.0 + jnp.exp(-z))
    return z * s, s * (1.0 + z * (1.0 - s))


def _dot(a, b):
    return jnp.dot(a.astype(BF16), b.astype(BF16), preferred_element_type=F32)


def _dot_nt(a, b):
    return lax.dot_general(a.astype(BF16), b.astype(BF16), (((1,), (1,)), ((), ())), preferred_element_type=F32)


def _dot_tn(a, b):
    return lax.dot_general(a.astype(BF16), b.astype(BF16), (((0,), (0,)), ((), ())), preferred_element_type=F32)


def _rms(x, g, n):
    r = lax.rsqrt(jnp.sum(x * x, axis=-1, keepdims=True) / n + EPS)
    return x * r * g, r


def _rms_bwd(x, g, r, dy, n):
    dyg = dy * g
    dx = r * dyg - x * (r * r * r) * (jnp.sum(dyg * x, axis=-1, keepdims=True) / n)
    dg = jnp.sum(dy * x * r, axis=0, keepdims=True)
    return dx, dg


def _col(tm, w, off):
    assert off % w == 0
    cb = off // w
    return pl.BlockSpec((tm, w), lambda i: (i, cb))


def _whole(shape):
    nd = len(shape)
    return pl.BlockSpec(shape, lambda *_: (0,) * nd)


def _lead_spec(block, fn, idx):
    if not idx:
        return pl.BlockSpec(block, fn)
    return pl.BlockSpec((None,) * len(idx) + block, lambda i, j, k: tuple(idx) + fn(i, j, k))


def _hosted_call(plan, body, ins, in_specs, out_specs, out_shape, scratch, grid, name):
    n_in, n_out, n_scr, n = len(ins), len(out_specs), len(scratch), plan.n
    n_steps = 1
    for g in grid:
        n_steps *= g

    def wrapped(*refs):
        a, c_in = refs[:n_in], refs[n_in:n_in + n]
        o, c_out = refs[n_in + n:n_in + n + n_out], refs[n_in + n + n_out:n_in + 2 * n + n_out]
        rest = refs[n_in + 2 * n + n_out:]
        scr, sems = rest[:n_scr], rest[n_scr:]
        step = 0
        for d, g in enumerate(grid):
            step = step * g + pl.program_id(d)

        @pl.when(step == 0)
        def _():
            plan.start(c_in, c_out, sems)

        body(*a, *o, *scr)

        @pl.when(step == n_steps // 2)
        def _():
            plan.relay(c_in, c_out, sems)

        @pl.when(step == n_steps - 1)
        def _():
            plan.finish(c_in, c_out, sems)

    res = pl.pallas_call(
        wrapped, grid=grid, in_specs=list(in_specs) + [HBM_SPEC] * n, out_specs=list(out_specs) + [HBM_SPEC] * n,
        out_shape=list(out_shape) + plan.out_shape, scratch_shapes=list(scratch) + plan.scratch, name=name,
        compiler_params=_params(("arbitrary",) * len(grid)),
    )(*ins, *plan.arrs)
    return list(res[:n_out]), list(res[n_out:])


def _matmul(a, b, mode, out_dtype, name, tm=1024, tn=1024, tk=2048, res=None, a_idx=(), b_idx=(), n_outer=False, comm=None):
    a_shape, b_shape = a.shape[len(a_idx):], b.shape[len(b_idx):]
    if mode == "nn":
        (M, K), N = a_shape, b_shape[1]
    elif mode == "nt":
        (M, K), N = a_shape, b_shape[0]
    else:
        (K, M), N = a_shape, b_shape[1]
    tm, tn, tk = _pick(M, tm, 128), _pick(N, tn, 128), _pick(K, tk, 128)
    nk = K // tk
    order = (lambda fn: (lambda j, i, k: fn(i, j, k))) if n_outer else (lambda fn: fn)
    if mode == "nn":
        a_spec = _lead_spec((tm, tk), order(lambda i, j, k: (i, k)), a_idx)
        b_spec = _lead_spec((tk, tn), order(lambda i, j, k: (k, j)), b_idx)
        op = _dot
    elif mode == "nt":
        a_spec = _lead_spec((tm, tk), order(lambda i, j, k: (i, k)), a_idx)
        b_spec = _lead_spec((tn, tk), order(lambda i, j, k: (j, k)), b_idx)
        op = _dot_nt
    else:
        a_spec = _lead_spec((tk, tm), order(lambda i, j, k: (k, i)), a_idx)
        b_spec = _lead_spec((tk, tn), order(lambda i, j, k: (k, j)), b_idx)
        op = _dot_tn
    o_spec = pl.BlockSpec((tm, tn), order(lambda i, j, k: (i, j)))
    has_res = res is not None

    def body(*refs):
        a_ref, b_ref = refs[0], refs[1]
        res_ref = refs[2] if has_res else None
        o_ref = refs[2 + has_res]
        part = op(a_ref[...], b_ref[...])

        def finish(acc):
            if has_res:
                acc = res_ref[...] + acc
            o_ref[...] = acc.astype(out_dtype)

        if nk == 1:
            finish(part)
        else:
            acc_ref = refs[3 + has_res]
            k = pl.program_id(2)

            @pl.when(k == 0)
            def _():
                acc_ref[...] = part

            @pl.when(k > 0)
            def _():
                acc_ref[...] += part

            @pl.when(k == nk - 1)
            def _():
                finish(acc_ref[...])

    ins = [a, b] + ([res] if has_res else [])
    in_specs = [a_spec, b_spec] + ([o_spec] if has_res else [])
    scratch = [pltpu.VMEM((tm, tn), F32)] if nk > 1 else []
    grid = (N // tn, M // tm, nk) if n_outer else (M // tm, N // tn, nk)
    out_shape = jax.ShapeDtypeStruct((M, N), out_dtype)
    if comm is not None:
        (out,), carried = _hosted_call(comm, body, ins, in_specs, [o_spec], [out_shape], scratch, grid, name)
        return out, carried
    return pl.pallas_call(
        body, grid=grid, in_specs=in_specs, out_specs=o_spec, out_shape=out_shape, scratch_shapes=scratch, name=name,
        compiler_params=_params(("parallel", "parallel", "arbitrary")),
    )(*ins)


def _rms_rows(x, width, off, g, name, with_transpose=False):
    S = x.shape[0]
    tm = _pick(S, 512, 128 if with_transpose else 16)

    def body(x_ref, g_ref, o_ref, *t_ref):
        y, _ = _rms(x_ref[...], g_ref[...], width)
        o_ref[...] = y.astype(BF16)
        if with_transpose:
            t_ref[0][...] = y.T.astype(BF16)

    out_specs = [_col(tm, width, 0)] + ([pl.BlockSpec((width, tm), lambda i: (0, i))] if with_transpose else [])
    out_shape = [jax.ShapeDtypeStruct((S, width), BF16)] + ([jax.ShapeDtypeStruct((width, S), BF16)] if with_transpose else [])
    res = pl.pallas_call(
        body, grid=(S // tm,), in_specs=[_col(tm, width, off), _whole((1, width))], out_specs=out_specs,
        out_shape=out_shape, name=name, compiler_params=_params(("parallel",)),
    )(x, g)
    return res if with_transpose else res[0]


def _rms_rows_bwd(x, width, off, g, dy, out_dtype, name, res=None):
    S = x.shape[0]
    tm = _pick(S, 256, 16)
    has_res = res is not None

    def body(*refs):
        x_ref, g_ref, dy_ref = refs[:3]
        res_ref = refs[3] if has_res else None
        dx_ref, dg_ref = refs[3 + has_res:]
        xv, gv = x_ref[...], g_ref[...]
        _, r = _rms(xv, gv, width)
        dx, dg = _rms_bwd(xv, gv, r, dy_ref[...], width)
        if has_res:
            dx = res_ref[...] + dx
        dx_ref[...] = dx.astype(out_dtype)

        @pl.when(pl.program_id(0) == 0)
        def _():
            dg_ref[...] = jnp.zeros_like(dg_ref)

        dg_ref[...] += dg

    ins = [x, g, dy] + ([res] if has_res else [])
    in_specs = [_col(tm, width, off), _whole((1, width)), _col(tm, width, 0)] + ([_col(tm, width, 0)] if has_res else [])
    return pl.pallas_call(
        body, grid=(S // tm,), in_specs=in_specs, out_specs=[_col(tm, width, 0), _whole((1, width))],
        out_shape=[jax.ShapeDtypeStruct((S, width), out_dtype), jax.ShapeDtypeStruct((1, width), F32)], name=name,
        compiler_params=_params(("arbitrary",)),
    )(*ins)


def _norm_g_grad(x, g, dy, name):
    S, width = x.shape
    tm = _pick(S, 256, 16)

    def body(x_ref, g_ref, dy_ref, dg_ref):
        xv = x_ref[...]
        _, r = _rms(xv, g_ref[...], width)

        @pl.when(pl.program_id(0) == 0)
        def _():
            dg_ref[...] = jnp.zeros_like(dg_ref)

        dg_ref[...] += jnp.sum(dy_ref[...] * xv * r, axis=0, keepdims=True)

    return pl.pallas_call(
        body, grid=(S // tm,), in_specs=[_col(tm, width, 0), _whole((1, width)), _col(tm, width, 0)],
        out_specs=_whole((1, width)), out_shape=jax.ShapeDtypeStruct((1, width), F32), name=name,
        compiler_params=_params(("arbitrary",)),
    )(x, g, dy)


def _loss_rows(y, target, name):
    S, D = y.shape
    tm = _pick(S, 512, 16)

    def body(y_ref, t_ref, g_ref, l_ref):
        err = y_ref[...] - t_ref[...]
        g_ref[...] = err / D

        @pl.when(pl.program_id(0) == 0)
        def _():
            l_ref[...] = jnp.zeros_like(l_ref)

        row = jnp.sum(err * err, axis=-1, keepdims=True) / D
        l_ref[...] += jnp.sum(row, axis=0, keepdims=True)

    return pl.pallas_call(
        body, grid=(S // tm,), in_specs=[_col(tm, D, 0), _col(tm, D, 0)], out_specs=[_col(tm, D, 0), _whole((1, 1))],
        out_shape=[jax.ShapeDtypeStruct((S, D), F32), jax.ShapeDtypeStruct((1, 1), F32)], name=name,
        compiler_params=_params(("arbitrary",)),
    )(y, target)


def _row_index(i, ts):
    return i * ts + lax.broadcasted_iota(jnp.int32, (ts, 1), 0)


def _pool_mixed(scr, pv_ref, g, t, ts):
    w, lo = POOL_WINDOWS[g], g * POOL_GW
    win = scr[HALO:HALO + ts, lo:lo + POOL_GW]
    for k in range(1, w):
        win = win + scr[HALO - k:HALO - k + ts, lo:lo + POOL_GW]
    cnt = jnp.minimum(t + 1, w).astype(F32)
    return win / cnt - pv_ref[:, lo:lo + POOL_GW], cnt


def _halo_before(ts, off):
    nb = ts // HALO
    cb = off // BRANCH_W
    return pl.BlockSpec((HALO, BRANCH_W), lambda i: (jnp.maximum(i * nb - 1, 0), cb))


def _halo_after(ts, off, n_tiles):
    nb = ts // HALO
    cb = off // BRANCH_W
    last = n_tiles * nb - 1
    return pl.BlockSpec((HALO, BRANCH_W), lambda i: (jnp.minimum((i + 1) * nb, last), cb))


def _pool_fwd(proj, offs, pool_w, pool_scale, name):
    S = proj.shape[0]
    ts = _pick(S, 512, HALO)

    def body(pv_ref, halo_ref, pz_ref, w_ref, sc_ref, y_ref, scr):
        i = pl.program_id(0)
        scr[0:HALO, :] = jnp.where(i > 0, halo_ref[...], 0.0)
        scr[HALO:HALO + ts, :] = pv_ref[...]
        t = _row_index(i, ts)
        for g in range(4):
            lo = g * POOL_GW
            mixed, _ = _pool_mixed(scr, pv_ref, g, t, ts)
            out = _dot(mixed, w_ref[g])
            y_ref[:, lo:lo + POOL_GW] = (out * sc_ref[:, lo:lo + POOL_GW] * _silu(pz_ref[:, lo:lo + POOL_GW])).astype(BF16)

    return pl.pallas_call(
        body, grid=(S // ts,),
        in_specs=[_col(ts, BRANCH_W, offs["pv"]), _halo_before(ts, offs["pv"]), _col(ts, BRANCH_W, offs["pz"]),
                  _whole((4, POOL_GW, POOL_GW)), _whole((1, BRANCH_W))],
        out_specs=_col(ts, BRANCH_W, 0), out_shape=jax.ShapeDtypeStruct((S, BRANCH_W), BF16),
        scratch_shapes=[pltpu.VMEM((HALO + ts, BRANCH_W), F32)], name=name, compiler_params=_params(("parallel",)),
    )(proj, proj, proj, pool_w, pool_scale)


def _pool_bwd_a(proj, offs, pool_w, pool_scale, dy, name):
    S = proj.shape[0]
    ts = _pick(S, 512, HALO)

    def body(pv_ref, halo_ref, pz_ref, w_ref, sc_ref, dy_ref, dpz_ref, e_ref, dw_ref, dsc_ref, scr):
        i = pl.program_id(0)
        scr[0:HALO, :] = jnp.where(i > 0, halo_ref[...], 0.0)
        scr[HALO:HALO + ts, :] = pv_ref[...]
        t = _row_index(i, ts)

        @pl.when(i == 0)
        def _():
            dw_ref[...] = jnp.zeros_like(dw_ref)
            dsc_ref[...] = jnp.zeros_like(dsc_ref)

        for g in range(4):
            lo = g * POOL_GW
            sl = slice(lo, lo + POOL_GW)
            mixed, cnt = _pool_mixed(scr, pv_ref, g, t, ts)
            out = _dot(mixed, w_ref[g])
            sz, dsz = _silu_and_grad(pz_ref[:, sl])
            dyv, sc = dy_ref[:, sl], sc_ref[:, sl]
            d_out = dyv * sc * sz
            dsc_ref[:, sl] += jnp.sum(dyv * out * sz, axis=0, keepdims=True)
            dpz_ref[:, sl] = (dyv * out * sc * dsz).astype(BF16)
            dw_ref[g] += _dot_tn(mixed, d_out)
            e_ref[:, sl] = _dot_nt(d_out, w_ref[g]) / cnt

    return pl.pallas_call(
        body, grid=(S // ts,),
        in_specs=[_col(ts, BRANCH_W, offs["pv"]), _halo_before(ts, offs["pv"]), _col(ts, BRANCH_W, offs["pz"]),
                  _whole((4, POOL_GW, POOL_GW)), _whole((1, BRANCH_W)), _col(ts, BRANCH_W, 0)],
        out_specs=[_col(ts, BRANCH_W, 0), _col(ts, BRANCH_W, 0), _whole((4, POOL_GW, POOL_GW)), _whole((1, BRANCH_W))],
        out_shape=[jax.ShapeDtypeStruct((S, BRANCH_W), BF16), jax.ShapeDtypeStruct((S, BRANCH_W), F32),
                   jax.ShapeDtypeStruct((4, POOL_GW, POOL_GW), F32), jax.ShapeDtypeStruct((1, BRANCH_W), F32)],
        scratch_shapes=[pltpu.VMEM((HALO + ts, BRANCH_W), F32)], name=name, compiler_params=_params(("arbitrary",)),
    )(proj, proj, proj, pool_w, pool_scale, dy)


def _pool_bwd_b(e, name):
    S = e.shape[0]
    ts = _pick(S, 512, HALO)
    n_tiles = S // ts

    def body(e_ref, halo_ref, dpv_ref, scr):
        i = pl.program_id(0)
        scr[0:ts, :] = e_ref[...]
        scr[ts:ts + HALO, :] = jnp.where(i < n_tiles - 1, halo_ref[...], 0.0)
        t = _row_index(i, ts)
        for g in range(4):
            w, lo = POOL_WINDOWS[g], g * POOL_GW
            acc = scr[0:ts, lo:lo + POOL_GW]
            ev = acc
            for k in range(1, w):
                acc = acc + scr[k:k + ts, lo:lo + POOL_GW]
            cnt = jnp.minimum(t + 1, w).astype(F32)
            dpv_ref[:, lo:lo + POOL_GW] = (acc - ev * cnt).astype(BF16)

    return pl.pallas_call(
        body, grid=(n_tiles,), in_specs=[_col(ts, BRANCH_W, 0), _halo_after(ts, 0, n_tiles)], out_specs=_col(ts, BRANCH_W, 0),
        out_shape=jax.ShapeDtypeStruct((S, BRANCH_W), BF16), scratch_shapes=[pltpu.VMEM((ts + HALO, BRANCH_W), F32)],
        name=name, compiler_params=_params(("parallel",)),
    )(e, e)


def _conv_fwd(proj, offs, conv_w, name):
    S = proj.shape[0]
    ts = _pick(S, 512, HALO)

    def body(cb_ref, cc_ref, cx_ref, cz_ref, hc_ref, hx_ref, w_ref, y_ref, scr):
        i = pl.program_id(0)
        scr[0:HALO, :] = jnp.where(i > 0, hc_ref[...] * hx_ref[...], 0.0)
        scr[HALO:HALO + ts, :] = cc_ref[...] * cx_ref[...]
        y = w_ref[0:1, :] * scr[HALO - 2:HALO - 2 + ts, :] + w_ref[1:2, :] * scr[HALO - 1:HALO - 1 + ts, :] + w_ref[2:3, :] * scr[HALO:HALO + ts, :]
        y_ref[...] = (cb_ref[...] * y * _silu(cz_ref[...])).astype(BF16)

    return pl.pallas_call(
        body, grid=(S // ts,),
        in_specs=[_col(ts, BRANCH_W, offs["cb"]), _col(ts, BRANCH_W, offs["cc"]), _col(ts, BRANCH_W, offs["cx"]),
                  _col(ts, BRANCH_W, offs["cz"]), _halo_before(ts, offs["cc"]), _halo_before(ts, offs["cx"]), _whole((3, BRANCH_W))],
        out_specs=_col(ts, BRANCH_W, 0), out_shape=jax.ShapeDtypeStruct((S, BRANCH_W), BF16),
        scratch_shapes=[pltpu.VMEM((HALO + ts, BRANCH_W), F32)], name=name, compiler_params=_params(("parallel",)),
    )(proj, proj, proj, proj, proj, proj, conv_w)


def _conv_bwd(proj, offs, conv_w, dy, name):
    S = proj.shape[0]
    ts = _pick(S, 256, HALO)
    n_tiles = S // ts

    def body(cb_ref, cc_ref, cx_ref, cz_ref, hc_ref, hx_ref, w_ref, dy_ref, ady_ref, acb_ref, acz_ref,
             dcb_ref, dcc_ref, dcx_ref, dcz_ref, dw_ref, scr_u, scr_d):
        i = pl.program_id(0)
        u = cc_ref[...] * cx_ref[...]
        scr_u[0:HALO, :] = jnp.where(i > 0, hc_ref[...] * hx_ref[...], 0.0)
        scr_u[HALO:HALO + ts, :] = u
        u1 = scr_u[HALO - 1:HALO - 1 + ts, :]
        u2 = scr_u[HALO - 2:HALO - 2 + ts, :]
        y = w_ref[0:1, :] * u2 + w_ref[1:2, :] * u1 + w_ref[2:3, :] * u
        sz, dsz = _silu_and_grad(cz_ref[...])
        dyv, cb = dy_ref[...], cb_ref[...]
        dcb_ref[...] = (dyv * y * sz).astype(BF16)
        dcz_ref[...] = (dyv * cb * y * dsz).astype(BF16)
        d_y = dyv * cb * sz
        scr_d[0:ts, :] = d_y
        scr_d[ts:ts + HALO, :] = jnp.where(i < n_tiles - 1, ady_ref[...] * acb_ref[...] * _silu(acz_ref[...]), 0.0)
        du = w_ref[2:3, :] * d_y + w_ref[1:2, :] * scr_d[1:1 + ts, :] + w_ref[0:1, :] * scr_d[2:2 + ts, :]
        dcc_ref[...] = (du * cx_ref[...]).astype(BF16)
        dcx_ref[...] = (du * cc_ref[...]).astype(BF16)

        @pl.when(i == 0)
        def _():
            dw_ref[...] = jnp.zeros_like(dw_ref)

        dw_ref[0:1, :] += jnp.sum(d_y * u2, axis=0, keepdims=True)
        dw_ref[1:2, :] += jnp.sum(d_y * u1, axis=0, keepdims=True)
        dw_ref[2:3, :] += jnp.sum(d_y * u, axis=0, keepdims=True)

    row = lambda off: _col(ts, BRANCH_W, off)
    return pl.pallas_call(
        body, grid=(n_tiles,),
        in_specs=[row(offs["cb"]), row(offs["cc"]), row(offs["cx"]), row(offs["cz"]), _halo_before(ts, offs["cc"]),
                  _halo_before(ts, offs["cx"]), _whole((3, BRANCH_W)), row(0), _halo_after(ts, 0, n_tiles),
                  _halo_after(ts, offs["cb"], n_tiles), _halo_after(ts, offs["cz"], n_tiles)],
        out_specs=[row(0), row(0), row(0), row(0), _whole((3, BRANCH_W))],
        out_shape=[jax.ShapeDtypeStruct((S, BRANCH_W), BF16)] * 4 + [jax.ShapeDtypeStruct((3, BRANCH_W), F32)],
        scratch_shapes=[pltpu.VMEM((HALO + ts, BRANCH_W), F32), pltpu.VMEM((ts + HALO, BRANCH_W), F32)], name=name,
        compiler_params=_params(("arbitrary",)),
    )(proj, proj, proj, proj, proj, proj, conv_w, dy, dy, proj, proj)


def _xattn_head(xq, kraw, v, qg, kg):
    qn, rq = _rms(xq, qg, XATTN_HD)
    kn, rk = _rms(kraw, kg, XATTN_HD)
    s = _dot_nt(qn, kn) * (XATTN_HD ** -0.5)
    e = jnp.exp(s - jnp.max(s, axis=-1, keepdims=True))
    p = e / jnp.sum(e, axis=-1, keepdims=True)
    return qn, rq, kn, rk, p, _dot(p, v)


def _xattn_fwd(proj, offs, mem_kv, qg, kg, name):
    S, M = proj.shape[0], mem_kv.shape[0]
    tm = _pick(S, 512, 16)

    def body(xq_ref, xz_ref, kv_ref, qg_ref, kg_ref, y_ref):
        for h in range(XATTN_HEADS):
            sl = slice(h * XATTN_HD, (h + 1) * XATTN_HD)
            vs = slice(BRANCH_W + h * XATTN_HD, BRANCH_W + (h + 1) * XATTN_HD)
            o = _xattn_head(xq_ref[:, sl], kv_ref[:, sl], kv_ref[:, vs], qg_ref[...], kg_ref[...])[-1]
            y_ref[:, sl] = (o * _silu(xz_ref[:, sl])).astype(BF16)

    return pl.pallas_call(
        body, grid=(S // tm,),
        in_specs=[_col(tm, BRANCH_W, offs["xq"]), _col(tm, BRANCH_W, offs["xz"]), _whole((M, 2 * BRANCH_W)),
                  _whole((1, XATTN_HD)), _whole((1, XATTN_HD))],
        out_specs=_col(tm, BRANCH_W, 0), out_shape=jax.ShapeDtypeStruct((S, BRANCH_W), BF16), name=name,
        compiler_params=_params(("parallel",)),
    )(proj, proj, mem_kv, qg, kg)


def _xattn_bwd(proj, offs, mem_kv, qg, kg, dy, name):
    S, M = proj.shape[0], mem_kv.shape[0]
    tm = _pick(S, 256, 16)
    n_tiles = S // tm

    def body(xq_ref, xz_ref, kv_ref, qg_ref, kg_ref, dy_ref, dxq_ref, dxz_ref, dkv_ref, dqg_ref, dkg_ref, dkn_acc):
        i = pl.program_id(0)

        @pl.when(i == 0)
        def _():
            dkv_ref[...] = jnp.zeros_like(dkv_ref)
            dqg_ref[...] = jnp.zeros_like(dqg_ref)
            dkg_ref[...] = jnp.zeros_like(dkg_ref)
            dkn_acc[...] = jnp.zeros_like(dkn_acc)

        qg_v, kg_v = qg_ref[...], kg_ref[...]
        for h in range(XATTN_HEADS):
            sl = slice(h * XATTN_HD, (h + 1) * XATTN_HD)
            vs = slice(BRANCH_W + h * XATTN_HD, BRANCH_W + (h + 1) * XATTN_HD)
            xq, v = xq_ref[:, sl], kv_ref[:, vs]
            qn, rq, kn, rk, p, o = _xattn_head(xq, kv_ref[:, sl], v, qg_v, kg_v)
            sz, dsz = _silu_and_grad(xz_ref[:, sl])
            dyv = dy_ref[:, sl]
            dxz_ref[:, sl] = (dyv * o * dsz).astype(BF16)
            do = dyv * sz
            dkv_ref[:, vs] += _dot_tn(p, do)
            dp = _dot_nt(do, v)
            ds = p * (dp - jnp.sum(dp * p, axis=-1, keepdims=True)) * (XATTN_HD ** -0.5)
            dqn = _dot(ds, kn)
            dkn_acc[:, sl] += _dot_tn(ds, qn)
            dxq, dqg = _rms_bwd(xq, qg_v, rq, dqn, XATTN_HD)
            dxq_ref[:, sl] = dxq.astype(BF16)
            dqg_ref[...] += dqg

        @pl.when(i == n_tiles - 1)
        def _():
            for h in range(XATTN_HEADS):
                sl = slice(h * XATTN_HD, (h + 1) * XATTN_HD)
                kraw = kv_ref[:, sl]
                _, rk = _rms(kraw, kg_v, XATTN_HD)
                dk, dkg = _rms_bwd(kraw, kg_v, rk, dkn_acc[:, sl], XATTN_HD)
                dkv_ref[:, sl] = dk
                dkg_ref[...] += dkg

    return pl.pallas_call(
        body, grid=(n_tiles,),
        in_specs=[_col(tm, BRANCH_W, offs["xq"]), _col(tm, BRANCH_W, offs["xz"]), _whole((M, 2 * BRANCH_W)),
                  _whole((1, XATTN_HD)), _whole((1, XATTN_HD)), _col(tm, BRANCH_W, 0)],
        out_specs=[_col(tm, BRANCH_W, 0), _col(tm, BRANCH_W, 0), _whole((M, 2 * BRANCH_W)), _whole((1, XATTN_HD)), _whole((1, XATTN_HD))],
        out_shape=[jax.ShapeDtypeStruct((S, BRANCH_W), BF16), jax.ShapeDtypeStruct((S, BRANCH_W), BF16),
                   jax.ShapeDtypeStruct((M, 2 * BRANCH_W), F32), jax.ShapeDtypeStruct((1, XATTN_HD), F32), jax.ShapeDtypeStruct((1, XATTN_HD), F32)],
        scratch_shapes=[pltpu.VMEM((M, BRANCH_W), F32)], name=name, compiler_params=_params(("arbitrary",)),
    )(proj, proj, mem_kv, qg, kg, dy)


def _rope(c, tc, ta, tb):
    return c * tc + pltpu.roll(c, 96, 1) * ta + pltpu.roll(c, 32, 1) * tb


def _rope_t(d, tc, ta, tb):
    return d * tc + pltpu.roll(d * ta, 32, 1) + pltpu.roll(d * tb, 96, 1)


def _mla_qk_fwd(qp, kvp, proj, offs, tabs, qg, kg, name):
    S = qp.shape[0]
    tm = _pick(S, 256, 16)
    W = MLA_HEADS * QK_PAD

    def body(q_ref, kv_ref, kr_ref, tc_ref, ta_ref, tb_ref, qg_ref, kg_ref, qo_ref, ko_ref, vo_ref):
        tc, ta, tb = tc_ref[...], ta_ref[...], tb_ref[...]
        qg_v, kg_v = qg_ref[...], kg_ref[...]
        kr = kr_ref[...]
        kr_ss = jnp.sum(kr * kr, axis=-1, keepdims=True)
        for h in range(MLA_HEADS):
            sl = slice(h * QK_PAD, (h + 1) * QK_PAD)
            qn, _ = _rms(q_ref[:, sl], qg_v, QK_HEAD)
            qo_ref[:, sl] = jnp.concatenate([qn[:, :QK_NOPE], _rope(qn[:, QK_NOPE:], tc, ta, tb)], axis=1).astype(BF16)
            kn = kv_ref[:, h * QK_NOPE:(h + 1) * QK_NOPE]
            r = lax.rsqrt((jnp.sum(kn * kn, axis=-1, keepdims=True) + kr_ss) / QK_HEAD + EPS)
            ko_ref[:, sl] = jnp.concatenate([kn * r * kg_v[:, :QK_NOPE], _rope(kr * r * kg_v[:, QK_NOPE:], tc, ta, tb)], axis=1).astype(BF16)
        vo_ref[...] = kv_ref[:, MLA_HEADS * QK_NOPE:].astype(BF16)

    return pl.pallas_call(
        body, grid=(S // tm,),
        in_specs=[_col(tm, W, 0), _col(tm, W, 0), _col(tm, 128, offs["kr"]), _col(tm, 128, 0), _col(tm, 128, 0), _col(tm, 128, 0),
                  _whole((1, QK_PAD)), _whole((1, QK_PAD))],
        out_specs=[_col(tm, W, 0), _col(tm, W, 0), _col(tm, BRANCH_W, 0)],
        out_shape=[jax.ShapeDtypeStruct((S, W), BF16), jax.ShapeDtypeStruct((S, W), BF16), jax.ShapeDtypeStruct((S, BRANCH_W), BF16)],
        name=name, compiler_params=_params(("parallel",)),
    )(qp, kvp, proj, *tabs, qg, kg)


def _mla_qk_bwd(qp, kvp, proj, offs, tabs, qg, kg, dQ, dK, dV, name):
    S = qp.shape[0]
    tm = _pick(S, 256, 16)
    W = MLA_HEADS * QK_PAD

    def body(q_ref, kv_ref, kr_ref, tc_ref, ta_ref, tb_ref, qg_ref, kg_ref, dQ_ref, dK_ref, dV_ref,
             dq_ref, dkv_ref, dkr_ref, dqg_ref, dkg_ref):
        @pl.when(pl.program_id(0) == 0)
        def _():
            dqg_ref[...] = jnp.zeros_like(dqg_ref)
            dkg_ref[...] = jnp.zeros_like(dkg_ref)

        tc, ta, tb = tc_ref[...], ta_ref[...], tb_ref[...]
        qg_v, kg_v = qg_ref[...], kg_ref[...]
        kg_n, kg_r = kg_v[:, :QK_NOPE], kg_v[:, QK_NOPE:]
        kr = kr_ref[...]
        kr_ss = jnp.sum(kr * kr, axis=-1, keepdims=True)
        dkr = jnp.zeros_like(kr)
        dkg_n = jnp.zeros((1, QK_NOPE), F32)
        dkg_r = jnp.zeros((1, QK_NOPE), F32)
        for h in range(MLA_HEADS):
            sl = slice(h * QK_PAD, (h + 1) * QK_PAD)
            qv = q_ref[:, sl]
            _, rq = _rms(qv, qg_v, QK_HEAD)
            dQv = dQ_ref[:, sl]
            dqn = jnp.concatenate([dQv[:, :QK_NOPE], _rope_t(dQv[:, QK_NOPE:], tc, ta, tb)], axis=1)
            dq, dqg = _rms_bwd(qv, qg_v, rq, dqn, QK_HEAD)
            dq_ref[:, sl] = dq.astype(BF16)
            dqg_ref[...] += dqg
            ns = slice(h * QK_NOPE, (h + 1) * QK_NOPE)
            kn = kv_ref[:, ns]
            r = lax.rsqrt((jnp.sum(kn * kn, axis=-1, keepdims=True) + kr_ss) / QK_HEAD + EPS)
            dKv = dK_ref[:, sl]
            dyn = dKv[:, :QK_NOPE]
            dyr = _rope_t(dKv[:, QK_NOPE:], tc, ta, tb)
            proj_s = (jnp.sum(dyn * kg_n * kn, axis=-1, keepdims=True) + jnp.sum(dyr * kg_r * kr, axis=-1, keepdims=True)) / QK_HEAD
            r3 = r * r * r
            dkv_ref[:, ns] = (r * dyn * kg_n - kn * r3 * proj_s).astype(BF16)
            dkr = dkr + (r * dyr * kg_r - kr * r3 * proj_s)
            dkg_n = dkg_n + jnp.sum(dyn * kn * r, axis=0, keepdims=True)
            dkg_r = dkg_r + jnp.sum(dyr * kr * r, axis=0, keepdims=True)
        dkg_ref[...] += jnp.concatenate([dkg_n, dkg_r], axis=1)
        dkr_ref[...] = dkr.astype(BF16)
        dkv_ref[:, MLA_HEADS * QK_NOPE:] = dV_ref[...].astype(BF16)

    return pl.pallas_call(
        body, grid=(S // tm,),
        in_specs=[_col(tm, W, 0), _col(tm, W, 0), _col(tm, 128, offs["kr"]), _col(tm, 128, 0), _col(tm, 128, 0), _col(tm, 128, 0),
                  _whole((1, QK_PAD)), _whole((1, QK_PAD)), _col(tm, W, 0), _col(tm, W, 0), _col(tm, BRANCH_W, 0)],
        out_specs=[_col(tm, W, 0), _col(tm, W, 0), _col(tm, 128, 0), _whole((1, QK_PAD)), _whole((1, QK_PAD))],
        out_shape=[jax.ShapeDtypeStruct((S, W), BF16), jax.ShapeDtypeStruct((S, W), BF16), jax.ShapeDtypeStruct((S, 128), BF16),
                   jax.ShapeDtypeStruct((1, QK_PAD), F32), jax.ShapeDtypeStruct((1, QK_PAD), F32)],
        name=name, compiler_params=_params(("arbitrary",)),
    )(qp, kvp, proj, *tabs, qg, kg, dQ, dK, dV)


def _causal_scores(q, k, qi, ki, tq, tk):
    s = _dot_nt(q, k) * SCORE_SCALE_LOG2
    rows = qi * tq + lax.broadcasted_iota(jnp.int32, (tq, tk), 0)
    cols = ki * tk + lax.broadcasted_iota(jnp.int32, (tq, tk), 1)
    return s, cols <= rows


def _flash_fwd(Q, K, V, proj, offs, name, comm=None):
    S = Q.shape[0]
    t = _pick(S, 1024, 128)
    chunks = t // 128
    mz_cb = offs["mz"] // V_HEAD

    def fold(x, op):
        r = x[:, 0:128]
        for ch in range(1, chunks):
            r = op(r, x[:, ch * 128:(ch + 1) * 128])
        return r

    def body(q_ref, k_ref, v_ref, mz_ref, o_ref, y_ref, lse_ref, mp_scr, lp_scr, acc_scr):
        qi = pl.program_id(1)
        q = q_ref[...]

        def raw_scores(j, masked):
            rows_k = pl.ds(pl.multiple_of(j * t, t), t)
            s = _dot_nt(q, k_ref[rows_k, :])
            if masked:
                rows = lax.broadcasted_iota(jnp.int32, (t, t), 0)
                cols = lax.broadcasted_iota(jnp.int32, (t, t), 1)
                s = jnp.where(cols <= rows, s, NEG)
            return s, rows_k

        mp_scr[...] = jnp.full_like(mp_scr, NEG)

        def max_sweep(j, carry):
            s, _ = raw_scores(j, False)
            mp_scr[...] = jnp.maximum(mp_scr[...], fold(s, jnp.maximum))
            return carry

        lax.fori_loop(0, qi, max_sweep, 0)
        s_diag, _ = raw_scores(qi, True)
        m = jnp.max(jnp.maximum(mp_scr[...], fold(s_diag, jnp.maximum)), axis=-1, keepdims=True) * SCORE_SCALE_LOG2

        lp_scr[...] = jnp.zeros_like(lp_scr)
        acc_scr[...] = jnp.zeros_like(acc_scr)

        def accumulate(j, masked):
            s, rows_k = raw_scores(j, masked)
            p = jnp.exp2(s * SCORE_SCALE_LOG2 - m)
            lp_scr[...] += fold(p, jnp.add)
            acc_scr[...] += _dot(p, v_ref[rows_k, :])

        def sum_sweep(j, carry):
            accumulate(j, False)
            return carry

        lax.fori_loop(0, qi, sum_sweep, 0)
        accumulate(qi, True)
        l = jnp.sum(lp_scr[...], axis=-1, keepdims=True)
        o = acc_scr[...] / l
        o_ref[...] = o
        y_ref[...] = (o * _silu(mz_ref[...])).astype(BF16)
        lse_ref[0] = m + jnp.log2(l)

    in_specs = [pl.BlockSpec((t, QK_PAD), lambda h, i: (i, h)), pl.BlockSpec((S, QK_PAD), lambda h, i: (0, h)),
                pl.BlockSpec((S, V_HEAD), lambda h, i: (0, h)), pl.BlockSpec((t, V_HEAD), lambda h, i: (i, mz_cb + h))]
    out_specs = [pl.BlockSpec((t, V_HEAD), lambda h, i: (i, h)), pl.BlockSpec((t, V_HEAD), lambda h, i: (i, h)),
                 pl.BlockSpec((1, t, 1), lambda h, i: (h, i, 0))]
    out_shape = [jax.ShapeDtypeStruct((S, BRANCH_W), F32), jax.ShapeDtypeStruct((S, BRANCH_W), BF16),
                 jax.ShapeDtypeStruct((MLA_HEADS, S, 1), F32)]
    scratch = [pltpu.VMEM((t, 128), F32), pltpu.VMEM((t, 128), F32), pltpu.VMEM((t, V_HEAD), F32)]
    ins = [Q, K, V, proj]
    if comm is not None:
        return _hosted_call(comm, body, ins, in_specs, out_specs, out_shape, scratch, (MLA_HEADS, S // t), name)
    res = pl.pallas_call(
        body, grid=(MLA_HEADS, S // t), in_specs=in_specs, out_specs=out_specs, out_shape=out_shape, scratch_shapes=scratch,
        name=name, compiler_params=_params(("parallel", "parallel")),
    )(*ins)
    return list(res), []


def _mla_gate_bwd(o, proj, offs, dy, name):
    S = o.shape[0]
    tm = _pick(S, 512, 16)

    def body(o_ref, mz_ref, dy_ref, do_ref, dmz_ref, dl_ref):
        sz, dsz = _silu_and_grad(mz_ref[...])
        dyv, ov = dy_ref[...], o_ref[...]
        do = (dyv * sz).astype(BF16)
        do_ref[...] = do
        dmz_ref[...] = (dyv * ov * dsz).astype(BF16)
        prod = do.astype(F32) * ov
        for h in range(MLA_HEADS):
            dl_ref[h] = jnp.sum(prod[:, h * V_HEAD:(h + 1) * V_HEAD], axis=-1, keepdims=True)

    return pl.pallas_call(
        body, grid=(S // tm,), in_specs=[_col(tm, BRANCH_W, 0), _col(tm, BRANCH_W, offs["mz"]), _col(tm, BRANCH_W, 0)],
        out_specs=[_col(tm, BRANCH_W, 0), _col(tm, BRANCH_W, 0), pl.BlockSpec((MLA_HEADS, tm, 1), lambda i: (0, i, 0))],
        out_shape=[jax.ShapeDtypeStruct((S, BRANCH_W), BF16), jax.ShapeDtypeStruct((S, BRANCH_W), BF16),
                   jax.ShapeDtypeStruct((MLA_HEADS, S, 1), F32)], name=name,
        compiler_params=_params(("parallel",)),
    )(o, proj, dy)


def _flash_bwd(Q, K, V, lse, delta, do, name, comm=None):
    S = Q.shape[0]
    tk = _pick(S, 1024, 128)
    nq = S // tk
    once = pl.Buffered(1)

    def body(q_ref, k_ref, v_ref, lse_ref, dl_ref, do_ref, dq_ref, dk_ref, dv_ref, dk_acc, dv_acc):
        ki = pl.program_id(1)

        @pl.when(ki == 0)
        def _():
            dq_ref[...] = jnp.zeros_like(dq_ref)

        dk_acc[...] = jnp.zeros_like(dk_acc)
        dv_acc[...] = jnp.zeros_like(dv_acc)
        k, v = k_ref[...], v_ref[...]

        def block(i, masked):
            rows = pl.ds(pl.multiple_of(i * tk, tk), tk)
            q, dov = q_ref[rows, :], do_ref[rows, :]
            s, mask = _causal_scores(q, k, i, ki, tk, tk)
            if masked:
                s = jnp.where(mask, s, NEG)
            p = jnp.exp2(s - lse_ref[0, rows, :])
            dv_acc[...] += _dot_tn(p, dov)
            dp = _dot_nt(dov, v)
            ds = p * (dp - dl_ref[0, rows, :]) * (QK_HEAD ** -0.5)
            dk_acc[...] += _dot_tn(ds, q)
            dq_ref[rows, :] += _dot(ds, k)

        def below_diagonal(i, carry):
            block(i, False)
            return carry

        block(ki, True)
        lax.fori_loop(ki + 1, nq, below_diagonal, 0)
        dk_ref[...] = dk_acc[...]
        dv_ref[...] = dv_acc[...]

    in_specs = [pl.BlockSpec((S, QK_PAD), lambda h, j: (0, h), pipeline_mode=once), pl.BlockSpec((tk, QK_PAD), lambda h, j: (j, h)),
                pl.BlockSpec((tk, V_HEAD), lambda h, j: (j, h)), pl.BlockSpec((1, S, 1), lambda h, j: (h, 0, 0), pipeline_mode=once),
                pl.BlockSpec((1, S, 1), lambda h, j: (h, 0, 0), pipeline_mode=once),
                pl.BlockSpec((S, V_HEAD), lambda h, j: (0, h), pipeline_mode=once)]
    out_specs = [pl.BlockSpec((S, QK_PAD), lambda h, j: (0, h)), pl.BlockSpec((tk, QK_PAD), lambda h, j: (j, h)),
                 pl.BlockSpec((tk, V_HEAD), lambda h, j: (j, h))]
    out_shape = [jax.ShapeDtypeStruct((S, MLA_HEADS * QK_PAD), F32), jax.ShapeDtypeStruct((S, MLA_HEADS * QK_PAD), F32),
                 jax.ShapeDtypeStruct((S, BRANCH_W), F32)]
    scratch = [pltpu.VMEM((tk, QK_PAD), F32), pltpu.VMEM((tk, V_HEAD), F32)]
    ins = [Q, K, V, lse, delta, do]
    if comm is not None:
        return _hosted_call(comm, body, ins, in_specs, out_specs, out_shape, scratch, (MLA_HEADS, nq), name)
    res = pl.pallas_call(
        body, grid=(MLA_HEADS, nq), in_specs=in_specs, out_specs=out_specs, out_shape=out_shape, scratch_shapes=scratch, name=name,
        compiler_params=_params(("parallel", "arbitrary")),
    )(*ins)
    return list(res), []


def _merge_specs(S, D, offs, tm, tn, layer):
    g_cb = offs["g"] // tn
    nj = D // tn
    y_specs = [pl.BlockSpec((tm, BRANCH_W), lambda j, i: (i, 0)) for _ in range(4)]
    w_spec = pl.BlockSpec((None, 4, BRANCH_W, tn), lambda j, i: (layer, 0, 0, j))
    g_specs = [pl.BlockSpec((tm, tn), functools.partial(lambda j, i, b: (i, g_cb + b * nj + j), b=b)) for b in range(4)]
    bias_spec = pl.BlockSpec((4, tn), lambda j, i: (0, j))
    return y_specs, w_spec, g_specs, bias_spec


def _merge_fwd(ys, wb, layer, proj, offs, gate_b, name):
    S, D = ys[0].shape[0], wb.shape[3]
    tm, tn = _pick(S, 512, 16), _pick(D, 512, 128)
    y_specs, w_spec, g_specs, bias_spec = _merge_specs(S, D, offs, tm, tn, layer)

    def body(y0, y1, y2, y3, w_ref, g0, g1, g2, g3, b_ref, o_ref, t_ref):
        acc = None
        for b, (y_ref, g_ref) in enumerate(zip((y0, y1, y2, y3), (g0, g1, g2, g3))):
            gate = 1.0 / (1.0 + jnp.exp(-(g_ref[...] + b_ref[b:b + 1, :])))
            t = _dot(y_ref[...], w_ref[b])
            t_ref[b] = t.astype(BF16)
            term = gate * t
            acc = term if acc is None else acc + term
        o_ref[...] = acc.astype(BF16)

    return pl.pallas_call(
        body, grid=(D // tn, S // tm), in_specs=y_specs + [w_spec] + g_specs + [bias_spec],
        out_specs=[pl.BlockSpec((tm, tn), lambda j, i: (i, j)), pl.BlockSpec((4, tm, tn), lambda j, i: (0, i, j))],
        out_shape=[jax.ShapeDtypeStruct((S, D), BF16), jax.ShapeDtypeStruct((4, S, D), BF16)], name=name,
        compiler_params=_params(("parallel", "parallel")),
    )(*ys, wb, proj, proj, proj, proj, gate_b)


def _merge_bwd(t, proj, offs, gate_b, dmerged, name):
    _, S, D = t.shape
    tm, tn = _pick(S, 512, 16), _pick(D, 512, 128)
    nj = D // tn
    _, _, g_specs, bias_spec = _merge_specs(S, D, offs, tm, tn, 0)

    def body(t_ref, g0, g1, g2, g3, b_ref, dm_ref, dt_ref, dg0, dg1, dg2, dg3, db_ref):
        @pl.when(pl.program_id(1) == 0)
        def _():
            db_ref[...] = jnp.zeros_like(db_ref)

        dm = dm_ref[...]
        for b, (g_ref, dg_ref) in enumerate(zip((g0, g1, g2, g3), (dg0, dg1, dg2, dg3))):
            gate = 1.0 / (1.0 + jnp.exp(-(g_ref[...] + b_ref[b:b + 1, :])))
            dt_ref[b] = (dm * gate).astype(BF16)
            dgp = dm * t_ref[b].astype(F32) * gate * (1.0 - gate)
            dg_ref[...] = dgp.astype(BF16)
            db_ref[b:b + 1, :] += jnp.sum(dgp, axis=0, keepdims=True)

    stacked = pl.BlockSpec((4, tm, tn), lambda j, i: (0, i, j))
    tile = pl.BlockSpec((tm, tn), lambda j, i: (i, j))
    return pl.pallas_call(
        body, grid=(nj, S // tm), in_specs=[stacked] + g_specs + [bias_spec, tile],
        out_specs=[stacked] + [tile] * 4 + [pl.BlockSpec((4, tn), lambda j, i: (0, j))],
        out_shape=[jax.ShapeDtypeStruct((4, S, D), BF16)] + [jax.ShapeDtypeStruct((S, D), BF16)] * 4 + [jax.ShapeDtypeStruct((4, D), F32)],
        name=name, compiler_params=_params(("parallel", "arbitrary")),
    )(t, proj, proj, proj, proj, gate_b, dmerged)


HBM_SPEC = pl.BlockSpec(memory_space=pl.ANY)


def _mesh_place():
    x, y, c = lax.axis_index("x"), lax.axis_index("y"), lax.axis_index("c")
    return x, y, c, [(1 - x, y), (x, 1 - y), (1 - x, 1 - y)]


def _remote(src, dst, send_sem, recv_sem, to):
    return pltpu.make_async_remote_copy(src_ref=src, dst_ref=dst, send_sem=send_sem, recv_sem=recv_sem, device_id=to,
                                        device_id_type=pl.DeviceIdType.MESH)


class _GatherPlan:
    def __init__(self, arrs):
        self.arrs, self.n = list(arrs), len(arrs)
        self.out_shape = [jax.ShapeDtypeStruct((N_DEV,) + a.shape, a.dtype) for a in arrs]
        self.scratch = [pltpu.SemaphoreType.DMA((self.n, 7)), pltpu.SemaphoreType.DMA((self.n, 7)), pltpu.SemaphoreType.DMA((self.n,))]

    def _copies(self, ins, dsts, sems):
        send_sems, recv_sems, local_sems = sems
        x, y, c, chips = _mesh_place()
        slot = lambda px, py, pc: 4 * px + 2 * py + pc
        me, sibling = slot(x, y, c), (x, y, 1 - c)

        def copy(a, k, src, block, to):
            return lambda: _remote(src(), dsts[a].at[block], send_sems.at[a, k], recv_sems.at[a, k], to)

        rng = range(self.n)
        shard = lambda a: (lambda: ins[a])
        local = [(lambda a=a: pltpu.make_async_copy(ins[a], dsts[a].at[me], local_sems.at[a])) for a in rng]
        own = [copy(a, 0, shard(a), me, sibling) for a in rng]
        own += [copy(a, 1 + j, shard(a), me, (*chip, c)) for a in rng for j, chip in enumerate(chips)]
        landed = [copy(a, 1 + j, shard(a), slot(*chip, c), sibling) for a in rng for j, chip in enumerate(chips)]
        passed = [copy(a, 4 + j, (lambda a=a, chip=chip: dsts[a].at[slot(*chip, c)]), slot(*chip, c), sibling)
                  for a in rng for j, chip in enumerate(chips)]
        from_sibling = [copy(a, 0, shard(a), slot(x, y, 1 - c), sibling) for a in rng]
        from_sibling += [copy(a, 4 + j, shard(a), slot(*chip, 1 - c), sibling) for a in rng for j, chip in enumerate(chips)]
        return local, own, landed, passed, from_sibling

    def start(self, ins, dsts, sems):
        local, own, _, _, _ = self._copies(ins, dsts, sems)
        for make in local + own:
            make().start()

    def relay(self, ins, dsts, sems):
        _, _, landed, passed, _ = self._copies(ins, dsts, sems)
        for arrived, onward in zip(landed, passed):
            arrived().wait_recv()
            onward().start()

    def finish(self, ins, dsts, sems):
        local, own, _, passed, from_sibling = self._copies(ins, dsts, sems)
        for make in from_sibling:
            make().wait_recv()
        for make in own + passed:
            make().wait_send()
        for make in local:
            make().wait()


class _ChipExchangePlan:
    def __init__(self, arrs):
        self.arrs, self.n = list(arrs), len(arrs)
        self.out_shape = [jax.ShapeDtypeStruct(a.shape, a.dtype) for a in arrs]
        self.scratch = [pltpu.SemaphoreType.DMA((self.n, 3)), pltpu.SemaphoreType.DMA((self.n, 3)), pltpu.SemaphoreType.DMA((self.n,))]

    def _copies(self, ins, dsts, sems):
        send_sems, recv_sems, local_sems = sems
        x, y, c, chips = _mesh_place()
        mine = 2 * x + y
        local = [pltpu.make_async_copy(ins[a].at[mine], dsts[a].at[mine], local_sems.at[a]) for a in range(self.n)]
        remote = [_remote(ins[a].at[2 * px + py], dsts[a].at[mine], send_sems.at[a, j], recv_sems.at[a, j], (px, py, c))
                  for a in range(self.n) for j, (px, py) in enumerate(chips)]
        return local, remote

    def start(self, ins, dsts, sems):
        local, remote = self._copies(ins, dsts, sems)
        for cp in local + remote:
            cp.start()

    def relay(self, ins, dsts, sems):
        pass

    def finish(self, ins, dsts, sems):
        local, remote = self._copies(ins, dsts, sems)
        for cp in remote + local:
            cp.wait()


def _run_plan(plan, name):
    n = plan.n

    def body(*refs):
        ins, dsts, sems = refs[:n], refs[n:2 * n], refs[2 * n:]
        plan.start(ins, dsts, sems)
        plan.relay(ins, dsts, sems)
        plan.finish(ins, dsts, sems)

    res = pl.pallas_call(body, in_specs=[HBM_SPEC] * n, out_specs=[HBM_SPEC] * n, out_shape=plan.out_shape,
                         scratch_shapes=plan.scratch, name=name)(*plan.arrs)
    return list(res)


def _sibling_swap(arrs, name):
    n = len(arrs)

    def body(*refs):
        ins, dsts = refs[:n], refs[n:2 * n]
        send_sems, recv_sems = refs[2 * n:]
        x, y, c, _ = _mesh_place()
        cps = [_remote(ins[a].at[q, 1 - c], dsts[a].at[q], send_sems.at[a, q], recv_sems.at[a, q], (x, y, 1 - c))
               for a in range(n) for q in range(4)]
        for cp in cps:
            cp.start()
        for cp in cps:
            cp.wait()

    res = pl.pallas_call(
        body, in_specs=[HBM_SPEC] * n, out_specs=[HBM_SPEC] * n,
        out_shape=[jax.ShapeDtypeStruct((4,) + a.shape[2:], a.dtype) for a in arrs],
        scratch_shapes=[pltpu.SemaphoreType.DMA((n, 4)), pltpu.SemaphoreType.DMA((n, 4))], name=name,
    )(*arrs)
    return list(res)


def _pair_sum(staged, theirs, core, name):
    _, _, R, C = staged.shape
    tr = _pick(R, max(16, ((1 << 19) // C) // 16 * 16), 16)

    def body(c_ref, a_ref, b_ref, o_ref):
        o_ref[...] = (a_ref[...].astype(F32) + b_ref[...].astype(F32)).astype(BF16)

    blk = pl.BlockSpec((None, tr, C), lambda q, i, c: (q, i, 0))
    grid_spec = pltpu.PrefetchScalarGridSpec(
        num_scalar_prefetch=1, grid=(4, R // tr),
        in_specs=[pl.BlockSpec((None, None, tr, C), lambda q, i, c: (q, c[0], i, 0)), blk], out_specs=blk)
    return pl.pallas_call(
        body, grid_spec=grid_spec, out_shape=jax.ShapeDtypeStruct((4, R, C), BF16), name=name,
        compiler_params=_params(("parallel", "parallel")),
    )(core, staged, theirs)


def _adamw(parts, w, m, v, name):
    R, C = w.shape
    n_parts = parts.shape[0]
    tr = _pick(R, max(16, ((1 << 18) // C) // 16 * 16), 16)

    def body(p_ref, w_ref, m_ref, v_ref, g_ref, d_ref, nm_ref, nv_ref):
        g = p_ref[0].astype(F32)
        for s in range(1, n_parts):
            g = g + p_ref[s].astype(F32)
        mn = ADAM_B1 * m_ref[...] + (1.0 - ADAM_B1) * g
        vn = ADAM_B2 * v_ref[...] + (1.0 - ADAM_B2) * (g * g)
        m_hat = mn / (1.0 - ADAM_B1 ** ADAM_STEP)
        v_hat = vn / (1.0 - ADAM_B2 ** ADAM_STEP)
        g_ref[...] = g
        d_ref[...] = -ADAM_LR * (m_hat / (jnp.sqrt(v_hat) + ADAM_EPS) + ADAM_WD * w_ref[...])
        nm_ref[...] = mn
        nv_ref[...] = vn

    blk = pl.BlockSpec((tr, C), lambda i: (i, 0))
    return pl.pallas_call(
        body, grid=(R // tr,), in_specs=[pl.BlockSpec((n_parts, tr, C), lambda i: (0, i, 0)), blk, blk, blk],
        out_specs=[blk] * 4, out_shape=[jax.ShapeDtypeStruct((R, C), F32)] * 4, name=name,
        compiler_params=_params(("parallel",)),
    )(parts, w, m, v)


def _offsets(L, D):
    o = {"pv": 0, "pz": BRANCH_W, "cq": 2 * BRANCH_W, "ckv": 2 * BRANCH_W + L, "mz": 2 * BRANCH_W + 2 * L}
    for prev, nxt in (("mz", "cb"), ("cb", "cc"), ("cc", "cx"), ("cx", "cz"), ("cz", "xq"), ("xq", "xz"), ("xz", "g")):
        o[nxt] = o[prev] + BRANCH_W
    o["kr"] = o["g"] + 4 * D
    o["n"] = o["kr"] + 128
    return o


def _unshard(gathered, axis):
    moved = jnp.moveaxis(gathered, 0, axis)
    shape = list(moved.shape)
    shape[axis:axis + 2] = [shape[axis] * shape[axis + 1]]
    return moved.reshape(shape)


def _reshard(full, axis):
    shape = list(full.shape)
    shape[axis:axis + 1] = [N_DEV, shape[axis] // N_DEV]
    return jnp.moveaxis(full.reshape(shape), axis, 0)


def _pad_heads(g, width):
    return jnp.pad(g, ((0, 0), (0, width - g.shape[1])))


def kernel(x, mem, positions, norm_g, w_in, gate_b, pool_w, pool_scale, q_a_norm_g, kv_a_norm_g, w_uq, w_ukv, mla_q_norm_g, mla_k_norm_g, conv_w, mem_norm_g, w_mem_kv, xattn_q_norm_g, xattn_k_norm_g, w_branch, w_out, loss_target, m_norm_g, m_w_in, m_gate_b, m_pool_w, m_pool_scale, m_q_a_norm_g, m_kv_a_norm_g, m_w_uq, m_w_ukv, m_mla_q_norm_g, m_mla_k_norm_g, m_conv_w, m_mem_norm_g, m_w_mem_kv, m_xattn_q_norm_g, m_xattn_k_norm_g, m_w_branch, m_w_out, v_norm_g, v_w_in, v_gate_b, v_pool_w, v_pool_scale, v_q_a_norm_g, v_kv_a_norm_g, v_w_uq, v_w_ukv, v_mla_q_norm_g, v_mla_k_norm_g, v_conv_w, v_mem_norm_g, v_w_mem_kv, v_xattn_q_norm_g, v_xattn_k_norm_g, v_w_branch, v_w_out):
    depth = norm_g.shape[0]
    S, D = x.shape[1], x.shape[2]
    L = w_uq.shape[1]
    offs = _offsets(L, D)
    seg_a = 2 * BRANCH_W + 2 * L
    xs = x.reshape(S, D)
    mems = mem.reshape(mem.shape[1], D)

    sharded = dict(w_in=w_in, pool_w=pool_w, w_uq=w_uq, w_ukv=w_ukv, conv_w=conv_w, w_mem_kv=w_mem_kv, w_branch=w_branch, w_out=w_out)
    shard_axis = dict(w_in=2, pool_w=2, w_uq=2, w_ukv=2, conv_w=2, w_mem_kv=1, w_branch=3, w_out=1)
    def permuted_w_in(g):
        f = _unshard(g, 1)
        return jnp.concatenate([f[:, :seg_a], f[:, seg_a + QK_ROPE:], f[:, seg_a:seg_a + QK_ROPE], jnp.zeros((D, 128 - QK_ROPE), BF16)], axis=-1)

    w_in_local = [w_in[l].astype(BF16) for l in range(depth)]
    w_in_p = [permuted_w_in(_run_plan(_GatherPlan([w_in_local[0]]), "gather_w_in0")[0])] + [None] * (depth - 1)
    rest_keys = [k for k in shard_axis if k != "w_in"]
    rest_plan = _GatherPlan([sharded[k] if k == "conv_w" else sharded[k].astype(BF16) for k in rest_keys])
    later_w_in_plan = _GatherPlan(w_in_local[1:]) if depth > 1 else None

    inv = ROPE_THETA ** (-jnp.arange(0, QK_ROPE, 2, dtype=F32) / QK_ROPE)
    ang = positions.reshape(S).astype(F32)[:, None] * inv
    cos, sin, z = jnp.cos(ang), jnp.sin(ang), jnp.zeros_like(ang)
    tabs = (jnp.concatenate([cos, cos, z, z], axis=1), jnp.concatenate([-sin, z, z, z], axis=1), jnp.concatenate([z, sin, z, z], axis=1))

    row = lambda a, l: a[l][None, :]
    saved = []
    cur = xs
    for l in range(depth):
        h, hT = _rms_rows(cur, D, 0, row(norm_g, l), f"norm_fwd{l}", with_transpose=True)
        proj_tiles = dict(tm=512, tn=3712, tk=2048, n_outer=True)
        if l == 0:
            proj, carried = _matmul(h, w_in_p[0], "nn", F32, "proj_fwd0", comm=rest_plan, **proj_tiles)
            gathered = {k: _unshard(g, shard_axis[k]) for k, g in zip(rest_keys, carried)}
            conv_full = gathered["conv_w"]
            pool_w_f = gathered["pool_w"]
            w_uq_f = gathered["w_uq"].reshape(depth, L, MLA_HEADS, QK_HEAD)
            w_uq_p = jnp.pad(w_uq_f, ((0, 0), (0, 0), (0, 0), (0, QK_PAD - QK_HEAD))).reshape(depth, L, MLA_HEADS * QK_PAD)
            w_ukv_f = gathered["w_ukv"].reshape(depth, L, MLA_HEADS, QK_NOPE + V_HEAD)
            w_ukv_p = jnp.concatenate([w_ukv_f[..., :QK_NOPE].reshape(depth, L, -1), w_ukv_f[..., QK_NOPE:].reshape(depth, L, -1)], axis=-1)
            w_mem_f = gathered["w_mem_kv"]
            w_br_f = gathered["w_branch"]
            w_out_f = gathered["w_out"]
        else:
            proj = _matmul(h, w_in_p[l], "nn", F32, f"proj_fwd{l}", **proj_tiles)
        y_pool = _pool_fwd(proj, offs, pool_w_f[l], row(pool_scale, l), f"pool_fwd{l}")
        cqn = _rms_rows(proj, L, offs["cq"], row(q_a_norm_g, l), f"cq_norm{l}")
        ckvn = _rms_rows(proj, L, offs["ckv"], row(kv_a_norm_g, l), f"ckv_norm{l}")
        qp = _matmul(cqn, w_uq_p[l], "nn", F32, f"uq_fwd{l}")
        kvp = _matmul(ckvn, w_ukv_p[l], "nn", F32, f"ukv_fwd{l}")
        qg = _pad_heads(row(mla_q_norm_g, l), QK_PAD)
        kg = _pad_heads(row(mla_k_norm_g, l), QK_PAD)
        Q, K, V = _mla_qk_fwd(qp, kvp, proj, offs, tabs, qg, kg, f"mla_qk_fwd{l}")
        (o, y_mla, lse), carried = _flash_fwd(Q, K, V, proj, offs, f"flash_fwd{l}", comm=later_w_in_plan if l == 0 else None)
        if l == 0 and later_w_in_plan is not None:
            w_in_p[1:] = [permuted_w_in(g) for g in carried]
        y_conv = _conv_fwd(proj, offs, conv_full[l], f"conv_fwd{l}")
        memn = _rms_rows(mems, D, 0, row(mem_norm_g, l), f"mem_norm{l}")
        mem_kv = _matmul(memn, w_mem_f, "nn", F32, f"mem_kv{l}", b_idx=(l,))
        y_mem = _xattn_fwd(proj, offs, mem_kv, row(xattn_q_norm_g, l), row(xattn_k_norm_g, l), f"xattn_fwd{l}")
        ys = (y_pool, y_mla, y_conv, y_mem)
        gb = gate_b[l].reshape(4, D)
        merged, tproj = _merge_fwd(ys, w_br_f, l, proj, offs, gb, f"merge_fwd{l}")
        nxt = _matmul(merged, w_out_f, "nn", F32, f"out_fwd{l}", res=cur, b_idx=(l,))
        saved.append(dict(x=cur, hT=hT, proj=proj, cqn=cqn, ckvn=ckvn, qp=qp, kvp=kvp, qg=qg, kg=kg, Q=Q, K=K, V=V, o=o, lse=lse,
                          memn=memn, mem_kv=mem_kv, ys=ys, gb=gb, merged=merged, tproj=tproj))
        cur = nxt

    gx, loss_acc = _loss_rows(cur, loss_target.reshape(S, D), "loss")
    loss = lax.psum(0.5 * loss_acc[0, 0], MESH_AXES)

    grads = {k: [None] * depth for k in ("norm_g", "w_in", "gate_b", "pool_w", "pool_scale", "q_a_norm_g", "kv_a_norm_g", "w_uq", "w_ukv",
                                         "mla_q_norm_g", "mla_k_norm_g", "conv_w", "mem_norm_g", "w_mem_kv", "xattn_q_norm_g",
                                         "xattn_k_norm_g", "w_branch", "w_out")}
    names_sh = list(shard_axis)
    core = lax.axis_index("c").astype(jnp.int32).reshape(1)
    received = [None] * depth
    pending = None

    def pair_sums(one_layer_grads, names, tag):
        staged = [_reshard(g, shard_axis[k] - 1).reshape(4, 2, -1, sharded[k].shape[-1]).astype(BF16) for g, k in zip(one_layer_grads, names)]
        given = _sibling_swap(staged, f"pair_swap_{tag}")
        return [_pair_sum(st, theirs, core, f"pair_sum_{k}_{tag}") for k, st, theirs in zip(names, staged, given)]

    for l in reversed(range(depth)):
        sv = saved[l]
        proj, ys = sv["proj"], sv["ys"]
        dmerged = _matmul(gx, w_out_f, "nt", F32, f"out_bwd_x{l}", b_idx=(l,))
        grads["w_out"][l] = _matmul(sv["merged"], gx, "tn", BF16, f"out_bwd_w{l}")
        dt, dgp0, dgp1, dgp2, dgp3, dgb = _merge_bwd(sv["tproj"], proj, offs, sv["gb"], dmerged, f"merge_bwd{l}")
        grads["gate_b"][l] = dgb.reshape(4 * D)
        dys, dwb = [], []
        for b in range(4):
            dys.append(_matmul(dt, w_br_f, "nt", F32, f"branch_bwd_x{l}_{b}", a_idx=(b,), b_idx=(l, b)))
            dwb.append(_matmul(ys[b], dt, "tn", BF16, f"branch_bwd_w{l}_{b}", b_idx=(b,)))
        grads["w_branch"][l] = jnp.stack(dwb)
        d_pz, e, d_pw, d_ps = _pool_bwd_a(proj, offs, pool_w_f[l], row(pool_scale, l), dys[0], f"pool_bwd_a{l}")
        d_pv = _pool_bwd_b(e, f"pool_bwd_b{l}")
        grads["pool_w"][l], grads["pool_scale"][l] = d_pw, d_ps[0]
        do, d_mz, delta = _mla_gate_bwd(sv["o"], proj, offs, dys[1], f"mla_gate_bwd{l}")
        (dQ, dK, dV), carried = _flash_bwd(sv["Q"], sv["K"], sv["V"], sv["lse"], delta, do, f"flash_bwd{l}",
                                           comm=None if pending is None else pending[1])
        if pending is not None:
            received[pending[0]][0], pending = carried[0], None
        dqp, dkvp, d_kr, dqg, dkg = _mla_qk_bwd(sv["qp"], sv["kvp"], proj, offs, tabs, sv["qg"], sv["kg"], dQ, dK, dV, f"mla_qk_bwd{l}")
        grads["mla_q_norm_g"][l], grads["mla_k_norm_g"][l] = dqg[0, :QK_HEAD], dkg[0, :QK_HEAD]
        dcqn = _matmul(dqp, w_uq_p[l], "nt", F32, f"uq_bwd_x{l}")
        d_wuq = _matmul(sv["cqn"], dqp, "tn", BF16, f"uq_bwd_w{l}")
        dckvn = _matmul(dkvp, w_ukv_p[l], "nt", F32, f"ukv_bwd_x{l}")
        d_wukv = _matmul(sv["ckvn"], dkvp, "tn", BF16, f"ukv_bwd_w{l}")
        grads["w_uq"][l] = d_wuq.reshape(L, MLA_HEADS, QK_PAD)[..., :QK_HEAD].reshape(L, MLA_HEADS * QK_HEAD)
        grads["w_ukv"][l] = jnp.concatenate([d_wukv[:, :MLA_HEADS * QK_NOPE].reshape(L, MLA_HEADS, QK_NOPE),
                                             d_wukv[:, MLA_HEADS * QK_NOPE:].reshape(L, MLA_HEADS, V_HEAD)], axis=-1).reshape(L, -1)
        d_cq, dqa = _rms_rows_bwd(proj, L, offs["cq"], row(q_a_norm_g, l), dcqn, BF16, f"cq_norm_bwd{l}")
        d_ckv, dkva = _rms_rows_bwd(proj, L, offs["ckv"], row(kv_a_norm_g, l), dckvn, BF16, f"ckv_norm_bwd{l}")
        grads["q_a_norm_g"][l], grads["kv_a_norm_g"][l] = dqa[0], dkva[0]
        d_cb, d_cc, d_cx, d_cz, d_cw = _conv_bwd(proj, offs, conv_full[l], dys[2], f"conv_bwd{l}")
        grads["conv_w"][l] = d_cw
        d_xq, d_xz, d_memkv, dxqg, dxkg = _xattn_bwd(proj, offs, sv["mem_kv"], row(xattn_q_norm_g, l), row(xattn_k_norm_g, l), dys[3], f"xattn_bwd{l}")
        grads["xattn_q_norm_g"][l], grads["xattn_k_norm_g"][l] = dxqg[0], dxkg[0]
        grads["w_mem_kv"][l] = _matmul(sv["memn"], d_memkv, "tn", BF16, f"mem_kv_bwd_w{l}")
        dmemn = _matmul(d_memkv, w_mem_f, "nt", F32, f"mem_kv_bwd_x{l}", b_idx=(l,))
        grads["mem_norm_g"][l] = _norm_g_grad(mems, row(mem_norm_g, l), dmemn, f"mem_norm_bwd{l}")[0]
        dproj = jnp.concatenate([d_pv, d_pz, d_cq, d_ckv, d_mz, d_cb, d_cc, d_cx, d_cz, d_xq, d_xz, dgp0, dgp1, dgp2, dgp3, d_kr], axis=1)
        early_names = [k for k in names_sh if k != "w_in"]
        early = pair_sums([grads[k][l] for k in early_names], early_names, f"early{l}")
        dh, early_received = _matmul(dproj, w_in_p[l], "nt", F32, f"proj_bwd_x{l}", tm=1024, tn=1024, tk=3712, comm=_ChipExchangePlan(early))
        n_p = offs["n"]
        w_tiles = dict(tm=512, tn=3712, tk=2048, n_outer=True)
        unpermute = lambda d: jnp.concatenate([d[:, :seg_a], d[:, n_p - 128:n_p - 128 + QK_ROPE], d[:, seg_a:n_p - 128]], axis=1)
        received[l] = [None] + early_received
        if l > 0:
            d_win = unpermute(_matmul(sv["hT"], dproj, "nn", BF16, f"proj_bwd_w{l}", **w_tiles))
            pending = (l, _ChipExchangePlan(pair_sums([d_win], ["w_in"], f"w_in{l}")))
        else:
            half = D // 2
            top = unpermute(_matmul(sv["hT"][:half], dproj, "nn", BF16, "proj_bwd_w0_top", **w_tiles))
            top_plan = _ChipExchangePlan(pair_sums([top], ["w_in"], "w_in0_top"))
            bottom, top_received = _matmul(sv["hT"][half:], dproj, "nn", BF16, "proj_bwd_w0_bottom", comm=top_plan, **w_tiles)
            bottom_plan = _ChipExchangePlan(pair_sums([unpermute(bottom)], ["w_in"], "w_in0_bottom"))
            received[0][0] = jnp.concatenate([top_received[0], _run_plan(bottom_plan, "chip_exchange_w_in0_bottom")[0]], axis=1)
        gx, dng = _rms_rows_bwd(sv["x"], D, 0, row(norm_g, l), dh, F32, f"norm_bwd{l}", res=gx)
        grads["norm_g"][l] = dng[0]
    grad_x = gx.reshape(1, S, D)
    full = {k: jnp.stack(v) for k, v in grads.items() if k not in shard_axis}

    moments = dict(w_in=(m_w_in, v_w_in), pool_w=(m_pool_w, v_pool_w), w_uq=(m_w_uq, v_w_uq), w_ukv=(m_w_ukv, v_w_ukv),
                   conv_w=(m_conv_w, v_conv_w), w_mem_kv=(m_w_mem_kv, v_w_mem_kv), w_branch=(m_w_branch, v_w_branch), w_out=(m_w_out, v_w_out))
    results = {}
    for ki, k in enumerate(names_sh):
        w = sharded[k]
        m, v = moments[k]
        flat = lambda a: a.reshape(-1, w.shape[-1])
        parts = jnp.concatenate([received[l][ki] for l in range(depth)], axis=1)
        outs = _adamw(parts, flat(w), flat(m), flat(v), f"adamw_{k}")
        results[k] = [o.reshape(w.shape) for o in outs]

    small = dict(norm_g=(norm_g, m_norm_g, v_norm_g), gate_b=(gate_b, m_gate_b, v_gate_b), pool_scale=(pool_scale, m_pool_scale, v_pool_scale),
                 q_a_norm_g=(q_a_norm_g, m_q_a_norm_g, v_q_a_norm_g), kv_a_norm_g=(kv_a_norm_g, m_kv_a_norm_g, v_kv_a_norm_g),
                 mla_q_norm_g=(mla_q_norm_g, m_mla_q_norm_g, v_mla_q_norm_g), mla_k_norm_g=(mla_k_norm_g, m_mla_k_norm_g, v_mla_k_norm_g),
                 mem_norm_g=(mem_norm_g, m_mem_norm_g, v_mem_norm_g), xattn_q_norm_g=(xattn_q_norm_g, m_xattn_q_norm_g, v_xattn_q_norm_g),
                 xattn_k_norm_g=(xattn_k_norm_g, m_xattn_k_norm_g, v_xattn_k_norm_g))
    names_sm = list(small)
    sizes = [small[k][0].size for k in names_sm]
    total = sum(sizes)
    rows = -(-total // (8 * 128)) * 8

    def pack(arrs):
        flat = jnp.concatenate([a.reshape(-1) for a in arrs])
        return jnp.pad(flat, (0, rows * 128 - total)).reshape(rows, 128)

    g_small = _run_plan(_GatherPlan([pack([full[k] for k in names_sm])]), "gather_small_grads")[0]
    outs = _adamw(g_small, pack([small[k][0] for k in names_sm]), pack([small[k][1] for k in names_sm]),
                  pack([small[k][2] for k in names_sm]), "adamw_small")
    for kind, o in enumerate(outs):
        flat, pos = o.reshape(-1), 0
        for k, n in zip(names_sm, sizes):
            results.setdefault(k, [None] * 4)[kind] = flat[pos:pos + n].reshape(small[k][0].shape)
            pos += n

    order = ["norm_g", "w_in", "gate_b", "pool_w", "pool_scale", "q_a_norm_g", "kv_a_norm_g", "w_uq", "w_ukv", "mla_q_norm_g",
             "mla_k_norm_g", "conv_w", "mem_norm_g", "w_mem_kv", "xattn_q_norm_g", "xattn_k_norm_g", "w_branch", "w_out"]
    return (loss, grad_x, *[results[k][0] for k in order], *[results[k][1] for k in order],
            *[results[k][2] for k in order], *[results[k][3] for k in order])
```

```python
import functools

import jax
import jax.numpy as jnp
from jax import lax
from jax.experimental import pallas as pl
from jax.experimental.pallas import tpu as pltpu

F32 = jnp.float32
BF16 = jnp.bfloat16
MESH_AXES = ("x", "y", "c")
N_DEV = 8

BRANCH_W = 1024
POOL_WINDOWS = (2, 4, 8, 16)
POOL_GW = BRANCH_W // 4
MLA_HEADS = 8
QK_NOPE = 128
QK_ROPE = 64
QK_HEAD = QK_NOPE + QK_ROPE
QK_PAD = 256
V_HEAD = 128
XATTN_HEADS = 4
XATTN_HD = BRANCH_W // XATTN_HEADS
ROPE_THETA = 10000.0
EPS = 1e-6
ADAM_LR = 0.001
ADAM_B1 = 0.9
ADAM_B2 = 0.999
ADAM_EPS = 1e-08
ADAM_WD = 0.01
ADAM_STEP = 10
HALO = 16
NEG = -1e30
SCORE_SCALE_LOG2 = (QK_HEAD ** -0.5) * 1.4426950408889634
VMEM_LIMIT = 56 << 20


def _pick(n, pref, align):
    if n <= pref:
        return n
    t = (pref // align) * align
    while t >= align:
        if n % t == 0:
            return t
        t -= align
    return n


def _params(sem):
    return pltpu.CompilerParams(dimension_semantics=sem, vmem_limit_bytes=VMEM_LIMIT)


def _silu(z):
    return z * (1.0 / (1.0 + jnp.exp(-z)))


def _silu_and_grad(z):
    s = 1.0 / (1.0 + jnp.exp(-z))
    return z * s, s * (1.0 + z * (1.0 - s))


def _dot(a, b):
    return jnp.dot(a.astype(BF16), b.astype(BF16), preferred_element_type=F32)


def _dot_nt(a, b):
    return lax.dot_general(a.astype(BF16), b.astype(BF16), (((1,), (1,)), ((), ())), preferred_element_type=F32)


def _dot_tn(a, b):
    return lax.dot_general(a.astype(BF16), b.astype(BF16), (((0,), (0,)), ((), ())), preferred_element_type=F32)


def _rms(x, g, n):
    r = lax.rsqrt(jnp.sum(x * x, axis=-1, keepdims=True) / n + EPS)
    return x * r * g, r


def _rms_bwd(x, g, r, dy, n):
    dyg = dy * g
    dx = r * dyg - x * (r * r * r) * (jnp.sum(dyg * x, axis=-1, keepdims=True) / n)
    dg = jnp.sum(dy * x * r, axis=0, keepdims=True)
    return dx, dg


def _col(tm, w, off):
    assert off % w == 0
    cb = off // w
    return pl.BlockSpec((tm, w), lambda i: (i, cb))


def _whole(shape):
    nd = len(shape)
    return pl.BlockSpec(shape, lambda *_: (0,) * nd)


def _lead_spec(block, fn, idx):
    if not idx:
        return pl.BlockSpec(block, fn)
    return pl.BlockSpec((None,) * len(idx) + block, lambda i, j, k: tuple(idx) + fn(i, j, k))


def _hosted_call(plan, body, ins, in_specs, out_specs, out_shape, scratch, grid, name):
    n_in, n_out, n_scr, n = len(ins), len(out_specs), len(scratch), plan.n
    n_steps = 1
    for g in grid:
        n_steps *= g

    def wrapped(*refs):
        a, c_in = refs[:n_in], refs[n_in:n_in + n]
        o, c_out = refs[n_in + n:n_in + n + n_out], refs[n_in + n + n_out:n_in + 2 * n + n_out]
        rest = refs[n_in + 2 * n + n_out:]
        scr, sems = rest[:n_scr], rest[n_scr:]
        step = 0
        for d, g in enumerate(grid):
            step = step * g + pl.program_id(d)

        @pl.when(step == 0)
        def _():
            plan.start(c_in, c_out, sems)

        body(*a, *o, *scr)

        @pl.when(step == n_steps // 2)
        def _():
            plan.relay(c_in, c_out, sems)

        @pl.when(step == n_steps - 1)
        def _():
            plan.finish(c_in, c_out, sems)

    res = pl.pallas_call(
        wrapped, grid=grid, in_specs=list(in_specs) + [HBM_SPEC] * n, out_specs=list(out_specs) + [HBM_SPEC] * n,
        out_shape=list(out_shape) + plan.out_shape, scratch_shapes=list(scratch) + plan.scratch, name=name,
        compiler_params=_params(("arbitrary",) * len(grid)),
    )(*ins, *plan.arrs)
    return list(res[:n_out]), list(res[n_out:])


def _matmul(a, b, mode, out_dtype, name, tm=1024, tn=1024, tk=2048, res=None, a_idx=(), b_idx=(), n_outer=False, comm=None):
    a_shape, b_shape = a.shape[len(a_idx):], b.shape[len(b_idx):]
    if mode == "nn":
        (M, K), N = a_shape, b_shape[1]
    elif mode == "nt":
        (M, K), N = a_shape, b_shape[0]
    else:
        (K, M), N = a_shape, b_shape[1]
    tm, tn, tk = _pick(M, tm, 128), _pick(N, tn, 128), _pick(K, tk, 128)
    nk = K // tk
    order = (lambda fn: (lambda j, i, k: fn(i, j, k))) if n_outer else (lambda fn: fn)
    if mode == "nn":
        a_spec = _lead_spec((tm, tk), order(lambda i, j, k: (i, k)), a_idx)
        b_spec = _lead_spec((tk, tn), order(lambda i, j, k: (k, j)), b_idx)
        op = _dot
    elif mode == "nt":
        a_spec = _lead_spec((tm, tk), order(lambda i, j, k: (i, k)), a_idx)
        b_spec = _lead_spec((tn, tk), order(lambda i, j, k: (j, k)), b_idx)
        op = _dot_nt
    else:
        a_spec = _lead_spec((tk, tm), order(lambda i, j, k: (k, i)), a_idx)
        b_spec = _lead_spec((tk, tn), order(lambda i, j, k: (k, j)), b_idx)
        op = _dot_tn
    o_spec = pl.BlockSpec((tm, tn), order(lambda i, j, k: (i, j)))
    has_res = res is not None

    def body(*refs):
        a_ref, b_ref = refs[0], refs[1]
        res_ref = refs[2] if has_res else None
        o_ref = refs[2 + has_res]
        part = op(a_ref[...], b_ref[...])

        def finish(acc):
            if has_res:
                acc = res_ref[...] + acc
            o_ref[...] = acc.astype(out_dtype)

        if nk == 1:
            finish(part)
        else:
            acc_ref = refs[3 + has_res]
            k = pl.program_id(2)

            @pl.when(k == 0)
            def _():
                acc_ref[...] = part

            @pl.when(k > 0)
            def _():
                acc_ref[...] += part

            @pl.when(k == nk - 1)
            def _():
                finish(acc_ref[...])

    ins = [a, b] + ([res] if has_res else [])
    in_specs = [a_spec, b_spec] + ([o_spec] if has_res else [])
    scratch = [pltpu.VMEM((tm, tn), F32)] if nk > 1 else []
    grid = (N // tn, M // tm, nk) if n_outer else (M // tm, N // tn, nk)
    out_shape = jax.ShapeDtypeStruct((M, N), out_dtype)
    if comm is not None:
        (out,), carried = _hosted_call(comm, body, ins, in_specs, [o_spec], [out_shape], scratch, grid, name)
        return out, carried
    return pl.pallas_call(
        body, grid=grid, in_specs=in_specs, out_specs=o_spec, out_shape=out_shape, scratch_shapes=scratch, name=name,
        compiler_params=_params(("parallel", "parallel", "arbitrary")),
    )(*ins)


def _rms_rows(x, width, off, g, name, with_transpose=False):
    S = x.shape[0]
    tm = _pick(S, 512, 128 if with_transpose else 16)

    def body(x_ref, g_ref, o_ref, *t_ref):
        y, _ = _rms(x_ref[...], g_ref[...], width)
        o_ref[...] = y.astype(BF16)
        if with_transpose:
            t_ref[0][...] = y.T.astype(BF16)

    out_specs = [_col(tm, width, 0)] + ([pl.BlockSpec((width, tm), lambda i: (0, i))] if with_transpose else [])
    out_shape = [jax.ShapeDtypeStruct((S, width), BF16)] + ([jax.ShapeDtypeStruct((width, S), BF16)] if with_transpose else [])
    res = pl.pallas_call(
        body, grid=(S // tm,), in_specs=[_col(tm, width, off), _whole((1, width))], out_specs=out_specs,
        out_shape=out_shape, name=name, compiler_params=_params(("parallel",)),
    )(x, g)
    return res if with_transpose else res[0]


def _rms_rows_bwd(x, width, off, g, dy, out_dtype, name, res=None):
    S = x.shape[0]
    tm = _pick(S, 256, 16)
    has_res = res is not None

    def body(*refs):
        x_ref, g_ref, dy_ref = refs[:3]
        res_ref = refs[3] if has_res else None
        dx_ref, dg_ref = refs[3 + has_res:]
        xv, gv = x_ref[...], g_ref[...]
        _, r = _rms(xv, gv, width)
        dx, dg = _rms_bwd(xv, gv, r, dy_ref[...], width)
        if has_res:
            dx = res_ref[...] + dx
        dx_ref[...] = dx.astype(out_dtype)

        @pl.when(pl.program_id(0) == 0)
        def _():
            dg_ref[...] = jnp.zeros_like(dg_ref)

        dg_ref[...] += dg

    ins = [x, g, dy] + ([res] if has_res else [])
    in_specs = [_col(tm, width, off), _whole((1, width)), _col(tm, width, 0)] + ([_col(tm, width, 0)] if has_res else [])
    return pl.pallas_call(
        body, grid=(S // tm,), in_specs=in_specs, out_specs=[_col(tm, width, 0), _whole((1, width))],
        out_shape=[jax.ShapeDtypeStruct((S, width), out_dtype), jax.ShapeDtypeStruct((1, width), F32)], name=name,
        compiler_params=_params(("arbitrary",)),
    )(*ins)


def _norm_g_grad(x, g, dy, name):
    S, width = x.shape
    tm = _pick(S, 256, 16)

    def body(x_ref, g_ref, dy_ref, dg_ref):
        xv = x_ref[...]
        _, r = _rms(xv, g_ref[...], width)

        @pl.when(pl.program_id(0) == 0)
        def _():
            dg_ref[...] = jnp.zeros_like(dg_ref)

        dg_ref[...] += jnp.sum(dy_ref[...] * xv * r, axis=0, keepdims=True)

    return pl.pallas_call(
        body, grid=(S // tm,), in_specs=[_col(tm, width, 0), _whole((1, width)), _col(tm, width, 0)],
        out_specs=_whole((1, width)), out_shape=jax.ShapeDtypeStruct((1, width), F32), name=name,
        compiler_params=_params(("arbitrary",)),
    )(x, g, dy)


def _loss_rows(y, target, name):
    S, D = y.shape
    tm = _pick(S, 512, 16)

    def body(y_ref, t_ref, g_ref, l_ref):
        err = y_ref[...] - t_ref[...]
        g_ref[...] = err / D

        @pl.when(pl.program_id(0) == 0)
        def _():
            l_ref[...] = jnp.zeros_like(l_ref)

        row = jnp.sum(err * err, axis=-1, keepdims=True) / D
        l_ref[...] += jnp.sum(row, axis=0, keepdims=True)

    return pl.pallas_call(
        body, grid=(S // tm,), in_specs=[_col(tm, D, 0), _col(tm, D, 0)], out_specs=[_col(tm, D, 0), _whole((1, 1))],
        out_shape=[jax.ShapeDtypeStruct((S, D), F32), jax.ShapeDtypeStruct((1, 1), F32)], name=name,
        compiler_params=_params(("arbitrary",)),
    )(y, target)


def _row_index(i, ts):
    return i * ts + lax.broadcasted_iota(jnp.int32, (ts, 1), 0)


def _pool_mixed(scr, pv_ref, g, t, ts):
    w, lo = POOL_WINDOWS[g], g * POOL_GW
    win = scr[HALO:HALO + ts, lo:lo + POOL_GW]
    for k in range(1, w):
        win = win + scr[HALO - k:HALO - k + ts, lo:lo + POOL_GW]
    cnt = jnp.minimum(t + 1, w).astype(F32)
    return win / cnt - pv_ref[:, lo:lo + POOL_GW], cnt


def _halo_before(ts, off):
    nb = ts // HALO
    cb = off // BRANCH_W
    return pl.BlockSpec((HALO, BRANCH_W), lambda i: (jnp.maximum(i * nb - 1, 0), cb))


def _halo_after(ts, off, n_tiles):
    nb = ts // HALO
    cb = off // BRANCH_W
    last = n_tiles * nb - 1
    return pl.BlockSpec((HALO, BRANCH_W), lambda i: (jnp.minimum((i + 1) * nb, last), cb))


def _pool_fwd(proj, offs, pool_w, pool_scale, name):
    S = proj.shape[0]
    ts = _pick(S, 512, HALO)

    def body(pv_ref, halo_ref, pz_ref, w_ref, sc_ref, y_ref, scr):
        i = pl.program_id(0)
        scr[0:HALO, :] = jnp.where(i > 0, halo_ref[...], 0.0)
        scr[HALO:HALO + ts, :] = pv_ref[...]
        t = _row_index(i, ts)
        for g in range(4):
            lo = g * POOL_GW
            mixed, _ = _pool_mixed(scr, pv_ref, g, t, ts)
            out = _dot(mixed, w_ref[g])
            y_ref[:, lo:lo + POOL_GW] = (out * sc_ref[:, lo:lo + POOL_GW] * _silu(pz_ref[:, lo:lo + POOL_GW])).astype(BF16)

    return pl.pallas_call(
        body, grid=(S // ts,),
        in_specs=[_col(ts, BRANCH_W, offs["pv"]), _halo_before(ts, offs["pv"]), _col(ts, BRANCH_W, offs["pz"]),
                  _whole((4, POOL_GW, POOL_GW)), _whole((1, BRANCH_W))],
        out_specs=_col(ts, BRANCH_W, 0), out_shape=jax.ShapeDtypeStruct((S, BRANCH_W), BF16),
        scratch_shapes=[pltpu.VMEM((HALO + ts, BRANCH_W), F32)], name=name, compiler_params=_params(("parallel",)),
    )(proj, proj, proj, pool_w, pool_scale)


def _pool_bwd_a(proj, offs, pool_w, pool_scale, dy, name):
    S = proj.shape[0]
    ts = _pick(S, 512, HALO)

    def body(pv_ref, halo_ref, pz_ref, w_ref, sc_ref, dy_ref, dpz_ref, e_ref, dw_ref, dsc_ref, scr):
        i = pl.program_id(0)
        scr[0:HALO, :] = jnp.where(i > 0, halo_ref[...], 0.0)
        scr[HALO:HALO + ts, :] = pv_ref[...]
        t = _row_index(i, ts)

        @pl.when(i == 0)
        def _():
            dw_ref[...] = jnp.zeros_like(dw_ref)
            dsc_ref[...] = jnp.zeros_like(dsc_ref)

        for g in range(4):
            lo = g * POOL_GW
            sl = slice(lo, lo + POOL_GW)
            mixed, cnt = _pool_mixed(scr, pv_ref, g, t, ts)
            out = _dot(mixed, w_ref[g])
            sz, dsz = _silu_and_grad(pz_ref[:, sl])
            dyv, sc = dy_ref[:, sl], sc_ref[:, sl]
            d_out = dyv * sc * sz
            dsc_ref[:, sl] += jnp.sum(dyv * out * sz, axis=0, keepdims=True)
            dpz_ref[:, sl] = (dyv * out * sc * dsz).astype(BF16)
            dw_ref[g] += _dot_tn(mixed, d_out)
            e_ref[:, sl] = _dot_nt(d_out, w_ref[g]) / cnt

    return pl.pallas_call(
        body, grid=(S // ts,),
        in_specs=[_col(ts, BRANCH_W, offs["pv"]), _halo_before(ts, offs["pv"]), _col(ts, BRANCH_W, offs["pz"]),
                  _whole((4, POOL_GW, POOL_GW)), _whole((1, BRANCH_W)), _col(ts, BRANCH_W, 0)],
        out_specs=[_col(ts, BRANCH_W, 0), _col(ts, BRANCH_W, 0), _whole((4, POOL_GW, POOL_GW)), _whole((1, BRANCH_W))],
        out_shape=[jax.ShapeDtypeStruct((S, BRANCH_W), BF16), jax.ShapeDtypeStruct((S, BRANCH_W), F32),
                   jax.ShapeDtypeStruct((4, POOL_GW, POOL_GW), F32), jax.ShapeDtypeStruct((1, BRANCH_W), F32)],
        scratch_shapes=[pltpu.VMEM((HALO + ts, BRANCH_W), F32)], name=name, compiler_params=_params(("arbitrary",)),
    )(proj, proj, proj, pool_w, pool_scale, dy)


def _pool_bwd_b(e, name):
    S = e.shape[0]
    ts = _pick(S, 512, HALO)
    n_tiles = S // ts

    def body(e_ref, halo_ref, dpv_ref, scr):
        i = pl.program_id(0)
        scr[0:ts, :] = e_ref[...]
        scr[ts:ts + HALO, :] = jnp.where(i < n_tiles - 1, halo_ref[...], 0.0)
        t = _row_index(i, ts)
        for g in range(4):
            w, lo = POOL_WINDOWS[g], g * POOL_GW
            acc = scr[0:ts, lo:lo + POOL_GW]
            ev = acc
            for k in range(1, w):
                acc = acc + scr[k:k + ts, lo:lo + POOL_GW]
            cnt = jnp.minimum(t + 1, w).astype(F32)
            dpv_ref[:, lo:lo + POOL_GW] = (acc - ev * cnt).astype(BF16)

    return pl.pallas_call(
        body, grid=(n_tiles,), in_specs=[_col(ts, BRANCH_W, 0), _halo_after(ts, 0, n_tiles)], out_specs=_col(ts, BRANCH_W, 0),
        out_shape=jax.ShapeDtypeStruct((S, BRANCH_W), BF16), scratch_shapes=[pltpu.VMEM((ts + HALO, BRANCH_W), F32)],
        name=name, compiler_params=_params(("parallel",)),
    )(e, e)


def _conv_fwd(proj, offs, conv_w, name):
    S = proj.shape[0]
    ts = _pick(S, 512, HALO)

    def body(cb_ref, cc_ref, cx_ref, cz_ref, hc_ref, hx_ref, w_ref, y_ref, scr):
        i = pl.program_id(0)
        scr[0:HALO, :] = jnp.where(i > 0, hc_ref[...] * hx_ref[...], 0.0)
        scr[HALO:HALO + ts, :] = cc_ref[...] * cx_ref[...]
        y = w_ref[0:1, :] * scr[HALO - 2:HALO - 2 + ts, :] + w_ref[1:2, :] * scr[HALO - 1:HALO - 1 + ts, :] + w_ref[2:3, :] * scr[HALO:HALO + ts, :]
        y_ref[...] = (cb_ref[...] * y * _silu(cz_ref[...])).astype(BF16)

    return pl.pallas_call(
        body, grid=(S // ts,),
        in_specs=[_col(ts, BRANCH_W, offs["cb"]), _col(ts, BRANCH_W, offs["cc"]), _col(ts, BRANCH_W, offs["cx"]),
                  _col(ts, BRANCH_W, offs["cz"]), _halo_before(ts, offs["cc"]), _halo_before(ts, offs["cx"]), _whole((3, BRANCH_W))],
        out_specs=_col(ts, BRANCH_W, 0), out_shape=jax.ShapeDtypeStruct((S, BRANCH_W), BF16),
        scratch_shapes=[pltpu.VMEM((HALO + ts, BRANCH_W), F32)], name=name, compiler_params=_params(("parallel",)),
    )(proj, proj, proj, proj, proj, proj, conv_w)


def _conv_bwd(proj, offs, conv_w, dy, name):
    S = proj.shape[0]
    ts = _pick(S, 256, HALO)
    n_tiles = S // ts

    def body(cb_ref, cc_ref, cx_ref, cz_ref, hc_ref, hx_ref, w_ref, dy_ref, ady_ref, acb_ref, acz_ref,
             dcb_ref, dcc_ref, dcx_ref, dcz_ref, dw_ref, scr_u, scr_d):
        i = pl.program_id(0)
        u = cc_ref[...] * cx_ref[...]
        scr_u[0:HALO, :] = jnp.where(i > 0, hc_ref[...] * hx_ref[...], 0.0)
        scr_u[HALO:HALO + ts, :] = u
        u1 = scr_u[HALO - 1:HALO - 1 + ts, :]
        u2 = scr_u[HALO - 2:HALO - 2 + ts, :]
        y = w_ref[0:1, :] * u2 + w_ref[1:2, :] * u1 + w_ref[2:3, :] * u
        sz, dsz = _silu_and_grad(cz_ref[...])
        dyv, cb = dy_ref[...], cb_ref[...]
        dcb_ref[...] = (dyv * y * sz).astype(BF16)
        dcz_ref[...] = (dyv * cb * y * dsz).astype(BF16)
        d_y = dyv * cb * sz
        scr_d[0:ts, :] = d_y
        scr_d[ts:ts + HALO, :] = jnp.where(i < n_tiles - 1, ady_ref[...] * acb_ref[...] * _silu(acz_ref[...]), 0.0)
        du = w_ref[2:3, :] * d_y + w_ref[1:2, :] * scr_d[1:1 + ts, :] + w_ref[0:1, :] * scr_d[2:2 + ts, :]
        dcc_ref[...] = (du * cx_ref[...]).astype(BF16)
        dcx_ref[...] = (du * cc_ref[...]).astype(BF16)

        @pl.when(i == 0)
        def _():
            dw_ref[...] = jnp.zeros_like(dw_ref)

        dw_ref[0:1, :] += jnp.sum(d_y * u2, axis=0, keepdims=True)
        dw_ref[1:2, :] += jnp.sum(d_y * u1, axis=0, keepdims=True)
        dw_ref[2:3, :] += jnp.sum(d_y * u, axis=0, keepdims=True)

    row = lambda off: _col(ts, BRANCH_W, off)
    return pl.pallas_call(
        body, grid=(n_tiles,),
        in_specs=[row(offs["cb"]), row(offs["cc"]), row(offs["cx"]), row(offs["cz"]), _halo_before(ts, offs["cc"]),
                  _halo_before(ts, offs["cx"]), _whole((3, BRANCH_W)), row(0), _halo_after(ts, 0, n_tiles),
                  _halo_after(ts, offs["cb"], n_tiles), _halo_after(ts, offs["cz"], n_tiles)],
        out_specs=[row(0), row(0), row(0), row(0), _whole((3, BRANCH_W))],
        out_shape=[jax.ShapeDtypeStruct((S, BRANCH_W), BF16)] * 4 + [jax.ShapeDtypeStruct((3, BRANCH_W), F32)],
        scratch_shapes=[pltpu.VMEM((HALO + ts, BRANCH_W), F32), pltpu.VMEM((ts + HALO, BRANCH_W), F32)], name=name,
        compiler_params=_params(("arbitrary",)),
    )(proj, proj, proj, proj, proj, proj, conv_w, dy, dy, proj, proj)


def _xattn_head(xq, kraw, v, qg, kg):
    qn, rq = _rms(xq, qg, XATTN_HD)
    kn, rk = _rms(kraw, kg, XATTN_HD)
    s = _dot_nt(qn, kn) * (XATTN_HD ** -0.5)
    e = jnp.exp(s - jnp.max(s, axis=-1, keepdims=True))
    p = e / jnp.sum(e, axis=-1, keepdims=True)
    return qn, rq, kn, rk, p, _dot(p, v)


def _xattn_fwd(proj, offs, mem_kv, qg, kg, name):
    S, M = proj.shape[0], mem_kv.shape[0]
    tm = _pick(S, 512, 16)

    def body(xq_ref, xz_ref, kv_ref, qg_ref, kg_ref, y_ref):
        for h in range(XATTN_HEADS):
            sl = slice(h * XATTN_HD, (h + 1) * XATTN_HD)
            vs = slice(BRANCH_W + h * XATTN_HD, BRANCH_W + (h + 1) * XATTN_HD)
            o = _xattn_head(xq_ref[:, sl], kv_ref[:, sl], kv_ref[:, vs], qg_ref[...], kg_ref[...])[-1]
            y_ref[:, sl] = (o * _silu(xz_ref[:, sl])).astype(BF16)

    return pl.pallas_call(
        body, grid=(S // tm,),
        in_specs=[_col(tm, BRANCH_W, offs["xq"]), _col(tm, BRANCH_W, offs["xz"]), _whole((M, 2 * BRANCH_W)),
                  _whole((1, XATTN_HD)), _whole((1, XATTN_HD))],
        out_specs=_col(tm, BRANCH_W, 0), out_shape=jax.ShapeDtypeStruct((S, BRANCH_W), BF16), name=name,
        compiler_params=_params(("parallel",)),
    )(proj, proj, mem_kv, qg, kg)


def _xattn_bwd(proj, offs, mem_kv, qg, kg, dy, name):
    S, M = proj.shape[0], mem_kv.shape[0]
    tm = _pick(S, 256, 16)
    n_tiles = S // tm

    def body(xq_ref, xz_ref, kv_ref, qg_ref, kg_ref, dy_ref, dxq_ref, dxz_ref, dkv_ref, dqg_ref, dkg_ref, dkn_acc):
        i = pl.program_id(0)

        @pl.when(i == 0)
        def _():
            dkv_ref[...] = jnp.zeros_like(dkv_ref)
            dqg_ref[...] = jnp.zeros_like(dqg_ref)
            dkg_ref[...] = jnp.zeros_like(dkg_ref)
            dkn_acc[...] = jnp.zeros_like(dkn_acc)

        qg_v, kg_v = qg_ref[...], kg_ref[...]
        for h in range(XATTN_HEADS):
            sl = slice(h * XATTN_HD, (h + 1) * XATTN_HD)
            vs = slice(BRANCH_W + h * XATTN_HD, BRANCH_W + (h + 1) * XATTN_HD)
            xq, v = xq_ref[:, sl], kv_ref[:, vs]
            qn, rq, kn, rk, p, o = _xattn_head(xq, kv_ref[:, sl], v, qg_v, kg_v)
            sz, dsz = _silu_and_grad(xz_ref[:, sl])
            dyv = dy_ref[:, sl]
            dxz_ref[:, sl] = (dyv * o * dsz).astype(BF16)
            do = dyv * sz
            dkv_ref[:, vs] += _dot_tn(p, do)
            dp = _dot_nt(do, v)
            ds = p * (dp - jnp.sum(dp * p, axis=-1, keepdims=True)) * (XATTN_HD ** -0.5)
            dqn = _dot(ds, kn)
            dkn_acc[:, sl] += _dot_tn(ds, qn)
            dxq, dqg = _rms_bwd(xq, qg_v, rq, dqn, XATTN_HD)
            dxq_ref[:, sl] = dxq.astype(BF16)
            dqg_ref[...] += dqg

        @pl.when(i == n_tiles - 1)
        def _():
            for h in range(XATTN_HEADS):
                sl = slice(h * XATTN_HD, (h + 1) * XATTN_HD)
                kraw = kv_ref[:, sl]
                _, rk = _rms(kraw, kg_v, XATTN_HD)
                dk, dkg = _rms_bwd(kraw, kg_v, rk, dkn_acc[:, sl], XATTN_HD)
                dkv_ref[:, sl] = dk
                dkg_ref[...] += dkg

    return pl.pallas_call(
        body, grid=(n_tiles,),
        in_specs=[_col(tm, BRANCH_W, offs["xq"]), _col(tm, BRANCH_W, offs["xz"]), _whole((M, 2 * BRANCH_W)),
                  _whole((1, XATTN_HD)), _whole((1, XATTN_HD)), _col(tm, BRANCH_W, 0)],
        out_specs=[_col(tm, BRANCH_W, 0), _col(tm, BRANCH_W, 0), _whole((M, 2 * BRANCH_W)), _whole((1, XATTN_HD)), _whole((1, XATTN_HD))],
        out_shape=[jax.ShapeDtypeStruct((S, BRANCH_W), BF16), jax.ShapeDtypeStruct((S, BRANCH_W), BF16),
                   jax.ShapeDtypeStruct((M, 2 * BRANCH_W), F32), jax.ShapeDtypeStruct((1, XATTN_HD), F32), jax.ShapeDtypeStruct((1, XATTN_HD), F32)],
        scratch_shapes=[pltpu.VMEM((M, BRANCH_W), F32)], name=name, compiler_params=_params(("arbitrary",)),
    )(proj, proj, mem_kv, qg, kg, dy)


def _rope(c, tc, ta, tb):
    return c * tc + pltpu.roll(c, 96, 1) * ta + pltpu.roll(c, 32, 1) * tb


def _rope_t(d, tc, ta, tb):
    return d * tc + pltpu.roll(d * ta, 32, 1) + pltpu.roll(d * tb, 96, 1)


def _mla_qk_fwd(qp, kvp, proj, offs, tabs, qg, kg, name):
    S = qp.shape[0]
    tm = _pick(S, 256, 16)
    W = MLA_HEADS * QK_PAD

    def body(q_ref, kv_ref, kr_ref, tc_ref, ta_ref, tb_ref, qg_ref, kg_ref, qo_ref, ko_ref, vo_ref):
        tc, ta, tb = tc_ref[...], ta_ref[...], tb_ref[...]
        qg_v, kg_v = qg_ref[...], kg_ref[...]
        kr = kr_ref[...]
        kr_ss = jnp.sum(kr * kr, axis=-1, keepdims=True)
        for h in range(MLA_HEADS):
            sl = slice(h * QK_PAD, (h + 1) * QK_PAD)
            qn, _ = _rms(q_ref[:, sl], qg_v, QK_HEAD)
            qo_ref[:, sl] = jnp.concatenate([qn[:, :QK_NOPE], _rope(qn[:, QK_NOPE:], tc, ta, tb)], axis=1).astype(BF16)
            kn = kv_ref[:, h * QK_NOPE:(h + 1) * QK_NOPE]
            r = lax.rsqrt((jnp.sum(kn * kn, axis=-1, keepdims=True) + kr_ss) / QK_HEAD + EPS)
            ko_ref[:, sl] = jnp.concatenate([kn * r * kg_v[:, :QK_NOPE], _rope(kr * r * kg_v[:, QK_NOPE:], tc, ta, tb)], axis=1).astype(BF16)
        vo_ref[...] = kv_ref[:, MLA_HEADS * QK_NOPE:].astype(BF16)

    return pl.pallas_call(
        body, grid=(S // tm,),
        in_specs=[_col(tm, W, 0), _col(tm, W, 0), _col(tm, 128, offs["kr"]), _col(tm, 128, 0), _col(tm, 128, 0), _col(tm, 128, 0),
                  _whole((1, QK_PAD)), _whole((1, QK_PAD))],
        out_specs=[_col(tm, W, 0), _col(tm, W, 0), _col(tm, BRANCH_W, 0)],
        out_shape=[jax.ShapeDtypeStruct((S, W), BF16), jax.ShapeDtypeStruct((S, W), BF16), jax.ShapeDtypeStruct((S, BRANCH_W), BF16)],
        name=name, compiler_params=_params(("parallel",)),
    )(qp, kvp, proj, *tabs, qg, kg)


def _mla_qk_bwd(qp, kvp, proj, offs, tabs, qg, kg, dQ, dK, dV, name):
    S = qp.shape[0]
    tm = _pick(S, 256, 16)
    W = MLA_HEADS * QK_PAD

    def body(q_ref, kv_ref, kr_ref, tc_ref, ta_ref, tb_ref, qg_ref, kg_ref, dQ_ref, dK_ref, dV_ref,
             dq_ref, dkv_ref, dkr_ref, dqg_ref, dkg_ref):
        @pl.when(pl.program_id(0) == 0)
        def _():
            dqg_ref[...] = jnp.zeros_like(dqg_ref)
            dkg_ref[...] = jnp.zeros_like(dkg_ref)

        tc, ta, tb = tc_ref[...], ta_ref[...], tb_ref[...]
        qg_v, kg_v = qg_ref[...], kg_ref[...]
        kg_n, kg_r = kg_v[:, :QK_NOPE], kg_v[:, QK_NOPE:]
        kr = kr_ref[...]
        kr_ss = jnp.sum(kr * kr, axis=-1, keepdims=True)
        dkr = jnp.zeros_like(kr)
        dkg_n = jnp.zeros((1, QK_NOPE), F32)
        dkg_r = jnp.zeros((1, QK_NOPE), F32)
        for h in range(MLA_HEADS):
            sl = slice(h * QK_PAD, (h + 1) * QK_PAD)
            qv = q_ref[:, sl]
            _, rq = _rms(qv, qg_v, QK_HEAD)
            dQv = dQ_ref[:, sl]
            dqn = jnp.concatenate([dQv[:, :QK_NOPE], _rope_t(dQv[:, QK_NOPE:], tc, ta, tb)], axis=1)
            dq, dqg = _rms_bwd(qv, qg_v, rq, dqn, QK_HEAD)
            dq_ref[:, sl] = dq.astype(BF16)
            dqg_ref[...] += dqg
            ns = slice(h * QK_NOPE, (h + 1) * QK_NOPE)
            kn = kv_ref[:, ns]
            r = lax.rsqrt((jnp.sum(kn * kn, axis=-1, keepdims=True) + kr_ss) / QK_HEAD + EPS)
            dKv = dK_ref[:, sl]
            dyn = dKv[:, :QK_NOPE]
            dyr = _rope_t(dKv[:, QK_NOPE:], tc, ta, tb)
            proj_s = (jnp.sum(dyn * kg_n * kn, axis=-1, keepdims=True) + jnp.sum(dyr * kg_r * kr, axis=-1, keepdims=True)) / QK_HEAD
            r3 = r * r * r
            dkv_ref[:, ns] = (r * dyn * kg_n - kn * r3 * proj_s).astype(BF16)
            dkr = dkr + (r * dyr * kg_r - kr * r3 * proj_s)
            dkg_n = dkg_n + jnp.sum(dyn * kn * r, axis=0, keepdims=True)
            dkg_r = dkg_r + jnp.sum(dyr * kr * r, axis=0, keepdims=True)
        dkg_ref[...] += jnp.concatenate([dkg_n, dkg_r], axis=1)
        dkr_ref[...] = dkr.astype(BF16)
        dkv_ref[:, MLA_HEADS * QK_NOPE:] = dV_ref[...].astype(BF16)

    return pl.pallas_call(
        body, grid=(S // tm,),
        in_specs=[_col(tm, W, 0), _col(tm, W, 0), _col(tm, 128, offs["kr"]), _col(tm, 128, 0), _col(tm, 128, 0), _col(tm, 128, 0),
                  _whole((1, QK_PAD)), _whole((1, QK_PAD)), _col(tm, W, 0), _col(tm, W, 0), _col(tm, BRANCH_W, 0)],
        out_specs=[_col(tm, W, 0), _col(tm, W, 0), _col(tm, 128, 0), _whole((1, QK_PAD)), _whole((1, QK_PAD))],
        out_shape=[jax.ShapeDtypeStruct((S, W), BF16), jax.ShapeDtypeStruct((S, W), BF16), jax.ShapeDtypeStruct((S, 128), BF16),
                   jax.ShapeDtypeStruct((1, QK_PAD), F32), jax.ShapeDtypeStruct((1, QK_PAD), F32)],
        name=name, compiler_params=_params(("arbitrary",)),
    )(qp, kvp, proj, *tabs, qg, kg, dQ, dK, dV)


def _flash_fwd(Q, K, V, proj, offs, name, comm=None):
    S = Q.shape[0]
    t = _pick(S, 1024, 128)
    assert t % 256 == 0
    mz_cb = offs["mz"] // V_HEAD

    half = t // 2

    def fold(x, op):
        r = x[:, 0:128]
        for ch in range(1, x.shape[1] // 128):
            r = op(r, x[:, ch * 128:(ch + 1) * 128])
        return r

    def body(q_ref, k_ref, v_ref, mz_ref, o_ref, y_ref, lse_ref, mp_scr, lp_scr, acc_scr):
        qi = pl.program_id(1)
        q = q_ref[...]

        def raw_scores(j):
            rows_k = pl.ds(pl.multiple_of(j * t, t), t)
            return _dot_nt(q, k_ref[rows_k, :]), rows_k

        def diagonal_scores(part):
            q_lo = part * half
            rows_k = pl.ds(pl.multiple_of(qi * t + part * half, half), half)
            s = _dot_nt(q[q_lo:], k_ref[rows_k, :])
            rows = lax.broadcasted_iota(jnp.int32, s.shape, 0)
            cols = lax.broadcasted_iota(jnp.int32, s.shape, 1)
            return jnp.where(cols <= rows, s, NEG), rows_k, q_lo

        mp_scr[...] = jnp.full_like(mp_scr, NEG)

        def max_sweep(j, carry):
            s, _ = raw_scores(j)
            mp_scr[...] = jnp.maximum(mp_scr[...], fold(s, jnp.maximum))
            return carry

        lax.fori_loop(0, qi, max_sweep, 0)
        for part in range(2):
            s, _, q_lo = diagonal_scores(part)
            mp_scr[q_lo:, :] = jnp.maximum(mp_scr[q_lo:, :], fold(s, jnp.maximum))
        m = jnp.max(mp_scr[...], axis=-1, keepdims=True) * SCORE_SCALE_LOG2

        lp_scr[...] = jnp.zeros_like(lp_scr)
        acc_scr[...] = jnp.zeros_like(acc_scr)

        def sum_sweep(j, carry):
            s, rows_k = raw_scores(j)
            p = jnp.exp2(s * SCORE_SCALE_LOG2 - m)
            lp_scr[...] += fold(p, jnp.add)
            acc_scr[...] += _dot(p, v_ref[rows_k, :])
            return carry

        lax.fori_loop(0, qi, sum_sweep, 0)
        for part in range(2):
            s, rows_k, q_lo = diagonal_scores(part)
            p = jnp.exp2(s * SCORE_SCALE_LOG2 - m[q_lo:])
            lp_scr[q_lo:, :] += fold(p, jnp.add)
            acc_scr[q_lo:, :] += _dot(p, v_ref[rows_k, :])
        l = jnp.sum(lp_scr[...], axis=-1, keepdims=True)
        o = acc_scr[...] / l
        o_ref[...] = o
        y_ref[...] = (o * _silu(mz_ref[...])).astype(BF16)
        lse_ref[0] = m + jnp.log2(l)

    in_specs = [pl.BlockSpec((t, QK_PAD), lambda h, i: (i, h)), pl.BlockSpec((S, QK_PAD), lambda h, i: (0, h)),
                pl.BlockSpec((S, V_HEAD), lambda h, i: (0, h)), pl.BlockSpec((t, V_HEAD), lambda h, i: (i, mz_cb + h))]
    out_specs = [pl.BlockSpec((t, V_HEAD), lambda h, i: (i, h)), pl.BlockSpec((t, V_HEAD), lambda h, i: (i, h)),
                 pl.BlockSpec((1, t, 1), lambda h, i: (h, i, 0))]
    out_shape = [jax.ShapeDtypeStruct((S, BRANCH_W), F32), jax.ShapeDtypeStruct((S, BRANCH_W), BF16),
                 jax.ShapeDtypeStruct((MLA_HEADS, S, 1), F32)]
    scratch = [pltpu.VMEM((t, 128), F32), pltpu.VMEM((t, 128), F32), pltpu.VMEM((t, V_HEAD), F32)]
    ins = [Q, K, V, proj]
    if comm is not None:
        return _hosted_call(comm, body, ins, in_specs, out_specs, out_shape, scratch, (MLA_HEADS, S // t), name)
    res = pl.pallas_call(
        body, grid=(MLA_HEADS, S // t), in_specs=in_specs, out_specs=out_specs, out_shape=out_shape, scratch_shapes=scratch,
        name=name, compiler_params=_params(("parallel", "parallel")),
    )(*ins)
    return list(res), []


def _mla_gate_bwd(o, proj, offs, dy, name):
    S = o.shape[0]
    tm = _pick(S, 512, 16)

    def body(o_ref, mz_ref, dy_ref, do_ref, dmz_ref, dl_ref):
        sz, dsz = _silu_and_grad(mz_ref[...])
        dyv, ov = dy_ref[...], o_ref[...]
        do = (dyv * sz).astype(BF16)
        do_ref[...] = do
        dmz_ref[...] = (dyv * ov * dsz).astype(BF16)
        prod = do.astype(F32) * ov
        for h in range(MLA_HEADS):
            dl_ref[h] = jnp.sum(prod[:, h * V_HEAD:(h + 1) * V_HEAD], axis=-1, keepdims=True)

    return pl.pallas_call(
        body, grid=(S // tm,), in_specs=[_col(tm, BRANCH_W, 0), _col(tm, BRANCH_W, offs["mz"]), _col(tm, BRANCH_W, 0)],
        out_specs=[_col(tm, BRANCH_W, 0), _col(tm, BRANCH_W, 0), pl.BlockSpec((MLA_HEADS, tm, 1), lambda i: (0, i, 0))],
        out_shape=[jax.ShapeDtypeStruct((S, BRANCH_W), BF16), jax.ShapeDtypeStruct((S, BRANCH_W), BF16),
                   jax.ShapeDtypeStruct((MLA_HEADS, S, 1), F32)], name=name,
        compiler_params=_params(("parallel",)),
    )(o, proj, dy)


def _flash_bwd(Q, K, V, lse, delta, do, name, comm=None):
    S = Q.shape[0]
    tk = _pick(S, 1024, 128)
    assert tk % 256 == 0
    nq, half = S // tk, tk // 2
    once = pl.Buffered(1)

    def body(q_ref, k_ref, v_ref, lse_ref, dl_ref, do_ref, dq_ref, dk_ref, dv_ref, dk_acc, dv_acc):
        ki = pl.program_id(1)

        @pl.when(ki == 0)
        def _():
            dq_ref[...] = jnp.zeros_like(dq_ref)

        dk_acc[...] = jnp.zeros_like(dk_acc)
        dv_acc[...] = jnp.zeros_like(dv_acc)

        def tile(i, q_off, q_n, k_n, masked):
            rows = pl.ds(pl.multiple_of(i * tk + q_off, half), q_n)
            q, dov = q_ref[rows, :], do_ref[rows, :]
            k, v = k_ref[0:k_n, :], v_ref[0:k_n, :]
            s = _dot_nt(q, k) * SCORE_SCALE_LOG2
            if masked:
                r = q_off + lax.broadcasted_iota(jnp.int32, s.shape, 0)
                c = lax.broadcasted_iota(jnp.int32, s.shape, 1)
                s = jnp.where(c <= r, s, NEG)
            p = jnp.exp2(s - lse_ref[0, rows, :])
            dv_acc[0:k_n, :] += _dot_tn(p, dov)
            dp = _dot_nt(dov, v)
            ds = p * (dp - dl_ref[0, rows, :]) * (QK_HEAD ** -0.5)
            dk_acc[0:k_n, :] += _dot_tn(ds, q)
            dq_ref[rows, :] += _dot(ds, k)

        def below_diagonal(i, carry):
            tile(i, 0, tk, tk, False)
            return carry

        tile(ki, 0, half, half, True)
        tile(ki, half, half, tk, True)
        lax.fori_loop(ki + 1, nq, below_diagonal, 0)
        dk_ref[...] = dk_acc[...]
        dv_ref[...] = dv_acc[...]

    in_specs = [pl.BlockSpec((S, QK_PAD), lambda h, j: (0, h), pipeline_mode=once), pl.BlockSpec((tk, QK_PAD), lambda h, j: (j, h)),
                pl.BlockSpec((tk, V_HEAD), lambda h, j: (j, h)), pl.BlockSpec((1, S, 1), lambda h, j: (h, 0, 0), pipeline_mode=once),
                pl.BlockSpec((1, S, 1), lambda h, j: (h, 0, 0), pipeline_mode=once),
                pl.BlockSpec((S, V_HEAD), lambda h, j: (0, h), pipeline_mode=once)]
    out_specs = [pl.BlockSpec((S, QK_PAD), lambda h, j: (0, h)), pl.BlockSpec((tk, QK_PAD), lambda h, j: (j, h)),
                 pl.BlockSpec((tk, V_HEAD), lambda h, j: (j, h))]
    out_shape = [jax.ShapeDtypeStruct((S, MLA_HEADS * QK_PAD), F32), jax.ShapeDtypeStruct((S, MLA_HEADS * QK_PAD), F32),
                 jax.ShapeDtypeStruct((S, BRANCH_W), F32)]
    scratch = [pltpu.VMEM((tk, QK_PAD), F32), pltpu.VMEM((tk, V_HEAD), F32)]
    ins = [Q, K, V, lse, delta, do]
    if comm is not None:
        return _hosted_call(comm, body, ins, in_specs, out_specs, out_shape, scratch, (MLA_HEADS, nq), name)
    res = pl.pallas_call(
        body, grid=(MLA_HEADS, nq), in_specs=in_specs, out_specs=out_specs, out_shape=out_shape, scratch_shapes=scratch, name=name,
        compiler_params=_params(("parallel", "arbitrary")),
    )(*ins)
    return list(res), []


def _merge_specs(S, D, offs, tm, tn, layer):
    g_cb = offs["g"] // tn
    nj = D // tn
    y_specs = [pl.BlockSpec((tm, BRANCH_W), lambda j, i: (i, 0)) for _ in range(4)]
    w_spec = pl.BlockSpec((None, 4, BRANCH_W, tn), lambda j, i: (layer, 0, 0, j))
    g_specs = [pl.BlockSpec((tm, tn), functools.partial(lambda j, i, b: (i, g_cb + b * nj + j), b=b)) for b in range(4)]
    bias_spec = pl.BlockSpec((4, tn), lambda j, i: (0, j))
    return y_specs, w_spec, g_specs, bias_spec


def _merge_fwd(ys, wb, layer, proj, offs, gate_b, name):
    S, D = ys[0].shape[0], wb.shape[3]
    tm, tn = _pick(S, 512, 16), _pick(D, 512, 128)
    y_specs, w_spec, g_specs, bias_spec = _merge_specs(S, D, offs, tm, tn, layer)

    def body(y0, y1, y2, y3, w_ref, g0, g1, g2, g3, b_ref, o_ref, t_ref):
        acc = None
        for b, (y_ref, g_ref) in enumerate(zip((y0, y1, y2, y3), (g0, g1, g2, g3))):
            gate = 1.0 / (1.0 + jnp.exp(-(g_ref[...] + b_ref[b:b + 1, :])))
            t = _dot(y_ref[...], w_ref[b])
            t_ref[b] = t.astype(BF16)
            term = gate * t
            acc = term if acc is None else acc + term
        o_ref[...] = acc.astype(BF16)

    return pl.pallas_call(
        body, grid=(D // tn, S // tm), in_specs=y_specs + [w_spec] + g_specs + [bias_spec],
        out_specs=[pl.BlockSpec((tm, tn), lambda j, i: (i, j)), pl.BlockSpec((4, tm, tn), lambda j, i: (0, i, j))],
        out_shape=[jax.ShapeDtypeStruct((S, D), BF16), jax.ShapeDtypeStruct((4, S, D), BF16)], name=name,
        compiler_params=_params(("parallel", "parallel")),
    )(*ys, wb, proj, proj, proj, proj, gate_b)


def _merge_bwd(t, proj, offs, gate_b, dmerged, name):
    _, S, D = t.shape
    tm, tn = _pick(S, 512, 16), _pick(D, 512, 128)
    nj = D // tn
    _, _, g_specs, bias_spec = _merge_specs(S, D, offs, tm, tn, 0)

    def body(t_ref, g0, g1, g2, g3, b_ref, dm_ref, dt_ref, dg0, dg1, dg2, dg3, db_ref):
        @pl.when(pl.program_id(1) == 0)
        def _():
            db_ref[...] = jnp.zeros_like(db_ref)

        dm = dm_ref[...]
        for b, (g_ref, dg_ref) in enumerate(zip((g0, g1, g2, g3), (dg0, dg1, dg2, dg3))):
            gate = 1.0 / (1.0 + jnp.exp(-(g_ref[...] + b_ref[b:b + 1, :])))
            dt_ref[b] = (dm * gate).astype(BF16)
            dgp = dm * t_ref[b].astype(F32) * gate * (1.0 - gate)
            dg_ref[...] = dgp.astype(BF16)
            db_ref[b:b + 1, :] += jnp.sum(dgp, axis=0, keepdims=True)

    stacked = pl.BlockSpec((4, tm, tn), lambda j, i: (0, i, j))
    tile = pl.BlockSpec((tm, tn), lambda j, i: (i, j))
    return pl.pallas_call(
        body, grid=(nj, S // tm), in_specs=[stacked] + g_specs + [bias_spec, tile],
        out_specs=[stacked] + [tile] * 4 + [pl.BlockSpec((4, tn), lambda j, i: (0, j))],
        out_shape=[jax.ShapeDtypeStruct((4, S, D), BF16)] + [jax.ShapeDtypeStruct((S, D), BF16)] * 4 + [jax.ShapeDtypeStruct((4, D), F32)],
        name=name, compiler_params=_params(("parallel", "arbitrary")),
    )(t, proj, proj, proj, proj, gate_b, dmerged)


HBM_SPEC = pl.BlockSpec(memory_space=pl.ANY)


def _mesh_place():
    x, y, c = lax.axis_index("x"), lax.axis_index("y"), lax.axis_index("c")
    return x, y, c, [(1 - x, y), (x, 1 - y), (1 - x, 1 - y)]


def _remote(src, dst, send_sem, recv_sem, to):
    return pltpu.make_async_remote_copy(src_ref=src, dst_ref=dst, send_sem=send_sem, recv_sem=recv_sem, device_id=to,
                                        device_id_type=pl.DeviceIdType.MESH)


class _GatherPlan:
    def __init__(self, arrs):
        self.arrs, self.n = list(arrs), len(arrs)
        self.out_shape = [jax.ShapeDtypeStruct((N_DEV,) + a.shape, a.dtype) for a in arrs]
        self.scratch = [pltpu.SemaphoreType.DMA((self.n, 7)), pltpu.SemaphoreType.DMA((self.n, 7)), pltpu.SemaphoreType.DMA((self.n,))]

    def _copies(self, ins, dsts, sems):
        send_sems, recv_sems, local_sems = sems
        x, y, c, chips = _mesh_place()
        slot = lambda px, py, pc: 4 * px + 2 * py + pc
        me, sibling = slot(x, y, c), (x, y, 1 - c)

        def copy(a, k, src, block, to):
            return lambda: _remote(src(), dsts[a].at[block], send_sems.at[a, k], recv_sems.at[a, k], to)

        rng = range(self.n)
        shard = lambda a: (lambda: ins[a])
        local = [(lambda a=a: pltpu.make_async_copy(ins[a], dsts[a].at[me], local_sems.at[a])) for a in rng]
        own = [copy(a, 0, shard(a), me, sibling) for a in rng]
        own += [copy(a, 1 + j, shard(a), me, (*chip, c)) for a in rng for j, chip in enumerate(chips)]
        landed = [copy(a, 1 + j, shard(a), slot(*chip, c), sibling) for a in rng for j, chip in enumerate(chips)]
        passed = [copy(a, 4 + j, (lambda a=a, chip=chip: dsts[a].at[slot(*chip, c)]), slot(*chip, c), sibling)
                  for a in rng for j, chip in enumerate(chips)]
        from_sibling = [copy(a, 0, shard(a), slot(x, y, 1 - c), sibling) for a in rng]
        from_sibling += [copy(a, 4 + j, shard(a), slot(*chip, 1 - c), sibling) for a in rng for j, chip in enumerate(chips)]
        return local, own, landed, passed, from_sibling

    def start(self, ins, dsts, sems):
        local, own, _, _, _ = self._copies(ins, dsts, sems)
        for make in local + own:
            make().start()

    def relay(self, ins, dsts, sems):
        _, _, landed, passed, _ = self._copies(ins, dsts, sems)
        for arrived, onward in zip(landed, passed):
            arrived().wait_recv()
            onward().start()

    def finish(self, ins, dsts, sems):
        local, own, _, passed, from_sibling = self._copies(ins, dsts, sems)
        for make in from_sibling:
            make().wait_recv()
        for make in own + passed:
            make().wait_send()
        for make in local:
            make().wait()


class _ChipExchangePlan:
    def __init__(self, arrs):
        self.arrs, self.n = list(arrs), len(arrs)
        self.out_shape = [jax.ShapeDtypeStruct(a.shape, a.dtype) for a in arrs]
        self.scratch = [pltpu.SemaphoreType.DMA((self.n, 3)), pltpu.SemaphoreType.DMA((self.n, 3)), pltpu.SemaphoreType.DMA((self.n,))]

    def _copies(self, ins, dsts, sems):
        send_sems, recv_sems, local_sems = sems
        x, y, c, chips = _mesh_place()
        mine = 2 * x + y
        local = [pltpu.make_async_copy(ins[a].at[mine], dsts[a].at[mine], local_sems.at[a]) for a in range(self.n)]
        remote = [_remote(ins[a].at[2 * px + py], dsts[a].at[mine], send_sems.at[a, j], recv_sems.at[a, j], (px, py, c))
                  for a in range(self.n) for j, (px, py) in enumerate(chips)]
        return local, remote

    def start(self, ins, dsts, sems):
        local, remote = self._copies(ins, dsts, sems)
        for cp in local + remote:
            cp.start()

    def relay(self, ins, dsts, sems):
        pass

    def finish(self, ins, dsts, sems):
        local, remote = self._copies(ins, dsts, sems)
        for cp in remote + local:
            cp.wait()


def _run_plan(plan, name):
    n = plan.n

    def body(*refs):
        ins, dsts, sems = refs[:n], refs[n:2 * n], refs[2 * n:]
        plan.start(ins, dsts, sems)
        plan.relay(ins, dsts, sems)
        plan.finish(ins, dsts, sems)

    res = pl.pallas_call(body, in_specs=[HBM_SPEC] * n, out_specs=[HBM_SPEC] * n, out_shape=plan.out_shape,
                         scratch_shapes=plan.scratch, name=name)(*plan.arrs)
    return list(res)


def _sibling_swap(arrs, name):
    n = len(arrs)

    def body(*refs):
        ins, dsts = refs[:n], refs[n:2 * n]
        send_sems, recv_sems = refs[2 * n:]
        x, y, c, _ = _mesh_place()
        cps = [_remote(ins[a].at[q, 1 - c], dsts[a].at[q], send_sems.at[a, q], recv_sems.at[a, q], (x, y, 1 - c))
               for a in range(n) for q in range(4)]
        for cp in cps:
            cp.start()
        for cp in cps:
            cp.wait()

    res = pl.pallas_call(
        body, in_specs=[HBM_SPEC] * n, out_specs=[HBM_SPEC] * n,
        out_shape=[jax.ShapeDtypeStruct((4,) + a.shape[2:], a.dtype) for a in arrs],
        scratch_shapes=[pltpu.SemaphoreType.DMA((n, 4)), pltpu.SemaphoreType.DMA((n, 4))], name=name,
    )(*arrs)
    return list(res)


def _pair_sum(staged, theirs, core, name):
    _, _, R, C = staged.shape
    tr = _pick(R, max(16, ((1 << 19) // C) // 16 * 16), 16)

    def body(c_ref, a_ref, b_ref, o_ref):
        o_ref[...] = (a_ref[...].astype(F32) + b_ref[...].astype(F32)).astype(BF16)

    blk = pl.BlockSpec((None, tr, C), lambda q, i, c: (q, i, 0))
    grid_spec = pltpu.PrefetchScalarGridSpec(
        num_scalar_prefetch=1, grid=(4, R // tr),
        in_specs=[pl.BlockSpec((None, None, tr, C), lambda q, i, c: (q, c[0], i, 0)), blk], out_specs=blk)
    return pl.pallas_call(
        body, grid_spec=grid_spec, out_shape=jax.ShapeDtypeStruct((4, R, C), BF16), name=name,
        compiler_params=_params(("parallel", "parallel")),
    )(core, staged, theirs)


def _adamw(parts, w, m, v, name):
    R, C = w.shape
    n_parts = parts[0].shape[0]
    rows = [p.shape[1] for p in parts]
    assert sum(rows) == R
    tr = _pick(min(rows), max(16, ((1 << 18) // C) // 16 * 16), 16)
    assert all(r % tr == 0 for r in rows)
    bounds, lo = [], 0
    for r in rows:
        bounds.append((lo, lo + r // tr))
        lo += r // tr
    n_lists = len(parts)

    def body(*refs):
        p_refs = refs[:n_lists]
        w_ref, m_ref, v_ref, g_ref, d_ref, nm_ref, nv_ref = refs[n_lists:]
        i = pl.program_id(0)

        def update(p_ref):
            g = p_ref[0].astype(F32)
            for s in range(1, n_parts):
                g = g + p_ref[s].astype(F32)
            mn = ADAM_B1 * m_ref[...] + (1.0 - ADAM_B1) * g
            vn = ADAM_B2 * v_ref[...] + (1.0 - ADAM_B2) * (g * g)
            m_hat = mn / (1.0 - ADAM_B1 ** ADAM_STEP)
            v_hat = vn / (1.0 - ADAM_B2 ** ADAM_STEP)
            g_ref[...] = g
            d_ref[...] = -ADAM_LR * (m_hat / (jnp.sqrt(v_hat) + ADAM_EPS) + ADAM_WD * w_ref[...])
            nm_ref[...] = mn
            nv_ref[...] = vn

        if n_lists == 1:
            update(p_refs[0])
        else:
            for p_ref, (first, end) in zip(p_refs, bounds):
                pl.when((i >= first) & (i < end))(functools.partial(update, p_ref))

    blk = pl.BlockSpec((tr, C), lambda i: (i, 0))
    p_specs = [pl.BlockSpec((n_parts, tr, C), functools.partial(lambda i, first, last: (0, jnp.clip(i - first, 0, last), 0),
                                                                first=first, last=end - first - 1)) for first, end in bounds]
    return pl.pallas_call(
        body, grid=(R // tr,), in_specs=p_specs + [blk, blk, blk],
        out_specs=[blk] * 4, out_shape=[jax.ShapeDtypeStruct((R, C), F32)] * 4, name=name,
        compiler_params=_params(("parallel",)),
    )(*parts, w, m, v)


def _offsets(L, D):
    o = {"pv": 0, "pz": BRANCH_W, "cq": 2 * BRANCH_W, "ckv": 2 * BRANCH_W + L, "mz": 2 * BRANCH_W + 2 * L}
    for prev, nxt in (("mz", "cb"), ("cb", "cc"), ("cc", "cx"), ("cx", "cz"), ("cz", "xq"), ("xq", "xz"), ("xz", "g")):
        o[nxt] = o[prev] + BRANCH_W
    o["kr"] = o["g"] + 4 * D
    o["n"] = o["kr"] + 128
    return o


def _unshard(gathered, axis):
    moved = jnp.moveaxis(gathered, 0, axis)
    shape = list(moved.shape)
    shape[axis:axis + 2] = [shape[axis] * shape[axis + 1]]
    return moved.reshape(shape)


def _reshard(full, axis):
    shape = list(full.shape)
    shape[axis:axis + 1] = [N_DEV, shape[axis] // N_DEV]
    return jnp.moveaxis(full.reshape(shape), axis, 0)


def _pad_heads(g, width):
    return jnp.pad(g, ((0, 0), (0, width - g.shape[1])))


def kernel(x, mem, positions, norm_g, w_in, gate_b, pool_w, pool_scale, q_a_norm_g, kv_a_norm_g, w_uq, w_ukv, mla_q_norm_g, mla_k_norm_g, conv_w, mem_norm_g, w_mem_kv, xattn_q_norm_g, xattn_k_norm_g, w_branch, w_out, loss_target, m_norm_g, m_w_in, m_gate_b, m_pool_w, m_pool_scale, m_q_a_norm_g, m_kv_a_norm_g, m_w_uq, m_w_ukv, m_mla_q_norm_g, m_mla_k_norm_g, m_conv_w, m_mem_norm_g, m_w_mem_kv, m_xattn_q_norm_g, m_xattn_k_norm_g, m_w_branch, m_w_out, v_norm_g, v_w_in, v_gate_b, v_pool_w, v_pool_scale, v_q_a_norm_g, v_kv_a_norm_g, v_w_uq, v_w_ukv, v_mla_q_norm_g, v_mla_k_norm_g, v_conv_w, v_mem_norm_g, v_w_mem_kv, v_xattn_q_norm_g, v_xattn_k_norm_g, v_w_branch, v_w_out):
    depth = norm_g.shape[0]
    S, D = x.shape[1], x.shape[2]
    L = w_uq.shape[1]
    offs = _offsets(L, D)
    seg_a = 2 * BRANCH_W + 2 * L
    xs = x.reshape(S, D)
    mems = mem.reshape(mem.shape[1], D)

    sharded = dict(w_in=w_in, pool_w=pool_w, w_uq=w_uq, w_ukv=w_ukv, conv_w=conv_w, w_mem_kv=w_mem_kv, w_branch=w_branch, w_out=w_out)
    shard_axis = dict(w_in=2, pool_w=2, w_uq=2, w_ukv=2, conv_w=2, w_mem_kv=1, w_branch=3, w_out=1)
    def permuted_w_in(g):
        f = _unshard(g, 1)
        return jnp.concatenate([f[:, :seg_a], f[:, seg_a + QK_ROPE:], f[:, seg_a:seg_a + QK_ROPE], jnp.zeros((D, 128 - QK_ROPE), BF16)], axis=-1)

    w_in_local = [w_in[l].astype(BF16) for l in range(depth)]
    w_in_p = [permuted_w_in(_run_plan(_GatherPlan([w_in_local[0]]), "gather_w_in0")[0])] + [None] * (depth - 1)
    rest_keys = [k for k in shard_axis if k != "w_in"]
    rest_plan = _GatherPlan([sharded[k] if k == "conv_w" else sharded[k].astype(BF16) for k in rest_keys])
    later_w_in_plan = _GatherPlan(w_in_local[1:]) if depth > 1 else None

    inv = ROPE_THETA ** (-jnp.arange(0, QK_ROPE, 2, dtype=F32) / QK_ROPE)
    ang = positions.reshape(S).astype(F32)[:, None] * inv
    cos, sin, z = jnp.cos(ang), jnp.sin(ang), jnp.zeros_like(ang)
    tabs = (jnp.concatenate([cos, cos, z, z], axis=1), jnp.concatenate([-sin, z, z, z], axis=1), jnp.concatenate([z, sin, z, z], axis=1))

    row = lambda a, l: a[l][None, :]
    saved = []
    cur = xs
    for l in range(depth):
        h, hT = _rms_rows(cur, D, 0, row(norm_g, l), f"norm_fwd{l}", with_transpose=True)
        proj_tiles = dict(tm=512, tn=3712, tk=2048, n_outer=True)
        if l == 0:
            proj, carried = _matmul(h, w_in_p[0], "nn", F32, "proj_fwd0", comm=rest_plan, **proj_tiles)
            gathered = {k: _unshard(g, shard_axis[k]) for k, g in zip(rest_keys, carried)}
            conv_full = gathered["conv_w"]
            pool_w_f = gathered["pool_w"]
            w_uq_f = gathered["w_uq"].reshape(depth, L, MLA_HEADS, QK_HEAD)
            w_uq_p = jnp.pad(w_uq_f, ((0, 0), (0, 0), (0, 0), (0, QK_PAD - QK_HEAD))).reshape(depth, L, MLA_HEADS * QK_PAD)
            w_ukv_f = gathered["w_ukv"].reshape(depth, L, MLA_HEADS, QK_NOPE + V_HEAD)
            w_ukv_p = jnp.concatenate([w_ukv_f[..., :QK_NOPE].reshape(depth, L, -1), w_ukv_f[..., QK_NOPE:].reshape(depth, L, -1)], axis=-1)
            w_mem_f = gathered["w_mem_kv"]
            w_br_f = gathered["w_branch"]
            w_out_f = gathered["w_out"]
        else:
            proj = _matmul(h, w_in_p[l], "nn", F32, f"proj_fwd{l}", **proj_tiles)
        y_pool = _pool_fwd(proj, offs, pool_w_f[l], row(pool_scale, l), f"pool_fwd{l}")
        cqn = _rms_rows(proj, L, offs["cq"], row(q_a_norm_g, l), f"cq_norm{l}")
        ckvn = _rms_rows(proj, L, offs["ckv"], row(kv_a_norm_g, l), f"ckv_norm{l}")
        qp = _matmul(cqn, w_uq_p[l], "nn", F32, f"uq_fwd{l}")
        kvp = _matmul(ckvn, w_ukv_p[l], "nn", F32, f"ukv_fwd{l}")
        qg = _pad_heads(row(mla_q_norm_g, l), QK_PAD)
        kg = _pad_heads(row(mla_k_norm_g, l), QK_PAD)
        Q, K, V = _mla_qk_fwd(qp, kvp, proj, offs, tabs, qg, kg, f"mla_qk_fwd{l}")
        (o, y_mla, lse), carried = _flash_fwd(Q, K, V, proj, offs, f"flash_fwd{l}", comm=later_w_in_plan if l == 0 else None)
        if l == 0 and later_w_in_plan is not None:
            w_in_p[1:] = [permuted_w_in(g) for g in carried]
        y_conv = _conv_fwd(proj, offs, conv_full[l], f"conv_fwd{l}")
        memn = _rms_rows(mems, D, 0, row(mem_norm_g, l), f"mem_norm{l}")
        mem_kv = _matmul(memn, w_mem_f, "nn", F32, f"mem_kv{l}", b_idx=(l,))
        y_mem = _xattn_fwd(proj, offs, mem_kv, row(xattn_q_norm_g, l), row(xattn_k_norm_g, l), f"xattn_fwd{l}")
        ys = (y_pool, y_mla, y_conv, y_mem)
        gb = gate_b[l].reshape(4, D)
        merged, tproj = _merge_fwd(ys, w_br_f, l, proj, offs, gb, f"merge_fwd{l}")
        nxt = _matmul(merged, w_out_f, "nn", F32, f"out_fwd{l}", res=cur, b_idx=(l,))
        saved.append(dict(x=cur, hT=hT, proj=proj, cqn=cqn, ckvn=ckvn, qp=qp, kvp=kvp, qg=qg, kg=kg, Q=Q, K=K, V=V, o=o, lse=lse,
                          memn=memn, mem_kv=mem_kv, ys=ys, gb=gb, merged=merged, tproj=tproj))
        cur = nxt

    gx, loss_acc = _loss_rows(cur, loss_target.reshape(S, D), "loss")
    loss = lax.psum(0.5 * loss_acc[0, 0], MESH_AXES)

    grads = {k: [None] * depth for k in ("norm_g", "w_in", "gate_b", "pool_w", "pool_scale", "q_a_norm_g", "kv_a_norm_g", "w_uq", "w_ukv",
                                         "mla_q_norm_g", "mla_k_norm_g", "conv_w", "mem_norm_g", "w_mem_kv", "xattn_q_norm_g",
                                         "xattn_k_norm_g", "w_branch", "w_out")}
    names_sh = list(shard_axis)
    core = lax.axis_index("c").astype(jnp.int32).reshape(1)
    received = [None] * depth
    pending = None

    def pair_sums(one_layer_grads, names, tag):
        staged = [_reshard(g, shard_axis[k] - 1).reshape(4, 2, -1, sharded[k].shape[-1]).astype(BF16) for g, k in zip(one_layer_grads, names)]
        given = _sibling_swap(staged, f"pair_swap_{tag}")
        return [_pair_sum(st, theirs, core, f"pair_sum_{k}_{tag}") for k, st, theirs in zip(names, staged, given)]

    for l in reversed(range(depth)):
        sv = saved[l]
        proj, ys = sv["proj"], sv["ys"]
        dmerged = _matmul(gx, w_out_f, "nt", F32, f"out_bwd_x{l}", b_idx=(l,))
        grads["w_out"][l] = _matmul(sv["merged"], gx, "tn", BF16, f"out_bwd_w{l}")
        dt, dgp0, dgp1, dgp2, dgp3, dgb = _merge_bwd(sv["tproj"], proj, offs, sv["gb"], dmerged, f"merge_bwd{l}")
        grads["gate_b"][l] = dgb.reshape(4 * D)
        dys, dwb = [], []
        for b in range(4):
            dys.append(_matmul(dt, w_br_f, "nt", F32, f"branch_bwd_x{l}_{b}", a_idx=(b,), b_idx=(l, b)))
            dwb.append(_matmul(ys[b], dt, "tn", BF16, f"branch_bwd_w{l}_{b}", b_idx=(b,)))
        grads["w_branch"][l] = jnp.stack(dwb)
        d_pz, e, d_pw, d_ps = _pool_bwd_a(proj, offs, pool_w_f[l], row(pool_scale, l), dys[0], f"pool_bwd_a{l}")
        d_pv = _pool_bwd_b(e, f"pool_bwd_b{l}")
        grads["pool_w"][l], grads["pool_scale"][l] = d_pw, d_ps[0]
        do, d_mz, delta = _mla_gate_bwd(sv["o"], proj, offs, dys[1], f"mla_gate_bwd{l}")
        (dQ, dK, dV), carried = _flash_bwd(sv["Q"], sv["K"], sv["V"], sv["lse"], delta, do, f"flash_bwd{l}",
                                           comm=None if pending is None else pending[1])
        if pending is not None:
            received[pending[0]][0], pending = carried[0], None
        dqp, dkvp, d_kr, dqg, dkg = _mla_qk_bwd(sv["qp"], sv["kvp"], proj, offs, tabs, sv["qg"], sv["kg"], dQ, dK, dV, f"mla_qk_bwd{l}")
        grads["mla_q_norm_g"][l], grads["mla_k_norm_g"][l] = dqg[0, :QK_HEAD], dkg[0, :QK_HEAD]
        dcqn = _matmul(dqp, w_uq_p[l], "nt", F32, f"uq_bwd_x{l}")
        d_wuq = _matmul(sv["cqn"], dqp, "tn", BF16, f"uq_bwd_w{l}")
        dckvn = _matmul(dkvp, w_ukv_p[l], "nt", F32, f"ukv_bwd_x{l}")
        d_wukv = _matmul(sv["ckvn"], dkvp, "tn", BF16, f"ukv_bwd_w{l}")
        grads["w_uq"][l] = d_wuq.reshape(L, MLA_HEADS, QK_PAD)[..., :QK_HEAD].reshape(L, MLA_HEADS * QK_HEAD)
        grads["w_ukv"][l] = jnp.concatenate([d_wukv[:, :MLA_HEADS * QK_NOPE].reshape(L, MLA_HEADS, QK_NOPE),
                                             d_wukv[:, MLA_HEADS * QK_NOPE:].reshape(L, MLA_HEADS, V_HEAD)], axis=-1).reshape(L, -1)
        d_cq, dqa = _rms_rows_bwd(proj, L, offs["cq"], row(q_a_norm_g, l), dcqn, BF16, f"cq_norm_bwd{l}")
        d_ckv, dkva = _rms_rows_bwd(proj, L, offs["ckv"], row(kv_a_norm_g, l), dckvn, BF16, f"ckv_norm_bwd{l}")
        grads["q_a_norm_g"][l], grads["kv_a_norm_g"][l] = dqa[0], dkva[0]
        d_cb, d_cc, d_cx, d_cz, d_cw = _conv_bwd(proj, offs, conv_full[l], dys[2], f"conv_bwd{l}")
        grads["conv_w"][l] = d_cw
        d_xq, d_xz, d_memkv, dxqg, dxkg = _xattn_bwd(proj, offs, sv["mem_kv"], row(xattn_q_norm_g, l), row(xattn_k_norm_g, l), dys[3], f"xattn_bwd{l}")
        grads["xattn_q_norm_g"][l], grads["xattn_k_norm_g"][l] = dxqg[0], dxkg[0]
        grads["w_mem_kv"][l] = _matmul(sv["memn"], d_memkv, "tn", BF16, f"mem_kv_bwd_w{l}")
        dmemn = _matmul(d_memkv, w_mem_f, "nt", F32, f"mem_kv_bwd_x{l}", b_idx=(l,))
        grads["mem_norm_g"][l] = _norm_g_grad(mems, row(mem_norm_g, l), dmemn, f"mem_norm_bwd{l}")[0]
        dproj = jnp.concatenate([d_pv, d_pz, d_cq, d_ckv, d_mz, d_cb, d_cc, d_cx, d_cz, d_xq, d_xz, dgp0, dgp1, dgp2, dgp3, d_kr], axis=1)
        early_names = [k for k in names_sh if k != "w_in"]
        early = pair_sums([grads[k][l] for k in early_names], early_names, f"early{l}")
        dh, early_received = _matmul(dproj, w_in_p[l], "nt", F32, f"proj_bwd_x{l}", tm=1024, tn=1024, tk=3712, comm=_ChipExchangePlan(early))
        n_p = offs["n"]
        w_tiles = dict(tm=512, tn=3712, tk=2048, n_outer=True)
        unpermute = lambda d: jnp.concatenate([d[:, :seg_a], d[:, n_p - 128:n_p - 128 + QK_ROPE], d[:, seg_a:n_p - 128]], axis=1)
        received[l] = [None] + early_received
        if l > 0:
            d_win = unpermute(_matmul(sv["hT"], dproj, "nn", BF16, f"proj_bwd_w{l}", **w_tiles))
            pending = (l, _ChipExchangePlan(pair_sums([d_win], ["w_in"], f"w_in{l}")))
        else:
            half = D // 2
            top = unpermute(_matmul(sv["hT"][:half], dproj, "nn", BF16, "proj_bwd_w0_top", **w_tiles))
            top_plan = _ChipExchangePlan(pair_sums([top], ["w_in"], "w_in0_top"))
            bottom, top_received = _matmul(sv["hT"][half:], dproj, "nn", BF16, "proj_bwd_w0_bottom", comm=top_plan, **w_tiles)
            bottom_plan = _ChipExchangePlan(pair_sums([unpermute(bottom)], ["w_in"], "w_in0_bottom"))
            received[0][0] = [top_received[0], _run_plan(bottom_plan, "chip_exchange_w_in0_bottom")[0]]
        gx, dng = _rms_rows_bwd(sv["x"], D, 0, row(norm_g, l), dh, F32, f"norm_bwd{l}", res=gx)
        grads["norm_g"][l] = dng[0]
    grad_x = gx.reshape(1, S, D)
    full = {k: jnp.stack(v) for k, v in grads.items() if k not in shard_axis}

    moments = dict(w_in=(m_w_in, v_w_in), pool_w=(m_pool_w, v_pool_w), w_uq=(m_w_uq, v_w_uq), w_ukv=(m_w_ukv, v_w_ukv),
                   conv_w=(m_conv_w, v_conv_w), w_mem_kv=(m_w_mem_kv, v_w_mem_kv), w_branch=(m_w_branch, v_w_branch), w_out=(m_w_out, v_w_out))
    results = {}
    for ki, k in enumerate(names_sh):
        w = sharded[k]
        m, v = moments[k]
        flat = lambda a: a.reshape(-1, w.shape[-1])
        parts = [p for l in range(depth) for p in (received[l][ki] if isinstance(received[l][ki], list) else [received[l][ki]])]
        if any(p.shape[1] % 16 for p in parts):
            parts = [jnp.concatenate(parts, axis=1)]
        outs = _adamw(parts, flat(w), flat(m), flat(v), f"adamw_{k}")
        results[k] = [o.reshape(w.shape) for o in outs]

    small = dict(norm_g=(norm_g, m_norm_g, v_norm_g), gate_b=(gate_b, m_gate_b, v_gate_b), pool_scale=(pool_scale, m_pool_scale, v_pool_scale),
                 q_a_norm_g=(q_a_norm_g, m_q_a_norm_g, v_q_a_norm_g), kv_a_norm_g=(kv_a_norm_g, m_kv_a_norm_g, v_kv_a_norm_g),
                 mla_q_norm_g=(mla_q_norm_g, m_mla_q_norm_g, v_mla_q_norm_g), mla_k_norm_g=(mla_k_norm_g, m_mla_k_norm_g, v_mla_k_norm_g),
                 mem_norm_g=(mem_norm_g, m_mem_norm_g, v_mem_norm_g), xattn_q_norm_g=(xattn_q_norm_g, m_xattn_q_norm_g, v_xattn_q_norm_g),
                 xattn_k_norm_g=(xattn_k_norm_g, m_xattn_k_norm_g, v_xattn_k_norm_g))
    names_sm = list(small)
    sizes = [small[k][0].size for k in names_sm]
    total = sum(sizes)
    rows = -(-total // (8 * 128)) * 8

    def pack(arrs):
        flat = jnp.concatenate([a.reshape(-1) for a in arrs])
        return jnp.pad(flat, (0, rows * 128 - total)).reshape(rows, 128)

    g_small = _run_plan(_GatherPlan([pack([full[k] for k in names_sm])]), "gather_small_grads")[0]
    outs = _adamw([g_small], pack([small[k][0] for k in names_sm]), pack([small[k][1] for k in names_sm]),
                  pack([small[k][2] for k in names_sm]), "adamw_small")
    for kind, o in enumerate(outs):
        flat, pos = o.reshape(-1), 0
        for k, n in zip(names_sm, sizes):
            results.setdefault(k, [None] * 4)[kind] = flat[pos:pos + n].reshape(small[k][0].shape)
            pos += n

    order = ["norm_g", "w_in", "gate_b", "pool_w", "pool_scale", "q_a_norm_g", "kv_a_norm_g", "w_uq", "w_ukv", "mla_q_norm_g",
             "mla_k_norm_g", "conv_w", "mem_norm_g", "w_mem_kv", "xattn_q_norm_g", "xattn_k_norm_g", "w_branch", "w_out"]
    return (loss, grad_x, *[results[k][0] for k in order], *[results[k][1] for k in order],
            *[results[k][2] for k in order], *[results[k][3] for k in order])
```

```python
import functools

import jax
import jax.numpy as jnp
from jax import lax
from jax.experimental import pallas as pl
from jax.experimental.pallas import tpu as pltpu

F32 = jnp.float32
BF16 = jnp.bfloat16
MESH_AXES = ("x", "y", "c")
N_DEV = 8

BRANCH_W = 1024
POOL_WINDOWS = (2, 4, 8, 16)
POOL_GW = BRANCH_W // 4
MLA_HEADS = 8
QK_NOPE = 128
QK_ROPE = 64
QK_HEAD = QK_NOPE + QK_ROPE
QK_PAD = 256
V_HEAD = 128
XATTN_HEADS = 4
XATTN_HD = BRANCH_W // XATTN_HEADS
ROPE_THETA = 10000.0
EPS = 1e-6
ADAM_LR = 0.001
ADAM_B1 = 0.9
ADAM_B2 = 0.999
ADAM_EPS = 1e-08
ADAM_WD = 0.01
ADAM_STEP = 10
HALO = 16
NEG = -1e30
SCORE_SCALE_LOG2 = (QK_HEAD ** -0.5) * 1.4426950408889634
VMEM_LIMIT = 56 << 20


def _pick(n, pref, align):
    if n <= pref:
        return n
    t = (pref // align) * align
    while t >= align:
        if n % t == 0:
            return t
        t -= align
    return n


def _params(sem):
    return pltpu.CompilerParams(dimension_semantics=sem, vmem_limit_bytes=VMEM_LIMIT)


def _silu(z):
    return z * (1.0 / (1.0 + jnp.exp(-z)))


def _silu_and_grad(z):
    s = 1.0 / (1.0 + jnp.exp(-z))
    return z * s, s * (1.0 + z * (1.0 - s))


def _dot(a, b):
    return jnp.dot(a.astype(BF16), b.astype(BF16), preferred_element_type=F32)


def _dot_nt(a, b):
    return lax.dot_general(a.astype(BF16), b.astype(BF16), (((1,), (1,)), ((), ())), preferred_element_type=F32)


def _dot_tn(a, b):
    return lax.dot_general(a.astype(BF16), b.astype(BF16), (((0,), (0,)), ((), ())), preferred_element_type=F32)


def _rms(x, g, n):
    r = lax.rsqrt(jnp.sum(x * x, axis=-1, keepdims=True) / n + EPS)
    return x * r * g, r


def _rms_bwd(x, g, r, dy, n):
    dyg = dy * g
    dx = r * dyg - x * (r * r * r) * (jnp.sum(dyg * x, axis=-1, keepdims=True) / n)
    dg = jnp.sum(dy * x * r, axis=0, keepdims=True)
    return dx, dg


def _col(tm, w, off):
    assert off % w == 0
    cb = off // w
    return pl.BlockSpec((tm, w), lambda i: (i, cb))


def _whole(shape):
    nd = len(shape)
    return pl.BlockSpec(shape, lambda *_: (0,) * nd)


def _lead_spec(block, fn, idx):
    if not idx:
        return pl.BlockSpec(block, fn)
    return pl.BlockSpec((None,) * len(idx) + block, lambda i, j, k: tuple(idx) + fn(i, j, k))


def _hosted_call(plan, body, ins, in_specs, out_specs, out_shape, scratch, grid, name):
    n_in, n_out, n_scr, n = len(ins), len(out_specs), len(scratch), plan.n
    n_steps = 1
    for g in grid:
        n_steps *= g

    def wrapped(*refs):
        a, c_in = refs[:n_in], refs[n_in:n_in + n]
        o, c_out = refs[n_in + n:n_in + n + n_out], refs[n_in + n + n_out:n_in + 2 * n + n_out]
        rest = refs[n_in + 2 * n + n_out:]
        scr, sems = rest[:n_scr], rest[n_scr:]
        step = 0
        for d, g in enumerate(grid):
            step = step * g + pl.program_id(d)

        @pl.when(step == 0)
        def _():
            plan.start(c_in, c_out, sems)

        body(*a, *o, *scr)

        @pl.when(step == n_steps // 2)
        def _():
            plan.relay(c_in, c_out, sems)

        @pl.when(step == n_steps - 1)
        def _():
            plan.finish(c_in, c_out, sems)

    res = pl.pallas_call(
        wrapped, grid=grid, in_specs=list(in_specs) + [HBM_SPEC] * n, out_specs=list(out_specs) + [HBM_SPEC] * n,
        out_shape=list(out_shape) + plan.out_shape, scratch_shapes=list(scratch) + plan.scratch, name=name,
        compiler_params=_params(("arbitrary",) * len(grid)),
    )(*ins, *plan.arrs)
    return list(res[:n_out]), list(res[n_out:])


def _matmul(a, b, mode, out_dtype, name, tm=1024, tn=1024, tk=2048, res=None, a_idx=(), b_idx=(), n_outer=False, comm=None):
    a_shape, b_shape = a.shape[len(a_idx):], b.shape[len(b_idx):]
    if mode == "nn":
        (M, K), N = a_shape, b_shape[1]
    elif mode == "nt":
        (M, K), N = a_shape, b_shape[0]
    else:
        (K, M), N = a_shape, b_shape[1]
    tm, tn, tk = _pick(M, tm, 128), _pick(N, tn, 128), _pick(K, tk, 128)
    nk = K // tk
    order = (lambda fn: (lambda j, i, k: fn(i, j, k))) if n_outer else (lambda fn: fn)
    if mode == "nn":
        a_spec = _lead_spec((tm, tk), order(lambda i, j, k: (i, k)), a_idx)
        b_spec = _lead_spec((tk, tn), order(lambda i, j, k: (k, j)), b_idx)
        op = _dot
    elif mode == "nt":
        a_spec = _lead_spec((tm, tk), order(lambda i, j, k: (i, k)), a_idx)
        b_spec = _lead_spec((tn, tk), order(lambda i, j, k: (j, k)), b_idx)
        op = _dot_nt
    else:
        a_spec = _lead_spec((tk, tm), order(lambda i, j, k: (k, i)), a_idx)
        b_spec = _lead_spec((tk, tn), order(lambda i, j, k: (k, j)), b_idx)
        op = _dot_tn
    o_spec = pl.BlockSpec((tm, tn), order(lambda i, j, k: (i, j)))
    has_res = res is not None

    def body(*refs):
        a_ref, b_ref = refs[0], refs[1]
        res_ref = refs[2] if has_res else None
        o_ref = refs[2 + has_res]
        part = op(a_ref[...], b_ref[...])

        def finish(acc):
            if has_res:
                acc = res_ref[...] + acc
            o_ref[...] = acc.astype(out_dtype)

        if nk == 1:
            finish(part)
        else:
            acc_ref = refs[3 + has_res]
            k = pl.program_id(2)

            @pl.when(k == 0)
            def _():
                acc_ref[...] = part

            @pl.when(k > 0)
            def _():
                acc_ref[...] += part

            @pl.when(k == nk - 1)
            def _():
                finish(acc_ref[...])

    ins = [a, b] + ([res] if has_res else [])
    in_specs = [a_spec, b_spec] + ([o_spec] if has_res else [])
    scratch = [pltpu.VMEM((tm, tn), F32)] if nk > 1 else []
    grid = (N // tn, M // tm, nk) if n_outer else (M // tm, N // tn, nk)
    out_shape = jax.ShapeDtypeStruct((M, N), out_dtype)
    if comm is not None:
        (out,), carried = _hosted_call(comm, body, ins, in_specs, [o_spec], [out_shape], scratch, grid, name)
        return out, carried
    return pl.pallas_call(
        body, grid=grid, in_specs=in_specs, out_specs=o_spec, out_shape=out_shape, scratch_shapes=scratch, name=name,
        compiler_params=_params(("parallel", "parallel", "arbitrary")),
    )(*ins)


def _rms_rows(x, width, off, g, name, with_transpose=False):
    S = x.shape[0]
    tm = _pick(S, 512, 128 if with_transpose else 16)

    def body(x_ref, g_ref, o_ref, *t_ref):
        y, _ = _rms(x_ref[...], g_ref[...], width)
        o_ref[...] = y.astype(BF16)
        if with_transpose:
            t_ref[0][...] = y.T.astype(BF16)

    out_specs = [_col(tm, width, 0)] + ([pl.BlockSpec((width, tm), lambda i: (0, i))] if with_transpose else [])
    out_shape = [jax.ShapeDtypeStruct((S, width), BF16)] + ([jax.ShapeDtypeStruct((width, S), BF16)] if with_transpose else [])
    res = pl.pallas_call(
        body, grid=(S // tm,), in_specs=[_col(tm, width, off), _whole((1, width))], out_specs=out_specs,
        out_shape=out_shape, name=name, compiler_params=_params(("parallel",)),
    )(x, g)
    return res if with_transpose else res[0]


def _rms_rows_bwd(x, width, off, g, dy, out_dtype, name, res=None):
    S = x.shape[0]
    tm = _pick(S, 256, 16)
    has_res = res is not None

    def body(*refs):
        x_ref, g_ref, dy_ref = refs[:3]
        res_ref = refs[3] if has_res else None
        dx_ref, dg_ref = refs[3 + has_res:]
        xv, gv = x_ref[...], g_ref[...]
        _, r = _rms(xv, gv, width)
        dx, dg = _rms_bwd(xv, gv, r, dy_ref[...], width)
        if has_res:
            dx = res_ref[...] + dx
        dx_ref[...] = dx.astype(out_dtype)

        @pl.when(pl.program_id(0) == 0)
        def _():
            dg_ref[...] = jnp.zeros_like(dg_ref)

        dg_ref[...] += dg

    ins = [x, g, dy] + ([res] if has_res else [])
    in_specs = [_col(tm, width, off), _whole((1, width)), _col(tm, width, 0)] + ([_col(tm, width, 0)] if has_res else [])
    return pl.pallas_call(
        body, grid=(S // tm,), in_specs=in_specs, out_specs=[_col(tm, width, 0), _whole((1, width))],
        out_shape=[jax.ShapeDtypeStruct((S, width), out_dtype), jax.ShapeDtypeStruct((1, width), F32)], name=name,
        compiler_params=_params(("arbitrary",)),
    )(*ins)


def _norm_g_grad(x, g, dy, name):
    S, width = x.shape
    tm = _pick(S, 256, 16)

    def body(x_ref, g_ref, dy_ref, dg_ref):
        xv = x_ref[...]
        _, r = _rms(xv, g_ref[...], width)

        @pl.when(pl.program_id(0) == 0)
        def _():
            dg_ref[...] = jnp.zeros_like(dg_ref)

        dg_ref[...] += jnp.sum(dy_ref[...] * xv * r, axis=0, keepdims=True)

    return pl.pallas_call(
        body, grid=(S // tm,), in_specs=[_col(tm, width, 0), _whole((1, width)), _col(tm, width, 0)],
        out_specs=_whole((1, width)), out_shape=jax.ShapeDtypeStruct((1, width), F32), name=name,
        compiler_params=_params(("arbitrary",)),
    )(x, g, dy)


def _loss_rows(y, target, name):
    S, D = y.shape
    tm = _pick(S, 512, 16)

    def body(y_ref, t_ref, g_ref, l_ref):
        err = y_ref[...] - t_ref[...]
        g_ref[...] = err / D

        @pl.when(pl.program_id(0) == 0)
        def _():
            l_ref[...] = jnp.zeros_like(l_ref)

        row = jnp.sum(err * err, axis=-1, keepdims=True) / D
        l_ref[...] += jnp.sum(row, axis=0, keepdims=True)

    return pl.pallas_call(
        body, grid=(S // tm,), in_specs=[_col(tm, D, 0), _col(tm, D, 0)], out_specs=[_col(tm, D, 0), _whole((1, 1))],
        out_shape=[jax.ShapeDtypeStruct((S, D), F32), jax.ShapeDtypeStruct((1, 1), F32)], name=name,
        compiler_params=_params(("arbitrary",)),
    )(y, target)


def _row_index(i, ts):
    return i * ts + lax.broadcasted_iota(jnp.int32, (ts, 1), 0)


def _pool_mixed(scr, pv_ref, g, t, ts):
    w, lo = POOL_WINDOWS[g], g * POOL_GW
    win = scr[HALO:HALO + ts, lo:lo + POOL_GW]
    for k in range(1, w):
        win = win + scr[HALO - k:HALO - k + ts, lo:lo + POOL_GW]
    cnt = jnp.minimum(t + 1, w).astype(F32)
    return win / cnt - pv_ref[:, lo:lo + POOL_GW], cnt


def _halo_before(ts, off):
    nb = ts // HALO
    cb = off // BRANCH_W
    return pl.BlockSpec((HALO, BRANCH_W), lambda i: (jnp.maximum(i * nb - 1, 0), cb))


def _halo_after(ts, off, n_tiles):
    nb = ts // HALO
    cb = off // BRANCH_W
    last = n_tiles * nb - 1
    return pl.BlockSpec((HALO, BRANCH_W), lambda i: (jnp.minimum((i + 1) * nb, last), cb))


def _pool_fwd(proj, offs, pool_w, pool_scale, name):
    S = proj.shape[0]
    ts = _pick(S, 512, HALO)

    def body(pv_ref, halo_ref, pz_ref, w_ref, sc_ref, y_ref, scr):
        i = pl.program_id(0)
        scr[0:HALO, :] = jnp.where(i > 0, halo_ref[...], 0.0)
        scr[HALO:HALO + ts, :] = pv_ref[...]
        t = _row_index(i, ts)
        for g in range(4):
            lo = g * POOL_GW
            mixed, _ = _pool_mixed(scr, pv_ref, g, t, ts)
            out = _dot(mixed, w_ref[g])
            y_ref[:, lo:lo + POOL_GW] = (out * sc_ref[:, lo:lo + POOL_GW] * _silu(pz_ref[:, lo:lo + POOL_GW])).astype(BF16)

    return pl.pallas_call(
        body, grid=(S // ts,),
        in_specs=[_col(ts, BRANCH_W, offs["pv"]), _halo_before(ts, offs["pv"]), _col(ts, BRANCH_W, offs["pz"]),
                  _whole((4, POOL_GW, POOL_GW)), _whole((1, BRANCH_W))],
        out_specs=_col(ts, BRANCH_W, 0), out_shape=jax.ShapeDtypeStruct((S, BRANCH_W), BF16),
        scratch_shapes=[pltpu.VMEM((HALO + ts, BRANCH_W), F32)], name=name, compiler_params=_params(("parallel",)),
    )(proj, proj, proj, pool_w, pool_scale)


def _pool_bwd_a(proj, offs, pool_w, pool_scale, dy, name):
    S = proj.shape[0]
    ts = _pick(S, 512, HALO)

    def body(pv_ref, halo_ref, pz_ref, w_ref, sc_ref, dy_ref, dpz_ref, e_ref, dw_ref, dsc_ref, scr):
        i = pl.program_id(0)
        scr[0:HALO, :] = jnp.where(i > 0, halo_ref[...], 0.0)
        scr[HALO:HALO + ts, :] = pv_ref[...]
        t = _row_index(i, ts)

        @pl.when(i == 0)
        def _():
            dw_ref[...] = jnp.zeros_like(dw_ref)
            dsc_ref[...] = jnp.zeros_like(dsc_ref)

        for g in range(4):
            lo = g * POOL_GW
            sl = slice(lo, lo + POOL_GW)
            mixed, cnt = _pool_mixed(scr, pv_ref, g, t, ts)
            out = _dot(mixed, w_ref[g])
            sz, dsz = _silu_and_grad(pz_ref[:, sl])
            dyv, sc = dy_ref[:, sl], sc_ref[:, sl]
            d_out = dyv * sc * sz
            dsc_ref[:, sl] += jnp.sum(dyv * out * sz, axis=0, keepdims=True)
            dpz_ref[:, sl] = (dyv * out * sc * dsz).astype(BF16)
            dw_ref[g] += _dot_tn(mixed, d_out)
            e_ref[:, sl] = _dot_nt(d_out, w_ref[g]) / cnt

    return pl.pallas_call(
        body, grid=(S // ts,),
        in_specs=[_col(ts, BRANCH_W, offs["pv"]), _halo_before(ts, offs["pv"]), _col(ts, BRANCH_W, offs["pz"]),
                  _whole((4, POOL_GW, POOL_GW)), _whole((1, BRANCH_W)), _col(ts, BRANCH_W, 0)],
        out_specs=[_col(ts, BRANCH_W, 0), _col(ts, BRANCH_W, 0), _whole((4, POOL_GW, POOL_GW)), _whole((1, BRANCH_W))],
        out_shape=[jax.ShapeDtypeStruct((S, BRANCH_W), BF16), jax.ShapeDtypeStruct((S, BRANCH_W), F32),
                   jax.ShapeDtypeStruct((4, POOL_GW, POOL_GW), F32), jax.ShapeDtypeStruct((1, BRANCH_W), F32)],
        scratch_shapes=[pltpu.VMEM((HALO + ts, BRANCH_W), F32)], name=name, compiler_params=_params(("arbitrary",)),
    )(proj, proj, proj, pool_w, pool_scale, dy)


def _pool_bwd_b(e, name):
    S = e.shape[0]
    ts = _pick(S, 512, HALO)
    n_tiles = S // ts

    def body(e_ref, halo_ref, dpv_ref, scr):
        i = pl.program_id(0)
        scr[0:ts, :] = e_ref[...]
        scr[ts:ts + HALO, :] = jnp.where(i < n_tiles - 1, halo_ref[...], 0.0)
        t = _row_index(i, ts)
        for g in range(4):
            w, lo = POOL_WINDOWS[g], g * POOL_GW
            acc = scr[0:ts, lo:lo + POOL_GW]
            ev = acc
            for k in range(1, w):
                acc = acc + scr[k:k + ts, lo:lo + POOL_GW]
            cnt = jnp.minimum(t + 1, w).astype(F32)
            dpv_ref[:, lo:lo + POOL_GW] = (acc - ev * cnt).astype(BF16)

    return pl.pallas_call(
        body, grid=(n_tiles,), in_specs=[_col(ts, BRANCH_W, 0), _halo_after(ts, 0, n_tiles)], out_specs=_col(ts, BRANCH_W, 0),
        out_shape=jax.ShapeDtypeStruct((S, BRANCH_W), BF16), scratch_shapes=[pltpu.VMEM((ts + HALO, BRANCH_W), F32)],
        name=name, compiler_params=_params(("parallel",)),
    )(e, e)


def _conv_fwd(proj, offs, conv_w, name):
    S = proj.shape[0]
    ts = _pick(S, 512, HALO)

    def body(cb_ref, cc_ref, cx_ref, cz_ref, hc_ref, hx_ref, w_ref, y_ref, scr):
        i = pl.program_id(0)
        scr[0:HALO, :] = jnp.where(i > 0, hc_ref[...] * hx_ref[...], 0.0)
        scr[HALO:HALO + ts, :] = cc_ref[...] * cx_ref[...]
        y = w_ref[0:1, :] * scr[HALO - 2:HALO - 2 + ts, :] + w_ref[1:2, :] * scr[HALO - 1:HALO - 1 + ts, :] + w_ref[2:3, :] * scr[HALO:HALO + ts, :]
        y_ref[...] = (cb_ref[...] * y * _silu(cz_ref[...])).astype(BF16)

    return pl.pallas_call(
        body, grid=(S // ts,),
        in_specs=[_col(ts, BRANCH_W, offs["cb"]), _col(ts, BRANCH_W, offs["cc"]), _col(ts, BRANCH_W, offs["cx"]),
                  _col(ts, BRANCH_W, offs["cz"]), _halo_before(ts, offs["cc"]), _halo_before(ts, offs["cx"]), _whole((3, BRANCH_W))],
        out_specs=_col(ts, BRANCH_W, 0), out_shape=jax.ShapeDtypeStruct((S, BRANCH_W), BF16),
        scratch_shapes=[pltpu.VMEM((HALO + ts, BRANCH_W), F32)], name=name, compiler_params=_params(("parallel",)),
    )(proj, proj, proj, proj, proj, proj, conv_w)


def _conv_bwd(proj, offs, conv_w, dy, name):
    S = proj.shape[0]
    ts = _pick(S, 256, HALO)
    n_tiles = S // ts

    def body(cb_ref, cc_ref, cx_ref, cz_ref, hc_ref, hx_ref, w_ref, dy_ref, ady_ref, acb_ref, acz_ref,
             dcb_ref, dcc_ref, dcx_ref, dcz_ref, dw_ref, scr_u, scr_d):
        i = pl.program_id(0)
        u = cc_ref[...] * cx_ref[...]
        scr_u[0:HALO, :] = jnp.where(i > 0, hc_ref[...] * hx_ref[...], 0.0)
        scr_u[HALO:HALO + ts, :] = u
        u1 = scr_u[HALO - 1:HALO - 1 + ts, :]
        u2 = scr_u[HALO - 2:HALO - 2 + ts, :]
        y = w_ref[0:1, :] * u2 + w_ref[1:2, :] * u1 + w_ref[2:3, :] * u
        sz, dsz = _silu_and_grad(cz_ref[...])
        dyv, cb = dy_ref[...], cb_ref[...]
        dcb_ref[...] = (dyv * y * sz).astype(BF16)
        dcz_ref[...] = (dyv * cb * y * dsz).astype(BF16)
        d_y = dyv * cb * sz
        scr_d[0:ts, :] = d_y
        scr_d[ts:ts + HALO, :] = jnp.where(i < n_tiles - 1, ady_ref[...] * acb_ref[...] * _silu(acz_ref[...]), 0.0)
        du = w_ref[2:3, :] * d_y + w_ref[1:2, :] * scr_d[1:1 + ts, :] + w_ref[0:1, :] * scr_d[2:2 + ts, :]
        dcc_ref[...] = (du * cx_ref[...]).astype(BF16)
        dcx_ref[...] = (du * cc_ref[...]).astype(BF16)

        @pl.when(i == 0)
        def _():
            dw_ref[...] = jnp.zeros_like(dw_ref)

        dw_ref[0:1, :] += jnp.sum(d_y * u2, axis=0, keepdims=True)
        dw_ref[1:2, :] += jnp.sum(d_y * u1, axis=0, keepdims=True)
        dw_ref[2:3, :] += jnp.sum(d_y * u, axis=0, keepdims=True)

    row = lambda off: _col(ts, BRANCH_W, off)
    return pl.pallas_call(
        body, grid=(n_tiles,),
        in_specs=[row(offs["cb"]), row(offs["cc"]), row(offs["cx"]), row(offs["cz"]), _halo_before(ts, offs["cc"]),
                  _halo_before(ts, offs["cx"]), _whole((3, BRANCH_W)), row(0), _halo_after(ts, 0, n_tiles),
                  _halo_after(ts, offs["cb"], n_tiles), _halo_after(ts, offs["cz"], n_tiles)],
        out_specs=[row(0), row(0), row(0), row(0), _whole((3, BRANCH_W))],
        out_shape=[jax.ShapeDtypeStruct((S, BRANCH_W), BF16)] * 4 + [jax.ShapeDtypeStruct((3, BRANCH_W), F32)],
        scratch_shapes=[pltpu.VMEM((HALO + ts, BRANCH_W), F32), pltpu.VMEM((ts + HALO, BRANCH_W), F32)], name=name,
        compiler_params=_params(("arbitrary",)),
    )(proj, proj, proj, proj, proj, proj, conv_w, dy, dy, proj, proj)


def _xattn_head(xq, kraw, v, qg, kg):
    qn, rq = _rms(xq, qg, XATTN_HD)
    kn, rk = _rms(kraw, kg, XATTN_HD)
    s = _dot_nt(qn, kn) * (XATTN_HD ** -0.5)
    e = jnp.exp(s - jnp.max(s, axis=-1, keepdims=True))
    p = e / jnp.sum(e, axis=-1, keepdims=True)
    return qn, rq, kn, rk, p, _dot(p, v)


def _xattn_fwd(proj, offs, mem_kv, qg, kg, name):
    S, M = proj.shape[0], mem_kv.shape[0]
    tm = _pick(S, 512, 16)

    def body(xq_ref, xz_ref, kv_ref, qg_ref, kg_ref, y_ref):
        for h in range(XATTN_HEADS):
            sl = slice(h * XATTN_HD, (h + 1) * XATTN_HD)
            vs = slice(BRANCH_W + h * XATTN_HD, BRANCH_W + (h + 1) * XATTN_HD)
            o = _xattn_head(xq_ref[:, sl], kv_ref[:, sl], kv_ref[:, vs], qg_ref[...], kg_ref[...])[-1]
            y_ref[:, sl] = (o * _silu(xz_ref[:, sl])).astype(BF16)

    return pl.pallas_call(
        body, grid=(S // tm,),
        in_specs=[_col(tm, BRANCH_W, offs["xq"]), _col(tm, BRANCH_W, offs["xz"]), _whole((M, 2 * BRANCH_W)),
                  _whole((1, XATTN_HD)), _whole((1, XATTN_HD))],
        out_specs=_col(tm, BRANCH_W, 0), out_shape=jax.ShapeDtypeStruct((S, BRANCH_W), BF16), name=name,
        compiler_params=_params(("parallel",)),
    )(proj, proj, mem_kv, qg, kg)


def _xattn_bwd(proj, offs, mem_kv, qg, kg, dy, name):
    S, M = proj.shape[0], mem_kv.shape[0]
    tm = _pick(S, 256, 16)
    n_tiles = S // tm

    def body(xq_ref, xz_ref, kv_ref, qg_ref, kg_ref, dy_ref, dxq_ref, dxz_ref, dkv_ref, dqg_ref, dkg_ref, dkn_acc):
        i = pl.program_id(0)

        @pl.when(i == 0)
        def _():
            dkv_ref[...] = jnp.zeros_like(dkv_ref)
            dqg_ref[...] = jnp.zeros_like(dqg_ref)
            dkg_ref[...] = jnp.zeros_like(dkg_ref)
            dkn_acc[...] = jnp.zeros_like(dkn_acc)

        qg_v, kg_v = qg_ref[...], kg_ref[...]
        for h in range(XATTN_HEADS):
            sl = slice(h * XATTN_HD, (h + 1) * XATTN_HD)
            vs = slice(BRANCH_W + h * XATTN_HD, BRANCH_W + (h + 1) * XATTN_HD)
            xq, v = xq_ref[:, sl], kv_ref[:, vs]
            qn, rq, kn, rk, p, o = _xattn_head(xq, kv_ref[:, sl], v, qg_v, kg_v)
            sz, dsz = _silu_and_grad(xz_ref[:, sl])
            dyv = dy_ref[:, sl]
            dxz_ref[:, sl] = (dyv * o * dsz).astype(BF16)
            do = dyv * sz
            dkv_ref[:, vs] += _dot_tn(p, do)
            dp = _dot_nt(do, v)
            ds = p * (dp - jnp.sum(dp * p, axis=-1, keepdims=True)) * (XATTN_HD ** -0.5)
            dqn = _dot(ds, kn)
            dkn_acc[:, sl] += _dot_tn(ds, qn)
            dxq, dqg = _rms_bwd(xq, qg_v, rq, dqn, XATTN_HD)
            dxq_ref[:, sl] = dxq.astype(BF16)
            dqg_ref[...] += dqg

        @pl.when(i == n_tiles - 1)
        def _():
            for h in range(XATTN_HEADS):
                sl = slice(h * XATTN_HD, (h + 1) * XATTN_HD)
                kraw = kv_ref[:, sl]
                _, rk = _rms(kraw, kg_v, XATTN_HD)
                dk, dkg = _rms_bwd(kraw, kg_v, rk, dkn_acc[:, sl], XATTN_HD)
                dkv_ref[:, sl] = dk
                dkg_ref[...] += dkg

    return pl.pallas_call(
        body, grid=(n_tiles,),
        in_specs=[_col(tm, BRANCH_W, offs["xq"]), _col(tm, BRANCH_W, offs["xz"]), _whole((M, 2 * BRANCH_W)),
                  _whole((1, XATTN_HD)), _whole((1, XATTN_HD)), _col(tm, BRANCH_W, 0)],
        out_specs=[_col(tm, BRANCH_W, 0), _col(tm, BRANCH_W, 0), _whole((M, 2 * BRANCH_W)), _whole((1, XATTN_HD)), _whole((1, XATTN_HD))],
        out_shape=[jax.ShapeDtypeStruct((S, BRANCH_W), BF16), jax.ShapeDtypeStruct((S, BRANCH_W), BF16),
                   jax.ShapeDtypeStruct((M, 2 * BRANCH_W), F32), jax.ShapeDtypeStruct((1, XATTN_HD), F32), jax.ShapeDtypeStruct((1, XATTN_HD), F32)],
        scratch_shapes=[pltpu.VMEM((M, BRANCH_W), F32)], name=name, compiler_params=_params(("arbitrary",)),
    )(proj, proj, mem_kv, qg, kg, dy)


def _rope(c, tc, ta, tb):
    return c * tc + pltpu.roll(c, 96, 1) * ta + pltpu.roll(c, 32, 1) * tb


def _rope_t(d, tc, ta, tb):
    return d * tc + pltpu.roll(d * ta, 32, 1) + pltpu.roll(d * tb, 96, 1)


def _mla_qk_fwd(qp, kvp, proj, offs, tabs, qg, kg, name):
    S = qp.shape[0]
    tm = _pick(S, 256, 16)
    W = MLA_HEADS * QK_PAD

    def body(q_ref, kv_ref, kr_ref, tc_ref, ta_ref, tb_ref, qg_ref, kg_ref, qo_ref, ko_ref, vo_ref):
        tc, ta, tb = tc_ref[...], ta_ref[...], tb_ref[...]
        qg_v, kg_v = qg_ref[...], kg_ref[...]
        kr = kr_ref[...]
        kr_ss = jnp.sum(kr * kr, axis=-1, keepdims=True)
        for h in range(MLA_HEADS):
            sl = slice(h * QK_PAD, (h + 1) * QK_PAD)
            qn, _ = _rms(q_ref[:, sl], qg_v, QK_HEAD)
            qo_ref[:, sl] = jnp.concatenate([qn[:, :QK_NOPE], _rope(qn[:, QK_NOPE:], tc, ta, tb)], axis=1).astype(BF16)
            kn = kv_ref[:, h * QK_NOPE:(h + 1) * QK_NOPE]
            r = lax.rsqrt((jnp.sum(kn * kn, axis=-1, keepdims=True) + kr_ss) / QK_HEAD + EPS)
            ko_ref[:, sl] = jnp.concatenate([kn * r * kg_v[:, :QK_NOPE], _rope(kr * r * kg_v[:, QK_NOPE:], tc, ta, tb)], axis=1).astype(BF16)
        vo_ref[...] = kv_ref[:, MLA_HEADS * QK_NOPE:].astype(BF16)

    return pl.pallas_call(
        body, grid=(S // tm,),
        in_specs=[_col(tm, W, 0), _col(tm, W, 0), _col(tm, 128, offs["kr"]), _col(tm, 128, 0), _col(tm, 128, 0), _col(tm, 128, 0),
                  _whole((1, QK_PAD)), _whole((1, QK_PAD))],
        out_specs=[_col(tm, W, 0), _col(tm, W, 0), _col(tm, BRANCH_W, 0)],
        out_shape=[jax.ShapeDtypeStruct((S, W), BF16), jax.ShapeDtypeStruct((S, W), BF16), jax.ShapeDtypeStruct((S, BRANCH_W), BF16)],
        name=name, compiler_params=_params(("parallel",)),
    )(qp, kvp, proj, *tabs, qg, kg)


def _mla_qk_bwd(qp, kvp, proj, offs, tabs, qg, kg, dQ, dK, dV, name):
    S = qp.shape[0]
    tm = _pick(S, 256, 16)
    W = MLA_HEADS * QK_PAD

    def body(q_ref, kv_ref, kr_ref, tc_ref, ta_ref, tb_ref, qg_ref, kg_ref, dQ_ref, dK_ref, dV_ref,
             dq_ref, dkv_ref, dkr_ref, dqg_ref, dkg_ref):
        @pl.when(pl.program_id(0) == 0)
        def _():
            dqg_ref[...] = jnp.zeros_like(dqg_ref)
            dkg_ref[...] = jnp.zeros_like(dkg_ref)

        tc, ta, tb = tc_ref[...], ta_ref[...], tb_ref[...]
        qg_v, kg_v = qg_ref[...], kg_ref[...]
        kg_n, kg_r = kg_v[:, :QK_NOPE], kg_v[:, QK_NOPE:]
        kr = kr_ref[...]
        kr_ss = jnp.sum(kr * kr, axis=-1, keepdims=True)
        dkr = jnp.zeros_like(kr)
        dkg_n = jnp.zeros((1, QK_NOPE), F32)
        dkg_r = jnp.zeros((1, QK_NOPE), F32)
        for h in range(MLA_HEADS):
            sl = slice(h * QK_PAD, (h + 1) * QK_PAD)
            qv = q_ref[:, sl]
            _, rq = _rms(qv, qg_v, QK_HEAD)
            dQv = dQ_ref[:, sl]
            dqn = jnp.concatenate([dQv[:, :QK_NOPE], _rope_t(dQv[:, QK_NOPE:], tc, ta, tb)], axis=1)
            dq, dqg = _rms_bwd(qv, qg_v, rq, dqn, QK_HEAD)
            dq_ref[:, sl] = dq.astype(BF16)
            dqg_ref[...] += dqg
            ns = slice(h * QK_NOPE, (h + 1) * QK_NOPE)
            kn = kv_ref[:, ns]
            r = lax.rsqrt((jnp.sum(kn * kn, axis=-1, keepdims=True) + kr_ss) / QK_HEAD + EPS)
            dKv = dK_ref[:, sl]
            dyn = dKv[:, :QK_NOPE]
            dyr = _rope_t(dKv[:, QK_NOPE:], tc, ta, tb)
            proj_s = (jnp.sum(dyn * kg_n * kn, axis=-1, keepdims=True) + jnp.sum(dyr * kg_r * kr, axis=-1, keepdims=True)) / QK_HEAD
            r3 = r * r * r
            dkv_ref[:, ns] = (r * dyn * kg_n - kn * r3 * proj_s).astype(BF16)
            dkr = dkr + (r * dyr * kg_r - kr * r3 * proj_s)
            dkg_n = dkg_n + jnp.sum(dyn * kn * r, axis=0, keepdims=True)
            dkg_r = dkg_r + jnp.sum(dyr * kr * r, axis=0, keepdims=True)
        dkg_ref[...] += jnp.concatenate([dkg_n, dkg_r], axis=1)
        dkr_ref[...] = dkr.astype(BF16)
        dkv_ref[:, MLA_HEADS * QK_NOPE:] = dV_ref[...].astype(BF16)

    return pl.pallas_call(
        body, grid=(S // tm,),
        in_specs=[_col(tm, W, 0), _col(tm, W, 0), _col(tm, 128, offs["kr"]), _col(tm, 128, 0), _col(tm, 128, 0), _col(tm, 128, 0),
                  _whole((1, QK_PAD)), _whole((1, QK_PAD)), _col(tm, W, 0), _col(tm, W, 0), _col(tm, BRANCH_W, 0)],
        out_specs=[_col(tm, W, 0), _col(tm, W, 0), _col(tm, 128, 0), _whole((1, QK_PAD)), _whole((1, QK_PAD))],
        out_shape=[jax.ShapeDtypeStruct((S, W), BF16), jax.ShapeDtypeStruct((S, W), BF16), jax.ShapeDtypeStruct((S, 128), BF16),
                   jax.ShapeDtypeStruct((1, QK_PAD), F32), jax.ShapeDtypeStruct((1, QK_PAD), F32)],
        name=name, compiler_params=_params(("arbitrary",)),
    )(qp, kvp, proj, *tabs, qg, kg, dQ, dK, dV)


def _flash_fwd(Q, K, V, proj, offs, name, comm=None):
    S = Q.shape[0]
    t = _pick(S, 1024, 128)
    assert t % 256 == 0
    mz_cb = offs["mz"] // V_HEAD

    half = t // 2

    def fold(x, op):
        r = x[:, 0:128]
        for ch in range(1, x.shape[1] // 128):
            r = op(r, x[:, ch * 128:(ch + 1) * 128])
        return r

    def body(q_ref, k_ref, v_ref, mz_ref, o_ref, y_ref, lse_ref, mp_scr, lp_scr, acc_scr):
        qi = pl.program_id(1)
        q = q_ref[...]

        def raw_scores(j):
            rows_k = pl.ds(pl.multiple_of(j * t, t), t)
            return _dot_nt(q, k_ref[rows_k, :]), rows_k

        def diagonal_scores(part):
            q_lo = part * half
            rows_k = pl.ds(pl.multiple_of(qi * t + part * half, half), half)
            s = _dot_nt(q[q_lo:], k_ref[rows_k, :])
            rows = lax.broadcasted_iota(jnp.int32, s.shape, 0)
            cols = lax.broadcasted_iota(jnp.int32, s.shape, 1)
            return jnp.where(cols <= rows, s, NEG), rows_k, q_lo

        mp_scr[...] = jnp.full_like(mp_scr, NEG)

        def max_sweep(j, carry):
            s, _ = raw_scores(j)
            mp_scr[...] = jnp.maximum(mp_scr[...], fold(s, jnp.maximum))
            return carry

        lax.fori_loop(0, qi, max_sweep, 0)
        for part in range(2):
            s, _, q_lo = diagonal_scores(part)
            mp_scr[q_lo:, :] = jnp.maximum(mp_scr[q_lo:, :], fold(s, jnp.maximum))
        m = jnp.max(mp_scr[...], axis=-1, keepdims=True) * SCORE_SCALE_LOG2

        lp_scr[...] = jnp.zeros_like(lp_scr)
        acc_scr[...] = jnp.zeros_like(acc_scr)

        def sum_sweep(j, carry):
            s, rows_k = raw_scores(j)
            p = jnp.exp2(s * SCORE_SCALE_LOG2 - m)
            lp_scr[...] += fold(p, jnp.add)
            acc_scr[...] += _dot(p, v_ref[rows_k, :])
            return carry

        lax.fori_loop(0, qi, sum_sweep, 0)
        for part in range(2):
            s, rows_k, q_lo = diagonal_scores(part)
            p = jnp.exp2(s * SCORE_SCALE_LOG2 - m[q_lo:])
            lp_scr[q_lo:, :] += fold(p, jnp.add)
            acc_scr[q_lo:, :] += _dot(p, v_ref[rows_k, :])
        l = jnp.sum(lp_scr[...], axis=-1, keepdims=True)
        o = acc_scr[...] / l
        o_ref[...] = o
        y_ref[...] = (o * _silu(mz_ref[...])).astype(BF16)
        lse_ref[0] = m + jnp.log2(l)

    in_specs = [pl.BlockSpec((t, QK_PAD), lambda h, i: (i, h)), pl.BlockSpec((S, QK_PAD), lambda h, i: (0, h)),
                pl.BlockSpec((S, V_HEAD), lambda h, i: (0, h)), pl.BlockSpec((t, V_HEAD), lambda h, i: (i, mz_cb + h))]
    out_specs = [pl.BlockSpec((t, V_HEAD), lambda h, i: (i, h)), pl.BlockSpec((t, V_HEAD), lambda h, i: (i, h)),
                 pl.BlockSpec((1, t, 1), lambda h, i: (h, i, 0))]
    out_shape = [jax.ShapeDtypeStruct((S, BRANCH_W), F32), jax.ShapeDtypeStruct((S, BRANCH_W), BF16),
                 jax.ShapeDtypeStruct((MLA_HEADS, S, 1), F32)]
    scratch = [pltpu.VMEM((t, 128), F32), pltpu.VMEM((t, 128), F32), pltpu.VMEM((t, V_HEAD), F32)]
    ins = [Q, K, V, proj]
    if comm is not None:
        return _hosted_call(comm, body, ins, in_specs, out_specs, out_shape, scratch, (MLA_HEADS, S // t), name)
    res = pl.pallas_call(
        body, grid=(MLA_HEADS, S // t), in_specs=in_specs, out_specs=out_specs, out_shape=out_shape, scratch_shapes=scratch,
        name=name, compiler_params=_params(("parallel", "parallel")),
    )(*ins)
    return list(res), []


def _mla_gate_bwd(o, proj, offs, dy, name):
    S = o.shape[0]
    tm = _pick(S, 512, 16)

    def body(o_ref, mz_ref, dy_ref, do_ref, dmz_ref, dl_ref):
        sz, dsz = _silu_and_grad(mz_ref[...])
        dyv, ov = dy_ref[...], o_ref[...]
        do = (dyv * sz).astype(BF16)
        do_ref[...] = do
        dmz_ref[...] = (dyv * ov * dsz).astype(BF16)
        prod = do.astype(F32) * ov
        for h in range(MLA_HEADS):
            dl_ref[h] = jnp.sum(prod[:, h * V_HEAD:(h + 1) * V_HEAD], axis=-1, keepdims=True)

    return pl.pallas_call(
        body, grid=(S // tm,), in_specs=[_col(tm, BRANCH_W, 0), _col(tm, BRANCH_W, offs["mz"]), _col(tm, BRANCH_W, 0)],
        out_specs=[_col(tm, BRANCH_W, 0), _col(tm, BRANCH_W, 0), pl.BlockSpec((MLA_HEADS, tm, 1), lambda i: (0, i, 0))],
        out_shape=[jax.ShapeDtypeStruct((S, BRANCH_W), BF16), jax.ShapeDtypeStruct((S, BRANCH_W), BF16),
                   jax.ShapeDtypeStruct((MLA_HEADS, S, 1), F32)], name=name,
        compiler_params=_params(("parallel",)),
    )(o, proj, dy)


def _flash_bwd(Q, K, V, lse, delta, do, name, comm=None):
    S = Q.shape[0]
    tk = _pick(S, 1024, 128)
    assert tk % 256 == 0
    nq, half = S // tk, tk // 2
    once = pl.Buffered(1)

    def body(q_ref, k_ref, v_ref, lse_ref, dl_ref, do_ref, dq_ref, dk_ref, dv_ref, dk_acc, dv_acc):
        ki = pl.program_id(1)

        @pl.when(ki == 0)
        def _():
            dq_ref[...] = jnp.zeros_like(dq_ref)

        dk_acc[...] = jnp.zeros_like(dk_acc)
        dv_acc[...] = jnp.zeros_like(dv_acc)

        def tile(i, q_off, q_n, k_n, masked):
            rows = pl.ds(pl.multiple_of(i * tk + q_off, half), q_n)
            q, dov = q_ref[rows, :], do_ref[rows, :]
            k, v = k_ref[0:k_n, :], v_ref[0:k_n, :]
            s = _dot_nt(q, k) * SCORE_SCALE_LOG2
            if masked:
                r = q_off + lax.broadcasted_iota(jnp.int32, s.shape, 0)
                c = lax.broadcasted_iota(jnp.int32, s.shape, 1)
                s = jnp.where(c <= r, s, NEG)
            p = jnp.exp2(s - lse_ref[0, rows, :])
            dv_acc[0:k_n, :] += _dot_tn(p, dov)
            dp = _dot_nt(dov, v)
            ds = p * (dp - dl_ref[0, rows, :]) * (QK_HEAD ** -0.5)
            dk_acc[0:k_n, :] += _dot_tn(ds, q)
            dq_ref[rows, :] += _dot(ds, k)

        def below_diagonal(i, carry):
            tile(i, 0, tk, tk, False)
            return carry

        tile(ki, 0, half, half, True)
        tile(ki, half, half, tk, True)
        lax.fori_loop(ki + 1, nq, below_diagonal, 0)
        dk_ref[...] = dk_acc[...]
        dv_ref[...] = dv_acc[...]

    in_specs = [pl.BlockSpec((S, QK_PAD), lambda h, j: (0, h), pipeline_mode=once), pl.BlockSpec((tk, QK_PAD), lambda h, j: (j, h)),
                pl.BlockSpec((tk, V_HEAD), lambda h, j: (j, h)), pl.BlockSpec((1, S, 1), lambda h, j: (h, 0, 0), pipeline_mode=once),
                pl.BlockSpec((1, S, 1), lambda h, j: (h, 0, 0), pipeline_mode=once),
                pl.BlockSpec((S, V_HEAD), lambda h, j: (0, h), pipeline_mode=once)]
    out_specs = [pl.BlockSpec((S, QK_PAD), lambda h, j: (0, h)), pl.BlockSpec((tk, QK_PAD), lambda h, j: (j, h)),
                 pl.BlockSpec((tk, V_HEAD), lambda h, j: (j, h))]
    out_shape = [jax.ShapeDtypeStruct((S, MLA_HEADS * QK_PAD), F32), jax.ShapeDtypeStruct((S, MLA_HEADS * QK_PAD), F32),
                 jax.ShapeDtypeStruct((S, BRANCH_W), F32)]
    scratch = [pltpu.VMEM((tk, QK_PAD), F32), pltpu.VMEM((tk, V_HEAD), F32)]
    ins = [Q, K, V, lse, delta, do]
    if comm is not None:
        return _hosted_call(comm, body, ins, in_specs, out_specs, out_shape, scratch, (MLA_HEADS, nq), name)
    res = pl.pallas_call(
        body, grid=(MLA_HEADS, nq), in_specs=in_specs, out_specs=out_specs, out_shape=out_shape, scratch_shapes=scratch, name=name,
        compiler_params=_params(("parallel", "arbitrary")),
    )(*ins)
    return list(res), []


def _merge_specs(S, D, offs, tm, tn, layer):
    g_cb = offs["g"] // tn
    nj = D // tn
    y_specs = [pl.BlockSpec((tm, BRANCH_W), lambda j, i: (i, 0)) for _ in range(4)]
    w_spec = pl.BlockSpec((None, 4, BRANCH_W, tn), lambda j, i: (layer, 0, 0, j))
    g_specs = [pl.BlockSpec((tm, tn), functools.partial(lambda j, i, b: (i, g_cb + b * nj + j), b=b)) for b in range(4)]
    bias_spec = pl.BlockSpec((4, tn), lambda j, i: (0, j))
    return y_specs, w_spec, g_specs, bias_spec


def _merge_fwd(ys, wb, layer, proj, offs, gate_b, name):
    S, D = ys[0].shape[0], wb.shape[3]
    tm, tn = _pick(S, 512, 16), _pick(D, 512, 128)
    y_specs, w_spec, g_specs, bias_spec = _merge_specs(S, D, offs, tm, tn, layer)

    def body(y0, y1, y2, y3, w_ref, g0, g1, g2, g3, b_ref, o_ref, t_ref):
        acc = None
        for b, (y_ref, g_ref) in enumerate(zip((y0, y1, y2, y3), (g0, g1, g2, g3))):
            gate = 1.0 / (1.0 + jnp.exp(-(g_ref[...] + b_ref[b:b + 1, :])))
            t = _dot(y_ref[...], w_ref[b])
            t_ref[b] = t.astype(BF16)
            term = gate * t
            acc = term if acc is None else acc + term
        o_ref[...] = acc.astype(BF16)

    return pl.pallas_call(
        body, grid=(D // tn, S // tm), in_specs=y_specs + [w_spec] + g_specs + [bias_spec],
        out_specs=[pl.BlockSpec((tm, tn), lambda j, i: (i, j)), pl.BlockSpec((4, tm, tn), lambda j, i: (0, i, j))],
        out_shape=[jax.ShapeDtypeStruct((S, D), BF16), jax.ShapeDtypeStruct((4, S, D), BF16)], name=name,
        compiler_params=_params(("parallel", "parallel")),
    )(*ys, wb, proj, proj, proj, proj, gate_b)


def _merge_bwd(t, proj, offs, gate_b, dmerged, name):
    _, S, D = t.shape
    tm, tn = _pick(S, 512, 16), _pick(D, 512, 128)
    nj = D // tn
    _, _, g_specs, bias_spec = _merge_specs(S, D, offs, tm, tn, 0)

    def body(t_ref, g0, g1, g2, g3, b_ref, dm_ref, dt_ref, dg0, dg1, dg2, dg3, db_ref):
        @pl.when(pl.program_id(1) == 0)
        def _():
            db_ref[...] = jnp.zeros_like(db_ref)

        dm = dm_ref[...]
        for b, (g_ref, dg_ref) in enumerate(zip((g0, g1, g2, g3), (dg0, dg1, dg2, dg3))):
            gate = 1.0 / (1.0 + jnp.exp(-(g_ref[...] + b_ref[b:b + 1, :])))
            dt_ref[b] = (dm * gate).astype(BF16)
            dgp = dm * t_ref[b].astype(F32) * gate * (1.0 - gate)
            dg_ref[...] = dgp.astype(BF16)
            db_ref[b:b + 1, :] += jnp.sum(dgp, axis=0, keepdims=True)

    stacked = pl.BlockSpec((4, tm, tn), lambda j, i: (0, i, j))
    tile = pl.BlockSpec((tm, tn), lambda j, i: (i, j))
    return pl.pallas_call(
        body, grid=(nj, S // tm), in_specs=[stacked] + g_specs + [bias_spec, tile],
        out_specs=[stacked] + [tile] * 4 + [pl.BlockSpec((4, tn), lambda j, i: (0, j))],
        out_shape=[jax.ShapeDtypeStruct((4, S, D), BF16)] + [jax.ShapeDtypeStruct((S, D), BF16)] * 4 + [jax.ShapeDtypeStruct((4, D), F32)],
        name=name, compiler_params=_params(("parallel", "arbitrary")),
    )(t, proj, proj, proj, proj, gate_b, dmerged)


HBM_SPEC = pl.BlockSpec(memory_space=pl.ANY)


def _mesh_place():
    x, y, c = lax.axis_index("x"), lax.axis_index("y"), lax.axis_index("c")
    return x, y, c, [(1 - x, y), (x, 1 - y), (1 - x, 1 - y)]


def _remote(src, dst, send_sem, recv_sem, to):
    return pltpu.make_async_remote_copy(src_ref=src, dst_ref=dst, send_sem=send_sem, recv_sem=recv_sem, device_id=to,
                                        device_id_type=pl.DeviceIdType.MESH)


class _GatherPlan:
    def __init__(self, arrs):
        self.arrs, self.n = list(arrs), len(arrs)
        self.out_shape = [jax.ShapeDtypeStruct((N_DEV,) + a.shape, a.dtype) for a in arrs]
        self.scratch = [pltpu.SemaphoreType.DMA((self.n, 7)), pltpu.SemaphoreType.DMA((self.n, 7)), pltpu.SemaphoreType.DMA((self.n,))]

    def _copies(self, ins, dsts, sems):
        send_sems, recv_sems, local_sems = sems
        x, y, c, _ = _mesh_place()
        slot = lambda chip, pc: 4 * chip[0] + 2 * chip[1] + pc
        here, sibling = (x, y), (x, y, 1 - c)
        first = (x + (1 - 2 * x) * (1 - c), y + (1 - 2 * y) * c)
        second = (x + (1 - 2 * x) * c, y + (1 - 2 * y) * (1 - c))
        far = (1 - x, 1 - y)

        def copy(a, k, src, block, to):
            return lambda: _remote(src(), dsts[a].at[block], send_sems.at[a, k], recv_sems.at[a, k], to)

        rng = range(self.n)
        shard = lambda a: (lambda: ins[a])
        held = lambda a, chip: (lambda: dsts[a].at[slot(chip, c)])
        local = [(lambda a=a: pltpu.make_async_copy(ins[a], dsts[a].at[slot(here, c)], local_sems.at[a])) for a in rng]
        own = [copy(a, k, shard(a), slot(here, c), to) for a in rng for k, to in ((0, sibling), (1, (*first, c)), (2, (*second, c)))]
        steps = []
        for a in rng:
            steps.append((copy(a, 1, shard(a), slot(first, c), sibling),
                          [copy(a, 4, held(a, first), slot(first, c), sibling), copy(a, 3, held(a, first), slot(first, c), (*second, c))]))
            steps.append((copy(a, 2, shard(a), slot(second, c), sibling), [copy(a, 5, held(a, second), slot(second, c), sibling)]))
            steps.append((copy(a, 3, shard(a), slot(far, c), sibling), [copy(a, 6, held(a, far), slot(far, c), sibling)]))
        from_sibling = [copy(a, k, shard(a), slot(chip, 1 - c), sibling) for a in rng
                        for k, chip in ((0, here), (4, second), (5, first), (6, far))]
        return local, own, steps, from_sibling

    def start(self, ins, dsts, sems):
        local, own, _, _ = self._copies(ins, dsts, sems)
        for make in local + own:
            make().start()

    def relay(self, ins, dsts, sems):
        _, _, steps, _ = self._copies(ins, dsts, sems)
        for arrived, onward in steps:
            arrived().wait_recv()
            for make in onward:
                make().start()

    def finish(self, ins, dsts, sems):
        local, own, steps, from_sibling = self._copies(ins, dsts, sems)
        for make in from_sibling:
            make().wait_recv()
        for make in own + [m for _, onward in steps for m in onward]:
            make().wait_send()
        for make in local:
            make().wait()


class _ChipExchangePlan:
    def __init__(self, arrs):
        self.arrs, self.n = list(arrs), len(arrs)
        self.out_shape = [jax.ShapeDtypeStruct(a.shape, a.dtype) for a in arrs]
        self.scratch = [pltpu.SemaphoreType.DMA((self.n, 3)), pltpu.SemaphoreType.DMA((self.n, 3)), pltpu.SemaphoreType.DMA((self.n,))]

    def _copies(self, ins, dsts, sems):
        send_sems, recv_sems, local_sems = sems
        x, y, c, chips = _mesh_place()
        mine = 2 * x + y
        local = [pltpu.make_async_copy(ins[a].at[mine], dsts[a].at[mine], local_sems.at[a]) for a in range(self.n)]
        remote = [_remote(ins[a].at[2 * px + py], dsts[a].at[mine], send_sems.at[a, j], recv_sems.at[a, j], (px, py, c))
                  for a in range(self.n) for j, (px, py) in enumerate(chips)]
        return local, remote

    def start(self, ins, dsts, sems):
        local, remote = self._copies(ins, dsts, sems)
        for cp in local + remote:
            cp.start()

    def relay(self, ins, dsts, sems):
        pass

    def finish(self, ins, dsts, sems):
        local, remote = self._copies(ins, dsts, sems)
        for cp in remote + local:
            cp.wait()


def _run_plan(plan, name):
    n = plan.n

    def body(*refs):
        ins, dsts, sems = refs[:n], refs[n:2 * n], refs[2 * n:]
        plan.start(ins, dsts, sems)
        plan.relay(ins, dsts, sems)
        plan.finish(ins, dsts, sems)

    res = pl.pallas_call(body, in_specs=[HBM_SPEC] * n, out_specs=[HBM_SPEC] * n, out_shape=plan.out_shape,
                         scratch_shapes=plan.scratch, name=name)(*plan.arrs)
    return list(res)


def _sibling_swap(arrs, name):
    n = len(arrs)

    def body(*refs):
        ins, dsts = refs[:n], refs[n:2 * n]
        send_sems, recv_sems = refs[2 * n:]
        x, y, c, _ = _mesh_place()
        cps = [_remote(ins[a].at[q, 1 - c], dsts[a].at[q], send_sems.at[a, q], recv_sems.at[a, q], (x, y, 1 - c))
               for a in range(n) for q in range(4)]
        for cp in cps:
            cp.start()
        for cp in cps:
            cp.wait()

    res = pl.pallas_call(
        body, in_specs=[HBM_SPEC] * n, out_specs=[HBM_SPEC] * n,
        out_shape=[jax.ShapeDtypeStruct((4,) + a.shape[2:], a.dtype) for a in arrs],
        scratch_shapes=[pltpu.SemaphoreType.DMA((n, 4)), pltpu.SemaphoreType.DMA((n, 4))], name=name,
    )(*arrs)
    return list(res)


def _pair_sum(staged, theirs, core, name):
    _, _, R, C = staged.shape
    tr = _pick(R, max(16, ((1 << 19) // C) // 16 * 16), 16)

    def body(c_ref, a_ref, b_ref, o_ref):
        o_ref[...] = (a_ref[...].astype(F32) + b_ref[...].astype(F32)).astype(BF16)

    blk = pl.BlockSpec((None, tr, C), lambda q, i, c: (q, i, 0))
    grid_spec = pltpu.PrefetchScalarGridSpec(
        num_scalar_prefetch=1, grid=(4, R // tr),
        in_specs=[pl.BlockSpec((None, None, tr, C), lambda q, i, c: (q, c[0], i, 0)), blk], out_specs=blk)
    return pl.pallas_call(
        body, grid_spec=grid_spec, out_shape=jax.ShapeDtypeStruct((4, R, C), BF16), name=name,
        compiler_params=_params(("parallel", "parallel")),
    )(core, staged, theirs)


def _adamw(parts, w, m, v, name):
    R, C = w.shape
    n_parts = parts[0].shape[0]
    rows = [p.shape[1] for p in parts]
    assert sum(rows) == R
    tr = _pick(min(rows), max(16, ((1 << 18) // C) // 16 * 16), 16)
    assert all(r % tr == 0 for r in rows)
    bounds, lo = [], 0
    for r in rows:
        bounds.append((lo, lo + r // tr))
        lo += r // tr
    n_lists = len(parts)

    def body(*refs):
        p_refs = refs[:n_lists]
        w_ref, m_ref, v_ref, g_ref, d_ref, nm_ref, nv_ref = refs[n_lists:]
        i = pl.program_id(0)

        def update(p_ref):
            g = p_ref[0].astype(F32)
            for s in range(1, n_parts):
                g = g + p_ref[s].astype(F32)
            mn = ADAM_B1 * m_ref[...] + (1.0 - ADAM_B1) * g
            vn = ADAM_B2 * v_ref[...] + (1.0 - ADAM_B2) * (g * g)
            m_hat = mn / (1.0 - ADAM_B1 ** ADAM_STEP)
            v_hat = vn / (1.0 - ADAM_B2 ** ADAM_STEP)
            g_ref[...] = g
            d_ref[...] = -ADAM_LR * (m_hat / (jnp.sqrt(v_hat) + ADAM_EPS) + ADAM_WD * w_ref[...])
            nm_ref[...] = mn
            nv_ref[...] = vn

        if n_lists == 1:
            update(p_refs[0])
        else:
            for p_ref, (first, end) in zip(p_refs, bounds):
                pl.when((i >= first) & (i < end))(functools.partial(update, p_ref))

    blk = pl.BlockSpec((tr, C), lambda i: (i, 0))
    p_specs = [pl.BlockSpec((n_parts, tr, C), functools.partial(lambda i, first, last: (0, jnp.clip(i - first, 0, last), 0),
                                                                first=first, last=end - first - 1)) for first, end in bounds]
    return pl.pallas_call(
        body, grid=(R // tr,), in_specs=p_specs + [blk, blk, blk],
        out_specs=[blk] * 4, out_shape=[jax.ShapeDtypeStruct((R, C), F32)] * 4, name=name,
        compiler_params=_params(("parallel",)),
    )(*parts, w, m, v)


def _offsets(L, D):
    o = {"pv": 0, "pz": BRANCH_W, "cq": 2 * BRANCH_W, "ckv": 2 * BRANCH_W + L, "mz": 2 * BRANCH_W + 2 * L}
    for prev, nxt in (("mz", "cb"), ("cb", "cc"), ("cc", "cx"), ("cx", "cz"), ("cz", "xq"), ("xq", "xz"), ("xz", "g")):
        o[nxt] = o[prev] + BRANCH_W
    o["kr"] = o["g"] + 4 * D
    o["n"] = o["kr"] + 128
    return o


def _unshard(gathered, axis):
    moved = jnp.moveaxis(gathered, 0, axis)
    shape = list(moved.shape)
    shape[axis:axis + 2] = [shape[axis] * shape[axis + 1]]
    return moved.reshape(shape)


def _reshard(full, axis):
    shape = list(full.shape)
    shape[axis:axis + 1] = [N_DEV, shape[axis] // N_DEV]
    return jnp.moveaxis(full.reshape(shape), axis, 0)


def _pad_heads(g, width):
    return jnp.pad(g, ((0, 0), (0, width - g.shape[1])))


def kernel(x, mem, positions, norm_g, w_in, gate_b, pool_w, pool_scale, q_a_norm_g, kv_a_norm_g, w_uq, w_ukv, mla_q_norm_g, mla_k_norm_g, conv_w, mem_norm_g, w_mem_kv, xattn_q_norm_g, xattn_k_norm_g, w_branch, w_out, loss_target, m_norm_g, m_w_in, m_gate_b, m_pool_w, m_pool_scale, m_q_a_norm_g, m_kv_a_norm_g, m_w_uq, m_w_ukv, m_mla_q_norm_g, m_mla_k_norm_g, m_conv_w, m_mem_norm_g, m_w_mem_kv, m_xattn_q_norm_g, m_xattn_k_norm_g, m_w_branch, m_w_out, v_norm_g, v_w_in, v_gate_b, v_pool_w, v_pool_scale, v_q_a_norm_g, v_kv_a_norm_g, v_w_uq, v_w_ukv, v_mla_q_norm_g, v_mla_k_norm_g, v_conv_w, v_mem_norm_g, v_w_mem_kv, v_xattn_q_norm_g, v_xattn_k_norm_g, v_w_branch, v_w_out):
    depth = norm_g.shape[0]
    S, D = x.shape[1], x.shape[2]
    L = w_uq.shape[1]
    offs = _offsets(L, D)
    seg_a = 2 * BRANCH_W + 2 * L
    xs = x.reshape(S, D)
    mems = mem.reshape(mem.shape[1], D)

    sharded = dict(w_in=w_in, pool_w=pool_w, w_uq=w_uq, w_ukv=w_ukv, conv_w=conv_w, w_mem_kv=w_mem_kv, w_branch=w_branch, w_out=w_out)
    shard_axis = dict(w_in=2, pool_w=2, w_uq=2, w_ukv=2, conv_w=2, w_mem_kv=1, w_branch=3, w_out=1)
    def permuted_w_in(g):
        f = _unshard(g, 1)
        return jnp.concatenate([f[:, :seg_a], f[:, seg_a + QK_ROPE:], f[:, seg_a:seg_a + QK_ROPE], jnp.zeros((D, 128 - QK_ROPE), BF16)], axis=-1)

    w_in_local = [w_in[l].astype(BF16) for l in range(depth)]
    w_in_p = [permuted_w_in(_run_plan(_GatherPlan([w_in_local[0]]), "gather_w_in0")[0])] + [None] * (depth - 1)
    rest_keys = [k for k in shard_axis if k != "w_in"]
    rest_plan = _GatherPlan([sharded[k] if k == "conv_w" else sharded[k].astype(BF16) for k in rest_keys])
    later_w_in_plan = _GatherPlan(w_in_local[1:]) if depth > 1 else None

    inv = ROPE_THETA ** (-jnp.arange(0, QK_ROPE, 2, dtype=F32) / QK_ROPE)
    ang = positions.reshape(S).astype(F32)[:, None] * inv
    cos, sin, z = jnp.cos(ang), jnp.sin(ang), jnp.zeros_like(ang)
    tabs = (jnp.concatenate([cos, cos, z, z], axis=1), jnp.concatenate([-sin, z, z, z], axis=1), jnp.concatenate([z, sin, z, z], axis=1))

    row = lambda a, l: a[l][None, :]
    saved = []
    cur = xs
    for l in range(depth):
        h, hT = _rms_rows(cur, D, 0, row(norm_g, l), f"norm_fwd{l}", with_transpose=True)
        proj_tiles = dict(tm=512, tn=3712, tk=2048, n_outer=True)
        if l == 0:
            proj, carried = _matmul(h, w_in_p[0], "nn", F32, "proj_fwd0", comm=rest_plan, **proj_tiles)
            gathered = {k: _unshard(g, shard_axis[k]) for k, g in zip(rest_keys, carried)}
            conv_full = gathered["conv_w"]
            pool_w_f = gathered["pool_w"]
            w_uq_f = gathered["w_uq"].reshape(depth, L, MLA_HEADS, QK_HEAD)
            w_uq_p = jnp.pad(w_uq_f, ((0, 0), (0, 0), (0, 0), (0, QK_PAD - QK_HEAD))).reshape(depth, L, MLA_HEADS * QK_PAD)
            w_ukv_f = gathered["w_ukv"].reshape(depth, L, MLA_HEADS, QK_NOPE + V_HEAD)
            w_ukv_p = jnp.concatenate([w_ukv_f[..., :QK_NOPE].reshape(depth, L, -1), w_ukv_f[..., QK_NOPE:].reshape(depth, L, -1)], axis=-1)
            w_mem_f = gathered["w_mem_kv"]
            w_br_f = gathered["w_branch"]
            w_out_f = gathered["w_out"]
        else:
            proj = _matmul(h, w_in_p[l], "nn", F32, f"proj_fwd{l}", **proj_tiles)
        y_pool = _pool_fwd(proj, offs, pool_w_f[l], row(pool_scale, l), f"pool_fwd{l}")
        cqn = _rms_rows(proj, L, offs["cq"], row(q_a_norm_g, l), f"cq_norm{l}")
        ckvn = _rms_rows(proj, L, offs["ckv"], row(kv_a_norm_g, l), f"ckv_norm{l}")
        qp = _matmul(cqn, w_uq_p[l], "nn", F32, f"uq_fwd{l}")
        kvp = _matmul(ckvn, w_ukv_p[l], "nn", F32, f"ukv_fwd{l}")
        qg = _pad_heads(row(mla_q_norm_g, l), QK_PAD)
        kg = _pad_heads(row(mla_k_norm_g, l), QK_PAD)
        Q, K, V = _mla_qk_fwd(qp, kvp, proj, offs, tabs, qg, kg, f"mla_qk_fwd{l}")
        (o, y_mla, lse), carried = _flash_fwd(Q, K, V, proj, offs, f"flash_fwd{l}", comm=later_w_in_plan if l == 0 else None)
        if l == 0 and later_w_in_plan is not None:
            w_in_p[1:] = [permuted_w_in(g) for g in carried]
        y_conv = _conv_fwd(proj, offs, conv_full[l], f"conv_fwd{l}")
        memn = _rms_rows(mems, D, 0, row(mem_norm_g, l), f"mem_norm{l}")
        mem_kv = _matmul(memn, w_mem_f, "nn", F32, f"mem_kv{l}", b_idx=(l,))
        y_mem = _xattn_fwd(proj, offs, mem_kv, row(xattn_q_norm_g, l), row(xattn_k_norm_g, l), f"xattn_fwd{l}")
        ys = (y_pool, y_mla, y_conv, y_mem)
        gb = gate_b[l].reshape(4, D)
        merged, tproj = _merge_fwd(ys, w_br_f, l, proj, offs, gb, f"merge_fwd{l}")
        nxt = _matmul(merged, w_out_f, "nn", F32, f"out_fwd{l}", res=cur, b_idx=(l,))
        saved.append(dict(x=cur, hT=hT, proj=proj, cqn=cqn, ckvn=ckvn, qp=qp, kvp=kvp, qg=qg, kg=kg, Q=Q, K=K, V=V, o=o, lse=lse,
                          memn=memn, mem_kv=mem_kv, ys=ys, gb=gb, merged=merged, tproj=tproj))
        cur = nxt

    gx, loss_acc = _loss_rows(cur, loss_target.reshape(S, D), "loss")
    loss = lax.psum(0.5 * loss_acc[0, 0], MESH_AXES)

    grads = {k: [None] * depth for k in ("norm_g", "w_in", "gate_b", "pool_w", "pool_scale", "q_a_norm_g", "kv_a_norm_g", "w_uq", "w_ukv",
                                         "mla_q_norm_g", "mla_k_norm_g", "conv_w", "mem_norm_g", "w_mem_kv", "xattn_q_norm_g",
                                         "xattn_k_norm_g", "w_branch", "w_out")}
    names_sh = list(shard_axis)
    core = lax.axis_index("c").astype(jnp.int32).reshape(1)
    received = [None] * depth
    pending = None

    def pair_sums(one_layer_grads, names, tag):
        staged = [_reshard(g, shard_axis[k] - 1).reshape(4, 2, -1, sharded[k].shape[-1]).astype(BF16) for g, k in zip(one_layer_grads, names)]
        given = _sibling_swap(staged, f"pair_swap_{tag}")
        return [_pair_sum(st, theirs, core, f"pair_sum_{k}_{tag}") for k, st, theirs in zip(names, staged, given)]

    for l in reversed(range(depth)):
        sv = saved[l]
        proj, ys = sv["proj"], sv["ys"]
        dmerged = _matmul(gx, w_out_f, "nt", F32, f"out_bwd_x{l}", b_idx=(l,))
        grads["w_out"][l] = _matmul(sv["merged"], gx, "tn", BF16, f"out_bwd_w{l}")
        dt, dgp0, dgp1, dgp2, dgp3, dgb = _merge_bwd(sv["tproj"], proj, offs, sv["gb"], dmerged, f"merge_bwd{l}")
        grads["gate_b"][l] = dgb.reshape(4 * D)
        dys, dwb = [], []
        for b in range(4):
            dys.append(_matmul(dt, w_br_f, "nt", F32, f"branch_bwd_x{l}_{b}", a_idx=(b,), b_idx=(l, b)))
            dwb.append(_matmul(ys[b], dt, "tn", BF16, f"branch_bwd_w{l}_{b}", b_idx=(b,)))
        grads["w_branch"][l] = jnp.stack(dwb)
        d_pz, e, d_pw, d_ps = _pool_bwd_a(proj, offs, pool_w_f[l], row(pool_scale, l), dys[0], f"pool_bwd_a{l}")
        d_pv = _pool_bwd_b(e, f"pool_bwd_b{l}")
        grads["pool_w"][l], grads["pool_scale"][l] = d_pw, d_ps[0]
        do, d_mz, delta = _mla_gate_bwd(sv["o"], proj, offs, dys[1], f"mla_gate_bwd{l}")
        (dQ, dK, dV), carried = _flash_bwd(sv["Q"], sv["K"], sv["V"], sv["lse"], delta, do, f"flash_bwd{l}",
                                           comm=None if pending is None else pending[1])
        if pending is not None:
            received[pending[0]][0], pending = carried[0], None
        dqp, dkvp, d_kr, dqg, dkg = _mla_qk_bwd(sv["qp"], sv["kvp"], proj, offs, tabs, sv["qg"], sv["kg"], dQ, dK, dV, f"mla_qk_bwd{l}")
        grads["mla_q_norm_g"][l], grads["mla_k_norm_g"][l] = dqg[0, :QK_HEAD], dkg[0, :QK_HEAD]
        dcqn = _matmul(dqp, w_uq_p[l], "nt", F32, f"uq_bwd_x{l}")
        d_wuq = _matmul(sv["cqn"], dqp, "tn", BF16, f"uq_bwd_w{l}")
        dckvn = _matmul(dkvp, w_ukv_p[l], "nt", F32, f"ukv_bwd_x{l}")
        d_wukv = _matmul(sv["ckvn"], dkvp, "tn", BF16, f"ukv_bwd_w{l}")
        grads["w_uq"][l] = d_wuq.reshape(L, MLA_HEADS, QK_PAD)[..., :QK_HEAD].reshape(L, MLA_HEADS * QK_HEAD)
        grads["w_ukv"][l] = jnp.concatenate([d_wukv[:, :MLA_HEADS * QK_NOPE].reshape(L, MLA_HEADS, QK_NOPE),
                                             d_wukv[:, MLA_HEADS * QK_NOPE:].reshape(L, MLA_HEADS, V_HEAD)], axis=-1).reshape(L, -1)
        d_cq, dqa = _rms_rows_bwd(proj, L, offs["cq"], row(q_a_norm_g, l), dcqn, BF16, f"cq_norm_bwd{l}")
        d_ckv, dkva = _rms_rows_bwd(proj, L, offs["ckv"], row(kv_a_norm_g, l), dckvn, BF16, f"ckv_norm_bwd{l}")
        grads["q_a_norm_g"][l], grads["kv_a_norm_g"][l] = dqa[0], dkva[0]
        d_cb, d_cc, d_cx, d_cz, d_cw = _conv_bwd(proj, offs, conv_full[l], dys[2], f"conv_bwd{l}")
        grads["conv_w"][l] = d_cw
        d_xq, d_xz, d_memkv, dxqg, dxkg = _xattn_bwd(proj, offs, sv["mem_kv"], row(xattn_q_norm_g, l), row(xattn_k_norm_g, l), dys[3], f"xattn_bwd{l}")
        grads["xattn_q_norm_g"][l], grads["xattn_k_norm_g"][l] = dxqg[0], dxkg[0]
        grads["w_mem_kv"][l] = _matmul(sv["memn"], d_memkv, "tn", BF16, f"mem_kv_bwd_w{l}")
        dmemn = _matmul(d_memkv, w_mem_f, "nt", F32, f"mem_kv_bwd_x{l}", b_idx=(l,))
        grads["mem_norm_g"][l] = _norm_g_grad(mems, row(mem_norm_g, l), dmemn, f"mem_norm_bwd{l}")[0]
        dproj = jnp.concatenate([d_pv, d_pz, d_cq, d_ckv, d_mz, d_cb, d_cc, d_cx, d_cz, d_xq, d_xz, dgp0, dgp1, dgp2, dgp3, d_kr], axis=1)
        early_names = [k for k in names_sh if k != "w_in"]
        early = pair_sums([grads[k][l] for k in early_names], early_names, f"early{l}")
        dh, early_received = _matmul(dproj, w_in_p[l], "nt", F32, f"proj_bwd_x{l}", tm=1024, tn=1024, tk=3712, comm=_ChipExchangePlan(early))
        n_p = offs["n"]
        w_tiles = dict(tm=512, tn=3712, tk=2048, n_outer=True)
        unpermute = lambda d: jnp.concatenate([d[:, :seg_a], d[:, n_p - 128:n_p - 128 + QK_ROPE], d[:, seg_a:n_p - 128]], axis=1)
        received[l] = [None] + early_received
        if l > 0:
            d_win = unpermute(_matmul(sv["hT"], dproj, "nn", BF16, f"proj_bwd_w{l}", **w_tiles))
            pending = (l, _ChipExchangePlan(pair_sums([d_win], ["w_in"], f"w_in{l}")))
        else:
            n_split = 4
            rows_per = D // n_split
            plan, pieces = None, []
            for part in range(n_split):
                hT_rows = sv["hT"][part * rows_per:(part + 1) * rows_per]
                if plan is None:
                    d = _matmul(hT_rows, dproj, "nn", BF16, f"proj_bwd_w0_part{part}", **w_tiles)
                else:
                    d, got = _matmul(hT_rows, dproj, "nn", BF16, f"proj_bwd_w0_part{part}", comm=plan, **w_tiles)
                    pieces.append(got[0])
                plan = _ChipExchangePlan(pair_sums([unpermute(d)], ["w_in"], f"w_in0_part{part}"))
            pieces.append(_run_plan(plan, "chip_exchange_w_in0_last")[0])
            received[0][0] = pieces
        gx, dng = _rms_rows_bwd(sv["x"], D, 0, row(norm_g, l), dh, F32, f"norm_bwd{l}", res=gx)
        grads["norm_g"][l] = dng[0]
    grad_x = gx.reshape(1, S, D)
    full = {k: jnp.stack(v) for k, v in grads.items() if k not in shard_axis}

    moments = dict(w_in=(m_w_in, v_w_in), pool_w=(m_pool_w, v_pool_w), w_uq=(m_w_uq, v_w_uq), w_ukv=(m_w_ukv, v_w_ukv),
                   conv_w=(m_conv_w, v_conv_w), w_mem_kv=(m_w_mem_kv, v_w_mem_kv), w_branch=(m_w_branch, v_w_branch), w_out=(m_w_out, v_w_out))
    results = {}
    for ki, k in enumerate(names_sh):
        w = sharded[k]
        m, v = moments[k]
        flat = lambda a: a.reshape(-1, w.shape[-1])
        parts = [p for l in range(depth) for p in (received[l][ki] if isinstance(received[l][ki], list) else [received[l][ki]])]
        if any(p.shape[1] % 16 for p in parts):
            parts = [jnp.concatenate(parts, axis=1)]
        outs = _adamw(parts, flat(w), flat(m), flat(v), f"adamw_{k}")
        results[k] = [o.reshape(w.shape) for o in outs]

    small = dict(norm_g=(norm_g, m_norm_g, v_norm_g), gate_b=(gate_b, m_gate_b, v_gate_b), pool_scale=(pool_scale, m_pool_scale, v_pool_scale),
                 q_a_norm_g=(q_a_norm_g, m_q_a_norm_g, v_q_a_norm_g), kv_a_norm_g=(kv_a_norm_g, m_kv_a_norm_g, v_kv_a_norm_g),
                 mla_q_norm_g=(mla_q_norm_g, m_mla_q_norm_g, v_mla_q_norm_g), mla_k_norm_g=(mla_k_norm_g, m_mla_k_norm_g, v_mla_k_norm_g),
                 mem_norm_g=(mem_norm_g, m_mem_norm_g, v_mem_norm_g), xattn_q_norm_g=(xattn_q_norm_g, m_xattn_q_norm_g, v_xattn_q_norm_g),
                 xattn_k_norm_g=(xattn_k_norm_g, m_xattn_k_norm_g, v_xattn_k_norm_g))
    names_sm = list(small)
    sizes = [small[k][0].size for k in names_sm]
    total = sum(sizes)
    rows = -(-total // (8 * 128)) * 8

    def pack(arrs):
        flat = jnp.concatenate([a.reshape(-1) for a in arrs])
        return jnp.pad(flat, (0, rows * 128 - total)).reshape(rows, 128)

    g_small = _run_plan(_GatherPlan([pack([full[k] for k in names_sm])]), "gather_small_grads")[0]
    outs = _adamw([g_small], pack([small[k][0] for k in names_sm]), pack([small[k][1] for k in names_sm]),
                  pack([small[k][2] for k in names_sm]), "adamw_small")
    for kind, o in enumerate(outs):
        flat, pos = o.reshape(-1), 0
        for k, n in zip(names_sm, sizes):
            results.setdefault(k, [None] * 4)[kind] = flat[pos:pos + n].reshape(small[k][0].shape)
            pos += n

    order = ["norm_g", "w_in", "gate_b", "pool_w", "pool_scale", "q_a_norm_g", "kv_a_norm_g", "w_uq", "w_ukv", "mla_q_norm_g",
             "mla_k_norm_g", "conv_w", "mem_norm_g", "w_mem_kv", "xattn_q_norm_g", "xattn_k_norm_g", "w_branch", "w_out"]
    return (loss, grad_x, *[results[k][0] for k in order], *[results[k][1] for k in order],
            *[results[k][2] for k in order], *[results[k][3] for k in order])
```

```python
import functools

import jax
import jax.numpy as jnp
from jax import lax
from jax.experimental import pallas as pl
from jax.experimental.pallas import tpu as pltpu

F32 = jnp.float32
BF16 = jnp.bfloat16
MESH_AXES = ("x", "y", "c")
N_DEV = 8

BRANCH_W = 1024
POOL_WINDOWS = (2, 4, 8, 16)
POOL_GW = BRANCH_W // 4
MLA_HEADS = 8
QK_NOPE = 128
QK_ROPE = 64
QK_HEAD = QK_NOPE + QK_ROPE
QK_PAD = 256
V_HEAD = 128
XATTN_HEADS = 4
XATTN_HD = BRANCH_W // XATTN_HEADS
ROPE_THETA = 10000.0
EPS = 1e-6
ADAM_LR = 0.001
ADAM_B1 = 0.9
ADAM_B2 = 0.999
ADAM_EPS = 1e-08
ADAM_WD = 0.01
ADAM_STEP = 10
HALO = 16
NEG = -1e30
SCORE_SCALE_LOG2 = (QK_HEAD ** -0.5) * 1.4426950408889634
VMEM_LIMIT = 56 << 20


def _pick(n, pref, align):
    if n <= pref:
        return n
    t = (pref // align) * align
    while t >= align:
        if n % t == 0:
            return t
        t -= align
    return n


def _params(sem):
    return pltpu.CompilerParams(dimension_semantics=sem, vmem_limit_bytes=VMEM_LIMIT)


def _silu(z):
    return z * (1.0 / (1.0 + jnp.exp(-z)))


def _silu_and_grad(z):
    s = 1.0 / (1.0 + jnp.exp(-z))
    return z * s, s * (1.0 + z * (1.0 - s))


def _dot(a, b):
    return jnp.dot(a.astype(BF16), b.astype(BF16), preferred_element_type=F32)


def _dot_nt(a, b):
    return lax.dot_general(a.astype(BF16), b.astype(BF16), (((1,), (1,)), ((), ())), preferred_element_type=F32)


def _dot_tn(a, b):
    return lax.dot_general(a.astype(BF16), b.astype(BF16), (((0,), (0,)), ((), ())), preferred_element_type=F32)


def _rms(x, g, n):
    r = lax.rsqrt(jnp.sum(x * x, axis=-1, keepdims=True) / n + EPS)
    return x * r * g, r


def _rms_bwd(x, g, r, dy, n):
    dyg = dy * g
    dx = r * dyg - x * (r * r * r) * (jnp.sum(dyg * x, axis=-1, keepdims=True) / n)
    dg = jnp.sum(dy * x * r, axis=0, keepdims=True)
    return dx, dg


def _col(tm, w, off):
    assert off % w == 0
    cb = off // w
    return pl.BlockSpec((tm, w), lambda i: (i, cb))


def _whole(shape):
    nd = len(shape)
    return pl.BlockSpec(shape, lambda *_: (0,) * nd)


def _lead_spec(block, fn, idx):
    if not idx:
        return pl.BlockSpec(block, fn)
    return pl.BlockSpec((None,) * len(idx) + block, lambda i, j, k: tuple(idx) + fn(i, j, k))


def _hosted_call(plan, body, ins, in_specs, out_specs, out_shape, scratch, grid, name):
    n_in, n_out, n_scr, n = len(ins), len(out_specs), len(scratch), plan.n
    n_steps = 1
    for g in grid:
        n_steps *= g

    def wrapped(*refs):
        a, c_in = refs[:n_in], refs[n_in:n_in + n]
        o, c_out = refs[n_in + n:n_in + n + n_out], refs[n_in + n + n_out:n_in + 2 * n + n_out]
        rest = refs[n_in + 2 * n + n_out:]
        scr, sems = rest[:n_scr], rest[n_scr:]
        step = 0
        for d, g in enumerate(grid):
            step = step * g + pl.program_id(d)

        @pl.when(step == 0)
        def _():
            plan.start(c_in, c_out, sems)

        body(*a, *o, *scr)

        @pl.when(step == n_steps // 3)
        def _():
            plan.relay(0, c_in, c_out, sems)

        @pl.when(step == (2 * n_steps) // 3)
        def _():
            plan.relay(1, c_in, c_out, sems)

        @pl.when(step == n_steps - 1)
        def _():
            plan.finish(c_in, c_out, sems)

    res = pl.pallas_call(
        wrapped, grid=grid, in_specs=list(in_specs) + [HBM_SPEC] * n, out_specs=list(out_specs) + [HBM_SPEC] * n,
        out_shape=list(out_shape) + plan.out_shape, scratch_shapes=list(scratch) + plan.scratch, name=name,
        compiler_params=_params(("arbitrary",) * len(grid)),
    )(*ins, *plan.arrs)
    return list(res[:n_out]), list(res[n_out:])


def _matmul(a, b, mode, out_dtype, name, tm=1024, tn=1024, tk=2048, res=None, a_idx=(), b_idx=(), n_outer=False, comm=None):
    a_shape, b_shape = a.shape[len(a_idx):], b.shape[len(b_idx):]
    if mode == "nn":
        (M, K), N = a_shape, b_shape[1]
    elif mode == "nt":
        (M, K), N = a_shape, b_shape[0]
    else:
        (K, M), N = a_shape, b_shape[1]
    tm, tn, tk = _pick(M, tm, 128), _pick(N, tn, 128), _pick(K, tk, 128)
    nk = K // tk
    order = (lambda fn: (lambda j, i, k: fn(i, j, k))) if n_outer else (lambda fn: fn)
    if mode == "nn":
        a_spec = _lead_spec((tm, tk), order(lambda i, j, k: (i, k)), a_idx)
        b_spec = _lead_spec((tk, tn), order(lambda i, j, k: (k, j)), b_idx)
        op = _dot
    elif mode == "nt":
        a_spec = _lead_spec((tm, tk), order(lambda i, j, k: (i, k)), a_idx)
        b_spec = _lead_spec((tn, tk), order(lambda i, j, k: (j, k)), b_idx)
        op = _dot_nt
    else:
        a_spec = _lead_spec((tk, tm), order(lambda i, j, k: (k, i)), a_idx)
        b_spec = _lead_spec((tk, tn), order(lambda i, j, k: (k, j)), b_idx)
        op = _dot_tn
    o_spec = pl.BlockSpec((tm, tn), order(lambda i, j, k: (i, j)))
    has_res = res is not None

    def body(*refs):
        a_ref, b_ref = refs[0], refs[1]
        res_ref = refs[2] if has_res else None
        o_ref = refs[2 + has_res]
        part = op(a_ref[...], b_ref[...])

        def finish(acc):
            if has_res:
                acc = res_ref[...] + acc
            o_ref[...] = acc.astype(out_dtype)

        if nk == 1:
            finish(part)
        else:
            acc_ref = refs[3 + has_res]
            k = pl.program_id(2)

            @pl.when(k == 0)
            def _():
                acc_ref[...] = part

            @pl.when(k > 0)
            def _():
                acc_ref[...] += part

            @pl.when(k == nk - 1)
            def _():
                finish(acc_ref[...])

    ins = [a, b] + ([res] if has_res else [])
    in_specs = [a_spec, b_spec] + ([o_spec] if has_res else [])
    scratch = [pltpu.VMEM((tm, tn), F32)] if nk > 1 else []
    grid = (N // tn, M // tm, nk) if n_outer else (M // tm, N // tn, nk)
    out_shape = jax.ShapeDtypeStruct((M, N), out_dtype)
    if comm is not None:
        (out,), carried = _hosted_call(comm, body, ins, in_specs, [o_spec], [out_shape], scratch, grid, name)
        return out, carried
    return pl.pallas_call(
        body, grid=grid, in_specs=in_specs, out_specs=o_spec, out_shape=out_shape, scratch_shapes=scratch, name=name,
        compiler_params=_params(("parallel", "parallel", "arbitrary")),
    )(*ins)


def _rms_rows(x, width, off, g, name, with_transpose=False):
    S = x.shape[0]
    tm = _pick(S, 512, 128 if with_transpose else 16)

    def body(x_ref, g_ref, o_ref, *t_ref):
        y, _ = _rms(x_ref[...], g_ref[...], width)
        o_ref[...] = y.astype(BF16)
        if with_transpose:
            t_ref[0][...] = y.T.astype(BF16)

    out_specs = [_col(tm, width, 0)] + ([pl.BlockSpec((width, tm), lambda i: (0, i))] if with_transpose else [])
    out_shape = [jax.ShapeDtypeStruct((S, width), BF16)] + ([jax.ShapeDtypeStruct((width, S), BF16)] if with_transpose else [])
    res = pl.pallas_call(
        body, grid=(S // tm,), in_specs=[_col(tm, width, off), _whole((1, width))], out_specs=out_specs,
        out_shape=out_shape, name=name, compiler_params=_params(("parallel",)),
    )(x, g)
    return res if with_transpose else res[0]


def _rms_rows_bwd(x, width, off, g, dy, out_dtype, name, res=None):
    S = x.shape[0]
    tm = _pick(S, 256, 16)
    has_res = res is not None

    def body(*refs):
        x_ref, g_ref, dy_ref = refs[:3]
        res_ref = refs[3] if has_res else None
        dx_ref, dg_ref = refs[3 + has_res:]
        xv, gv = x_ref[...], g_ref[...]
        _, r = _rms(xv, gv, width)
        dx, dg = _rms_bwd(xv, gv, r, dy_ref[...], width)
        if has_res:
            dx = res_ref[...] + dx
        dx_ref[...] = dx.astype(out_dtype)

        @pl.when(pl.program_id(0) == 0)
        def _():
            dg_ref[...] = jnp.zeros_like(dg_ref)

        dg_ref[...] += dg

    ins = [x, g, dy] + ([res] if has_res else [])
    in_specs = [_col(tm, width, off), _whole((1, width)), _col(tm, width, 0)] + ([_col(tm, width, 0)] if has_res else [])
    return pl.pallas_call(
        body, grid=(S // tm,), in_specs=in_specs, out_specs=[_col(tm, width, 0), _whole((1, width))],
        out_shape=[jax.ShapeDtypeStruct((S, width), out_dtype), jax.ShapeDtypeStruct((1, width), F32)], name=name,
        compiler_params=_params(("arbitrary",)),
    )(*ins)


def _norm_g_grad(x, g, dy, name):
    S, width = x.shape
    tm = _pick(S, 256, 16)

    def body(x_ref, g_ref, dy_ref, dg_ref):
        xv = x_ref[...]
        _, r = _rms(xv, g_ref[...], width)

        @pl.when(pl.program_id(0) == 0)
        def _():
            dg_ref[...] = jnp.zeros_like(dg_ref)

        dg_ref[...] += jnp.sum(dy_ref[...] * xv * r, axis=0, keepdims=True)

    return pl.pallas_call(
        body, grid=(S // tm,), in_specs=[_col(tm, width, 0), _whole((1, width)), _col(tm, width, 0)],
        out_specs=_whole((1, width)), out_shape=jax.ShapeDtypeStruct((1, width), F32), name=name,
        compiler_params=_params(("arbitrary",)),
    )(x, g, dy)


def _loss_rows(y, target, name):
    S, D = y.shape
    tm = _pick(S, 512, 16)

    def body(y_ref, t_ref, g_ref, l_ref):
        err = y_ref[...] - t_ref[...]
        g_ref[...] = err / D

        @pl.when(pl.program_id(0) == 0)
        def _():
            l_ref[...] = jnp.zeros_like(l_ref)

        row = jnp.sum(err * err, axis=-1, keepdims=True) / D
        l_ref[...] += jnp.sum(row, axis=0, keepdims=True)

    return pl.pallas_call(
        body, grid=(S // tm,), in_specs=[_col(tm, D, 0), _col(tm, D, 0)], out_specs=[_col(tm, D, 0), _whole((1, 1))],
        out_shape=[jax.ShapeDtypeStruct((S, D), F32), jax.ShapeDtypeStruct((1, 1), F32)], name=name,
        compiler_params=_params(("arbitrary",)),
    )(y, target)


def _row_index(i, ts):
    return i * ts + lax.broadcasted_iota(jnp.int32, (ts, 1), 0)


def _pool_mixed(scr, pv_ref, g, t, ts):
    w, lo = POOL_WINDOWS[g], g * POOL_GW
    win = scr[HALO:HALO + ts, lo:lo + POOL_GW]
    for k in range(1, w):
        win = win + scr[HALO - k:HALO - k + ts, lo:lo + POOL_GW]
    cnt = jnp.minimum(t + 1, w).astype(F32)
    return win / cnt - pv_ref[:, lo:lo + POOL_GW], cnt


def _halo_before(ts, off):
    nb = ts // HALO
    cb = off // BRANCH_W
    return pl.BlockSpec((HALO, BRANCH_W), lambda i: (jnp.maximum(i * nb - 1, 0), cb))


def _halo_after(ts, off, n_tiles):
    nb = ts // HALO
    cb = off // BRANCH_W
    last = n_tiles * nb - 1
    return pl.BlockSpec((HALO, BRANCH_W), lambda i: (jnp.minimum((i + 1) * nb, last), cb))


def _pool_fwd(proj, offs, pool_w, pool_scale, name):
    S = proj.shape[0]
    ts = _pick(S, 512, HALO)

    def body(pv_ref, halo_ref, pz_ref, w_ref, sc_ref, y_ref, scr):
        i = pl.program_id(0)
        scr[0:HALO, :] = jnp.where(i > 0, halo_ref[...], 0.0)
        scr[HALO:HALO + ts, :] = pv_ref[...]
        t = _row_index(i, ts)
        for g in range(4):
            lo = g * POOL_GW
            mixed, _ = _pool_mixed(scr, pv_ref, g, t, ts)
            out = _dot(mixed, w_ref[g])
            y_ref[:, lo:lo + POOL_GW] = (out * sc_ref[:, lo:lo + POOL_GW] * _silu(pz_ref[:, lo:lo + POOL_GW])).astype(BF16)

    return pl.pallas_call(
        body, grid=(S // ts,),
        in_specs=[_col(ts, BRANCH_W, offs["pv"]), _halo_before(ts, offs["pv"]), _col(ts, BRANCH_W, offs["pz"]),
                  _whole((4, POOL_GW, POOL_GW)), _whole((1, BRANCH_W))],
        out_specs=_col(ts, BRANCH_W, 0), out_shape=jax.ShapeDtypeStruct((S, BRANCH_W), BF16),
        scratch_shapes=[pltpu.VMEM((HALO + ts, BRANCH_W), F32)], name=name, compiler_params=_params(("parallel",)),
    )(proj, proj, proj, pool_w, pool_scale)


def _pool_bwd_a(proj, offs, pool_w, pool_scale, dy, name):
    S = proj.shape[0]
    ts = _pick(S, 512, HALO)

    def body(pv_ref, halo_ref, pz_ref, w_ref, sc_ref, dy_ref, dpz_ref, e_ref, dw_ref, dsc_ref, scr):
        i = pl.program_id(0)
        scr[0:HALO, :] = jnp.where(i > 0, halo_ref[...], 0.0)
        scr[HALO:HALO + ts, :] = pv_ref[...]
        t = _row_index(i, ts)

        @pl.when(i == 0)
        def _():
            dw_ref[...] = jnp.zeros_like(dw_ref)
            dsc_ref[...] = jnp.zeros_like(dsc_ref)

        for g in range(4):
            lo = g * POOL_GW
            sl = slice(lo, lo + POOL_GW)
            mixed, cnt = _pool_mixed(scr, pv_ref, g, t, ts)
            out = _dot(mixed, w_ref[g])
            sz, dsz = _silu_and_grad(pz_ref[:, sl])
            dyv, sc = dy_ref[:, sl], sc_ref[:, sl]
            d_out = dyv * sc * sz
            dsc_ref[:, sl] += jnp.sum(dyv * out * sz, axis=0, keepdims=True)
            dpz_ref[:, sl] = (dyv * out * sc * dsz).astype(BF16)
            dw_ref[g] += _dot_tn(mixed, d_out)
            e_ref[:, sl] = _dot_nt(d_out, w_ref[g]) / cnt

    return pl.pallas_call(
        body, grid=(S // ts,),
        in_specs=[_col(ts, BRANCH_W, offs["pv"]), _halo_before(ts, offs["pv"]), _col(ts, BRANCH_W, offs["pz"]),
                  _whole((4, POOL_GW, POOL_GW)), _whole((1, BRANCH_W)), _col(ts, BRANCH_W, 0)],
        out_specs=[_col(ts, BRANCH_W, 0), _col(ts, BRANCH_W, 0), _whole((4, POOL_GW, POOL_GW)), _whole((1, BRANCH_W))],
        out_shape=[jax.ShapeDtypeStruct((S, BRANCH_W), BF16), jax.ShapeDtypeStruct((S, BRANCH_W), F32),
                   jax.ShapeDtypeStruct((4, POOL_GW, POOL_GW), F32), jax.ShapeDtypeStruct((1, BRANCH_W), F32)],
        scratch_shapes=[pltpu.VMEM((HALO + ts, BRANCH_W), F32)], name=name, compiler_params=_params(("arbitrary",)),
    )(proj, proj, proj, pool_w, pool_scale, dy)


def _pool_bwd_b(e, name):
    S = e.shape[0]
    ts = _pick(S, 512, HALO)
    n_tiles = S // ts

    def body(e_ref, halo_ref, dpv_ref, scr):
        i = pl.program_id(0)
        scr[0:ts, :] = e_ref[...]
        scr[ts:ts + HALO, :] = jnp.where(i < n_tiles - 1, halo_ref[...], 0.0)
        t = _row_index(i, ts)
        for g in range(4):
            w, lo = POOL_WINDOWS[g], g * POOL_GW
            acc = scr[0:ts, lo:lo + POOL_GW]
            ev = acc
            for k in range(1, w):
                acc = acc + scr[k:k + ts, lo:lo + POOL_GW]
            cnt = jnp.minimum(t + 1, w).astype(F32)
            dpv_ref[:, lo:lo + POOL_GW] = (acc - ev * cnt).astype(BF16)

    return pl.pallas_call(
        body, grid=(n_tiles,), in_specs=[_col(ts, BRANCH_W, 0), _halo_after(ts, 0, n_tiles)], out_specs=_col(ts, BRANCH_W, 0),
        out_shape=jax.ShapeDtypeStruct((S, BRANCH_W), BF16), scratch_shapes=[pltpu.VMEM((ts + HALO, BRANCH_W), F32)],
        name=name, compiler_params=_params(("parallel",)),
    )(e, e)


def _conv_fwd(proj, offs, conv_w, name):
    S = proj.shape[0]
    ts = _pick(S, 512, HALO)

    def body(cb_ref, cc_ref, cx_ref, cz_ref, hc_ref, hx_ref, w_ref, y_ref, scr):
        i = pl.program_id(0)
        scr[0:HALO, :] = jnp.where(i > 0, hc_ref[...] * hx_ref[...], 0.0)
        scr[HALO:HALO + ts, :] = cc_ref[...] * cx_ref[...]
        y = w_ref[0:1, :] * scr[HALO - 2:HALO - 2 + ts, :] + w_ref[1:2, :] * scr[HALO - 1:HALO - 1 + ts, :] + w_ref[2:3, :] * scr[HALO:HALO + ts, :]
        y_ref[...] = (cb_ref[...] * y * _silu(cz_ref[...])).astype(BF16)

    return pl.pallas_call(
        body, grid=(S // ts,),
        in_specs=[_col(ts, BRANCH_W, offs["cb"]), _col(ts, BRANCH_W, offs["cc"]), _col(ts, BRANCH_W, offs["cx"]),
                  _col(ts, BRANCH_W, offs["cz"]), _halo_before(ts, offs["cc"]), _halo_before(ts, offs["cx"]), _whole((3, BRANCH_W))],
        out_specs=_col(ts, BRANCH_W, 0), out_shape=jax.ShapeDtypeStruct((S, BRANCH_W), BF16),
        scratch_shapes=[pltpu.VMEM((HALO + ts, BRANCH_W), F32)], name=name, compiler_params=_params(("parallel",)),
    )(proj, proj, proj, proj, proj, proj, conv_w)


def _conv_bwd(proj, offs, conv_w, dy, name):
    S = proj.shape[0]
    ts = _pick(S, 256, HALO)
    n_tiles = S // ts

    def body(cb_ref, cc_ref, cx_ref, cz_ref, hc_ref, hx_ref, w_ref, dy_ref, ady_ref, acb_ref, acz_ref,
             dcb_ref, dcc_ref, dcx_ref, dcz_ref, dw_ref, scr_u, scr_d):
        i = pl.program_id(0)
        u = cc_ref[...] * cx_ref[...]
        scr_u[0:HALO, :] = jnp.where(i > 0, hc_ref[...] * hx_ref[...], 0.0)
        scr_u[HALO:HALO + ts, :] = u
        u1 = scr_u[HALO - 1:HALO - 1 + ts, :]
        u2 = scr_u[HALO - 2:HALO - 2 + ts, :]
        y = w_ref[0:1, :] * u2 + w_ref[1:2, :] * u1 + w_ref[2:3, :] * u
        sz, dsz = _silu_and_grad(cz_ref[...])
        dyv, cb = dy_ref[...], cb_ref[...]
        dcb_ref[...] = (dyv * y * sz).astype(BF16)
        dcz_ref[...] = (dyv * cb * y * dsz).astype(BF16)
        d_y = dyv * cb * sz
        scr_d[0:ts, :] = d_y
        scr_d[ts:ts + HALO, :] = jnp.where(i < n_tiles - 1, ady_ref[...] * acb_ref[...] * _silu(acz_ref[...]), 0.0)
        du = w_ref[2:3, :] * d_y + w_ref[1:2, :] * scr_d[1:1 + ts, :] + w_ref[0:1, :] * scr_d[2:2 + ts, :]
        dcc_ref[...] = (du * cx_ref[...]).astype(BF16)
        dcx_ref[...] = (du * cc_ref[...]).astype(BF16)

        @pl.when(i == 0)
        def _():
            dw_ref[...] = jnp.zeros_like(dw_ref)

        dw_ref[0:1, :] += jnp.sum(d_y * u2, axis=0, keepdims=True)
        dw_ref[1:2, :] += jnp.sum(d_y * u1, axis=0, keepdims=True)
        dw_ref[2:3, :] += jnp.sum(d_y * u, axis=0, keepdims=True)

    row = lambda off: _col(ts, BRANCH_W, off)
    return pl.pallas_call(
        body, grid=(n_tiles,),
        in_specs=[row(offs["cb"]), row(offs["cc"]), row(offs["cx"]), row(offs["cz"]), _halo_before(ts, offs["cc"]),
                  _halo_before(ts, offs["cx"]), _whole((3, BRANCH_W)), row(0), _halo_after(ts, 0, n_tiles),
                  _halo_after(ts, offs["cb"], n_tiles), _halo_after(ts, offs["cz"], n_tiles)],
        out_specs=[row(0), row(0), row(0), row(0), _whole((3, BRANCH_W))],
        out_shape=[jax.ShapeDtypeStruct((S, BRANCH_W), BF16)] * 4 + [jax.ShapeDtypeStruct((3, BRANCH_W), F32)],
        scratch_shapes=[pltpu.VMEM((HALO + ts, BRANCH_W), F32), pltpu.VMEM((ts + HALO, BRANCH_W), F32)], name=name,
        compiler_params=_params(("arbitrary",)),
    )(proj, proj, proj, proj, proj, proj, conv_w, dy, dy, proj, proj)


def _xattn_head(xq, kraw, v, qg, kg):
    qn, rq = _rms(xq, qg, XATTN_HD)
    kn, rk = _rms(kraw, kg, XATTN_HD)
    s = _dot_nt(qn, kn) * (XATTN_HD ** -0.5)
    e = jnp.exp(s - jnp.max(s, axis=-1, keepdims=True))
    p = e / jnp.sum(e, axis=-1, keepdims=True)
    return qn, rq, kn, rk, p, _dot(p, v)


def _xattn_fwd(proj, offs, mem_kv, qg, kg, name):
    S, M = proj.shape[0], mem_kv.shape[0]
    tm = _pick(S, 512, 16)

    def body(xq_ref, xz_ref, kv_ref, qg_ref, kg_ref, y_ref):
        for h in range(XATTN_HEADS):
            sl = slice(h * XATTN_HD, (h + 1) * XATTN_HD)
            vs = slice(BRANCH_W + h * XATTN_HD, BRANCH_W + (h + 1) * XATTN_HD)
            o = _xattn_head(xq_ref[:, sl], kv_ref[:, sl], kv_ref[:, vs], qg_ref[...], kg_ref[...])[-1]
            y_ref[:, sl] = (o * _silu(xz_ref[:, sl])).astype(BF16)

    return pl.pallas_call(
        body, grid=(S // tm,),
        in_specs=[_col(tm, BRANCH_W, offs["xq"]), _col(tm, BRANCH_W, offs["xz"]), _whole((M, 2 * BRANCH_W)),
                  _whole((1, XATTN_HD)), _whole((1, XATTN_HD))],
        out_specs=_col(tm, BRANCH_W, 0), out_shape=jax.ShapeDtypeStruct((S, BRANCH_W), BF16), name=name,
        compiler_params=_params(("parallel",)),
    )(proj, proj, mem_kv, qg, kg)


def _xattn_bwd(proj, offs, mem_kv, qg, kg, dy, name):
    S, M = proj.shape[0], mem_kv.shape[0]
    tm = _pick(S, 256, 16)
    n_tiles = S // tm

    def body(xq_ref, xz_ref, kv_ref, qg_ref, kg_ref, dy_ref, dxq_ref, dxz_ref, dkv_ref, dqg_ref, dkg_ref, dkn_acc):
        i = pl.program_id(0)

        @pl.when(i == 0)
        def _():
            dkv_ref[...] = jnp.zeros_like(dkv_ref)
            dqg_ref[...] = jnp.zeros_like(dqg_ref)
            dkg_ref[...] = jnp.zeros_like(dkg_ref)
            dkn_acc[...] = jnp.zeros_like(dkn_acc)

        qg_v, kg_v = qg_ref[...], kg_ref[...]
        for h in range(XATTN_HEADS):
            sl = slice(h * XATTN_HD, (h + 1) * XATTN_HD)
            vs = slice(BRANCH_W + h * XATTN_HD, BRANCH_W + (h + 1) * XATTN_HD)
            xq, v = xq_ref[:, sl], kv_ref[:, vs]
            qn, rq, kn, rk, p, o = _xattn_head(xq, kv_ref[:, sl], v, qg_v, kg_v)
            sz, dsz = _silu_and_grad(xz_ref[:, sl])
            dyv = dy_ref[:, sl]
            dxz_ref[:, sl] = (dyv * o * dsz).astype(BF16)
            do = dyv * sz
            dkv_ref[:, vs] += _dot_tn(p, do)
            dp = _dot_nt(do, v)
            ds = p * (dp - jnp.sum(dp * p, axis=-1, keepdims=True)) * (XATTN_HD ** -0.5)
            dqn = _dot(ds, kn)
            dkn_acc[:, sl] += _dot_tn(ds, qn)
            dxq, dqg = _rms_bwd(xq, qg_v, rq, dqn, XATTN_HD)
            dxq_ref[:, sl] = dxq.astype(BF16)
            dqg_ref[...] += dqg

        @pl.when(i == n_tiles - 1)
        def _():
            for h in range(XATTN_HEADS):
                sl = slice(h * XATTN_HD, (h + 1) * XATTN_HD)
                kraw = kv_ref[:, sl]
                _, rk = _rms(kraw, kg_v, XATTN_HD)
                dk, dkg = _rms_bwd(kraw, kg_v, rk, dkn_acc[:, sl], XATTN_HD)
                dkv_ref[:, sl] = dk
                dkg_ref[...] += dkg

    return pl.pallas_call(
        body, grid=(n_tiles,),
        in_specs=[_col(tm, BRANCH_W, offs["xq"]), _col(tm, BRANCH_W, offs["xz"]), _whole((M, 2 * BRANCH_W)),
                  _whole((1, XATTN_HD)), _whole((1, XATTN_HD)), _col(tm, BRANCH_W, 0)],
        out_specs=[_col(tm, BRANCH_W, 0), _col(tm, BRANCH_W, 0), _whole((M, 2 * BRANCH_W)), _whole((1, XATTN_HD)), _whole((1, XATTN_HD))],
        out_shape=[jax.ShapeDtypeStruct((S, BRANCH_W), BF16), jax.ShapeDtypeStruct((S, BRANCH_W), BF16),
                   jax.ShapeDtypeStruct((M, 2 * BRANCH_W), F32), jax.ShapeDtypeStruct((1, XATTN_HD), F32), jax.ShapeDtypeStruct((1, XATTN_HD), F32)],
        scratch_shapes=[pltpu.VMEM((M, BRANCH_W), F32)], name=name, compiler_params=_params(("arbitrary",)),
    )(proj, proj, mem_kv, qg, kg, dy)


def _rope(c, tc, ta, tb):
    return c * tc + pltpu.roll(c, 96, 1) * ta + pltpu.roll(c, 32, 1) * tb


def _rope_t(d, tc, ta, tb):
    return d * tc + pltpu.roll(d * ta, 32, 1) + pltpu.roll(d * tb, 96, 1)


def _mla_qk_fwd(qp, kvp, proj, offs, tabs, qg, kg, name):
    S = qp.shape[0]
    tm = _pick(S, 256, 16)
    W = MLA_HEADS * QK_PAD

    def body(q_ref, kv_ref, kr_ref, tc_ref, ta_ref, tb_ref, qg_ref, kg_ref, qo_ref, ko_ref, vo_ref):
        tc, ta, tb = tc_ref[...], ta_ref[...], tb_ref[...]
        qg_v, kg_v = qg_ref[...], kg_ref[...]
        kr = kr_ref[...]
        kr_ss = jnp.sum(kr * kr, axis=-1, keepdims=True)
        for h in range(MLA_HEADS):
            sl = slice(h * QK_PAD, (h + 1) * QK_PAD)
            qn, _ = _rms(q_ref[:, sl], qg_v, QK_HEAD)
            qo_ref[:, sl] = jnp.concatenate([qn[:, :QK_NOPE], _rope(qn[:, QK_NOPE:], tc, ta, tb)], axis=1).astype(BF16)
            kn = kv_ref[:, h * QK_NOPE:(h + 1) * QK_NOPE]
            r = lax.rsqrt((jnp.sum(kn * kn, axis=-1, keepdims=True) + kr_ss) / QK_HEAD + EPS)
            ko_ref[:, sl] = jnp.concatenate([kn * r * kg_v[:, :QK_NOPE], _rope(kr * r * kg_v[:, QK_NOPE:], tc, ta, tb)], axis=1).astype(BF16)
        vo_ref[...] = kv_ref[:, MLA_HEADS * QK_NOPE:].astype(BF16)

    return pl.pallas_call(
        body, grid=(S // tm,),
        in_specs=[_col(tm, W, 0), _col(tm, W, 0), _col(tm, 128, offs["kr"]), _col(tm, 128, 0), _col(tm, 128, 0), _col(tm, 128, 0),
                  _whole((1, QK_PAD)), _whole((1, QK_PAD))],
        out_specs=[_col(tm, W, 0), _col(tm, W, 0), _col(tm, BRANCH_W, 0)],
        out_shape=[jax.ShapeDtypeStruct((S, W), BF16), jax.ShapeDtypeStruct((S, W), BF16), jax.ShapeDtypeStruct((S, BRANCH_W), BF16)],
        name=name, compiler_params=_params(("parallel",)),
    )(qp, kvp, proj, *tabs, qg, kg)


def _mla_qk_bwd(qp, kvp, proj, offs, tabs, qg, kg, dQ, dK, dV, name):
    S = qp.shape[0]
    tm = _pick(S, 256, 16)
    W = MLA_HEADS * QK_PAD

    def body(q_ref, kv_ref, kr_ref, tc_ref, ta_ref, tb_ref, qg_ref, kg_ref, dQ_ref, dK_ref, dV_ref,
             dq_ref, dkv_ref, dkr_ref, dqg_ref, dkg_ref):
        @pl.when(pl.program_id(0) == 0)
        def _():
            dqg_ref[...] = jnp.zeros_like(dqg_ref)
            dkg_ref[...] = jnp.zeros_like(dkg_ref)

        tc, ta, tb = tc_ref[...], ta_ref[...], tb_ref[...]
        qg_v, kg_v = qg_ref[...], kg_ref[...]
        kg_n, kg_r = kg_v[:, :QK_NOPE], kg_v[:, QK_NOPE:]
        kr = kr_ref[...]
        kr_ss = jnp.sum(kr * kr, axis=-1, keepdims=True)
        dkr = jnp.zeros_like(kr)
        dkg_n = jnp.zeros((1, QK_NOPE), F32)
        dkg_r = jnp.zeros((1, QK_NOPE), F32)
        for h in range(MLA_HEADS):
            sl = slice(h * QK_PAD, (h + 1) * QK_PAD)
            qv = q_ref[:, sl]
            _, rq = _rms(qv, qg_v, QK_HEAD)
            dQv = dQ_ref[:, sl]
            dqn = jnp.concatenate([dQv[:, :QK_NOPE], _rope_t(dQv[:, QK_NOPE:], tc, ta, tb)], axis=1)
            dq, dqg = _rms_bwd(qv, qg_v, rq, dqn, QK_HEAD)
            dq_ref[:, sl] = dq.astype(BF16)
            dqg_ref[...] += dqg
            ns = slice(h * QK_NOPE, (h + 1) * QK_NOPE)
            kn = kv_ref[:, ns]
            r = lax.rsqrt((jnp.sum(kn * kn, axis=-1, keepdims=True) + kr_ss) / QK_HEAD + EPS)
            dKv = dK_ref[:, sl]
            dyn = dKv[:, :QK_NOPE]
            dyr = _rope_t(dKv[:, QK_NOPE:], tc, ta, tb)
            proj_s = (jnp.sum(dyn * kg_n * kn, axis=-1, keepdims=True) + jnp.sum(dyr * kg_r * kr, axis=-1, keepdims=True)) / QK_HEAD
            r3 = r * r * r
            dkv_ref[:, ns] = (r * dyn * kg_n - kn * r3 * proj_s).astype(BF16)
            dkr = dkr + (r * dyr * kg_r - kr * r3 * proj_s)
            dkg_n = dkg_n + jnp.sum(dyn * kn * r, axis=0, keepdims=True)
            dkg_r = dkg_r + jnp.sum(dyr * kr * r, axis=0, keepdims=True)
        dkg_ref[...] += jnp.concatenate([dkg_n, dkg_r], axis=1)
        dkr_ref[...] = dkr.astype(BF16)
        dkv_ref[:, MLA_HEADS * QK_NOPE:] = dV_ref[...].astype(BF16)

    return pl.pallas_call(
        body, grid=(S // tm,),
        in_specs=[_col(tm, W, 0), _col(tm, W, 0), _col(tm, 128, offs["kr"]), _col(tm, 128, 0), _col(tm, 128, 0), _col(tm, 128, 0),
                  _whole((1, QK_PAD)), _whole((1, QK_PAD)), _col(tm, W, 0), _col(tm, W, 0), _col(tm, BRANCH_W, 0)],
        out_specs=[_col(tm, W, 0), _col(tm, W, 0), _col(tm, 128, 0), _whole((1, QK_PAD)), _whole((1, QK_PAD))],
        out_shape=[jax.ShapeDtypeStruct((S, W), BF16), jax.ShapeDtypeStruct((S, W), BF16), jax.ShapeDtypeStruct((S, 128), BF16),
                   jax.ShapeDtypeStruct((1, QK_PAD), F32), jax.ShapeDtypeStruct((1, QK_PAD), F32)],
        name=name, compiler_params=_params(("arbitrary",)),
    )(qp, kvp, proj, *tabs, qg, kg, dQ, dK, dV)


def _flash_fwd(Q, K, V, proj, offs, name, comm=None):
    S = Q.shape[0]
    t = _pick(S, 1024, 128)
    assert t % 256 == 0
    mz_cb = offs["mz"] // V_HEAD

    half = t // 2

    def fold(x, op):
        r = x[:, 0:128]
        for ch in range(1, x.shape[1] // 128):
            r = op(r, x[:, ch * 128:(ch + 1) * 128])
        return r

    def body(q_ref, k_ref, v_ref, mz_ref, o_ref, y_ref, lse_ref, mp_scr, lp_scr, acc_scr):
        qi = pl.program_id(1)
        q = q_ref[...]

        def raw_scores(j):
            rows_k = pl.ds(pl.multiple_of(j * t, t), t)
            return _dot_nt(q, k_ref[rows_k, :]), rows_k

        def diagonal_scores(part):
            q_lo = part * half
            rows_k = pl.ds(pl.multiple_of(qi * t + part * half, half), half)
            s = _dot_nt(q[q_lo:], k_ref[rows_k, :])
            rows = lax.broadcasted_iota(jnp.int32, s.shape, 0)
            cols = lax.broadcasted_iota(jnp.int32, s.shape, 1)
            return jnp.where(cols <= rows, s, NEG), rows_k, q_lo

        mp_scr[...] = jnp.full_like(mp_scr, NEG)

        def max_sweep(j, carry):
            s, _ = raw_scores(j)
            mp_scr[...] = jnp.maximum(mp_scr[...], fold(s, jnp.maximum))
            return carry

        lax.fori_loop(0, qi, max_sweep, 0)
        for part in range(2):
            s, _, q_lo = diagonal_scores(part)
            mp_scr[q_lo:, :] = jnp.maximum(mp_scr[q_lo:, :], fold(s, jnp.maximum))
        m = jnp.max(mp_scr[...], axis=-1, keepdims=True) * SCORE_SCALE_LOG2

        lp_scr[...] = jnp.zeros_like(lp_scr)
        acc_scr[...] = jnp.zeros_like(acc_scr)

        def sum_sweep(j, carry):
            s, rows_k = raw_scores(j)
            p = jnp.exp2(s * SCORE_SCALE_LOG2 - m)
            lp_scr[...] += fold(p, jnp.add)
            acc_scr[...] += _dot(p, v_ref[rows_k, :])
            return carry

        lax.fori_loop(0, qi, sum_sweep, 0)
        for part in range(2):
            s, rows_k, q_lo = diagonal_scores(part)
            p = jnp.exp2(s * SCORE_SCALE_LOG2 - m[q_lo:])
            lp_scr[q_lo:, :] += fold(p, jnp.add)
            acc_scr[q_lo:, :] += _dot(p, v_ref[rows_k, :])
        l = jnp.sum(lp_scr[...], axis=-1, keepdims=True)
        o = acc_scr[...] / l
        o_ref[...] = o
        y_ref[...] = (o * _silu(mz_ref[...])).astype(BF16)
        lse_ref[0] = m + jnp.log2(l)

    in_specs = [pl.BlockSpec((t, QK_PAD), lambda h, i: (i, h)), pl.BlockSpec((S, QK_PAD), lambda h, i: (0, h)),
                pl.BlockSpec((S, V_HEAD), lambda h, i: (0, h)), pl.BlockSpec((t, V_HEAD), lambda h, i: (i, mz_cb + h))]
    out_specs = [pl.BlockSpec((t, V_HEAD), lambda h, i: (i, h)), pl.BlockSpec((t, V_HEAD), lambda h, i: (i, h)),
                 pl.BlockSpec((1, t, 1), lambda h, i: (h, i, 0))]
    out_shape = [jax.ShapeDtypeStruct((S, BRANCH_W), F32), jax.ShapeDtypeStruct((S, BRANCH_W), BF16),
                 jax.ShapeDtypeStruct((MLA_HEADS, S, 1), F32)]
    scratch = [pltpu.VMEM((t, 128), F32), pltpu.VMEM((t, 128), F32), pltpu.VMEM((t, V_HEAD), F32)]
    ins = [Q, K, V, proj]
    if comm is not None:
        return _hosted_call(comm, body, ins, in_specs, out_specs, out_shape, scratch, (MLA_HEADS, S // t), name)
    res = pl.pallas_call(
        body, grid=(MLA_HEADS, S // t), in_specs=in_specs, out_specs=out_specs, out_shape=out_shape, scratch_shapes=scratch,
        name=name, compiler_params=_params(("parallel", "parallel")),
    )(*ins)
    return list(res), []


def _mla_gate_bwd(o, proj, offs, dy, name):
    S = o.shape[0]
    tm = _pick(S, 512, 16)

    def body(o_ref, mz_ref, dy_ref, do_ref, dmz_ref, dl_ref):
        sz, dsz = _silu_and_grad(mz_ref[...])
        dyv, ov = dy_ref[...], o_ref[...]
        do = (dyv * sz).astype(BF16)
        do_ref[...] = do
        dmz_ref[...] = (dyv * ov * dsz).astype(BF16)
        prod = do.astype(F32) * ov
        for h in range(MLA_HEADS):
            dl_ref[h] = jnp.sum(prod[:, h * V_HEAD:(h + 1) * V_HEAD], axis=-1, keepdims=True)

    return pl.pallas_call(
        body, grid=(S // tm,), in_specs=[_col(tm, BRANCH_W, 0), _col(tm, BRANCH_W, offs["mz"]), _col(tm, BRANCH_W, 0)],
        out_specs=[_col(tm, BRANCH_W, 0), _col(tm, BRANCH_W, 0), pl.BlockSpec((MLA_HEADS, tm, 1), lambda i: (0, i, 0))],
        out_shape=[jax.ShapeDtypeStruct((S, BRANCH_W), BF16), jax.ShapeDtypeStruct((S, BRANCH_W), BF16),
                   jax.ShapeDtypeStruct((MLA_HEADS, S, 1), F32)], name=name,
        compiler_params=_params(("parallel",)),
    )(o, proj, dy)


def _flash_bwd(Q, K, V, lse, delta, do, name, comm=None):
    S = Q.shape[0]
    tk = _pick(S, 1024, 128)
    assert tk % 256 == 0
    nq, half = S // tk, tk // 2
    once = pl.Buffered(1)

    def body(q_ref, k_ref, v_ref, lse_ref, dl_ref, do_ref, dq_ref, dk_ref, dv_ref, dk_acc, dv_acc):
        ki = pl.program_id(1)

        @pl.when(ki == 0)
        def _():
            dq_ref[...] = jnp.zeros_like(dq_ref)

        dk_acc[...] = jnp.zeros_like(dk_acc)
        dv_acc[...] = jnp.zeros_like(dv_acc)

        def tile(i, q_off, q_n, k_n, masked):
            rows = pl.ds(pl.multiple_of(i * tk + q_off, half), q_n)
            q, dov = q_ref[rows, :], do_ref[rows, :]
            k, v = k_ref[0:k_n, :], v_ref[0:k_n, :]
            s = _dot_nt(q, k) * SCORE_SCALE_LOG2
            if masked:
                r = q_off + lax.broadcasted_iota(jnp.int32, s.shape, 0)
                c = lax.broadcasted_iota(jnp.int32, s.shape, 1)
                s = jnp.where(c <= r, s, NEG)
            p = jnp.exp2(s - lse_ref[0, rows, :])
            dv_acc[0:k_n, :] += _dot_tn(p, dov)
            dp = _dot_nt(dov, v)
            ds = p * (dp - dl_ref[0, rows, :]) * (QK_HEAD ** -0.5)
            dk_acc[0:k_n, :] += _dot_tn(ds, q)
            dq_ref[rows, :] += _dot(ds, k)

        def below_diagonal(i, carry):
            tile(i, 0, tk, tk, False)
            return carry

        tile(ki, 0, half, half, True)
        tile(ki, half, half, tk, True)
        lax.fori_loop(ki + 1, nq, below_diagonal, 0)
        dk_ref[...] = dk_acc[...]
        dv_ref[...] = dv_acc[...]

    in_specs = [pl.BlockSpec((S, QK_PAD), lambda h, j: (0, h), pipeline_mode=once), pl.BlockSpec((tk, QK_PAD), lambda h, j: (j, h)),
                pl.BlockSpec((tk, V_HEAD), lambda h, j: (j, h)), pl.BlockSpec((1, S, 1), lambda h, j: (h, 0, 0), pipeline_mode=once),
                pl.BlockSpec((1, S, 1), lambda h, j: (h, 0, 0), pipeline_mode=once),
                pl.BlockSpec((S, V_HEAD), lambda h, j: (0, h), pipeline_mode=once)]
    out_specs = [pl.BlockSpec((S, QK_PAD), lambda h, j: (0, h)), pl.BlockSpec((tk, QK_PAD), lambda h, j: (j, h)),
                 pl.BlockSpec((tk, V_HEAD), lambda h, j: (j, h))]
    out_shape = [jax.ShapeDtypeStruct((S, MLA_HEADS * QK_PAD), F32), jax.ShapeDtypeStruct((S, MLA_HEADS * QK_PAD), F32),
                 jax.ShapeDtypeStruct((S, BRANCH_W), F32)]
    scratch = [pltpu.VMEM((tk, QK_PAD), F32), pltpu.VMEM((tk, V_HEAD), F32)]
    ins = [Q, K, V, lse, delta, do]
    if comm is not None:
        return _hosted_call(comm, body, ins, in_specs, out_specs, out_shape, scratch, (MLA_HEADS, nq), name)
    res = pl.pallas_call(
        body, grid=(MLA_HEADS, nq), in_specs=in_specs, out_specs=out_specs, out_shape=out_shape, scratch_shapes=scratch, name=name,
        compiler_params=_params(("parallel", "arbitrary")),
    )(*ins)
    return list(res), []


def _merge_specs(S, D, offs, tm, tn, layer):
    g_cb = offs["g"] // tn
    nj = D // tn
    y_specs = [pl.BlockSpec((tm, BRANCH_W), lambda j, i: (i, 0)) for _ in range(4)]
    w_spec = pl.BlockSpec((None, 4, BRANCH_W, tn), lambda j, i: (layer, 0, 0, j))
    g_specs = [pl.BlockSpec((tm, tn), functools.partial(lambda j, i, b: (i, g_cb + b * nj + j), b=b)) for b in range(4)]
    bias_spec = pl.BlockSpec((4, tn), lambda j, i: (0, j))
    return y_specs, w_spec, g_specs, bias_spec


def _merge_fwd(ys, wb, layer, proj, offs, gate_b, name):
    S, D = ys[0].shape[0], wb.shape[3]
    tm, tn = _pick(S, 512, 16), _pick(D, 512, 128)
    y_specs, w_spec, g_specs, bias_spec = _merge_specs(S, D, offs, tm, tn, layer)

    def body(y0, y1, y2, y3, w_ref, g0, g1, g2, g3, b_ref, o_ref, t_ref):
        acc = None
        for b, (y_ref, g_ref) in enumerate(zip((y0, y1, y2, y3), (g0, g1, g2, g3))):
            gate = 1.0 / (1.0 + jnp.exp(-(g_ref[...] + b_ref[b:b + 1, :])))
            t = _dot(y_ref[...], w_ref[b])
            t_ref[b] = t.astype(BF16)
            term = gate * t
            acc = term if acc is None else acc + term
        o_ref[...] = acc.astype(BF16)

    return pl.pallas_call(
        body, grid=(D // tn, S // tm), in_specs=y_specs + [w_spec] + g_specs + [bias_spec],
        out_specs=[pl.BlockSpec((tm, tn), lambda j, i: (i, j)), pl.BlockSpec((4, tm, tn), lambda j, i: (0, i, j))],
        out_shape=[jax.ShapeDtypeStruct((S, D), BF16), jax.ShapeDtypeStruct((4, S, D), BF16)], name=name,
        compiler_params=_params(("parallel", "parallel")),
    )(*ys, wb, proj, proj, proj, proj, gate_b)


def _merge_bwd(t, proj, offs, gate_b, dmerged, name):
    _, S, D = t.shape
    tm, tn = _pick(S, 512, 16), _pick(D, 512, 128)
    nj = D // tn
    _, _, g_specs, bias_spec = _merge_specs(S, D, offs, tm, tn, 0)

    def body(t_ref, g0, g1, g2, g3, b_ref, dm_ref, dt_ref, dg0, dg1, dg2, dg3, db_ref):
        @pl.when(pl.program_id(1) == 0)
        def _():
            db_ref[...] = jnp.zeros_like(db_ref)

        dm = dm_ref[...]
        for b, (g_ref, dg_ref) in enumerate(zip((g0, g1, g2, g3), (dg0, dg1, dg2, dg3))):
            gate = 1.0 / (1.0 + jnp.exp(-(g_ref[...] + b_ref[b:b + 1, :])))
            dt_ref[b] = (dm * gate).astype(BF16)
            dgp = dm * t_ref[b].astype(F32) * gate * (1.0 - gate)
            dg_ref[...] = dgp.astype(BF16)
            db_ref[b:b + 1, :] += jnp.sum(dgp, axis=0, keepdims=True)

    stacked = pl.BlockSpec((4, tm, tn), lambda j, i: (0, i, j))
    tile = pl.BlockSpec((tm, tn), lambda j, i: (i, j))
    return pl.pallas_call(
        body, grid=(nj, S // tm), in_specs=[stacked] + g_specs + [bias_spec, tile],
        out_specs=[stacked] + [tile] * 4 + [pl.BlockSpec((4, tn), lambda j, i: (0, j))],
        out_shape=[jax.ShapeDtypeStruct((4, S, D), BF16)] + [jax.ShapeDtypeStruct((S, D), BF16)] * 4 + [jax.ShapeDtypeStruct((4, D), F32)],
        name=name, compiler_params=_params(("parallel", "arbitrary")),
    )(t, proj, proj, proj, proj, gate_b, dmerged)


HBM_SPEC = pl.BlockSpec(memory_space=pl.ANY)


def _mesh_place():
    x, y, c = lax.axis_index("x"), lax.axis_index("y"), lax.axis_index("c")
    return x, y, c, [(1 - x, y), (x, 1 - y), (1 - x, 1 - y)]


def _remote(src, dst, send_sem, recv_sem, to):
    return pltpu.make_async_remote_copy(src_ref=src, dst_ref=dst, send_sem=send_sem, recv_sem=recv_sem, device_id=to,
                                        device_id_type=pl.DeviceIdType.MESH)


class _GatherPlan:
    def __init__(self, arrs):
        self.arrs, self.n = list(arrs), len(arrs)
        self.out_shape = [jax.ShapeDtypeStruct((N_DEV,) + a.shape, a.dtype) for a in arrs]
        self.scratch = [pltpu.SemaphoreType.DMA((self.n, 7)), pltpu.SemaphoreType.DMA((self.n, 7)), pltpu.SemaphoreType.DMA((self.n,))]

    def _copies(self, ins, dsts, sems):
        send_sems, recv_sems, local_sems = sems
        x, y, c, _ = _mesh_place()
        slot = lambda chip, pc: 4 * chip[0] + 2 * chip[1] + pc
        here, sibling = (x, y), (x, y, 1 - c)
        first = (x + (1 - 2 * x) * (1 - c), y + (1 - 2 * y) * c)
        second = (x + (1 - 2 * x) * c, y + (1 - 2 * y) * (1 - c))
        far = (1 - x, 1 - y)

        def copy(a, k, src, block, to):
            return lambda: _remote(src(), dsts[a].at[block], send_sems.at[a, k], recv_sems.at[a, k], to)

        rng = range(self.n)
        shard = lambda a: (lambda: ins[a])
        held = lambda a, chip: (lambda: dsts[a].at[slot(chip, c)])
        local = [(lambda a=a: pltpu.make_async_copy(ins[a], dsts[a].at[slot(here, c)], local_sems.at[a])) for a in rng]
        own = [copy(a, k, shard(a), slot(here, c), to) for a in rng for k, to in ((0, sibling), (1, (*first, c)), (2, (*second, c)))]
        steps = []
        for a in rng:
            steps.append((copy(a, 1, shard(a), slot(first, c), sibling),
                          [copy(a, 4, held(a, first), slot(first, c), sibling), copy(a, 3, held(a, first), slot(first, c), (*second, c))]))
            steps.append((copy(a, 2, shard(a), slot(second, c), sibling), [copy(a, 5, held(a, second), slot(second, c), sibling)]))
            steps.append((copy(a, 3, shard(a), slot(far, c), sibling), [copy(a, 6, held(a, far), slot(far, c), sibling)]))
        from_sibling = [copy(a, k, shard(a), slot(chip, 1 - c), sibling) for a in rng
                        for k, chip in ((0, here), (4, second), (5, first), (6, far))]
        return local, own, steps, from_sibling

    def start(self, ins, dsts, sems):
        local, own, _, _ = self._copies(ins, dsts, sems)
        for make in local + own:
            make().start()

    def relay(self, stage, ins, dsts, sems):
        _, _, steps, _ = self._copies(ins, dsts, sems)
        for index, (arrived, onward) in enumerate(steps):
            if (index % 3 == 2) == (stage == 1):
                arrived().wait_recv()
                for make in onward:
                    make().start()

    def finish(self, ins, dsts, sems):
        local, own, steps, from_sibling = self._copies(ins, dsts, sems)
        for make in from_sibling:
            make().wait_recv()
        for make in own + [m for _, onward in steps for m in onward]:
            make().wait_send()
        for make in local:
            make().wait()


class _ChipExchangePlan:
    def __init__(self, arrs):
        self.arrs, self.n = list(arrs), len(arrs)
        self.out_shape = [jax.ShapeDtypeStruct(a.shape, a.dtype) for a in arrs]
        self.scratch = [pltpu.SemaphoreType.DMA((self.n, 3)), pltpu.SemaphoreType.DMA((self.n, 3)), pltpu.SemaphoreType.DMA((self.n,))]

    def _copies(self, ins, dsts, sems):
        send_sems, recv_sems, local_sems = sems
        x, y, c, chips = _mesh_place()
        mine = 2 * x + y
        local = [pltpu.make_async_copy(ins[a].at[mine], dsts[a].at[mine], local_sems.at[a]) for a in range(self.n)]
        remote = [_remote(ins[a].at[2 * px + py], dsts[a].at[mine], send_sems.at[a, j], recv_sems.at[a, j], (px, py, c))
                  for a in range(self.n) for j, (px, py) in enumerate(chips)]
        return local, remote

    def start(self, ins, dsts, sems):
        local, remote = self._copies(ins, dsts, sems)
        for cp in local + remote:
            cp.start()

    def relay(self, stage, ins, dsts, sems):
        pass

    def finish(self, ins, dsts, sems):
        local, remote = self._copies(ins, dsts, sems)
        for cp in remote + local:
            cp.wait()


def _run_plan(plan, name):
    n = plan.n

    def body(*refs):
        ins, dsts, sems = refs[:n], refs[n:2 * n], refs[2 * n:]
        plan.start(ins, dsts, sems)
        plan.relay(0, ins, dsts, sems)
        plan.relay(1, ins, dsts, sems)
        plan.finish(ins, dsts, sems)

    res = pl.pallas_call(body, in_specs=[HBM_SPEC] * n, out_specs=[HBM_SPEC] * n, out_shape=plan.out_shape,
                         scratch_shapes=plan.scratch, name=name)(*plan.arrs)
    return list(res)


def _sibling_swap(arrs, name):
    n = len(arrs)

    def body(*refs):
        ins, dsts = refs[:n], refs[n:2 * n]
        send_sems, recv_sems = refs[2 * n:]
        x, y, c, _ = _mesh_place()
        cps = [_remote(ins[a].at[q, 1 - c], dsts[a].at[q], send_sems.at[a, q], recv_sems.at[a, q], (x, y, 1 - c))
               for a in range(n) for q in range(4)]
        for cp in cps:
            cp.start()
        for cp in cps:
            cp.wait()

    res = pl.pallas_call(
        body, in_specs=[HBM_SPEC] * n, out_specs=[HBM_SPEC] * n,
        out_shape=[jax.ShapeDtypeStruct((4,) + a.shape[2:], a.dtype) for a in arrs],
        scratch_shapes=[pltpu.SemaphoreType.DMA((n, 4)), pltpu.SemaphoreType.DMA((n, 4))], name=name,
    )(*arrs)
    return list(res)


def _pair_sum(staged, theirs, core, name):
    _, _, R, C = staged.shape
    tr = _pick(R, max(16, ((1 << 19) // C) // 16 * 16), 16)

    def body(c_ref, a_ref, b_ref, o_ref):
        o_ref[...] = (a_ref[...].astype(F32) + b_ref[...].astype(F32)).astype(BF16)

    blk = pl.BlockSpec((None, tr, C), lambda q, i, c: (q, i, 0))
    grid_spec = pltpu.PrefetchScalarGridSpec(
        num_scalar_prefetch=1, grid=(4, R // tr),
        in_specs=[pl.BlockSpec((None, None, tr, C), lambda q, i, c: (q, c[0], i, 0)), blk], out_specs=blk)
    return pl.pallas_call(
        body, grid_spec=grid_spec, out_shape=jax.ShapeDtypeStruct((4, R, C), BF16), name=name,
        compiler_params=_params(("parallel", "parallel")),
    )(core, staged, theirs)


def _adamw(parts, w, m, v, name):
    R, C = w.shape
    n_parts = parts[0].shape[0]
    rows = [p.shape[1] for p in parts]
    assert sum(rows) == R
    tr = _pick(min(rows), max(16, ((1 << 18) // C) // 16 * 16), 16)
    assert all(r % tr == 0 for r in rows)
    bounds, lo = [], 0
    for r in rows:
        bounds.append((lo, lo + r // tr))
        lo += r // tr
    n_lists = len(parts)

    def body(*refs):
        p_refs = refs[:n_lists]
        w_ref, m_ref, v_ref, g_ref, d_ref, nm_ref, nv_ref = refs[n_lists:]
        i = pl.program_id(0)

        def update(p_ref):
            g = p_ref[0].astype(F32)
            for s in range(1, n_parts):
                g = g + p_ref[s].astype(F32)
            mn = ADAM_B1 * m_ref[...] + (1.0 - ADAM_B1) * g
            vn = ADAM_B2 * v_ref[...] + (1.0 - ADAM_B2) * (g * g)
            m_hat = mn / (1.0 - ADAM_B1 ** ADAM_STEP)
            v_hat = vn / (1.0 - ADAM_B2 ** ADAM_STEP)
            g_ref[...] = g
            d_ref[...] = -ADAM_LR * (m_hat / (jnp.sqrt(v_hat) + ADAM_EPS) + ADAM_WD * w_ref[...])
            nm_ref[...] = mn
            nv_ref[...] = vn

        if n_lists == 1:
            update(p_refs[0])
        else:
            for p_ref, (first, end) in zip(p_refs, bounds):
                pl.when((i >= first) & (i < end))(functools.partial(update, p_ref))

    blk = pl.BlockSpec((tr, C), lambda i: (i, 0))
    p_specs = [pl.BlockSpec((n_parts, tr, C), functools.partial(lambda i, first, last: (0, jnp.clip(i - first, 0, last), 0),
                                                                first=first, last=end - first - 1)) for first, end in bounds]
    return pl.pallas_call(
        body, grid=(R // tr,), in_specs=p_specs + [blk, blk, blk],
        out_specs=[blk] * 4, out_shape=[jax.ShapeDtypeStruct((R, C), F32)] * 4, name=name,
        compiler_params=_params(("parallel",)),
    )(*parts, w, m, v)


def _offsets(L, D):
    o = {"pv": 0, "pz": BRANCH_W, "cq": 2 * BRANCH_W, "ckv": 2 * BRANCH_W + L, "mz": 2 * BRANCH_W + 2 * L}
    for prev, nxt in (("mz", "cb"), ("cb", "cc"), ("cc", "cx"), ("cx", "cz"), ("cz", "xq"), ("xq", "xz"), ("xz", "g")):
        o[nxt] = o[prev] + BRANCH_W
    o["kr"] = o["g"] + 4 * D
    o["n"] = o["kr"] + 128
    return o


def _unshard(gathered, axis):
    moved = jnp.moveaxis(gathered, 0, axis)
    shape = list(moved.shape)
    shape[axis:axis + 2] = [shape[axis] * shape[axis + 1]]
    return moved.reshape(shape)


def _reshard(full, axis):
    shape = list(full.shape)
    shape[axis:axis + 1] = [N_DEV, shape[axis] // N_DEV]
    return jnp.moveaxis(full.reshape(shape), axis, 0)


def _pad_heads(g, width):
    return jnp.pad(g, ((0, 0), (0, width - g.shape[1])))


def kernel(x, mem, positions, norm_g, w_in, gate_b, pool_w, pool_scale, q_a_norm_g, kv_a_norm_g, w_uq, w_ukv, mla_q_norm_g, mla_k_norm_g, conv_w, mem_norm_g, w_mem_kv, xattn_q_norm_g, xattn_k_norm_g, w_branch, w_out, loss_target, m_norm_g, m_w_in, m_gate_b, m_pool_w, m_pool_scale, m_q_a_norm_g, m_kv_a_norm_g, m_w_uq, m_w_ukv, m_mla_q_norm_g, m_mla_k_norm_g, m_conv_w, m_mem_norm_g, m_w_mem_kv, m_xattn_q_norm_g, m_xattn_k_norm_g, m_w_branch, m_w_out, v_norm_g, v_w_in, v_gate_b, v_pool_w, v_pool_scale, v_q_a_norm_g, v_kv_a_norm_g, v_w_uq, v_w_ukv, v_mla_q_norm_g, v_mla_k_norm_g, v_conv_w, v_mem_norm_g, v_w_mem_kv, v_xattn_q_norm_g, v_xattn_k_norm_g, v_w_branch, v_w_out):
    depth = norm_g.shape[0]
    S, D = x.shape[1], x.shape[2]
    L = w_uq.shape[1]
    offs = _offsets(L, D)
    seg_a = 2 * BRANCH_W + 2 * L
    xs = x.reshape(S, D)
    mems = mem.reshape(mem.shape[1], D)

    sharded = dict(w_in=w_in, pool_w=pool_w, w_uq=w_uq, w_ukv=w_ukv, conv_w=conv_w, w_mem_kv=w_mem_kv, w_branch=w_branch, w_out=w_out)
    shard_axis = dict(w_in=2, pool_w=2, w_uq=2, w_ukv=2, conv_w=2, w_mem_kv=1, w_branch=3, w_out=1)
    def permuted_w_in(g):
        f = _unshard(g, 1)
        return jnp.concatenate([f[:, :seg_a], f[:, seg_a + QK_ROPE:], f[:, seg_a:seg_a + QK_ROPE], jnp.zeros((D, 128 - QK_ROPE), BF16)], axis=-1)

    w_in_local = [w_in[l].astype(BF16) for l in range(depth)]
    w_in_p = [permuted_w_in(_run_plan(_GatherPlan([w_in_local[0]]), "gather_w_in0")[0])] + [None] * (depth - 1)
    rest_keys = [k for k in shard_axis if k != "w_in"]
    rest_plan = _GatherPlan([sharded[k] if k == "conv_w" else sharded[k].astype(BF16) for k in rest_keys])
    later_w_in_plan = _GatherPlan(w_in_local[1:]) if depth > 1 else None

    inv = ROPE_THETA ** (-jnp.arange(0, QK_ROPE, 2, dtype=F32) / QK_ROPE)
    ang = positions.reshape(S).astype(F32)[:, None] * inv
    cos, sin, z = jnp.cos(ang), jnp.sin(ang), jnp.zeros_like(ang)
    tabs = (jnp.concatenate([cos, cos, z, z], axis=1), jnp.concatenate([-sin, z, z, z], axis=1), jnp.concatenate([z, sin, z, z], axis=1))

    row = lambda a, l: a[l][None, :]
    saved = []
    cur = xs
    for l in range(depth):
        h, hT = _rms_rows(cur, D, 0, row(norm_g, l), f"norm_fwd{l}", with_transpose=True)
        proj_tiles = dict(tm=512, tn=3712, tk=2048, n_outer=True)
        if l == 0:
            proj, carried = _matmul(h, w_in_p[0], "nn", F32, "proj_fwd0", comm=rest_plan, **proj_tiles)
            gathered = {k: _unshard(g, shard_axis[k]) for k, g in zip(rest_keys, carried)}
            conv_full = gathered["conv_w"]
            pool_w_f = gathered["pool_w"]
            w_uq_f = gathered["w_uq"].reshape(depth, L, MLA_HEADS, QK_HEAD)
            w_uq_p = jnp.pad(w_uq_f, ((0, 0), (0, 0), (0, 0), (0, QK_PAD - QK_HEAD))).reshape(depth, L, MLA_HEADS * QK_PAD)
            w_ukv_f = gathered["w_ukv"].reshape(depth, L, MLA_HEADS, QK_NOPE + V_HEAD)
            w_ukv_p = jnp.concatenate([w_ukv_f[..., :QK_NOPE].reshape(depth, L, -1), w_ukv_f[..., QK_NOPE:].reshape(depth, L, -1)], axis=-1)
            w_mem_f = gathered["w_mem_kv"]
            w_br_f = gathered["w_branch"]
            w_out_f = gathered["w_out"]
        else:
            proj = _matmul(h, w_in_p[l], "nn", F32, f"proj_fwd{l}", **proj_tiles)
        y_pool = _pool_fwd(proj, offs, pool_w_f[l], row(pool_scale, l), f"pool_fwd{l}")
        cqn = _rms_rows(proj, L, offs["cq"], row(q_a_norm_g, l), f"cq_norm{l}")
        ckvn = _rms_rows(proj, L, offs["ckv"], row(kv_a_norm_g, l), f"ckv_norm{l}")
        qp = _matmul(cqn, w_uq_p[l], "nn", F32, f"uq_fwd{l}")
        kvp = _matmul(ckvn, w_ukv_p[l], "nn", F32, f"ukv_fwd{l}")
        qg = _pad_heads(row(mla_q_norm_g, l), QK_PAD)
        kg = _pad_heads(row(mla_k_norm_g, l), QK_PAD)
        Q, K, V = _mla_qk_fwd(qp, kvp, proj, offs, tabs, qg, kg, f"mla_qk_fwd{l}")
        (o, y_mla, lse), carried = _flash_fwd(Q, K, V, proj, offs, f"flash_fwd{l}", comm=later_w_in_plan if l == 0 else None)
        if l == 0 and later_w_in_plan is not None:
            w_in_p[1:] = [permuted_w_in(g) for g in carried]
        y_conv = _conv_fwd(proj, offs, conv_full[l], f"conv_fwd{l}")
        memn = _rms_rows(mems, D, 0, row(mem_norm_g, l), f"mem_norm{l}")
        mem_kv = _matmul(memn, w_mem_f, "nn", F32, f"mem_kv{l}", b_idx=(l,))
        y_mem = _xattn_fwd(proj, offs, mem_kv, row(xattn_q_norm_g, l), row(xattn_k_norm_g, l), f"xattn_fwd{l}")
        ys = (y_pool, y_mla, y_conv, y_mem)
        gb = gate_b[l].reshape(4, D)
        merged, tproj = _merge_fwd(ys, w_br_f, l, proj, offs, gb, f"merge_fwd{l}")
        nxt = _matmul(merged, w_out_f, "nn", F32, f"out_fwd{l}", res=cur, b_idx=(l,))
        saved.append(dict(x=cur, hT=hT, proj=proj, cqn=cqn, ckvn=ckvn, qp=qp, kvp=kvp, qg=qg, kg=kg, Q=Q, K=K, V=V, o=o, lse=lse,
                          memn=memn, mem_kv=mem_kv, ys=ys, gb=gb, merged=merged, tproj=tproj))
        cur = nxt

    gx, loss_acc = _loss_rows(cur, loss_target.reshape(S, D), "loss")
    loss = lax.psum(0.5 * loss_acc[0, 0], MESH_AXES)

    grads = {k: [None] * depth for k in ("norm_g", "w_in", "gate_b", "pool_w", "pool_scale", "q_a_norm_g", "kv_a_norm_g", "w_uq", "w_ukv",
                                         "mla_q_norm_g", "mla_k_norm_g", "conv_w", "mem_norm_g", "w_mem_kv", "xattn_q_norm_g",
                                         "xattn_k_norm_g", "w_branch", "w_out")}
    names_sh = list(shard_axis)
    core = lax.axis_index("c").astype(jnp.int32).reshape(1)
    received = [None] * depth
    pending = None

    def pair_sums(one_layer_grads, names, tag):
        staged = [_reshard(g, shard_axis[k] - 1).reshape(4, 2, -1, sharded[k].shape[-1]).astype(BF16) for g, k in zip(one_layer_grads, names)]
        given = _sibling_swap(staged, f"pair_swap_{tag}")
        return [_pair_sum(st, theirs, core, f"pair_sum_{k}_{tag}") for k, st, theirs in zip(names, staged, given)]

    for l in reversed(range(depth)):
        sv = saved[l]
        proj, ys = sv["proj"], sv["ys"]
        dmerged = _matmul(gx, w_out_f, "nt", F32, f"out_bwd_x{l}", b_idx=(l,))
        grads["w_out"][l] = _matmul(sv["merged"], gx, "tn", BF16, f"out_bwd_w{l}")
        dt, dgp0, dgp1, dgp2, dgp3, dgb = _merge_bwd(sv["tproj"], proj, offs, sv["gb"], dmerged, f"merge_bwd{l}")
        grads["gate_b"][l] = dgb.reshape(4 * D)
        dys, dwb = [], []
        for b in range(4):
            dys.append(_matmul(dt, w_br_f, "nt", F32, f"branch_bwd_x{l}_{b}", a_idx=(b,), b_idx=(l, b)))
            dwb.append(_matmul(ys[b], dt, "tn", BF16, f"branch_bwd_w{l}_{b}", b_idx=(b,)))
        grads["w_branch"][l] = jnp.stack(dwb)
        d_pz, e, d_pw, d_ps = _pool_bwd_a(proj, offs, pool_w_f[l], row(pool_scale, l), dys[0], f"pool_bwd_a{l}")
        d_pv = _pool_bwd_b(e, f"pool_bwd_b{l}")
        grads["pool_w"][l], grads["pool_scale"][l] = d_pw, d_ps[0]
        do, d_mz, delta = _mla_gate_bwd(sv["o"], proj, offs, dys[1], f"mla_gate_bwd{l}")
        (dQ, dK, dV), carried = _flash_bwd(sv["Q"], sv["K"], sv["V"], sv["lse"], delta, do, f"flash_bwd{l}",
                                           comm=None if pending is None else pending[1])
        if pending is not None:
            received[pending[0]][0], pending = carried[0], None
        dqp, dkvp, d_kr, dqg, dkg = _mla_qk_bwd(sv["qp"], sv["kvp"], proj, offs, tabs, sv["qg"], sv["kg"], dQ, dK, dV, f"mla_qk_bwd{l}")
        grads["mla_q_norm_g"][l], grads["mla_k_norm_g"][l] = dqg[0, :QK_HEAD], dkg[0, :QK_HEAD]
        dcqn = _matmul(dqp, w_uq_p[l], "nt", F32, f"uq_bwd_x{l}")
        d_wuq = _matmul(sv["cqn"], dqp, "tn", BF16, f"uq_bwd_w{l}")
        dckvn = _matmul(dkvp, w_ukv_p[l], "nt", F32, f"ukv_bwd_x{l}")
        d_wukv = _matmul(sv["ckvn"], dkvp, "tn", BF16, f"ukv_bwd_w{l}")
        grads["w_uq"][l] = d_wuq.reshape(L, MLA_HEADS, QK_PAD)[..., :QK_HEAD].reshape(L, MLA_HEADS * QK_HEAD)
        grads["w_ukv"][l] = jnp.concatenate([d_wukv[:, :MLA_HEADS * QK_NOPE].reshape(L, MLA_HEADS, QK_NOPE),
                                             d_wukv[:, MLA_HEADS * QK_NOPE:].reshape(L, MLA_HEADS, V_HEAD)], axis=-1).reshape(L, -1)
        d_cq, dqa = _rms_rows_bwd(proj, L, offs["cq"], row(q_a_norm_g, l), dcqn, BF16, f"cq_norm_bwd{l}")
        d_ckv, dkva = _rms_rows_bwd(proj, L, offs["ckv"], row(kv_a_norm_g, l), dckvn, BF16, f"ckv_norm_bwd{l}")
        grads["q_a_norm_g"][l], grads["kv_a_norm_g"][l] = dqa[0], dkva[0]
        d_cb, d_cc, d_cx, d_cz, d_cw = _conv_bwd(proj, offs, conv_full[l], dys[2], f"conv_bwd{l}")
        grads["conv_w"][l] = d_cw
        d_xq, d_xz, d_memkv, dxqg, dxkg = _xattn_bwd(proj, offs, sv["mem_kv"], row(xattn_q_norm_g, l), row(xattn_k_norm_g, l), dys[3], f"xattn_bwd{l}")
        grads["xattn_q_norm_g"][l], grads["xattn_k_norm_g"][l] = dxqg[0], dxkg[0]
        grads["w_mem_kv"][l] = _matmul(sv["memn"], d_memkv, "tn", BF16, f"mem_kv_bwd_w{l}")
        dmemn = _matmul(d_memkv, w_mem_f, "nt", F32, f"mem_kv_bwd_x{l}", b_idx=(l,))
        grads["mem_norm_g"][l] = _norm_g_grad(mems, row(mem_norm_g, l), dmemn, f"mem_norm_bwd{l}")[0]
        dproj = jnp.concatenate([d_pv, d_pz, d_cq, d_ckv, d_mz, d_cb, d_cc, d_cx, d_cz, d_xq, d_xz, dgp0, dgp1, dgp2, dgp3, d_kr], axis=1)
        early_names = [k for k in names_sh if k != "w_in"]
        early = pair_sums([grads[k][l] for k in early_names], early_names, f"early{l}")
        dh, early_received = _matmul(dproj, w_in_p[l], "nt", F32, f"proj_bwd_x{l}", tm=1024, tn=1024, tk=3712, comm=_ChipExchangePlan(early))
        n_p = offs["n"]
        w_tiles = dict(tm=512, tn=3712, tk=2048, n_outer=True)
        unpermute = lambda d: jnp.concatenate([d[:, :seg_a], d[:, n_p - 128:n_p - 128 + QK_ROPE], d[:, seg_a:n_p - 128]], axis=1)
        received[l] = [None] + early_received
        if l > 0:
            d_win = unpermute(_matmul(sv["hT"], dproj, "nn", BF16, f"proj_bwd_w{l}", **w_tiles))
            pending = (l, _ChipExchangePlan(pair_sums([d_win], ["w_in"], f"w_in{l}")))
        else:
            n_split = 4
            rows_per = D // n_split
            plan, pieces = None, []
            for part in range(n_split):
                hT_rows = sv["hT"][part * rows_per:(part + 1) * rows_per]
                if plan is None:
                    d = _matmul(hT_rows, dproj, "nn", BF16, f"proj_bwd_w0_part{part}", **w_tiles)
                else:
                    d, got = _matmul(hT_rows, dproj, "nn", BF16, f"proj_bwd_w0_part{part}", comm=plan, **w_tiles)
                    pieces.append(got[0])
                plan = _ChipExchangePlan(pair_sums([unpermute(d)], ["w_in"], f"w_in0_part{part}"))
            pieces.append(_run_plan(plan, "chip_exchange_w_in0_last")[0])
            received[0][0] = pieces
        gx, dng = _rms_rows_bwd(sv["x"], D, 0, row(norm_g, l), dh, F32, f"norm_bwd{l}", res=gx)
        grads["norm_g"][l] = dng[0]
    grad_x = gx.reshape(1, S, D)
    full = {k: jnp.stack(v) for k, v in grads.items() if k not in shard_axis}

    moments = dict(w_in=(m_w_in, v_w_in), pool_w=(m_pool_w, v_pool_w), w_uq=(m_w_uq, v_w_uq), w_ukv=(m_w_ukv, v_w_ukv),
                   conv_w=(m_conv_w, v_conv_w), w_mem_kv=(m_w_mem_kv, v_w_mem_kv), w_branch=(m_w_branch, v_w_branch), w_out=(m_w_out, v_w_out))
    results = {}
    for ki, k in enumerate(names_sh):
        w = sharded[k]
        m, v = moments[k]
        flat = lambda a: a.reshape(-1, w.shape[-1])
        parts = [p for l in range(depth) for p in (received[l][ki] if isinstance(received[l][ki], list) else [received[l][ki]])]
        if any(p.shape[1] % 16 for p in parts):
            parts = [jnp.concatenate(parts, axis=1)]
        outs = _adamw(parts, flat(w), flat(m), flat(v), f"adamw_{k}")
        results[k] = [o.reshape(w.shape) for o in outs]

    small = dict(norm_g=(norm_g, m_norm_g, v_norm_g), gate_b=(gate_b, m_gate_b, v_gate_b), pool_scale=(pool_scale, m_pool_scale, v_pool_scale),
                 q_a_norm_g=(q_a_norm_g, m_q_a_norm_g, v_q_a_norm_g), kv_a_norm_g=(kv_a_norm_g, m_kv_a_norm_g, v_kv_a_norm_g),
                 mla_q_norm_g=(mla_q_norm_g, m_mla_q_norm_g, v_mla_q_norm_g), mla_k_norm_g=(mla_k_norm_g, m_mla_k_norm_g, v_mla_k_norm_g),
                 mem_norm_g=(mem_norm_g, m_mem_norm_g, v_mem_norm_g), xattn_q_norm_g=(xattn_q_norm_g, m_xattn_q_norm_g, v_xattn_q_norm_g),
                 xattn_k_norm_g=(xattn_k_norm_g, m_xattn_k_norm_g, v_xattn_k_norm_g))
    names_sm = list(small)
    sizes = [small[k][0].size for k in names_sm]
    total = sum(sizes)
    rows = -(-total // (8 * 128)) * 8

    def pack(arrs):
        flat = jnp.concatenate([a.reshape(-1) for a in arrs])
        return jnp.pad(flat, (0, rows * 128 - total)).reshape(rows, 128)

    g_small = _run_plan(_GatherPlan([pack([full[k] for k in names_sm])]), "gather_small_grads")[0]
    outs = _adamw([g_small], pack([small[k][0] for k in names_sm]), pack([small[k][1] for k in names_sm]),
                  pack([small[k][2] for k in names_sm]), "adamw_small")
    for kind, o in enumerate(outs):
        flat, pos = o.reshape(-1), 0
        for k, n in zip(names_sm, sizes):
            results.setdefault(k, [None] * 4)[kind] = flat[pos:pos + n].reshape(small[k][0].shape)
            pos += n

    order = ["norm_g", "w_in", "gate_b", "pool_w", "pool_scale", "q_a_norm_g", "kv_a_norm_g", "w_uq", "w_ukv", "mla_q_norm_g",
             "mla_k_norm_g", "conv_w", "mem_norm_g", "w_mem_kv", "xattn_q_norm_g", "xattn_k_norm_g", "w_branch", "w_out"]
    return (loss, grad_x, *[results[k][0] for k in order], *[results[k][1] for k in order],
            *[results[k][2] for k in order], *[results[k][3] for k in order])
```

```python
import functools

import jax
import jax.numpy as jnp
from jax import lax
from jax.experimental import pallas as pl
from jax.experimental.pallas import tpu as pltpu

F32 = jnp.float32
BF16 = jnp.bfloat16
MESH_AXES = ("x", "y", "c")
N_DEV = 8

BRANCH_W = 1024
POOL_WINDOWS = (2, 4, 8, 16)
POOL_GW = BRANCH_W // 4
MLA_HEADS = 8
QK_NOPE = 128
QK_ROPE = 64
QK_HEAD = QK_NOPE + QK_ROPE
QK_PAD = 256
V_HEAD = 128
XATTN_HEADS = 4
XATTN_HD = BRANCH_W // XATTN_HEADS
ROPE_THETA = 10000.0
EPS = 1e-6
ADAM_LR = 0.001
ADAM_B1 = 0.9
ADAM_B2 = 0.999
ADAM_EPS = 1e-08
ADAM_WD = 0.01
ADAM_STEP = 10
HALO = 16
NEG = -1e30
SCORE_SCALE_LOG2 = (QK_HEAD ** -0.5) * 1.4426950408889634
VMEM_LIMIT = 56 << 20


def _pick(n, pref, align):
    if n <= pref:
        return n
    t = (pref // align) * align
    while t >= align:
        if n % t == 0:
            return t
        t -= align
    return n


def _params(sem):
    return pltpu.CompilerParams(dimension_semantics=sem, vmem_limit_bytes=VMEM_LIMIT)


def _silu(z):
    return z * (1.0 / (1.0 + jnp.exp(-z)))


def _silu_and_grad(z):
    s = 1.0 / (1.0 + jnp.exp(-z))
    return z * s, s * (1.0 + z * (1.0 - s))


def _dot(a, b):
    return jnp.dot(a.astype(BF16), b.astype(BF16), preferred_element_type=F32)


def _dot_nt(a, b):
    return lax.dot_general(a.astype(BF16), b.astype(BF16), (((1,), (1,)), ((), ())), preferred_element_type=F32)


def _dot_tn(a, b):
    return lax.dot_general(a.astype(BF16), b.astype(BF16), (((0,), (0,)), ((), ())), preferred_element_type=F32)


def _rms(x, g, n):
    r = lax.rsqrt(jnp.sum(x * x, axis=-1, keepdims=True) / n + EPS)
    return x * r * g, r


def _rms_bwd(x, g, r, dy, n):
    dyg = dy * g
    dx = r * dyg - x * (r * r * r) * (jnp.sum(dyg * x, axis=-1, keepdims=True) / n)
    dg = jnp.sum(dy * x * r, axis=0, keepdims=True)
    return dx, dg


def _col(tm, w, off):
    assert off % w == 0
    cb = off // w
    return pl.BlockSpec((tm, w), lambda i: (i, cb))


def _whole(shape):
    nd = len(shape)
    return pl.BlockSpec(shape, lambda *_: (0,) * nd)


def _lead_spec(block, fn, idx):
    if not idx:
        return pl.BlockSpec(block, fn)
    return pl.BlockSpec((None,) * len(idx) + block, lambda i, j, k: tuple(idx) + fn(i, j, k))


def _hosted_call(plan, body, ins, in_specs, out_specs, out_shape, scratch, grid, name):
    n_in, n_out, n_scr, n = len(ins), len(out_specs), len(scratch), plan.n
    n_steps = 1
    for g in grid:
        n_steps *= g

    def wrapped(*refs):
        a, c_in = refs[:n_in], refs[n_in:n_in + n]
        o, c_out = refs[n_in + n:n_in + n + n_out], refs[n_in + n + n_out:n_in + 2 * n + n_out]
        rest = refs[n_in + 2 * n + n_out:]
        scr, sems = rest[:n_scr], rest[n_scr:]
        step = 0
        for d, g in enumerate(grid):
            step = step * g + pl.program_id(d)

        @pl.when(step == 0)
        def _():
            plan.start(c_in, c_out, sems)

        body(*a, *o, *scr)

        @pl.when(step == n_steps // 3)
        def _():
            plan.relay(0, c_in, c_out, sems)

        @pl.when(step == (2 * n_steps) // 3)
        def _():
            plan.relay(1, c_in, c_out, sems)

        @pl.when(step == n_steps - 1)
        def _():
            plan.finish(c_in, c_out, sems)

    res = pl.pallas_call(
        wrapped, grid=grid, in_specs=list(in_specs) + [HBM_SPEC] * n, out_specs=list(out_specs) + [HBM_SPEC] * n,
        out_shape=list(out_shape) + plan.out_shape, scratch_shapes=list(scratch) + plan.scratch, name=name,
        compiler_params=_params(("arbitrary",) * len(grid)),
    )(*ins, *plan.arrs)
    return list(res[:n_out]), list(res[n_out:])


def _matmul(a, b, mode, out_dtype, name, tm=1024, tn=1024, tk=2048, res=None, a_idx=(), b_idx=(), n_outer=False, comm=None):
    a_shape, b_shape = a.shape[len(a_idx):], b.shape[len(b_idx):]
    if mode == "nn":
        (M, K), N = a_shape, b_shape[1]
    elif mode == "nt":
        (M, K), N = a_shape, b_shape[0]
    else:
        (K, M), N = a_shape, b_shape[1]
    tm, tn, tk = _pick(M, tm, 128), _pick(N, tn, 128), _pick(K, tk, 128)
    nk = K // tk
    order = (lambda fn: (lambda j, i, k: fn(i, j, k))) if n_outer else (lambda fn: fn)
    if mode == "nn":
        a_spec = _lead_spec((tm, tk), order(lambda i, j, k: (i, k)), a_idx)
        b_spec = _lead_spec((tk, tn), order(lambda i, j, k: (k, j)), b_idx)
        op = _dot
    elif mode == "nt":
        a_spec = _lead_spec((tm, tk), order(lambda i, j, k: (i, k)), a_idx)
        b_spec = _lead_spec((tn, tk), order(lambda i, j, k: (j, k)), b_idx)
        op = _dot_nt
    else:
        a_spec = _lead_spec((tk, tm), order(lambda i, j, k: (k, i)), a_idx)
        b_spec = _lead_spec((tk, tn), order(lambda i, j, k: (k, j)), b_idx)
        op = _dot_tn
    o_spec = pl.BlockSpec((tm, tn), order(lambda i, j, k: (i, j)))
    has_res = res is not None

    def body(*refs):
        a_ref, b_ref = refs[0], refs[1]
        res_ref = refs[2] if has_res else None
        o_ref = refs[2 + has_res]
        part = op(a_ref[...], b_ref[...])

        def finish(acc):
            if has_res:
                acc = res_ref[...] + acc
            o_ref[...] = acc.astype(out_dtype)

        if nk == 1:
            finish(part)
        else:
            acc_ref = refs[3 + has_res]
            k = pl.program_id(2)

            @pl.when(k == 0)
            def _():
                acc_ref[...] = part

            @pl.when(k > 0)
            def _():
                acc_ref[...] += part

            @pl.when(k == nk - 1)
            def _():
                finish(acc_ref[...])

    ins = [a, b] + ([res] if has_res else [])
    in_specs = [a_spec, b_spec] + ([o_spec] if has_res else [])
    scratch = [pltpu.VMEM((tm, tn), F32)] if nk > 1 else []
    grid = (N // tn, M // tm, nk) if n_outer else (M // tm, N // tn, nk)
    out_shape = jax.ShapeDtypeStruct((M, N), out_dtype)
    if comm is not None:
        (out,), carried = _hosted_call(comm, body, ins, in_specs, [o_spec], [out_shape], scratch, grid, name)
        return out, carried
    return pl.pallas_call(
        body, grid=grid, in_specs=in_specs, out_specs=o_spec, out_shape=out_shape, scratch_shapes=scratch, name=name,
        compiler_params=_params(("parallel", "parallel", "arbitrary")),
    )(*ins)


def _rms_rows(x, width, off, g, name, with_transpose=False):
    S = x.shape[0]
    tm = _pick(S, 512, 128 if with_transpose else 16)

    def body(x_ref, g_ref, o_ref, *t_ref):
        y, _ = _rms(x_ref[...], g_ref[...], width)
        o_ref[...] = y.astype(BF16)
        if with_transpose:
            t_ref[0][...] = y.T.astype(BF16)

    out_specs = [_col(tm, width, 0)] + ([pl.BlockSpec((width, tm), lambda i: (0, i))] if with_transpose else [])
    out_shape = [jax.ShapeDtypeStruct((S, width), BF16)] + ([jax.ShapeDtypeStruct((width, S), BF16)] if with_transpose else [])
    res = pl.pallas_call(
        body, grid=(S // tm,), in_specs=[_col(tm, width, off), _whole((1, width))], out_specs=out_specs,
        out_shape=out_shape, name=name, compiler_params=_params(("parallel",)),
    )(x, g)
    return res if with_transpose else res[0]


def _rms_rows_bwd(x, width, off, g, dy, out_dtype, name, res=None, with_bf16=False):
    S = x.shape[0]
    tm = _pick(S, 256, 16)
    has_res = res is not None

    def body(*refs):
        x_ref, g_ref, dy_ref = refs[:3]
        res_ref = refs[3] if has_res else None
        dx_ref, dg_ref = refs[3 + has_res:5 + has_res]
        xv, gv = x_ref[...], g_ref[...]
        _, r = _rms(xv, gv, width)
        dx, dg = _rms_bwd(xv, gv, r, dy_ref[...], width)
        if has_res:
            dx = res_ref[...] + dx
        dx_ref[...] = dx.astype(out_dtype)
        if with_bf16:
            refs[5 + has_res][...] = dx.astype(BF16)

        @pl.when(pl.program_id(0) == 0)
        def _():
            dg_ref[...] = jnp.zeros_like(dg_ref)

        dg_ref[...] += dg

    ins = [x, g, dy] + ([res] if has_res else [])
    in_specs = [_col(tm, width, off), _whole((1, width)), _col(tm, width, 0)] + ([_col(tm, width, 0)] if has_res else [])
    extra = with_bf16 * 1
    return pl.pallas_call(
        body, grid=(S // tm,), in_specs=in_specs, out_specs=[_col(tm, width, 0), _whole((1, width))] + [_col(tm, width, 0)] * extra,
        out_shape=[jax.ShapeDtypeStruct((S, width), out_dtype), jax.ShapeDtypeStruct((1, width), F32)]
        + [jax.ShapeDtypeStruct((S, width), BF16)] * extra, name=name,
        compiler_params=_params(("arbitrary",)),
    )(*ins)


def _norm_g_grad(x, g, dy, name):
    S, width = x.shape
    tm = _pick(S, 256, 16)

    def body(x_ref, g_ref, dy_ref, dg_ref):
        xv = x_ref[...]
        _, r = _rms(xv, g_ref[...], width)

        @pl.when(pl.program_id(0) == 0)
        def _():
            dg_ref[...] = jnp.zeros_like(dg_ref)

        dg_ref[...] += jnp.sum(dy_ref[...] * xv * r, axis=0, keepdims=True)

    return pl.pallas_call(
        body, grid=(S // tm,), in_specs=[_col(tm, width, 0), _whole((1, width)), _col(tm, width, 0)],
        out_specs=_whole((1, width)), out_shape=jax.ShapeDtypeStruct((1, width), F32), name=name,
        compiler_params=_params(("arbitrary",)),
    )(x, g, dy)


def _loss_rows(y, target, name):
    S, D = y.shape
    tm = _pick(S, 512, 16)

    def body(y_ref, t_ref, g_ref, gb_ref, l_ref):
        err = y_ref[...] - t_ref[...]
        g = err / D
        g_ref[...] = g
        gb_ref[...] = g.astype(BF16)

        @pl.when(pl.program_id(0) == 0)
        def _():
            l_ref[...] = jnp.zeros_like(l_ref)

        row = jnp.sum(err * err, axis=-1, keepdims=True) / D
        l_ref[...] += jnp.sum(row, axis=0, keepdims=True)

    return pl.pallas_call(
        body, grid=(S // tm,), in_specs=[_col(tm, D, 0), _col(tm, D, 0)], out_specs=[_col(tm, D, 0), _col(tm, D, 0), _whole((1, 1))],
        out_shape=[jax.ShapeDtypeStruct((S, D), F32), jax.ShapeDtypeStruct((S, D), BF16), jax.ShapeDtypeStruct((1, 1), F32)], name=name,
        compiler_params=_params(("arbitrary",)),
    )(y, target)


def _row_index(i, ts):
    return i * ts + lax.broadcasted_iota(jnp.int32, (ts, 1), 0)


def _pool_mixed(scr, pv_ref, g, t, ts):
    w, lo = POOL_WINDOWS[g], g * POOL_GW
    win = scr[HALO:HALO + ts, lo:lo + POOL_GW]
    for k in range(1, w):
        win = win + scr[HALO - k:HALO - k + ts, lo:lo + POOL_GW]
    cnt = jnp.minimum(t + 1, w).astype(F32)
    return win / cnt - pv_ref[:, lo:lo + POOL_GW], cnt


def _halo_before(ts, off):
    nb = ts // HALO
    cb = off // BRANCH_W
    return pl.BlockSpec((HALO, BRANCH_W), lambda i: (jnp.maximum(i * nb - 1, 0), cb))


def _halo_after(ts, off, n_tiles):
    nb = ts // HALO
    cb = off // BRANCH_W
    last = n_tiles * nb - 1
    return pl.BlockSpec((HALO, BRANCH_W), lambda i: (jnp.minimum((i + 1) * nb, last), cb))


def _pool_fwd(proj, offs, pool_w, pool_scale, name):
    S = proj.shape[0]
    ts = _pick(S, 512, HALO)

    def body(pv_ref, halo_ref, pz_ref, w_ref, sc_ref, y_ref, scr):
        i = pl.program_id(0)
        scr[0:HALO, :] = jnp.where(i > 0, halo_ref[...], 0.0)
        scr[HALO:HALO + ts, :] = pv_ref[...]
        t = _row_index(i, ts)
        for g in range(4):
            lo = g * POOL_GW
            mixed, _ = _pool_mixed(scr, pv_ref, g, t, ts)
            out = _dot(mixed, w_ref[g])
            y_ref[:, lo:lo + POOL_GW] = (out * sc_ref[:, lo:lo + POOL_GW] * _silu(pz_ref[:, lo:lo + POOL_GW])).astype(BF16)

    return pl.pallas_call(
        body, grid=(S // ts,),
        in_specs=[_col(ts, BRANCH_W, offs["pv"]), _halo_before(ts, offs["pv"]), _col(ts, BRANCH_W, offs["pz"]),
                  _whole((4, POOL_GW, POOL_GW)), _whole((1, BRANCH_W))],
        out_specs=_col(ts, BRANCH_W, 0), out_shape=jax.ShapeDtypeStruct((S, BRANCH_W), BF16),
        scratch_shapes=[pltpu.VMEM((HALO + ts, BRANCH_W), F32)], name=name, compiler_params=_params(("parallel",)),
    )(proj, proj, proj, pool_w, pool_scale)


def _pool_bwd_a(proj, offs, pool_w, pool_scale, dy, name):
    S = proj.shape[0]
    ts = _pick(S, 512, HALO)

    def body(pv_ref, halo_ref, pz_ref, w_ref, sc_ref, dy_ref, dpz_ref, e_ref, dw_ref, dsc_ref, scr):
        i = pl.program_id(0)
        scr[0:HALO, :] = jnp.where(i > 0, halo_ref[...], 0.0)
        scr[HALO:HALO + ts, :] = pv_ref[...]
        t = _row_index(i, ts)

        @pl.when(i == 0)
        def _():
            dw_ref[...] = jnp.zeros_like(dw_ref)
            dsc_ref[...] = jnp.zeros_like(dsc_ref)

        for g in range(4):
            lo = g * POOL_GW
            sl = slice(lo, lo + POOL_GW)
            mixed, cnt = _pool_mixed(scr, pv_ref, g, t, ts)
            out = _dot(mixed, w_ref[g])
            sz, dsz = _silu_and_grad(pz_ref[:, sl])
            dyv, sc = dy_ref[:, sl], sc_ref[:, sl]
            d_out = dyv * sc * sz
            dsc_ref[:, sl] += jnp.sum(dyv * out * sz, axis=0, keepdims=True)
            dpz_ref[:, sl] = (dyv * out * sc * dsz).astype(BF16)
            dw_ref[g] += _dot_tn(mixed, d_out)
            e_ref[:, sl] = _dot_nt(d_out, w_ref[g]) / cnt

    return pl.pallas_call(
        body, grid=(S // ts,),
        in_specs=[_col(ts, BRANCH_W, offs["pv"]), _halo_before(ts, offs["pv"]), _col(ts, BRANCH_W, offs["pz"]),
                  _whole((4, POOL_GW, POOL_GW)), _whole((1, BRANCH_W)), _col(ts, BRANCH_W, 0)],
        out_specs=[_col(ts, BRANCH_W, 0), _col(ts, BRANCH_W, 0), _whole((4, POOL_GW, POOL_GW)), _whole((1, BRANCH_W))],
        out_shape=[jax.ShapeDtypeStruct((S, BRANCH_W), BF16), jax.ShapeDtypeStruct((S, BRANCH_W), F32),
                   jax.ShapeDtypeStruct((4, POOL_GW, POOL_GW), F32), jax.ShapeDtypeStruct((1, BRANCH_W), F32)],
        scratch_shapes=[pltpu.VMEM((HALO + ts, BRANCH_W), F32)], name=name, compiler_params=_params(("arbitrary",)),
    )(proj, proj, proj, pool_w, pool_scale, dy)


def _pool_bwd_b(e, name):
    S = e.shape[0]
    ts = _pick(S, 512, HALO)
    n_tiles = S // ts

    def body(e_ref, halo_ref, dpv_ref, scr):
        i = pl.program_id(0)
        scr[0:ts, :] = e_ref[...]
        scr[ts:ts + HALO, :] = jnp.where(i < n_tiles - 1, halo_ref[...], 0.0)
        t = _row_index(i, ts)
        for g in range(4):
            w, lo = POOL_WINDOWS[g], g * POOL_GW
            acc = scr[0:ts, lo:lo + POOL_GW]
            ev = acc
            for k in range(1, w):
                acc = acc + scr[k:k + ts, lo:lo + POOL_GW]
            cnt = jnp.minimum(t + 1, w).astype(F32)
            dpv_ref[:, lo:lo + POOL_GW] = (acc - ev * cnt).astype(BF16)

    return pl.pallas_call(
        body, grid=(n_tiles,), in_specs=[_col(ts, BRANCH_W, 0), _halo_after(ts, 0, n_tiles)], out_specs=_col(ts, BRANCH_W, 0),
        out_shape=jax.ShapeDtypeStruct((S, BRANCH_W), BF16), scratch_shapes=[pltpu.VMEM((ts + HALO, BRANCH_W), F32)],
        name=name, compiler_params=_params(("parallel",)),
    )(e, e)


def _conv_fwd(proj, offs, conv_w, name):
    S = proj.shape[0]
    ts = _pick(S, 512, HALO)

    def body(cb_ref, cc_ref, cx_ref, cz_ref, hc_ref, hx_ref, w_ref, y_ref, scr):
        i = pl.program_id(0)
        scr[0:HALO, :] = jnp.where(i > 0, hc_ref[...] * hx_ref[...], 0.0)
        scr[HALO:HALO + ts, :] = cc_ref[...] * cx_ref[...]
        y = w_ref[0:1, :] * scr[HALO - 2:HALO - 2 + ts, :] + w_ref[1:2, :] * scr[HALO - 1:HALO - 1 + ts, :] + w_ref[2:3, :] * scr[HALO:HALO + ts, :]
        y_ref[...] = (cb_ref[...] * y * _silu(cz_ref[...])).astype(BF16)

    return pl.pallas_call(
        body, grid=(S // ts,),
        in_specs=[_col(ts, BRANCH_W, offs["cb"]), _col(ts, BRANCH_W, offs["cc"]), _col(ts, BRANCH_W, offs["cx"]),
                  _col(ts, BRANCH_W, offs["cz"]), _halo_before(ts, offs["cc"]), _halo_before(ts, offs["cx"]), _whole((3, BRANCH_W))],
        out_specs=_col(ts, BRANCH_W, 0), out_shape=jax.ShapeDtypeStruct((S, BRANCH_W), BF16),
        scratch_shapes=[pltpu.VMEM((HALO + ts, BRANCH_W), F32)], name=name, compiler_params=_params(("parallel",)),
    )(proj, proj, proj, proj, proj, proj, conv_w)


def _conv_bwd(proj, offs, conv_w, dy, name):
    S = proj.shape[0]
    ts = _pick(S, 256, HALO)
    n_tiles = S // ts

    def body(cb_ref, cc_ref, cx_ref, cz_ref, hc_ref, hx_ref, w_ref, dy_ref, ady_ref, acb_ref, acz_ref,
             dcb_ref, dcc_ref, dcx_ref, dcz_ref, dw_ref, scr_u, scr_d):
        i = pl.program_id(0)
        u = cc_ref[...] * cx_ref[...]
        scr_u[0:HALO, :] = jnp.where(i > 0, hc_ref[...] * hx_ref[...], 0.0)
        scr_u[HALO:HALO + ts, :] = u
        u1 = scr_u[HALO - 1:HALO - 1 + ts, :]
        u2 = scr_u[HALO - 2:HALO - 2 + ts, :]
        y = w_ref[0:1, :] * u2 + w_ref[1:2, :] * u1 + w_ref[2:3, :] * u
        sz, dsz = _silu_and_grad(cz_ref[...])
        dyv, cb = dy_ref[...], cb_ref[...]
        dcb_ref[...] = (dyv * y * sz).astype(BF16)
        dcz_ref[...] = (dyv * cb * y * dsz).astype(BF16)
        d_y = dyv * cb * sz
        scr_d[0:ts, :] = d_y
        scr_d[ts:ts + HALO, :] = jnp.where(i < n_tiles - 1, ady_ref[...] * acb_ref[...] * _silu(acz_ref[...]), 0.0)
        du = w_ref[2:3, :] * d_y + w_ref[1:2, :] * scr_d[1:1 + ts, :] + w_ref[0:1, :] * scr_d[2:2 + ts, :]
        dcc_ref[...] = (du * cx_ref[...]).astype(BF16)
        dcx_ref[...] = (du * cc_ref[...]).astype(BF16)

        @pl.when(i == 0)
        def _():
            dw_ref[...] = jnp.zeros_like(dw_ref)

        dw_ref[0:1, :] += jnp.sum(d_y * u2, axis=0, keepdims=True)
        dw_ref[1:2, :] += jnp.sum(d_y * u1, axis=0, keepdims=True)
        dw_ref[2:3, :] += jnp.sum(d_y * u, axis=0, keepdims=True)

    row = lambda off: _col(ts, BRANCH_W, off)
    return pl.pallas_call(
        body, grid=(n_tiles,),
        in_specs=[row(offs["cb"]), row(offs["cc"]), row(offs["cx"]), row(offs["cz"]), _halo_before(ts, offs["cc"]),
                  _halo_before(ts, offs["cx"]), _whole((3, BRANCH_W)), row(0), _halo_after(ts, 0, n_tiles),
                  _halo_after(ts, offs["cb"], n_tiles), _halo_after(ts, offs["cz"], n_tiles)],
        out_specs=[row(0), row(0), row(0), row(0), _whole((3, BRANCH_W))],
        out_shape=[jax.ShapeDtypeStruct((S, BRANCH_W), BF16)] * 4 + [jax.ShapeDtypeStruct((3, BRANCH_W), F32)],
        scratch_shapes=[pltpu.VMEM((HALO + ts, BRANCH_W), F32), pltpu.VMEM((ts + HALO, BRANCH_W), F32)], name=name,
        compiler_params=_params(("arbitrary",)),
    )(proj, proj, proj, proj, proj, proj, conv_w, dy, dy, proj, proj)


def _xattn_head(xq, kraw, v, qg, kg):
    qn, rq = _rms(xq, qg, XATTN_HD)
    kn, rk = _rms(kraw, kg, XATTN_HD)
    s = _dot_nt(qn, kn) * (XATTN_HD ** -0.5)
    e = jnp.exp(s - jnp.max(s, axis=-1, keepdims=True))
    p = e / jnp.sum(e, axis=-1, keepdims=True)
    return qn, rq, kn, rk, p, _dot(p, v)


def _xattn_fwd(proj, offs, mem_kv, qg, kg, name):
    S, M = proj.shape[0], mem_kv.shape[0]
    tm = _pick(S, 512, 16)

    def body(xq_ref, xz_ref, kv_ref, qg_ref, kg_ref, y_ref):
        for h in range(XATTN_HEADS):
            sl = slice(h * XATTN_HD, (h + 1) * XATTN_HD)
            vs = slice(BRANCH_W + h * XATTN_HD, BRANCH_W + (h + 1) * XATTN_HD)
            o = _xattn_head(xq_ref[:, sl], kv_ref[:, sl], kv_ref[:, vs], qg_ref[...], kg_ref[...])[-1]
            y_ref[:, sl] = (o * _silu(xz_ref[:, sl])).astype(BF16)

    return pl.pallas_call(
        body, grid=(S // tm,),
        in_specs=[_col(tm, BRANCH_W, offs["xq"]), _col(tm, BRANCH_W, offs["xz"]), _whole((M, 2 * BRANCH_W)),
                  _whole((1, XATTN_HD)), _whole((1, XATTN_HD))],
        out_specs=_col(tm, BRANCH_W, 0), out_shape=jax.ShapeDtypeStruct((S, BRANCH_W), BF16), name=name,
        compiler_params=_params(("parallel",)),
    )(proj, proj, mem_kv, qg, kg)


def _xattn_bwd(proj, offs, mem_kv, qg, kg, dy, name):
    S, M = proj.shape[0], mem_kv.shape[0]
    tm = _pick(S, 256, 16)
    n_tiles = S // tm

    def body(xq_ref, xz_ref, kv_ref, qg_ref, kg_ref, dy_ref, dxq_ref, dxz_ref, dkv_ref, dqg_ref, dkg_ref, dkn_acc):
        i = pl.program_id(0)

        @pl.when(i == 0)
        def _():
            dkv_ref[...] = jnp.zeros_like(dkv_ref)
            dqg_ref[...] = jnp.zeros_like(dqg_ref)
            dkg_ref[...] = jnp.zeros_like(dkg_ref)
            dkn_acc[...] = jnp.zeros_like(dkn_acc)

        qg_v, kg_v = qg_ref[...], kg_ref[...]
        for h in range(XATTN_HEADS):
            sl = slice(h * XATTN_HD, (h + 1) * XATTN_HD)
            vs = slice(BRANCH_W + h * XATTN_HD, BRANCH_W + (h + 1) * XATTN_HD)
            xq, v = xq_ref[:, sl], kv_ref[:, vs]
            qn, rq, kn, rk, p, o = _xattn_head(xq, kv_ref[:, sl], v, qg_v, kg_v)
            sz, dsz = _silu_and_grad(xz_ref[:, sl])
            dyv = dy_ref[:, sl]
            dxz_ref[:, sl] = (dyv * o * dsz).astype(BF16)
            do = dyv * sz
            dkv_ref[:, vs] += _dot_tn(p, do)
            dp = _dot_nt(do, v)
            ds = p * (dp - jnp.sum(dp * p, axis=-1, keepdims=True)) * (XATTN_HD ** -0.5)
            dqn = _dot(ds, kn)
            dkn_acc[:, sl] += _dot_tn(ds, qn)
            dxq, dqg = _rms_bwd(xq, qg_v, rq, dqn, XATTN_HD)
            dxq_ref[:, sl] = dxq.astype(BF16)
            dqg_ref[...] += dqg

        @pl.when(i == n_tiles - 1)
        def _():
            for h in range(XATTN_HEADS):
                sl = slice(h * XATTN_HD, (h + 1) * XATTN_HD)
                kraw = kv_ref[:, sl]
                _, rk = _rms(kraw, kg_v, XATTN_HD)
                dk, dkg = _rms_bwd(kraw, kg_v, rk, dkn_acc[:, sl], XATTN_HD)
                dkv_ref[:, sl] = dk
                dkg_ref[...] += dkg

    return pl.pallas_call(
        body, grid=(n_tiles,),
        in_specs=[_col(tm, BRANCH_W, offs["xq"]), _col(tm, BRANCH_W, offs["xz"]), _whole((M, 2 * BRANCH_W)),
                  _whole((1, XATTN_HD)), _whole((1, XATTN_HD)), _col(tm, BRANCH_W, 0)],
        out_specs=[_col(tm, BRANCH_W, 0), _col(tm, BRANCH_W, 0), _whole((M, 2 * BRANCH_W)), _whole((1, XATTN_HD)), _whole((1, XATTN_HD))],
        out_shape=[jax.ShapeDtypeStruct((S, BRANCH_W), BF16), jax.ShapeDtypeStruct((S, BRANCH_W), BF16),
                   jax.ShapeDtypeStruct((M, 2 * BRANCH_W), F32), jax.ShapeDtypeStruct((1, XATTN_HD), F32), jax.ShapeDtypeStruct((1, XATTN_HD), F32)],
        scratch_shapes=[pltpu.VMEM((M, BRANCH_W), F32)], name=name, compiler_params=_params(("arbitrary",)),
    )(proj, proj, mem_kv, qg, kg, dy)


def _rope(c, tc, ta, tb):
    return c * tc + pltpu.roll(c, 96, 1) * ta + pltpu.roll(c, 32, 1) * tb


def _rope_t(d, tc, ta, tb):
    return d * tc + pltpu.roll(d * ta, 32, 1) + pltpu.roll(d * tb, 96, 1)


def _mla_qk_fwd(qp, kvp, proj, offs, tabs, qg, kg, name):
    S = qp.shape[0]
    tm = _pick(S, 256, 16)
    W = MLA_HEADS * QK_PAD

    def body(q_ref, kv_ref, kr_ref, tc_ref, ta_ref, tb_ref, qg_ref, kg_ref, qo_ref, ko_ref, vo_ref):
        tc, ta, tb = tc_ref[...], ta_ref[...], tb_ref[...]
        qg_v, kg_v = qg_ref[...], kg_ref[...]
        kr = kr_ref[...]
        kr_ss = jnp.sum(kr * kr, axis=-1, keepdims=True)
        for h in range(MLA_HEADS):
            sl = slice(h * QK_PAD, (h + 1) * QK_PAD)
            qn, _ = _rms(q_ref[:, sl], qg_v, QK_HEAD)
            qo_ref[:, sl] = jnp.concatenate([qn[:, :QK_NOPE], _rope(qn[:, QK_NOPE:], tc, ta, tb)], axis=1).astype(BF16)
            kn = kv_ref[:, h * QK_NOPE:(h + 1) * QK_NOPE]
            r = lax.rsqrt((jnp.sum(kn * kn, axis=-1, keepdims=True) + kr_ss) / QK_HEAD + EPS)
            ko_ref[:, sl] = jnp.concatenate([kn * r * kg_v[:, :QK_NOPE], _rope(kr * r * kg_v[:, QK_NOPE:], tc, ta, tb)], axis=1).astype(BF16)
        vo_ref[...] = kv_ref[:, MLA_HEADS * QK_NOPE:].astype(BF16)

    return pl.pallas_call(
        body, grid=(S // tm,),
        in_specs=[_col(tm, W, 0), _col(tm, W, 0), _col(tm, 128, offs["kr"]), _col(tm, 128, 0), _col(tm, 128, 0), _col(tm, 128, 0),
                  _whole((1, QK_PAD)), _whole((1, QK_PAD))],
        out_specs=[_col(tm, W, 0), _col(tm, W, 0), _col(tm, BRANCH_W, 0)],
        out_shape=[jax.ShapeDtypeStruct((S, W), BF16), jax.ShapeDtypeStruct((S, W), BF16), jax.ShapeDtypeStruct((S, BRANCH_W), BF16)],
        name=name, compiler_params=_params(("parallel",)),
    )(qp, kvp, proj, *tabs, qg, kg)


def _mla_qk_bwd(qp, kvp, proj, offs, tabs, qg, kg, dQ, dK, dV, name):
    S = qp.shape[0]
    tm = _pick(S, 256, 16)
    W = MLA_HEADS * QK_PAD

    def body(q_ref, kv_ref, kr_ref, tc_ref, ta_ref, tb_ref, qg_ref, kg_ref, dQ_ref, dK_ref, dV_ref,
             dq_ref, dkv_ref, dkr_ref, dqg_ref, dkg_ref):
        @pl.when(pl.program_id(0) == 0)
        def _():
            dqg_ref[...] = jnp.zeros_like(dqg_ref)
            dkg_ref[...] = jnp.zeros_like(dkg_ref)

        tc, ta, tb = tc_ref[...], ta_ref[...], tb_ref[...]
        qg_v, kg_v = qg_ref[...], kg_ref[...]
        kg_n, kg_r = kg_v[:, :QK_NOPE], kg_v[:, QK_NOPE:]
        kr = kr_ref[...]
        kr_ss = jnp.sum(kr * kr, axis=-1, keepdims=True)
        dkr = jnp.zeros_like(kr)
        dkg_n = jnp.zeros((1, QK_NOPE), F32)
        dkg_r = jnp.zeros((1, QK_NOPE), F32)
        for h in range(MLA_HEADS):
            sl = slice(h * QK_PAD, (h + 1) * QK_PAD)
            qv = q_ref[:, sl]
            _, rq = _rms(qv, qg_v, QK_HEAD)
            dQv = dQ_ref[:, sl]
            dqn = jnp.concatenate([dQv[:, :QK_NOPE], _rope_t(dQv[:, QK_NOPE:], tc, ta, tb)], axis=1)
            dq, dqg = _rms_bwd(qv, qg_v, rq, dqn, QK_HEAD)
            dq_ref[:, sl] = dq.astype(BF16)
            dqg_ref[...] += dqg
            ns = slice(h * QK_NOPE, (h + 1) * QK_NOPE)
            kn = kv_ref[:, ns]
            r = lax.rsqrt((jnp.sum(kn * kn, axis=-1, keepdims=True) + kr_ss) / QK_HEAD + EPS)
            dKv = dK_ref[:, sl]
            dyn = dKv[:, :QK_NOPE]
            dyr = _rope_t(dKv[:, QK_NOPE:], tc, ta, tb)
            proj_s = (jnp.sum(dyn * kg_n * kn, axis=-1, keepdims=True) + jnp.sum(dyr * kg_r * kr, axis=-1, keepdims=True)) / QK_HEAD
            r3 = r * r * r
            dkv_ref[:, ns] = (r * dyn * kg_n - kn * r3 * proj_s).astype(BF16)
            dkr = dkr + (r * dyr * kg_r - kr * r3 * proj_s)
            dkg_n = dkg_n + jnp.sum(dyn * kn * r, axis=0, keepdims=True)
            dkg_r = dkg_r + jnp.sum(dyr * kr * r, axis=0, keepdims=True)
        dkg_ref[...] += jnp.concatenate([dkg_n, dkg_r], axis=1)
        dkr_ref[...] = dkr.astype(BF16)
        dkv_ref[:, MLA_HEADS * QK_NOPE:] = dV_ref[...].astype(BF16)

    return pl.pallas_call(
        body, grid=(S // tm,),
        in_specs=[_col(tm, W, 0), _col(tm, W, 0), _col(tm, 128, offs["kr"]), _col(tm, 128, 0), _col(tm, 128, 0), _col(tm, 128, 0),
                  _whole((1, QK_PAD)), _whole((1, QK_PAD)), _col(tm, W, 0), _col(tm, W, 0), _col(tm, BRANCH_W, 0)],
        out_specs=[_col(tm, W, 0), _col(tm, W, 0), _col(tm, 128, 0), _whole((1, QK_PAD)), _whole((1, QK_PAD))],
        out_shape=[jax.ShapeDtypeStruct((S, W), BF16), jax.ShapeDtypeStruct((S, W), BF16), jax.ShapeDtypeStruct((S, 128), BF16),
                   jax.ShapeDtypeStruct((1, QK_PAD), F32), jax.ShapeDtypeStruct((1, QK_PAD), F32)],
        name=name, compiler_params=_params(("arbitrary",)),
    )(qp, kvp, proj, *tabs, qg, kg, dQ, dK, dV)


def _flash_fwd(Q, K, V, proj, offs, name, comm=None):
    S = Q.shape[0]
    t = _pick(S, 1024, 128)
    assert t % 256 == 0
    mz_cb = offs["mz"] // V_HEAD

    half = t // 2

    def fold(x, op):
        r = x[:, 0:128]
        for ch in range(1, x.shape[1] // 128):
            r = op(r, x[:, ch * 128:(ch + 1) * 128])
        return r

    def body(q_ref, k_ref, v_ref, mz_ref, o_ref, y_ref, lse_ref, mp_scr, lp_scr, acc_scr):
        qi = pl.program_id(1)
        q = q_ref[...]

        def raw_scores(j):
            rows_k = pl.ds(pl.multiple_of(j * t, t), t)
            return _dot_nt(q, k_ref[rows_k, :]), rows_k

        def diagonal_scores(part):
            q_lo = part * half
            rows_k = pl.ds(pl.multiple_of(qi * t + part * half, half), half)
            s = _dot_nt(q[q_lo:], k_ref[rows_k, :])
            rows = lax.broadcasted_iota(jnp.int32, s.shape, 0)
            cols = lax.broadcasted_iota(jnp.int32, s.shape, 1)
            return jnp.where(cols <= rows, s, NEG), rows_k, q_lo

        mp_scr[...] = jnp.full_like(mp_scr, NEG)

        def max_sweep(j, carry):
            s, _ = raw_scores(j)
            mp_scr[...] = jnp.maximum(mp_scr[...], fold(s, jnp.maximum))
            return carry

        lax.fori_loop(0, qi, max_sweep, 0)
        for part in range(2):
            s, _, q_lo = diagonal_scores(part)
            mp_scr[q_lo:, :] = jnp.maximum(mp_scr[q_lo:, :], fold(s, jnp.maximum))
        m = jnp.max(mp_scr[...], axis=-1, keepdims=True) * SCORE_SCALE_LOG2

        lp_scr[...] = jnp.zeros_like(lp_scr)
        acc_scr[...] = jnp.zeros_like(acc_scr)

        def sum_sweep(j, carry):
            s, rows_k = raw_scores(j)
            p = jnp.exp2(s * SCORE_SCALE_LOG2 - m)
            lp_scr[...] += fold(p, jnp.add)
            acc_scr[...] += _dot(p, v_ref[rows_k, :])
            return carry

        lax.fori_loop(0, qi, sum_sweep, 0)
        for part in range(2):
            s, rows_k, q_lo = diagonal_scores(part)
            p = jnp.exp2(s * SCORE_SCALE_LOG2 - m[q_lo:])
            lp_scr[q_lo:, :] += fold(p, jnp.add)
            acc_scr[q_lo:, :] += _dot(p, v_ref[rows_k, :])
        l = jnp.sum(lp_scr[...], axis=-1, keepdims=True)
        o = acc_scr[...] / l
        o_ref[...] = o
        y_ref[...] = (o * _silu(mz_ref[...])).astype(BF16)
        lse_ref[0] = m + jnp.log2(l)

    in_specs = [pl.BlockSpec((t, QK_PAD), lambda h, i: (i, h)), pl.BlockSpec((S, QK_PAD), lambda h, i: (0, h)),
                pl.BlockSpec((S, V_HEAD), lambda h, i: (0, h)), pl.BlockSpec((t, V_HEAD), lambda h, i: (i, mz_cb + h))]
    out_specs = [pl.BlockSpec((t, V_HEAD), lambda h, i: (i, h)), pl.BlockSpec((t, V_HEAD), lambda h, i: (i, h)),
                 pl.BlockSpec((1, t, 1), lambda h, i: (h, i, 0))]
    out_shape = [jax.ShapeDtypeStruct((S, BRANCH_W), F32), jax.ShapeDtypeStruct((S, BRANCH_W), BF16),
                 jax.ShapeDtypeStruct((MLA_HEADS, S, 1), F32)]
    scratch = [pltpu.VMEM((t, 128), F32), pltpu.VMEM((t, 128), F32), pltpu.VMEM((t, V_HEAD), F32)]
    ins = [Q, K, V, proj]
    if comm is not None:
        return _hosted_call(comm, body, ins, in_specs, out_specs, out_shape, scratch, (MLA_HEADS, S // t), name)
    res = pl.pallas_call(
        body, grid=(MLA_HEADS, S // t), in_specs=in_specs, out_specs=out_specs, out_shape=out_shape, scratch_shapes=scratch,
        name=name, compiler_params=_params(("parallel", "parallel")),
    )(*ins)
    return list(res), []


def _mla_gate_bwd(o, proj, offs, dy, name):
    S = o.shape[0]
    tm = _pick(S, 512, 16)

    def body(o_ref, mz_ref, dy_ref, do_ref, dmz_ref, dl_ref):
        sz, dsz = _silu_and_grad(mz_ref[...])
        dyv, ov = dy_ref[...], o_ref[...]
        do = (dyv * sz).astype(BF16)
        do_ref[...] = do
        dmz_ref[...] = (dyv * ov * dsz).astype(BF16)
        prod = do.astype(F32) * ov
        for h in range(MLA_HEADS):
            dl_ref[h] = jnp.sum(prod[:, h * V_HEAD:(h + 1) * V_HEAD], axis=-1, keepdims=True)

    return pl.pallas_call(
        body, grid=(S // tm,), in_specs=[_col(tm, BRANCH_W, 0), _col(tm, BRANCH_W, offs["mz"]), _col(tm, BRANCH_W, 0)],
        out_specs=[_col(tm, BRANCH_W, 0), _col(tm, BRANCH_W, 0), pl.BlockSpec((MLA_HEADS, tm, 1), lambda i: (0, i, 0))],
        out_shape=[jax.ShapeDtypeStruct((S, BRANCH_W), BF16), jax.ShapeDtypeStruct((S, BRANCH_W), BF16),
                   jax.ShapeDtypeStruct((MLA_HEADS, S, 1), F32)], name=name,
        compiler_params=_params(("parallel",)),
    )(o, proj, dy)


def _flash_bwd(Q, K, V, lse, delta, do, name, comm=None):
    S = Q.shape[0]
    tk = _pick(S, 1024, 128)
    assert tk % 256 == 0
    nq, half = S // tk, tk // 2
    once = pl.Buffered(1)

    def body(q_ref, k_ref, v_ref, lse_ref, dl_ref, do_ref, dq_ref, dk_ref, dv_ref, dk_acc, dv_acc):
        ki = pl.program_id(1)

        @pl.when(ki == 0)
        def _():
            dq_ref[...] = jnp.zeros_like(dq_ref)

        dk_acc[...] = jnp.zeros_like(dk_acc)
        dv_acc[...] = jnp.zeros_like(dv_acc)

        def tile(i, q_off, q_n, k_n, masked):
            rows = pl.ds(pl.multiple_of(i * tk + q_off, half), q_n)
            q, dov = q_ref[rows, :], do_ref[rows, :]
            k, v = k_ref[0:k_n, :], v_ref[0:k_n, :]
            s = _dot_nt(q, k) * SCORE_SCALE_LOG2
            if masked:
                r = q_off + lax.broadcasted_iota(jnp.int32, s.shape, 0)
                c = lax.broadcasted_iota(jnp.int32, s.shape, 1)
                s = jnp.where(c <= r, s, NEG)
            p = jnp.exp2(s - lse_ref[0, rows, :])
            dv_acc[0:k_n, :] += _dot_tn(p, dov)
            dp = _dot_nt(dov, v)
            ds = p * (dp - dl_ref[0, rows, :]) * (QK_HEAD ** -0.5)
            dk_acc[0:k_n, :] += _dot_tn(ds, q)
            dq_ref[rows, :] += _dot(ds, k)

        def below_diagonal(i, carry):
            tile(i, 0, tk, tk, False)
            return carry

        tile(ki, 0, half, half, True)
        tile(ki, half, half, tk, True)
        lax.fori_loop(ki + 1, nq, below_diagonal, 0)
        dk_ref[...] = dk_acc[...]
        dv_ref[...] = dv_acc[...]

    in_specs = [pl.BlockSpec((S, QK_PAD), lambda h, j: (0, h), pipeline_mode=once), pl.BlockSpec((tk, QK_PAD), lambda h, j: (j, h)),
                pl.BlockSpec((tk, V_HEAD), lambda h, j: (j, h)), pl.BlockSpec((1, S, 1), lambda h, j: (h, 0, 0), pipeline_mode=once),
                pl.BlockSpec((1, S, 1), lambda h, j: (h, 0, 0), pipeline_mode=once),
                pl.BlockSpec((S, V_HEAD), lambda h, j: (0, h), pipeline_mode=once)]
    out_specs = [pl.BlockSpec((S, QK_PAD), lambda h, j: (0, h)), pl.BlockSpec((tk, QK_PAD), lambda h, j: (j, h)),
                 pl.BlockSpec((tk, V_HEAD), lambda h, j: (j, h))]
    out_shape = [jax.ShapeDtypeStruct((S, MLA_HEADS * QK_PAD), F32), jax.ShapeDtypeStruct((S, MLA_HEADS * QK_PAD), F32),
                 jax.ShapeDtypeStruct((S, BRANCH_W), F32)]
    scratch = [pltpu.VMEM((tk, QK_PAD), F32), pltpu.VMEM((tk, V_HEAD), F32)]
    ins = [Q, K, V, lse, delta, do]
    if comm is not None:
        return _hosted_call(comm, body, ins, in_specs, out_specs, out_shape, scratch, (MLA_HEADS, nq), name)
    res = pl.pallas_call(
        body, grid=(MLA_HEADS, nq), in_specs=in_specs, out_specs=out_specs, out_shape=out_shape, scratch_shapes=scratch, name=name,
        compiler_params=_params(("parallel", "arbitrary")),
    )(*ins)
    return list(res), []


def _merge_specs(S, D, offs, tm, tn, layer):
    g_cb = offs["g"] // tn
    nj = D // tn
    y_specs = [pl.BlockSpec((tm, BRANCH_W), lambda j, i: (i, 0)) for _ in range(4)]
    w_spec = pl.BlockSpec((None, 4, BRANCH_W, tn), lambda j, i: (layer, 0, 0, j))
    g_specs = [pl.BlockSpec((tm, tn), functools.partial(lambda j, i, b: (i, g_cb + b * nj + j), b=b)) for b in range(4)]
    bias_spec = pl.BlockSpec((4, tn), lambda j, i: (0, j))
    return y_specs, w_spec, g_specs, bias_spec


def _merge_fwd(ys, wb, layer, proj, offs, gate_b, name):
    S, D = ys[0].shape[0], wb.shape[3]
    tm, tn = _pick(S, 512, 16), _pick(D, 512, 128)
    y_specs, w_spec, g_specs, bias_spec = _merge_specs(S, D, offs, tm, tn, layer)

    def body(y0, y1, y2, y3, w_ref, g0, g1, g2, g3, b_ref, o_ref, t_ref):
        acc = None
        for b, (y_ref, g_ref) in enumerate(zip((y0, y1, y2, y3), (g0, g1, g2, g3))):
            gate = 1.0 / (1.0 + jnp.exp(-(g_ref[...] + b_ref[b:b + 1, :])))
            t = _dot(y_ref[...], w_ref[b])
            t_ref[b] = t.astype(BF16)
            term = gate * t
            acc = term if acc is None else acc + term
        o_ref[...] = acc.astype(BF16)

    return pl.pallas_call(
        body, grid=(D // tn, S // tm), in_specs=y_specs + [w_spec] + g_specs + [bias_spec],
        out_specs=[pl.BlockSpec((tm, tn), lambda j, i: (i, j)), pl.BlockSpec((4, tm, tn), lambda j, i: (0, i, j))],
        out_shape=[jax.ShapeDtypeStruct((S, D), BF16), jax.ShapeDtypeStruct((4, S, D), BF16)], name=name,
        compiler_params=_params(("parallel", "parallel")),
    )(*ys, wb, proj, proj, proj, proj, gate_b)


def _merge_bwd(t, proj, offs, gate_b, dmerged, name):
    _, S, D = t.shape
    tm, tn = _pick(S, 512, 16), _pick(D, 512, 128)
    nj = D // tn
    _, _, g_specs, bias_spec = _merge_specs(S, D, offs, tm, tn, 0)

    def body(t_ref, g0, g1, g2, g3, b_ref, dm_ref, dt_ref, dg0, dg1, dg2, dg3, db_ref):
        @pl.when(pl.program_id(1) == 0)
        def _():
            db_ref[...] = jnp.zeros_like(db_ref)

        dm = dm_ref[...]
        for b, (g_ref, dg_ref) in enumerate(zip((g0, g1, g2, g3), (dg0, dg1, dg2, dg3))):
            gate = 1.0 / (1.0 + jnp.exp(-(g_ref[...] + b_ref[b:b + 1, :])))
            dt_ref[b] = (dm * gate).astype(BF16)
            dgp = dm * t_ref[b].astype(F32) * gate * (1.0 - gate)
            dg_ref[...] = dgp.astype(BF16)
            db_ref[b:b + 1, :] += jnp.sum(dgp, axis=0, keepdims=True)

    stacked = pl.BlockSpec((4, tm, tn), lambda j, i: (0, i, j))
    tile = pl.BlockSpec((tm, tn), lambda j, i: (i, j))
    return pl.pallas_call(
        body, grid=(nj, S // tm), in_specs=[stacked] + g_specs + [bias_spec, tile],
        out_specs=[stacked] + [tile] * 4 + [pl.BlockSpec((4, tn), lambda j, i: (0, j))],
        out_shape=[jax.ShapeDtypeStruct((4, S, D), BF16)] + [jax.ShapeDtypeStruct((S, D), BF16)] * 4 + [jax.ShapeDtypeStruct((4, D), F32)],
        name=name, compiler_params=_params(("parallel", "arbitrary")),
    )(t, proj, proj, proj, proj, gate_b, dmerged)


HBM_SPEC = pl.BlockSpec(memory_space=pl.ANY)


def _mesh_place():
    x, y, c = lax.axis_index("x"), lax.axis_index("y"), lax.axis_index("c")
    return x, y, c, [(1 - x, y), (x, 1 - y), (1 - x, 1 - y)]


def _remote(src, dst, send_sem, recv_sem, to):
    return pltpu.make_async_remote_copy(src_ref=src, dst_ref=dst, send_sem=send_sem, recv_sem=recv_sem, device_id=to,
                                        device_id_type=pl.DeviceIdType.MESH)


class _GatherPlan:
    def __init__(self, arrs):
        self.arrs, self.n = list(arrs), len(arrs)
        self.out_shape = [jax.ShapeDtypeStruct((N_DEV,) + a.shape, a.dtype) for a in arrs]
        self.scratch = [pltpu.SemaphoreType.DMA((self.n, 7)), pltpu.SemaphoreType.DMA((self.n, 7)), pltpu.SemaphoreType.DMA((self.n,))]

    def _copies(self, ins, dsts, sems):
        send_sems, recv_sems, local_sems = sems
        x, y, c, _ = _mesh_place()
        slot = lambda chip, pc: 4 * chip[0] + 2 * chip[1] + pc
        here, sibling = (x, y), (x, y, 1 - c)
        first = (x + (1 - 2 * x) * (1 - c), y + (1 - 2 * y) * c)
        second = (x + (1 - 2 * x) * c, y + (1 - 2 * y) * (1 - c))
        far = (1 - x, 1 - y)

        def copy(a, k, src, block, to):
            return lambda: _remote(src(), dsts[a].at[block], send_sems.at[a, k], recv_sems.at[a, k], to)

        rng = range(self.n)
        shard = lambda a: (lambda: ins[a])
        held = lambda a, chip: (lambda: dsts[a].at[slot(chip, c)])
        local = [(lambda a=a: pltpu.make_async_copy(ins[a], dsts[a].at[slot(here, c)], local_sems.at[a])) for a in rng]
        own = [copy(a, k, shard(a), slot(here, c), to) for a in rng for k, to in ((0, sibling), (1, (*first, c)), (2, (*second, c)))]
        steps = []
        for a in rng:
            steps.append((copy(a, 1, shard(a), slot(first, c), sibling),
                          [copy(a, 4, held(a, first), slot(first, c), sibling), copy(a, 3, held(a, first), slot(first, c), (*second, c))]))
            steps.append((copy(a, 2, shard(a), slot(second, c), sibling), [copy(a, 5, held(a, second), slot(second, c), sibling)]))
            steps.append((copy(a, 3, shard(a), slot(far, c), sibling), [copy(a, 6, held(a, far), slot(far, c), sibling)]))
        from_sibling = [copy(a, k, shard(a), slot(chip, 1 - c), sibling) for a in rng
                        for k, chip in ((0, here), (4, second), (5, first), (6, far))]
        return local, own, steps, from_sibling

    def start(self, ins, dsts, sems):
        local, own, _, _ = self._copies(ins, dsts, sems)
        for make in local + own:
            make().start()

    def relay(self, stage, ins, dsts, sems):
        _, _, steps, _ = self._copies(ins, dsts, sems)
        for index, (arrived, onward) in enumerate(steps):
            if (index % 3 == 2) == (stage == 1):
                arrived().wait_recv()
                for make in onward:
                    make().start()

    def finish(self, ins, dsts, sems):
        local, own, steps, from_sibling = self._copies(ins, dsts, sems)
        for make in from_sibling:
            make().wait_recv()
        for make in own + [m for _, onward in steps for m in onward]:
            make().wait_send()
        for make in local:
            make().wait()


class _ChipExchangePlan:
    def __init__(self, arrs):
        self.arrs, self.n = list(arrs), len(arrs)
        self.out_shape = [jax.ShapeDtypeStruct(a.shape, a.dtype) for a in arrs]
        self.scratch = [pltpu.SemaphoreType.DMA((self.n, 3)), pltpu.SemaphoreType.DMA((self.n, 3)), pltpu.SemaphoreType.DMA((self.n,))]

    def _copies(self, ins, dsts, sems):
        send_sems, recv_sems, local_sems = sems
        x, y, c, chips = _mesh_place()
        mine = 2 * x + y
        local = [pltpu.make_async_copy(ins[a].at[mine], dsts[a].at[mine], local_sems.at[a]) for a in range(self.n)]
        remote = [_remote(ins[a].at[2 * px + py], dsts[a].at[mine], send_sems.at[a, j], recv_sems.at[a, j], (px, py, c))
                  for a in range(self.n) for j, (px, py) in enumerate(chips)]
        return local, remote

    def start(self, ins, dsts, sems):
        local, remote = self._copies(ins, dsts, sems)
        for cp in local + remote:
            cp.start()

    def relay(self, stage, ins, dsts, sems):
        pass

    def finish(self, ins, dsts, sems):
        local, remote = self._copies(ins, dsts, sems)
        for cp in remote + local:
            cp.wait()


def _run_plan(plan, name):
    n = plan.n

    def body(*refs):
        ins, dsts, sems = refs[:n], refs[n:2 * n], refs[2 * n:]
        plan.start(ins, dsts, sems)
        plan.relay(0, ins, dsts, sems)
        plan.relay(1, ins, dsts, sems)
        plan.finish(ins, dsts, sems)

    res = pl.pallas_call(body, in_specs=[HBM_SPEC] * n, out_specs=[HBM_SPEC] * n, out_shape=plan.out_shape,
                         scratch_shapes=plan.scratch, name=name)(*plan.arrs)
    return list(res)


def _sibling_swap(arrs, name):
    n = len(arrs)

    def body(*refs):
        ins, dsts = refs[:n], refs[n:2 * n]
        send_sems, recv_sems = refs[2 * n:]
        x, y, c, _ = _mesh_place()
        cps = [_remote(ins[a].at[q, 1 - c], dsts[a].at[q], send_sems.at[a, q], recv_sems.at[a, q], (x, y, 1 - c))
               for a in range(n) for q in range(4)]
        for cp in cps:
            cp.start()
        for cp in cps:
            cp.wait()

    res = pl.pallas_call(
        body, in_specs=[HBM_SPEC] * n, out_specs=[HBM_SPEC] * n,
        out_shape=[jax.ShapeDtypeStruct((4,) + a.shape[2:], a.dtype) for a in arrs],
        scratch_shapes=[pltpu.SemaphoreType.DMA((n, 4)), pltpu.SemaphoreType.DMA((n, 4))], name=name,
    )(*arrs)
    return list(res)


def _pair_sum(staged, theirs, core, name):
    _, _, R, C = staged.shape
    tr = _pick(R, max(16, ((1 << 19) // C) // 16 * 16), 16)

    def body(c_ref, a_ref, b_ref, o_ref):
        o_ref[...] = (a_ref[...].astype(F32) + b_ref[...].astype(F32)).astype(BF16)

    blk = pl.BlockSpec((None, tr, C), lambda q, i, c: (q, i, 0))
    grid_spec = pltpu.PrefetchScalarGridSpec(
        num_scalar_prefetch=1, grid=(4, R // tr),
        in_specs=[pl.BlockSpec((None, None, tr, C), lambda q, i, c: (q, c[0], i, 0)), blk], out_specs=blk)
    return pl.pallas_call(
        body, grid_spec=grid_spec, out_shape=jax.ShapeDtypeStruct((4, R, C), BF16), name=name,
        compiler_params=_params(("parallel", "parallel")),
    )(core, staged, theirs)


def _adamw(parts, w, m, v, name):
    R, C = w.shape
    n_parts = parts[0].shape[0]
    rows = [p.shape[1] for p in parts]
    assert sum(rows) == R
    tr = _pick(min(rows), max(16, ((1 << 18) // C) // 16 * 16), 16)
    assert all(r % tr == 0 for r in rows)
    bounds, lo = [], 0
    for r in rows:
        bounds.append((lo, lo + r // tr))
        lo += r // tr
    n_lists = len(parts)

    def body(*refs):
        p_refs = refs[:n_lists]
        w_ref, m_ref, v_ref, g_ref, d_ref, nm_ref, nv_ref = refs[n_lists:]
        i = pl.program_id(0)

        def update(p_ref):
            g = p_ref[0].astype(F32)
            for s in range(1, n_parts):
                g = g + p_ref[s].astype(F32)
            mn = ADAM_B1 * m_ref[...] + (1.0 - ADAM_B1) * g
            vn = ADAM_B2 * v_ref[...] + (1.0 - ADAM_B2) * (g * g)
            m_hat = mn / (1.0 - ADAM_B1 ** ADAM_STEP)
            v_hat = vn / (1.0 - ADAM_B2 ** ADAM_STEP)
            g_ref[...] = g
            d_ref[...] = -ADAM_LR * (m_hat / (jnp.sqrt(v_hat) + ADAM_EPS) + ADAM_WD * w_ref[...])
            nm_ref[...] = mn
            nv_ref[...] = vn

        if n_lists == 1:
            update(p_refs[0])
        else:
            for p_ref, (first, end) in zip(p_refs, bounds):
                pl.when((i >= first) & (i < end))(functools.partial(update, p_ref))

    blk = pl.BlockSpec((tr, C), lambda i: (i, 0))
    p_specs = [pl.BlockSpec((n_parts, tr, C), functools.partial(lambda i, first, last: (0, jnp.clip(i - first, 0, last), 0),
                                                                first=first, last=end - first - 1)) for first, end in bounds]
    return pl.pallas_call(
        body, grid=(R // tr,), in_specs=p_specs + [blk, blk, blk],
        out_specs=[blk] * 4, out_shape=[jax.ShapeDtypeStruct((R, C), F32)] * 4, name=name,
        compiler_params=_params(("parallel",)),
    )(*parts, w, m, v)


def _offsets(L, D):
    o = {"pv": 0, "pz": BRANCH_W, "cq": 2 * BRANCH_W, "ckv": 2 * BRANCH_W + L, "mz": 2 * BRANCH_W + 2 * L}
    for prev, nxt in (("mz", "cb"), ("cb", "cc"), ("cc", "cx"), ("cx", "cz"), ("cz", "xq"), ("xq", "xz"), ("xz", "g")):
        o[nxt] = o[prev] + BRANCH_W
    o["kr"] = o["g"] + 4 * D
    o["n"] = o["kr"] + 128
    return o


def _unshard(gathered, axis):
    moved = jnp.moveaxis(gathered, 0, axis)
    shape = list(moved.shape)
    shape[axis:axis + 2] = [shape[axis] * shape[axis + 1]]
    return moved.reshape(shape)


def _reshard(full, axis):
    shape = list(full.shape)
    shape[axis:axis + 1] = [N_DEV, shape[axis] // N_DEV]
    return jnp.moveaxis(full.reshape(shape), axis, 0)


def _pad_heads(g, width):
    return jnp.pad(g, ((0, 0), (0, width - g.shape[1])))


def kernel(x, mem, positions, norm_g, w_in, gate_b, pool_w, pool_scale, q_a_norm_g, kv_a_norm_g, w_uq, w_ukv, mla_q_norm_g, mla_k_norm_g, conv_w, mem_norm_g, w_mem_kv, xattn_q_norm_g, xattn_k_norm_g, w_branch, w_out, loss_target, m_norm_g, m_w_in, m_gate_b, m_pool_w, m_pool_scale, m_q_a_norm_g, m_kv_a_norm_g, m_w_uq, m_w_ukv, m_mla_q_norm_g, m_mla_k_norm_g, m_conv_w, m_mem_norm_g, m_w_mem_kv, m_xattn_q_norm_g, m_xattn_k_norm_g, m_w_branch, m_w_out, v_norm_g, v_w_in, v_gate_b, v_pool_w, v_pool_scale, v_q_a_norm_g, v_kv_a_norm_g, v_w_uq, v_w_ukv, v_mla_q_norm_g, v_mla_k_norm_g, v_conv_w, v_mem_norm_g, v_w_mem_kv, v_xattn_q_norm_g, v_xattn_k_norm_g, v_w_branch, v_w_out):
    depth = norm_g.shape[0]
    S, D = x.shape[1], x.shape[2]
    L = w_uq.shape[1]
    offs = _offsets(L, D)
    seg_a = 2 * BRANCH_W + 2 * L
    xs = x.reshape(S, D)
    mems = mem.reshape(mem.shape[1], D)

    sharded = dict(w_in=w_in, pool_w=pool_w, w_uq=w_uq, w_ukv=w_ukv, conv_w=conv_w, w_mem_kv=w_mem_kv, w_branch=w_branch, w_out=w_out)
    shard_axis = dict(w_in=2, pool_w=2, w_uq=2, w_ukv=2, conv_w=2, w_mem_kv=1, w_branch=3, w_out=1)
    def permuted_w_in(g):
        f = _unshard(g, 1)
        return jnp.concatenate([f[:, :seg_a], f[:, seg_a + QK_ROPE:], f[:, seg_a:seg_a + QK_ROPE], jnp.zeros((D, 128 - QK_ROPE), BF16)], axis=-1)

    w_in_local = [w_in[l].astype(BF16) for l in range(depth)]
    w_in_p = [permuted_w_in(_run_plan(_GatherPlan([w_in_local[0]]), "gather_w_in0")[0])] + [None] * (depth - 1)
    rest_keys = [k for k in shard_axis if k != "w_in"]
    rest_plan = _GatherPlan([sharded[k] if k == "conv_w" else sharded[k].astype(BF16) for k in rest_keys])
    later_w_in_plan = _GatherPlan(w_in_local[1:]) if depth > 1 else None

    inv = ROPE_THETA ** (-jnp.arange(0, QK_ROPE, 2, dtype=F32) / QK_ROPE)
    ang = positions.reshape(S).astype(F32)[:, None] * inv
    cos, sin, z = jnp.cos(ang), jnp.sin(ang), jnp.zeros_like(ang)
    tabs = (jnp.concatenate([cos, cos, z, z], axis=1), jnp.concatenate([-sin, z, z, z], axis=1), jnp.concatenate([z, sin, z, z], axis=1))

    row = lambda a, l: a[l][None, :]
    saved = []
    cur = xs
    for l in range(depth):
        h, hT = _rms_rows(cur, D, 0, row(norm_g, l), f"norm_fwd{l}", with_transpose=True)
        proj_tiles = dict(tm=512, tn=3712, tk=2048, n_outer=True)
        if l == 0:
            proj, carried = _matmul(h, w_in_p[0], "nn", F32, "proj_fwd0", comm=rest_plan, **proj_tiles)
            gathered = {k: _unshard(g, shard_axis[k]) for k, g in zip(rest_keys, carried)}
            conv_full = gathered["conv_w"]
            pool_w_f = gathered["pool_w"]
            w_uq_f = gathered["w_uq"].reshape(depth, L, MLA_HEADS, QK_HEAD)
            w_uq_p = jnp.pad(w_uq_f, ((0, 0), (0, 0), (0, 0), (0, QK_PAD - QK_HEAD))).reshape(depth, L, MLA_HEADS * QK_PAD)
            w_ukv_f = gathered["w_ukv"].reshape(depth, L, MLA_HEADS, QK_NOPE + V_HEAD)
            w_ukv_p = jnp.concatenate([w_ukv_f[..., :QK_NOPE].reshape(depth, L, -1), w_ukv_f[..., QK_NOPE:].reshape(depth, L, -1)], axis=-1)
            w_mem_f = gathered["w_mem_kv"]
            w_br_f = gathered["w_branch"]
            w_out_f = gathered["w_out"]
        else:
            proj = _matmul(h, w_in_p[l], "nn", F32, f"proj_fwd{l}", **proj_tiles)
        y_pool = _pool_fwd(proj, offs, pool_w_f[l], row(pool_scale, l), f"pool_fwd{l}")
        cqn = _rms_rows(proj, L, offs["cq"], row(q_a_norm_g, l), f"cq_norm{l}")
        ckvn = _rms_rows(proj, L, offs["ckv"], row(kv_a_norm_g, l), f"ckv_norm{l}")
        qp = _matmul(cqn, w_uq_p[l], "nn", F32, f"uq_fwd{l}")
        kvp = _matmul(ckvn, w_ukv_p[l], "nn", F32, f"ukv_fwd{l}")
        qg = _pad_heads(row(mla_q_norm_g, l), QK_PAD)
        kg = _pad_heads(row(mla_k_norm_g, l), QK_PAD)
        Q, K, V = _mla_qk_fwd(qp, kvp, proj, offs, tabs, qg, kg, f"mla_qk_fwd{l}")
        (o, y_mla, lse), carried = _flash_fwd(Q, K, V, proj, offs, f"flash_fwd{l}", comm=later_w_in_plan if l == 0 else None)
        if l == 0 and later_w_in_plan is not None:
            w_in_p[1:] = [permuted_w_in(g) for g in carried]
        y_conv = _conv_fwd(proj, offs, conv_full[l], f"conv_fwd{l}")
        memn = _rms_rows(mems, D, 0, row(mem_norm_g, l), f"mem_norm{l}")
        mem_kv = _matmul(memn, w_mem_f, "nn", F32, f"mem_kv{l}", b_idx=(l,))
        y_mem = _xattn_fwd(proj, offs, mem_kv, row(xattn_q_norm_g, l), row(xattn_k_norm_g, l), f"xattn_fwd{l}")
        ys = (y_pool, y_mla, y_conv, y_mem)
        gb = gate_b[l].reshape(4, D)
        merged, tproj = _merge_fwd(ys, w_br_f, l, proj, offs, gb, f"merge_fwd{l}")
        nxt = _matmul(merged, w_out_f, "nn", F32, f"out_fwd{l}", res=cur, b_idx=(l,))
        saved.append(dict(x=cur, hT=hT, proj=proj, cqn=cqn, ckvn=ckvn, qp=qp, kvp=kvp, qg=qg, kg=kg, Q=Q, K=K, V=V, o=o, lse=lse,
                          memn=memn, mem_kv=mem_kv, ys=ys, gb=gb, merged=merged, tproj=tproj))
        cur = nxt

    gx, gx_b, loss_acc = _loss_rows(cur, loss_target.reshape(S, D), "loss")
    loss = lax.psum(0.5 * loss_acc[0, 0], MESH_AXES)

    grads = {k: [None] * depth for k in ("norm_g", "w_in", "gate_b", "pool_w", "pool_scale", "q_a_norm_g", "kv_a_norm_g", "w_uq", "w_ukv",
                                         "mla_q_norm_g", "mla_k_norm_g", "conv_w", "mem_norm_g", "w_mem_kv", "xattn_q_norm_g",
                                         "xattn_k_norm_g", "w_branch", "w_out")}
    names_sh = list(shard_axis)
    core = lax.axis_index("c").astype(jnp.int32).reshape(1)
    received = [None] * depth
    pending = None

    def pair_sums(one_layer_grads, names, tag):
        staged = [_reshard(g, shard_axis[k] - 1).reshape(4, 2, -1, sharded[k].shape[-1]).astype(BF16) for g, k in zip(one_layer_grads, names)]
        given = _sibling_swap(staged, f"pair_swap_{tag}")
        return [_pair_sum(st, theirs, core, f"pair_sum_{k}_{tag}") for k, st, theirs in zip(names, staged, given)]

    for l in reversed(range(depth)):
        sv = saved[l]
        proj, ys = sv["proj"], sv["ys"]
        dmerged = _matmul(gx_b, w_out_f, "nt", F32, f"out_bwd_x{l}", b_idx=(l,))
        grads["w_out"][l] = _matmul(sv["merged"], gx_b, "tn", BF16, f"out_bwd_w{l}")
        dt, dgp0, dgp1, dgp2, dgp3, dgb = _merge_bwd(sv["tproj"], proj, offs, sv["gb"], dmerged, f"merge_bwd{l}")
        grads["gate_b"][l] = dgb.reshape(4 * D)
        dys, dwb = [], []
        for b in range(4):
            dys.append(_matmul(dt, w_br_f, "nt", F32, f"branch_bwd_x{l}_{b}", a_idx=(b,), b_idx=(l, b)))
            dwb.append(_matmul(ys[b], dt, "tn", BF16, f"branch_bwd_w{l}_{b}", b_idx=(b,)))
        grads["w_branch"][l] = jnp.stack(dwb)
        d_pz, e, d_pw, d_ps = _pool_bwd_a(proj, offs, pool_w_f[l], row(pool_scale, l), dys[0], f"pool_bwd_a{l}")
        d_pv = _pool_bwd_b(e, f"pool_bwd_b{l}")
        grads["pool_w"][l], grads["pool_scale"][l] = d_pw, d_ps[0]
        do, d_mz, delta = _mla_gate_bwd(sv["o"], proj, offs, dys[1], f"mla_gate_bwd{l}")
        (dQ, dK, dV), carried = _flash_bwd(sv["Q"], sv["K"], sv["V"], sv["lse"], delta, do, f"flash_bwd{l}",
                                           comm=None if pending is None else pending[1])
        if pending is not None:
            received[pending[0]][0], pending = carried[0], None
        dqp, dkvp, d_kr, dqg, dkg = _mla_qk_bwd(sv["qp"], sv["kvp"], proj, offs, tabs, sv["qg"], sv["kg"], dQ, dK, dV, f"mla_qk_bwd{l}")
        grads["mla_q_norm_g"][l], grads["mla_k_norm_g"][l] = dqg[0, :QK_HEAD], dkg[0, :QK_HEAD]
        dcqn = _matmul(dqp, w_uq_p[l], "nt", F32, f"uq_bwd_x{l}")
        d_wuq = _matmul(sv["cqn"], dqp, "tn", BF16, f"uq_bwd_w{l}")
        dckvn = _matmul(dkvp, w_ukv_p[l], "nt", F32, f"ukv_bwd_x{l}")
        d_wukv = _matmul(sv["ckvn"], dkvp, "tn", BF16, f"ukv_bwd_w{l}")
        grads["w_uq"][l] = d_wuq.reshape(L, MLA_HEADS, QK_PAD)[..., :QK_HEAD].reshape(L, MLA_HEADS * QK_HEAD)
        grads["w_ukv"][l] = jnp.concatenate([d_wukv[:, :MLA_HEADS * QK_NOPE].reshape(L, MLA_HEADS, QK_NOPE),
                                             d_wukv[:, MLA_HEADS * QK_NOPE:].reshape(L, MLA_HEADS, V_HEAD)], axis=-1).reshape(L, -1)
        d_cq, dqa = _rms_rows_bwd(proj, L, offs["cq"], row(q_a_norm_g, l), dcqn, BF16, f"cq_norm_bwd{l}")
        d_ckv, dkva = _rms_rows_bwd(proj, L, offs["ckv"], row(kv_a_norm_g, l), dckvn, BF16, f"ckv_norm_bwd{l}")
        grads["q_a_norm_g"][l], grads["kv_a_norm_g"][l] = dqa[0], dkva[0]
        d_cb, d_cc, d_cx, d_cz, d_cw = _conv_bwd(proj, offs, conv_full[l], dys[2], f"conv_bwd{l}")
        grads["conv_w"][l] = d_cw
        d_xq, d_xz, d_memkv, dxqg, dxkg = _xattn_bwd(proj, offs, sv["mem_kv"], row(xattn_q_norm_g, l), row(xattn_k_norm_g, l), dys[3], f"xattn_bwd{l}")
        grads["xattn_q_norm_g"][l], grads["xattn_k_norm_g"][l] = dxqg[0], dxkg[0]
        grads["w_mem_kv"][l] = _matmul(sv["memn"], d_memkv, "tn", BF16, f"mem_kv_bwd_w{l}")
        dmemn = _matmul(d_memkv, w_mem_f, "nt", F32, f"mem_kv_bwd_x{l}", b_idx=(l,))
        grads["mem_norm_g"][l] = _norm_g_grad(mems, row(mem_norm_g, l), dmemn, f"mem_norm_bwd{l}")[0]
        dproj = jnp.concatenate([d_pv, d_pz, d_cq, d_ckv, d_mz, d_cb, d_cc, d_cx, d_cz, d_xq, d_xz, dgp0, dgp1, dgp2, dgp3, d_kr], axis=1)
        early_names = [k for k in names_sh if k != "w_in"]
        early = pair_sums([grads[k][l] for k in early_names], early_names, f"early{l}")
        dh, early_received = _matmul(dproj, w_in_p[l], "nt", F32, f"proj_bwd_x{l}", tm=1024, tn=1024, tk=3712, comm=_ChipExchangePlan(early))
        n_p = offs["n"]
        w_tiles = dict(tm=512, tn=3712, tk=2048, n_outer=True)
        unpermute = lambda d: jnp.concatenate([d[:, :seg_a], d[:, n_p - 128:n_p - 128 + QK_ROPE], d[:, seg_a:n_p - 128]], axis=1)
        received[l] = [None] + early_received
        if l > 0:
            d_win = unpermute(_matmul(sv["hT"], dproj, "nn", BF16, f"proj_bwd_w{l}", **w_tiles))
            pending = (l, _ChipExchangePlan(pair_sums([d_win], ["w_in"], f"w_in{l}")))
        else:
            n_split = 4
            rows_per = D // n_split
            plan, pieces = None, []
            for part in range(n_split):
                hT_rows = sv["hT"][part * rows_per:(part + 1) * rows_per]
                if plan is None:
                    d = _matmul(hT_rows, dproj, "nn", BF16, f"proj_bwd_w0_part{part}", **w_tiles)
                else:
                    d, got = _matmul(hT_rows, dproj, "nn", BF16, f"proj_bwd_w0_part{part}", comm=plan, **w_tiles)
                    pieces.append(got[0])
                plan = _ChipExchangePlan(pair_sums([unpermute(d)], ["w_in"], f"w_in0_part{part}"))
            pieces.append(_run_plan(plan, "chip_exchange_w_in0_last")[0])
            received[0][0] = pieces
        if l > 0:
            gx, dng, gx_b = _rms_rows_bwd(sv["x"], D, 0, row(norm_g, l), dh, F32, f"norm_bwd{l}", res=gx, with_bf16=True)
        else:
            gx, dng = _rms_rows_bwd(sv["x"], D, 0, row(norm_g, l), dh, F32, f"norm_bwd{l}", res=gx)
        grads["norm_g"][l] = dng[0]
    grad_x = gx.reshape(1, S, D)
    full = {k: jnp.stack(v) for k, v in grads.items() if k not in shard_axis}

    moments = dict(w_in=(m_w_in, v_w_in), pool_w=(m_pool_w, v_pool_w), w_uq=(m_w_uq, v_w_uq), w_ukv=(m_w_ukv, v_w_ukv),
                   conv_w=(m_conv_w, v_conv_w), w_mem_kv=(m_w_mem_kv, v_w_mem_kv), w_branch=(m_w_branch, v_w_branch), w_out=(m_w_out, v_w_out))
    results = {}
    for ki, k in enumerate(names_sh):
        w = sharded[k]
        m, v = moments[k]
        flat = lambda a: a.reshape(-1, w.shape[-1])
        parts = [p for l in range(depth) for p in (received[l][ki] if isinstance(received[l][ki], list) else [received[l][ki]])]
        if any(p.shape[1] % 16 for p in parts):
            parts = [jnp.concatenate(parts, axis=1)]
        outs = _adamw(parts, flat(w), flat(m), flat(v), f"adamw_{k}")
        results[k] = [o.reshape(w.shape) for o in outs]

    small = dict(norm_g=(norm_g, m_norm_g, v_norm_g), gate_b=(gate_b, m_gate_b, v_gate_b), pool_scale=(pool_scale, m_pool_scale, v_pool_scale),
                 q_a_norm_g=(q_a_norm_g, m_q_a_norm_g, v_q_a_norm_g), kv_a_norm_g=(kv_a_norm_g, m_kv_a_norm_g, v_kv_a_norm_g),
                 mla_q_norm_g=(mla_q_norm_g, m_mla_q_norm_g, v_mla_q_norm_g), mla_k_norm_g=(mla_k_norm_g, m_mla_k_norm_g, v_mla_k_norm_g),
                 mem_norm_g=(mem_norm_g, m_mem_norm_g, v_mem_norm_g), xattn_q_norm_g=(xattn_q_norm_g, m_xattn_q_norm_g, v_xattn_q_norm_g),
                 xattn_k_norm_g=(xattn_k_norm_g, m_xattn_k_norm_g, v_xattn_k_norm_g))
    names_sm = list(small)
    sizes = [small[k][0].size for k in names_sm]
    total = sum(sizes)
    rows = -(-total // (8 * 128)) * 8

    def pack(arrs):
        flat = jnp.concatenate([a.reshape(-1) for a in arrs])
        return jnp.pad(flat, (0, rows * 128 - total)).reshape(rows, 128)

    g_small = _run_plan(_GatherPlan([pack([full[k] for k in names_sm])]), "gather_small_grads")[0]
    outs = _adamw([g_small], pack([small[k][0] for k in names_sm]), pack([small[k][1] for k in names_sm]),
                  pack([small[k][2] for k in names_sm]), "adamw_small")
    for kind, o in enumerate(outs):
        flat, pos = o.reshape(-1), 0
        for k, n in zip(names_sm, sizes):
            results.setdefault(k, [None] * 4)[kind] = flat[pos:pos + n].reshape(small[k][0].shape)
            pos += n

    order = ["norm_g", "w_in", "gate_b", "pool_w", "pool_scale", "q_a_norm_g", "kv_a_norm_g", "w_uq", "w_ukv", "mla_q_norm_g",
             "mla_k_norm_g", "conv_w", "mem_norm_g", "w_mem_kv", "xattn_q_norm_g", "xattn_k_norm_g", "w_branch", "w_out"]
    return (loss, grad_x, *[results[k][0] for k in order], *[results[k][1] for k in order],
            *[results[k][2] for k in order], *[results[k][3] for k in order])
```
